```python
import jax, jax.numpy as jnp
from jax import lax
import numpy as np

D_MODEL = 1024
BATCH = 2
SEQ = 8192
DEPTH = 2

CTX_LEN = 256
GRID_W = 64
D_MIX = D_MODEL
GLA_WIDTH = D_MIX // 2
CONV_WIDTH = D_MIX // 4
POOL_WIDTH = D_MIX - GLA_WIDTH - CONV_WIDTH
GLA_HEADS = 4
GLA_DV = GLA_WIDTH // GLA_HEADS
GLA_DK = GLA_DV // 2
GLA_KEY_WIDTH = GLA_HEADS * GLA_DK
GATE_RANK = 16
GATE_TAU = 16.0
CHUNK = 64
CONV_K = 3
CONV_HEADS = 4
POOL_WINDOWS = (2, 4, 8, 16)
POOL_GROUP = POOL_WIDTH // len(POOL_WINDOWS)
D_FF = 4 * D_MODEL
N_MOD = 6
EPS = 1e-6

Q_OFF = 0
K_OFF = Q_OFF + GLA_KEY_WIDTH
V_OFF = K_OFF + GLA_KEY_WIDTH
G_OFF = V_OFF + GLA_WIDTH
R_OFF = G_OFF + 2 * GATE_RANK
CB_OFF = R_OFF + GLA_WIDTH
CC_OFF = CB_OFF + CONV_WIDTH
CH_OFF = CC_OFF + CONV_WIDTH
P_OFF = CH_OFF + CONV_WIDTH
IN_COLS = P_OFF + POOL_WIDTH

kernel_name = 'hybrid_gla_conv_pool_dit_trunk'


def rmsnorm(x, w):
    xf = x.astype(jnp.float32)
    y = xf * lax.rsqrt(jnp.mean(xf * xf, axis=-1, keepdims=True) + EPS)
    return (y * w.astype(jnp.float32)).astype(x.dtype)


def modulate(x, w, shift, scale):
    return rmsnorm(x, w) * (1 + scale) + shift


def adaln(cvec, w_mod, b_mod, n):
    m = jax.nn.silu(cvec) @ w_mod[:, : n * D_MODEL] + b_mod[: n * D_MODEL]
    return jnp.split(m, n, axis=-1)


def to_heads(t, h):
    b, n, w = t.shape
    return t.reshape(b, n, h, w // h).transpose(0, 2, 1, 3)


def rev(t):
    return jnp.flip(t, axis=2)


def gla_kvg(k_raw, v_raw, z_raw, gate_w, gate_b):
    k = to_heads(k_raw, GLA_HEADS).astype(jnp.float32)
    v = to_heads(v_raw, GLA_HEADS).astype(jnp.float32)
    gates = tuple(
        to_heads(jax.nn.log_sigmoid((z_raw[..., d * GATE_RANK:(d + 1) * GATE_RANK] @ gate_w[d]
                                     + gate_b[d]).astype(jnp.float32)) / GATE_TAU, GLA_HEADS)
        for d in range(2))
    return k, v, gates


def gla_chunked(q, k, v, g, s0):
    bsz, h, n, _ = q.shape
    nc = n // CHUNK
    ch = lambda t: t.reshape(bsz, h, nc, CHUNK, t.shape[-1])
    q, k, v, g = ch(q), ch(k), ch(v), ch(g)
    b = jnp.cumsum(g, axis=3)
    b_last = b[:, :, :, -1:, :]
    u = jnp.einsum('bhncd,bhnce->bhnde', k * jnp.exp(b_last - b), v)
    decay = jnp.exp(b_last[:, :, :, 0, :])

    def step(s, inp):
        u_n, a_n = inp
        return a_n[..., None] * s + u_n, s

    s_final, s_before = lax.scan(step, s0, (jnp.moveaxis(u, 2, 0), jnp.moveaxis(decay, 2, 0)))
    s_before = jnp.moveaxis(s_before, 0, 2)
    qe = q * jnp.exp(b)
    ki = k * jnp.exp(-b)
    tri = jnp.tril(jnp.ones((CHUNK, CHUNK), dtype=bool))
    a = jnp.where(tri, jnp.einsum('bhntd,bhnsd->bhnts', qe, ki), 0.0)
    o = jnp.einsum('bhnts,bhnse->bhnte', a, v) + jnp.einsum('bhntd,bhnde->bhnte', qe, s_before)
    return o.reshape(bsz, h, n, -1), s_final


def gla_final_state(k, v, g):
    gc = jnp.cumsum(g, axis=2)
    return jnp.einsum('bhnd,bhne->bhde', k * jnp.exp(gc[:, :, -1:, :] - gc), v)


def gla_bidir(q, k, v, gates, s_init):
    o_f, s_f = gla_chunked(q, k, v, gates[0], s_init[0])
    o_b, s_b = gla_chunked(rev(q), rev(k), rev(v), rev(gates[1]), s_init[1])
    return o_f + rev(o_b), (s_f, s_b)


def gla_out(o, r, norm_w):
    o = o * lax.rsqrt(jnp.mean(o * o, axis=-1, keepdims=True) + EPS) * norm_w.astype(jnp.float32)
    bsz, h, n, dv = o.shape
    o = o.transpose(0, 2, 1, 3).reshape(bsz, n, h * dv)
    return (o * jax.nn.silu(r.astype(jnp.float32))).astype(r.dtype)


def short_conv(bg, cg, hv, conv_w):
    z = cg * hv
    length = z.shape[-2]
    zp = jnp.pad(z, [(0, 0)] * (z.ndim - 2) + [(CONV_K // 2, CONV_K // 2), (0, 0)])
    y = sum(conv_w[j] * zp[..., j:j + length, :] for j in range(CONV_K))
    return bg * y


def multi_scale_pool(u, pool_w, pool_scale):
    length = u.shape[-2]
    uf = u.astype(jnp.float32)
    cs = jnp.concatenate([jnp.zeros_like(uf[..., :1, :]), jnp.cumsum(uf, axis=-2)], axis=-2)
    t = np.arange(length)
    outs = []
    for gi, w in enumerate(POOL_WINDOWS):
        lo = np.clip(t - w // 2, 0, length)
        hi = np.clip(t + w // 2, 0, length)
        sl = slice(gi * POOL_GROUP, (gi + 1) * POOL_GROUP)
        csg = cs[..., sl]
        mean = (jnp.take(csg, hi, axis=-2) - jnp.take(csg, lo, axis=-2)) / (hi - lo).astype(np.float32)[:, None]
        outs.append(mean - uf[..., sl])
    p = jnp.stack(outs, axis=-2)
    y = jnp.einsum('...lgc,gcd->...lgd', p, pool_w.astype(jnp.float32))
    y = y.reshape(y.shape[:-2] + (POOL_WIDTH,)) * pool_scale.astype(jnp.float32)
    return y.astype(u.dtype)


def mix_and_mlp(x, proj, o_gla, n_seg, g1, sh2, sc2, g2, norm2_w, gla_norm_w, conv_w, pool_w,
                pool_scale, w_out, w_mlp1, w_mlp2):
    bsz, n, _ = proj.shape
    seg = lambda t: t.reshape(bsz, n_seg, n // n_seg, t.shape[-1])
    y_gla = gla_out(o_gla, proj[..., R_OFF:CB_OFF], gla_norm_w)
    y_conv = short_conv(seg(proj[..., CB_OFF:CC_OFF]), seg(proj[..., CC_OFF:CH_OFF]),
                        seg(proj[..., CH_OFF:P_OFF]), conv_w).reshape(bsz, n, CONV_WIDTH)
    y_pool = multi_scale_pool(seg(proj[..., P_OFF:IN_COLS]), pool_w, pool_scale).reshape(bsz, n, POOL_WIDTH)
    mix = jnp.concatenate([y_gla, y_conv, y_pool], axis=-1) @ w_out
    x = x + g1 * mix
    h2 = modulate(x, norm2_w, sh2, sc2)
    return x + g2 * (jnp.square(jax.nn.relu(h2 @ w_mlp1)) @ w_mlp2)


def setup_inputs(seed: int = 0) -> dict:
    key = jax.random.key(seed)
    ks = jax.random.split(key, 20)
    f32 = jnp.float32
    nrm = lambda k, shape, s: jax.random.normal(k, shape, f32) * s
    return {
        'x': nrm(ks[0], (BATCH, SEQ, D_MODEL), 1.0),
        'c': nrm(ks[1], (BATCH, D_MODEL), 1.0),
        'ctx': nrm(ks[2], (BATCH, CTX_LEN, D_MODEL), 1.0),
        'c_ctx': nrm(ks[3], (D_MODEL,), 1.0),
        'w_mod': nrm(ks[4], (DEPTH, D_MODEL, N_MOD * D_MODEL), 0.5 * D_MODEL ** -0.5),
        'b_mod': nrm(ks[5], (DEPTH, N_MOD * D_MODEL), 0.02),
        'norm1_w': 1.0 + nrm(ks[6], (DEPTH, D_MODEL), 0.02),
        'norm2_w': 1.0 + nrm(ks[7], (DEPTH, D_MODEL), 0.02),
        'w_in': nrm(ks[8], (DEPTH, D_MODEL, IN_COLS), D_MODEL ** -0.5),
        'gla_gate_w': nrm(ks[9], (DEPTH, 2, GATE_RANK, GLA_KEY_WIDTH), GATE_RANK ** -0.5),
        'gla_gate_b': jax.random.uniform(ks[10], (DEPTH, 2, GLA_KEY_WIDTH), f32, 1.0, 5.0),
        'gla_norm_w': 1.0 + nrm(ks[11], (DEPTH, GLA_DV), 0.02),
        'conv_w': nrm(ks[12], (DEPTH, CONV_K, CONV_WIDTH), CONV_K ** -0.5),
        'pool_w': nrm(ks[13], (DEPTH, len(POOL_WINDOWS), POOL_GROUP, POOL_GROUP), POOL_GROUP ** -0.5),
        'pool_scale': 1.0 + nrm(ks[14], (DEPTH, POOL_WIDTH), 0.02),
        'w_out': nrm(ks[15], (DEPTH, D_MIX, D_MODEL), D_MIX ** -0.5),
        'w_mlp1': nrm(ks[16], (DEPTH, D_MODEL, D_FF), D_MODEL ** -0.5),
        'w_mlp2': nrm(ks[17], (DEPTH, D_FF, D_MODEL), D_FF ** -0.5),
        'final_norm_w': 1.0 + nrm(ks[18], (D_MODEL,), 0.02),
    }


def reference(x, c, ctx, c_ctx, w_mod, b_mod, norm1_w, norm2_w, w_in, gla_gate_w, gla_gate_b,
              gla_norm_w, conv_w, pool_w, pool_scale, w_out, w_mlp1, w_mlp2, final_norm_w):
    bsz, n, _ = x.shape
    rows = n // GRID_W
    xc = ctx
    for l in range(DEPTH):
        last = l == DEPTH - 1
        sh1, sc1, g1, sh2, sc2, g2 = [m[:, None, :] for m in adaln(c, w_mod[l], b_mod[l], N_MOD)]
        h = modulate(x, norm1_w[l], sh1, sc1)
        proj = h @ w_in[l]
        q = to_heads(proj[..., Q_OFF:K_OFF], GLA_HEADS).astype(jnp.float32) * GLA_DK ** -0.5
        k, v, gates = gla_kvg(proj[..., K_OFF:V_OFF], proj[..., V_OFF:G_OFF], proj[..., G_OFF:R_OFF],
                              gla_gate_w[l], gla_gate_b[l])
        if last:
            csh1, csc1 = adaln(c_ctx, w_mod[l], b_mod[l], 2)
            hc = modulate(xc, norm1_w[l], csh1, csc1)
            pc = hc @ w_in[l][:, K_OFF:R_OFF]
            kc, vc, gc = gla_kvg(pc[..., :V_OFF - K_OFF], pc[..., V_OFF - K_OFF:G_OFF - K_OFF],
                                 pc[..., G_OFF - K_OFF:], gla_gate_w[l], gla_gate_b[l])
            s_ctx = (gla_final_state(kc, vc, gc[0]), gla_final_state(rev(kc), rev(vc), rev(gc[1])))
        else:
            csh1, csc1, cg1, csh2, csc2, cg2 = adaln(c_ctx, w_mod[l], b_mod[l], N_MOD)
            hc = modulate(xc, norm1_w[l], csh1, csc1)
            pc = hc @ w_in[l]
            qc = to_heads(pc[..., Q_OFF:K_OFF], GLA_HEADS).astype(jnp.float32) * GLA_DK ** -0.5
            kc, vc, gc = gla_kvg(pc[..., K_OFF:V_OFF], pc[..., V_OFF:G_OFF], pc[..., G_OFF:R_OFF],
                                 gla_gate_w[l], gla_gate_b[l])
            s0 = jnp.zeros((bsz, GLA_HEADS, GLA_DK, GLA_DV), jnp.float32)
            oc, s_ctx = gla_bidir(qc, kc, vc, gc, (s0, s0))
            xc = mix_and_mlp(xc, pc, oc, 1, cg1, csh2, csc2, cg2, norm2_w[l], gla_norm_w[l], conv_w[l],
                             pool_w[l], pool_scale[l], w_out[l], w_mlp1[l], w_mlp2[l])
        o_lat, _ = gla_bidir(q, k, v, gates, s_ctx)
        x = mix_and_mlp(x, proj, o_lat, rows, g1, sh2, sc2, g2, norm2_w[l], gla_norm_w[l], conv_w[l],
                        pool_w[l], pool_scale[l], w_out[l], w_mlp1[l], w_mlp2[l])
    return rmsnorm(x, final_norm_w)
```

```python
import functools

import jax
import jax.numpy as jnp
from jax import lax
from jax.experimental import pallas as pl
from jax.experimental.pallas import tpu as pltpu

D_MODEL = 1024
GLA_HEADS = 4
GLA_DK = 64
GLA_DV = 128
SLAB = 2 * GLA_DK
KEY_COLS = GLA_HEADS * SLAB
VAL_COLS = GLA_HEADS * GLA_DV
GATE_RANK = 16
GATE_TAU = 16.0
GLA_CHUNK = 128
CONV_WIDTH = 256
POOL_WIDTH = 256
POOL_GROUP = 64
POOL_HALF_WINDOWS = (1, 2, 4, 8)
D_FF = 4096
FF_CHUNK = 1024
N_MOD = 6
EPS = 1e-6
GRID_W = 64
RC_COLS = VAL_COLS + 3 * CONV_WIDTH + POOL_WIDTH
MOD_ROWS = 8
ADALN_COLS = 1536
VMEM_LIMIT_BYTES = 56 * 1024 * 1024

_NT = (((1,), (1,)), ((), ()))
_BF16 = jnp.bfloat16
_F32 = jnp.float32


def _dot(a, b):
    return jnp.dot(a, b, preferred_element_type=_F32)


def _sigmoid(x):
    return 1.0 / (1.0 + jnp.exp(-x))


def _adaln_kernel(cv_ref, w_ref, b_ref, o_ref):
    cv = cv_ref[...]
    s = (cv * _sigmoid(cv)).astype(_BF16)
    o_ref[...] = _dot(s, w_ref[...].astype(_BF16)) + b_ref[...]


def _adaln(cv, w_mod, b_mod):
    depth = w_mod.shape[0]
    cols = w_mod.shape[2]
    return pl.pallas_call(
        _adaln_kernel,
        grid=(depth, cols // ADALN_COLS),
        in_specs=[
            pl.BlockSpec((MOD_ROWS, D_MODEL), lambda l, j: (0, 0)),
            pl.BlockSpec((None, D_MODEL, ADALN_COLS), lambda l, j: (l, 0, j)),
            pl.BlockSpec((None, 1, ADALN_COLS), lambda l, j: (l, 0, j)),
        ],
        out_specs=pl.BlockSpec((None, MOD_ROWS, ADALN_COLS), lambda l, j: (l, 0, j)),
        out_shape=jax.ShapeDtypeStruct((depth, MOD_ROWS, cols), _F32),
        compiler_params=pltpu.CompilerParams(vmem_limit_bytes=VMEM_LIMIT_BYTES),
    )(cv, w_mod, b_mod.reshape(depth, 1, cols))


def _inproj_kernel(x_ref, mod_ref, n1w_ref, wq_ref, wk_ref, wz_ref, wvt_ref, wrc_ref, gw_ref, gb_ref,
                   qe_ref, ki_ref, vt_ref, ut_ref, dec_ref, rc_ref):
    tm = x_ref.shape[0]
    x = x_ref[...]
    sh1 = mod_ref[:, 0:D_MODEL]
    sc1 = mod_ref[:, D_MODEL:2 * D_MODEL]
    xn = x * lax.rsqrt(jnp.mean(x * x, axis=-1, keepdims=True) + EPS) * n1w_ref[...]
    hb = (xn * (1.0 + sc1) + sh1).astype(_BF16)

    q2 = _dot(hb, wq_ref[...]) * (GLA_DK ** -0.5)
    k2 = _dot(hb, wk_ref[...])
    z = _dot(hb, wz_ref[...])
    vtb = lax.dot_general(wvt_ref[...], hb, _NT, preferred_element_type=_F32).astype(_BF16)
    vt_ref[...] = vtb
    rc_ref[...] = _dot(hb, wrc_ref[...]).astype(_BF16)

    pre = _dot(z.astype(_BF16), gw_ref[...]) + gb_ref[...]
    g = (jnp.minimum(pre, 0.0) - jnp.log1p(jnp.exp(-jnp.abs(pre)))) * (1.0 / GATE_TAU)
    g_hi = g.astype(_BF16)
    g_lo = (g - g_hi.astype(_F32)).astype(_BF16)

    row = lax.broadcasted_iota(jnp.int32, (GLA_CHUNK, GLA_CHUNK), 0)
    col = lax.broadcasted_iota(jnp.int32, (GLA_CHUNK, GLA_CHUNK), 1)
    tri = (col <= row).astype(_BF16)
    lane = lax.broadcasted_iota(jnp.int32, (1, KEY_COLS), 1)
    is_fwd = (lane % SLAB) < GLA_DK
    mid = GLA_CHUNK // 2

    for c in range(tm // GLA_CHUNK):
        rows = slice(c * GLA_CHUNK, (c + 1) * GLA_CHUNK)
        prefix = _dot(tri, g_hi[rows]) + _dot(tri, g_lo[rows])
        total = prefix[GLA_CHUNK - 1:GLA_CHUNK, :]
        suffix = total - prefix + g[rows]
        e = jnp.where(is_fwd, prefix, suffix)
        mvec = jnp.where(is_fwd, e[mid - 1:mid, :], e[mid:mid + 1, :])
        bm = e - mvec
        qe_ref[rows, :] = (q2[rows] * jnp.exp(bm)).astype(_BF16)
        ki_ref[rows, :] = (k2[rows] * jnp.exp(-bm)).astype(_BF16)
        kd = (k2[rows] * jnp.exp(total - e)).astype(_BF16)
        dec_ref[c] = jnp.concatenate([total, mvec, jnp.zeros((6, KEY_COLS), _F32)], axis=0)
        for h in range(GLA_HEADS):
            vt_h = vtb[h * GLA_DV:(h + 1) * GLA_DV, rows]
            ut_ref[c, h] = _dot(vt_h, kd[:, h * SLAB:(h + 1) * SLAB])


def _inproj(x2d, mods3, mod_row_of_tile, n1w, wts, tm):
    tokens = x2d.shape[0]
    n_tiles = tokens // tm
    n_chunks = tokens // GLA_CHUNK
    cpt = tm // GLA_CHUNK
    const = lambda shape: pl.BlockSpec(shape, lambda j: (0,) * len(shape))
    return pl.pallas_call(
        _inproj_kernel,
        grid=(n_tiles,),
        in_specs=[
            pl.BlockSpec((tm, D_MODEL), lambda j: (j, 0)),
            pl.BlockSpec((None, 1, N_MOD * D_MODEL), lambda j: (mod_row_of_tile(j), 0, 0)),
            const((1, D_MODEL)),
            const((D_MODEL, KEY_COLS)),
            const((D_MODEL, KEY_COLS)),
            const((D_MODEL, 2 * GATE_RANK)),
            const((VAL_COLS, D_MODEL)),
            const((D_MODEL, RC_COLS)),
            const((2 * GATE_RANK, KEY_COLS)),
            const((1, KEY_COLS)),
        ],
        out_specs=[
            pl.BlockSpec((tm, KEY_COLS), lambda j: (j, 0)),
            pl.BlockSpec((tm, KEY_COLS), lambda j: (j, 0)),
            pl.BlockSpec((VAL_COLS, tm), lambda j: (0, j)),
            pl.BlockSpec((cpt, GLA_HEADS, GLA_DV, SLAB), lambda j: (j, 0, 0, 0)),
            pl.BlockSpec((cpt, 8, KEY_COLS), lambda j: (j, 0, 0)),
            pl.BlockSpec((tm, RC_COLS), lambda j: (j, 0)),
        ],
        out_shape=[
            jax.ShapeDtypeStruct((tokens, KEY_COLS), _BF16),
            jax.ShapeDtypeStruct((tokens, KEY_COLS), _BF16),
            jax.ShapeDtypeStruct((VAL_COLS, tokens), _BF16),
            jax.ShapeDtypeStruct((n_chunks, GLA_HEADS, GLA_DV, SLAB), _F32),
            jax.ShapeDtypeStruct((n_chunks, 8, KEY_COLS), _F32),
            jax.ShapeDtypeStruct((tokens, RC_COLS), _BF16),
        ],
        compiler_params=pltpu.CompilerParams(vmem_limit_bytes=VMEM_LIMIT_BYTES),
    )(x2d, mods3, n1w, wts["wq"], wts["wk"], wts["wz"], wts["wvt"], wts["wrc"], wts["gw"], wts["gb"])


def _scan_kernel(ut_ref, dec_ref, s0_ref, spt_ref, sfin_ref):
    nc = ut_ref.shape[0]
    lane = lax.broadcasted_iota(jnp.int32, (1, SLAB), 1)
    is_fwd = lane < GLA_DK

    def step(i, st):
        j = nc - 1 - i
        dec_i = dec_ref[i]
        dec_j = dec_ref[j]
        log_decay = jnp.where(is_fwd, dec_i[0:1, :], dec_j[0:1, :])
        log_mid = jnp.where(is_fwd, dec_i[1:2, :], dec_j[1:2, :])
        entering = (st * jnp.exp(log_mid)).astype(_BF16)
        spt_ref[i, :, 0:GLA_DK] = entering[:, 0:GLA_DK]
        spt_ref[j, :, GLA_DK:SLAB] = entering[:, GLA_DK:SLAB]
        inc = jnp.where(is_fwd, ut_ref[i], ut_ref[j])
        return st * jnp.exp(log_decay) + inc

    sfin_ref[...] = lax.fori_loop(0, nc, step, s0_ref[...])


def _scan(ut, dec, s0, chunks_per_seq):
    n_chunks = ut.shape[0]
    bsz = n_chunks // chunks_per_seq
    return pl.pallas_call(
        _scan_kernel,
        grid=(bsz, GLA_HEADS),
        in_specs=[
            pl.BlockSpec((chunks_per_seq, None, GLA_DV, SLAB), lambda b, h: (b, h, 0, 0)),
            pl.BlockSpec((chunks_per_seq, 8, SLAB), lambda b, h: (b, 0, h)),
            pl.BlockSpec((None, None, GLA_DV, SLAB), lambda b, h: (b, h, 0, 0)),
        ],
        out_specs=[
            pl.BlockSpec((chunks_per_seq, None, GLA_DV, SLAB), lambda b, h: (b, h, 0, 0)),
            pl.BlockSpec((None, None, GLA_DV, SLAB), lambda b, h: (b, h, 0, 0)),
        ],
        out_shape=[
            jax.ShapeDtypeStruct((n_chunks, GLA_HEADS, GLA_DV, SLAB), _BF16),
            jax.ShapeDtypeStruct((bsz, GLA_HEADS, GLA_DV, SLAB), _F32),
        ],
        compiler_params=pltpu.CompilerParams(vmem_limit_bytes=VMEM_LIMIT_BYTES),
    )(ut, dec, s0)


def _mix_kernel(x_ref, mod_ref, n2w_ref, qe_ref, ki_ref, vt_ref, spt_ref, rc_ref, gnw_ref, cw_ref,
                pw_ref, ps_ref, wo_ref, w1_ref, w2_ref, fnw_ref, o_ref, y_scr, *, seg, final):
    tm = x_ref.shape[0]
    g1 = mod_ref[:, 2 * D_MODEL:3 * D_MODEL]
    sh2 = mod_ref[:, 3 * D_MODEL:4 * D_MODEL]
    sc2 = mod_ref[:, 4 * D_MODEL:5 * D_MODEL]
    g2 = mod_ref[:, 5 * D_MODEL:6 * D_MODEL]

    row = lax.broadcasted_iota(jnp.int32, (GLA_CHUNK, GLA_CHUNK), 0)
    col = lax.broadcasted_iota(jnp.int32, (GLA_CHUNK, GLA_CHUNK), 1)
    lane = lax.broadcasted_iota(jnp.int32, (GLA_CHUNK, SLAB), 1)
    fwd_lane = lane < GLA_DK
    gnw = gnw_ref[...]
    for c in range(tm // GLA_CHUNK):
        rows = slice(c * GLA_CHUNK, (c + 1) * GLA_CHUNK)
        for h in range(GLA_HEADS):
            lanes = slice(h * SLAB, (h + 1) * SLAB)
            qe = qe_ref[rows, lanes]
            ki = ki_ref[rows, lanes]
            zero = jnp.zeros_like(ki)
            keys = jnp.concatenate([jnp.where(fwd_lane, ki, zero), jnp.where(fwd_lane, zero, ki)], axis=0)
            a2 = lax.dot_general(qe, keys, _NT, preferred_element_type=_F32)
            am = jnp.where(col <= row, a2[:, 0:GLA_CHUNK], 0.0) + jnp.where(col >= row, a2[:, GLA_CHUNK:], 0.0)
            lhs = jnp.concatenate([am.astype(_BF16), qe], axis=1)
            rhs = jnp.concatenate([vt_ref[h * GLA_DV:(h + 1) * GLA_DV, rows], spt_ref[c, h]], axis=1)
            o = lax.dot_general(lhs, rhs, _NT, preferred_element_type=_F32)
            o = o * lax.rsqrt(jnp.mean(o * o, axis=-1, keepdims=True) + EPS) * gnw
            r = rc_ref[rows, h * GLA_DV:(h + 1) * GLA_DV].astype(_F32)
            y_scr[rows, h * GLA_DV:(h + 1) * GLA_DV] = (o * (r * _sigmoid(r))).astype(_BF16)

    pos = lax.broadcasted_iota(jnp.int32, (tm, 1), 0) % seg
    c0 = VAL_COLS
    cb = rc_ref[:, c0:c0 + CONV_WIDTH].astype(_F32)
    cc = rc_ref[:, c0 + CONV_WIDTH:c0 + 2 * CONV_WIDTH].astype(_F32)
    ch = rc_ref[:, c0 + 2 * CONV_WIDTH:c0 + 3 * CONV_WIDTH].astype(_F32)
    zc = cc * ch
    z_prev = jnp.where(pos >= 1, pltpu.roll(zc, 1, 0), 0.0)
    z_next = jnp.where(pos <= seg - 2, pltpu.roll(zc, tm - 1, 0), 0.0)
    yc = cb * (cw_ref[0:1, :] * z_prev + cw_ref[1:2, :] * zc + cw_ref[2:3, :] * z_next)
    y_scr[:, VAL_COLS:VAL_COLS + CONV_WIDTH] = yc.astype(_BF16)

    p0 = VAL_COLS + 3 * CONV_WIDTH
    lane128 = lax.broadcasted_iota(jnp.int32, (1, 2 * POOL_GROUP), 1)
    pooled = []
    for blk in range(POOL_WIDTH // (2 * POOL_GROUP)):
        u = rc_ref[:, p0 + blk * 2 * POOL_GROUP:p0 + (blk + 1) * 2 * POOL_GROUP].astype(_F32)
        h_lo, h_hi = POOL_HALF_WINDOWS[2 * blk], POOL_HALF_WINDOWS[2 * blk + 1]
        half = jnp.where(lane128 < POOL_GROUP, h_lo, h_hi)
        acc = jnp.zeros_like(u)
        for d in range(-h_hi, h_hi):
            shifted = u if d == 0 else pltpu.roll(u, (-d) % tm, 0)
            valid = (pos + d >= 0) & (pos + d < seg) & (d >= -half) & (d < half)
            acc = acc + jnp.where(valid, shifted, 0.0)
        cnt = (jnp.minimum(pos + half, seg) - jnp.maximum(pos - half, 0)).astype(_F32)
        pooled.append(acc / cnt - u)
    pool_in = jnp.concatenate(pooled, axis=1).astype(_BF16)
    yp = _dot(pool_in, pw_ref[...]) * ps_ref[...]
    y_scr[:, VAL_COLS + CONV_WIDTH:] = yp.astype(_BF16)

    x1 = x_ref[...] + g1 * _dot(y_scr[...], wo_ref[...])
    xn = x1 * lax.rsqrt(jnp.mean(x1 * x1, axis=-1, keepdims=True) + EPS) * n2w_ref[...]
    h2 = (xn * (1.0 + sc2) + sh2).astype(_BF16)
    acc = jnp.zeros((tm, D_MODEL), _F32)
    for f in range(D_FF // FF_CHUNK):
        cols = slice(f * FF_CHUNK, (f + 1) * FF_CHUNK)
        a = jnp.maximum(_dot(h2, w1_ref[:, cols]), 0.0)
        acc = acc + _dot((a * a).astype(_BF16), w2_ref[cols, :])
    out = x1 + g2 * acc
    if final:
        out = out * lax.rsqrt(jnp.mean(out * out, axis=-1, keepdims=True) + EPS) * fnw_ref[...]
    o_ref[...] = out


def _mix(x2d, mods3, mod_row_of_tile, n2w, qe, ki, vt, spt, rc, wts, fnw, tm, seg, final):
    tokens = x2d.shape[0]
    cpt = tm // GLA_CHUNK
    resident = lambda shape: pl.BlockSpec(shape, lambda j: (0,) * len(shape), pipeline_mode=pl.Buffered(1))
    return pl.pallas_call(
        functools.partial(_mix_kernel, seg=seg, final=final),
        grid=(tokens // tm,),
        in_specs=[
            pl.BlockSpec((tm, D_MODEL), lambda j: (j, 0)),
            pl.BlockSpec((None, 1, N_MOD * D_MODEL), lambda j: (mod_row_of_tile(j), 0, 0)),
            resident((1, D_MODEL)),
            pl.BlockSpec((tm, KEY_COLS), lambda j: (j, 0)),
            pl.BlockSpec((tm, KEY_COLS), lambda j: (j, 0)),
            pl.BlockSpec((VAL_COLS, tm), lambda j: (0, j)),
            pl.BlockSpec((cpt, GLA_HEADS, GLA_DV, SLAB), lambda j: (j, 0, 0, 0)),
            pl.BlockSpec((tm, RC_COLS), lambda j: (j, 0)),
            resident((1, GLA_DV)),
            resident((3, CONV_WIDTH)),
            resident((POOL_WIDTH, POOL_WIDTH)),
            resident((1, POOL_WIDTH)),
            resident((D_MODEL, D_MODEL)),
            resident((D_MODEL, D_FF)),
            resident((D_FF, D_MODEL)),
            resident((1, D_MODEL)),
        ],
        out_specs=pl.BlockSpec((tm, D_MODEL), lambda j: (j, 0)),
        out_shape=jax.ShapeDtypeStruct((tokens, D_MODEL), _F32),
        scratch_shapes=[pltpu.VMEM((tm, D_MODEL), _BF16)],
        compiler_params=pltpu.CompilerParams(vmem_limit_bytes=VMEM_LIMIT_BYTES),
    )(x2d, mods3, n2w, qe, ki, vt, spt, rc, wts["gnw"], wts["cw"], wts["pw"], wts["ps"],
      wts["wo"], wts["w1"], wts["w2"], fnw)


def _layer_weights(w_in, gate_w, gate_b, gla_norm_w, conv_w, pool_w, pool_scale, w_out, w_mlp1, w_mlp2):
    key_w = GLA_HEADS * GLA_DK
    q_off, k_off, v_off = 0, key_w, 2 * key_w
    g_off = v_off + VAL_COLS
    r_off = g_off + 2 * GATE_RANK

    def per_direction(w):
        w = w.reshape(D_MODEL, GLA_HEADS, 1, GLA_DK)
        return jnp.broadcast_to(w, (D_MODEL, GLA_HEADS, 2, GLA_DK)).reshape(D_MODEL, KEY_COLS)

    zeros = jnp.zeros((GATE_RANK, GLA_HEADS, GLA_DK), _F32)
    gf = gate_w[0].reshape(GATE_RANK, GLA_HEADS, GLA_DK)
    gb = gate_w[1].reshape(GATE_RANK, GLA_HEADS, GLA_DK)
    gw = jnp.concatenate([jnp.concatenate([gf, zeros], -1), jnp.concatenate([zeros, gb], -1)], 0)
    gbias = jnp.concatenate([gate_b[0].reshape(GLA_HEADS, GLA_DK), gate_b[1].reshape(GLA_HEADS, GLA_DK)], -1)
    pw = jnp.zeros((POOL_WIDTH, POOL_WIDTH), _F32)
    for gi in range(POOL_WIDTH // POOL_GROUP):
        sl = slice(gi * POOL_GROUP, (gi + 1) * POOL_GROUP)
        pw = pw.at[sl, sl].set(pool_w[gi])
    return {
        "wq": per_direction(w_in[:, q_off:k_off]).astype(_BF16),
        "wk": per_direction(w_in[:, k_off:v_off]).astype(_BF16),
        "wz": w_in[:, g_off:r_off].astype(_BF16),
        "wvt": w_in[:, v_off:g_off].T.astype(_BF16),
        "wrc": w_in[:, r_off:].astype(_BF16),
        "gw": gw.reshape(2 * GATE_RANK, KEY_COLS).astype(_BF16),
        "gb": gbias.reshape(1, KEY_COLS),
        "gnw": gla_norm_w.reshape(1, GLA_DV),
        "cw": conv_w,
        "pw": pw.astype(_BF16),
        "ps": pool_scale.reshape(1, POOL_WIDTH),
        "wo": w_out.astype(_BF16),
        "w1": w_mlp1.astype(_BF16),
        "w2": w_mlp2.astype(_BF16),
    }


def kernel(x, c, ctx, c_ctx, w_mod, b_mod, norm1_w, norm2_w, w_in, gla_gate_w, gla_gate_b, gla_norm_w, conv_w, pool_w, pool_scale, w_out, w_mlp1, w_mlp2, final_norm_w):
    bsz, n, _ = x.shape
    ctx_len = ctx.shape[1]
    depth = w_in.shape[0]
    assert bsz + 1 <= MOD_ROWS and n % 512 == 0 and ctx_len % GLA_CHUNK == 0
    lat_tm, ctx_tm = 512, ctx_len
    ctx_row = bsz

    cv = jnp.zeros((MOD_ROWS, D_MODEL), _F32).at[:bsz].set(c).at[ctx_row].set(c_ctx)
    mods = _adaln(cv, w_mod, b_mod)

    xl = x.reshape(bsz * n, D_MODEL)
    xc = ctx.reshape(bsz * ctx_len, D_MODEL)
    lat_row = lambda j: j // (n // lat_tm)
    ctx_mod_row = lambda j: ctx_row
    fnw = final_norm_w.reshape(1, D_MODEL)
    s_zero = jnp.zeros((bsz, GLA_HEADS, GLA_DV, SLAB), _F32)

    for l in range(depth):
        last = l == depth - 1
        wts = _layer_weights(w_in[l], gla_gate_w[l], gla_gate_b[l], gla_norm_w[l], conv_w[l], pool_w[l],
                             pool_scale[l], w_out[l], w_mlp1[l], w_mlp2[l])
        mods3 = mods[l].reshape(MOD_ROWS, 1, N_MOD * D_MODEL)
        n1w = norm1_w[l].reshape(1, D_MODEL)
        n2w = norm2_w[l].reshape(1, D_MODEL)

        cqe, cki, cvt, cut, cdec, crc = _inproj(xc, mods3, ctx_mod_row, n1w, wts, ctx_tm)
        cspt, s_ctx = _scan(cut, cdec, s_zero, ctx_len // GLA_CHUNK)
        if not last:
            xc = _mix(xc, mods3, ctx_mod_row, n2w, cqe, cki, cvt, cspt, crc, wts, fnw, ctx_tm, ctx_len, False)

        qe, ki, vt, ut, dec, rc = _inproj(xl, mods3, lat_row, n1w, wts, lat_tm)
        spt, _ = _scan(ut, dec, s_ctx, n // GLA_CHUNK)
        xl = _mix(xl, mods3, lat_row, n2w, qe, ki, vt, spt, rc, wts, fnw, lat_tm, GRID_W, last)
    return xl.reshape(bsz, n, D_MODEL)
```

```python
import functools

import jax
import jax.numpy as jnp
from jax import lax
from jax.experimental import pallas as pl
from jax.experimental.pallas import tpu as pltpu

D_MODEL = 1024
GLA_HEADS = 4
GLA_DK = 64
GLA_DV = 128
SLAB = 2 * GLA_DK
KEY_COLS = GLA_HEADS * SLAB
VAL_COLS = GLA_HEADS * GLA_DV
GATE_RANK = 16
GATE_TAU = 16.0
GLA_CHUNK = 128
SUB_ROWS = 256
CONV_WIDTH = 256
POOL_WIDTH = 256
POOL_GROUP = 64
POOL_HALF_WINDOWS = (1, 2, 4, 8)
D_FF = 4096
FF_CHUNK = 1024
N_MOD = 6
EPS = 1e-6
GRID_W = 64
RC_COLS = VAL_COLS + 3 * CONV_WIDTH + POOL_WIDTH
MOD_ROWS = 8
ADALN_COLS = 1536
VMEM_LIMIT_BYTES = 56 * 1024 * 1024

_NT = (((1,), (1,)), ((), ()))
_BF16 = jnp.bfloat16
_F32 = jnp.float32


def _dot(a, b):
    return jnp.dot(a, b, preferred_element_type=_F32)


def _sigmoid(x):
    return 1.0 / (1.0 + jnp.exp(-x))


def _adaln_kernel(cv_ref, w_ref, b_ref, o_ref):
    cv = cv_ref[...]
    s = (cv * _sigmoid(cv)).astype(_BF16)
    o_ref[...] = _dot(s, w_ref[...].astype(_BF16)) + b_ref[...]


def _adaln(cv, w_mod, b_mod):
    depth = w_mod.shape[0]
    cols = w_mod.shape[2]
    return pl.pallas_call(
        _adaln_kernel,
        grid=(depth, cols // ADALN_COLS),
        in_specs=[
            pl.BlockSpec((MOD_ROWS, D_MODEL), lambda l, j: (0, 0)),
            pl.BlockSpec((None, D_MODEL, ADALN_COLS), lambda l, j: (l, 0, j)),
            pl.BlockSpec((None, 1, ADALN_COLS), lambda l, j: (l, 0, j)),
        ],
        out_specs=pl.BlockSpec((None, MOD_ROWS, ADALN_COLS), lambda l, j: (l, 0, j)),
        out_shape=jax.ShapeDtypeStruct((depth, MOD_ROWS, cols), _F32),
        compiler_params=pltpu.CompilerParams(vmem_limit_bytes=VMEM_LIMIT_BYTES),
    )(cv, w_mod, b_mod.reshape(depth, 1, cols))


def _per_direction(t):
    lane = lax.broadcasted_iota(jnp.int32, (1, SLAB), 1)
    first = lane < GLA_DK
    slabs = []
    for p in range(GLA_HEADS // 2):
        pair = t[:, p * SLAB:(p + 1) * SLAB]
        swapped = pltpu.roll(pair, GLA_DK, 1)
        slabs += [jnp.where(first, pair, swapped), jnp.where(first, swapped, pair)]
    return jnp.concatenate(slabs, axis=1)


def _inproj_kernel(x_ref, mod_ref, n1w_ref, wq_ref, wk_ref, wz_ref, wvt_ref, wrc_ref, gw_ref, gb_ref,
                   qe_ref, ki_ref, vt_ref, ut_ref, dec_ref, rc_ref):
    tm = x_ref.shape[0]
    sh1 = mod_ref[:, 0:D_MODEL]
    sc1 = mod_ref[:, D_MODEL:2 * D_MODEL]
    row = lax.broadcasted_iota(jnp.int32, (GLA_CHUNK, GLA_CHUNK), 0)
    col = lax.broadcasted_iota(jnp.int32, (GLA_CHUNK, GLA_CHUNK), 1)
    tri = (col <= row).astype(_BF16)
    lane = lax.broadcasted_iota(jnp.int32, (1, KEY_COLS), 1)
    is_fwd = (lane % SLAB) < GLA_DK
    mid = GLA_CHUNK // 2

    n_sub = tm // SUB_ROWS
    cps = SUB_ROWS // GLA_CHUNK
    sub_rows = [slice(s * SUB_ROWS, (s + 1) * SUB_ROWS) for s in range(n_sub)]
    hbs, gates, q2s, k2s = [], [], [], []
    for s in range(n_sub):
        x = x_ref[sub_rows[s], :]
        xn = x * lax.rsqrt(jnp.mean(x * x, axis=-1, keepdims=True) + EPS) * n1w_ref[...]
        hbs.append((xn * (1.0 + sc1) + sh1).astype(_BF16))
    for s in range(n_sub):
        z = _dot(hbs[s], wz_ref[...])
        pre = _dot(z.astype(_BF16), gw_ref[...]) + gb_ref[...]
        g = (jnp.minimum(pre, 0.0) - jnp.log(1.0 + jnp.exp(-jnp.abs(pre)))) * (1.0 / GATE_TAU)
        g_hi = g.astype(_BF16)
        gates.append((g, g_hi, (g - g_hi.astype(_F32)).astype(_BF16)))
        q2s.append(_per_direction(_dot(hbs[s], wq_ref[...]) * (GLA_DK ** -0.5)))
        k2s.append(_per_direction(_dot(hbs[s], wk_ref[...])))

    prefixes = []
    for s in range(n_sub):
        g, g_hi, g_lo = gates[s]
        for cl in range(cps):
            lrows = slice(cl * GLA_CHUNK, (cl + 1) * GLA_CHUNK)
            prefixes.append(_dot(tri, g_hi[lrows]) + _dot(tri, g_lo[lrows]))

    vtbs = []
    for s in range(n_sub):
        vtb = lax.dot_general(wvt_ref[...], hbs[s], _NT, preferred_element_type=_F32).astype(_BF16)
        vt_ref[:, sub_rows[s]] = vtb
        vtbs.append(vtb)
        rc_ref[sub_rows[s], :] = _dot(hbs[s], wrc_ref[...]).astype(_BF16)

    for s in range(n_sub):
        g = gates[s][0]
        for cl in range(cps):
            c = s * cps + cl
            lrows = slice(cl * GLA_CHUNK, (cl + 1) * GLA_CHUNK)
            rows = slice(c * GLA_CHUNK, (c + 1) * GLA_CHUNK)
            prefix = prefixes[c]
            total = prefix[GLA_CHUNK - 1:GLA_CHUNK, :]
            suffix = total - prefix + g[lrows]
            e = jnp.where(is_fwd, prefix, suffix)
            mvec = jnp.where(is_fwd, e[mid - 1:mid, :], e[mid:mid + 1, :])
            bm = e - mvec
            qe_ref[rows, :] = (q2s[s][lrows] * jnp.exp(bm)).astype(_BF16)
            ki_ref[rows, :] = (k2s[s][lrows] * jnp.exp(-bm)).astype(_BF16)
            kd = (k2s[s][lrows] * jnp.exp(total - e)).astype(_BF16)
            dec_ref[c] = jnp.concatenate([total, mvec, jnp.zeros((6, KEY_COLS), _F32)], axis=0)
            for h in range(GLA_HEADS):
                vt_h = vtbs[s][h * GLA_DV:(h + 1) * GLA_DV, lrows]
                ut_ref[c, h] = _dot(vt_h, kd[:, h * SLAB:(h + 1) * SLAB])


def _inproj(x2d, mods3, mod_row_of_tile, n1w, wts, tm):
    tokens = x2d.shape[0]
    n_tiles = tokens // tm
    n_chunks = tokens // GLA_CHUNK
    cpt = tm // GLA_CHUNK
    const = lambda shape: pl.BlockSpec(shape, lambda j: (0,) * len(shape))
    return pl.pallas_call(
        _inproj_kernel,
        grid=(n_tiles,),
        in_specs=[
            pl.BlockSpec((tm, D_MODEL), lambda j: (j, 0)),
            pl.BlockSpec((None, 1, N_MOD * D_MODEL), lambda j: (mod_row_of_tile(j), 0, 0)),
            const((1, D_MODEL)),
            const((D_MODEL, GLA_HEADS * GLA_DK)),
            const((D_MODEL, GLA_HEADS * GLA_DK)),
            const((D_MODEL, 2 * GATE_RANK)),
            const((VAL_COLS, D_MODEL)),
            const((D_MODEL, RC_COLS)),
            const((2 * GATE_RANK, KEY_COLS)),
            const((1, KEY_COLS)),
        ],
        out_specs=[
            pl.BlockSpec((tm, KEY_COLS), lambda j: (j, 0)),
            pl.BlockSpec((tm, KEY_COLS), lambda j: (j, 0)),
            pl.BlockSpec((VAL_COLS, tm), lambda j: (0, j)),
            pl.BlockSpec((cpt, GLA_HEADS, GLA_DV, SLAB), lambda j: (j, 0, 0, 0)),
            pl.BlockSpec((cpt, 8, KEY_COLS), lambda j: (j, 0, 0)),
            pl.BlockSpec((tm, RC_COLS), lambda j: (j, 0)),
        ],
        out_shape=[
            jax.ShapeDtypeStruct((tokens, KEY_COLS), _BF16),
            jax.ShapeDtypeStruct((tokens, KEY_COLS), _BF16),
            jax.ShapeDtypeStruct((VAL_COLS, tokens), _BF16),
            jax.ShapeDtypeStruct((n_chunks, GLA_HEADS, GLA_DV, SLAB), _F32),
            jax.ShapeDtypeStruct((n_chunks, 8, KEY_COLS), _F32),
            jax.ShapeDtypeStruct((tokens, RC_COLS), _BF16),
        ],
        compiler_params=pltpu.CompilerParams(vmem_limit_bytes=VMEM_LIMIT_BYTES),
    )(x2d, mods3, n1w, wts["wq"], wts["wk"], wts["wz"], wts["wvt"], wts["wrc"], wts["gw"], wts["gb"])


def _scan_kernel(ut_ref, dec_ref, s0_ref, spt_ref, sfin_ref):
    nc = ut_ref.shape[0]
    lane = lax.broadcasted_iota(jnp.int32, (1, SLAB), 1)
    is_fwd = lane < GLA_DK

    def step(i, st):
        j = nc - 1 - i
        dec_i = dec_ref[i]
        dec_j = dec_ref[j]
        log_decay = jnp.where(is_fwd, dec_i[0:1, :], dec_j[0:1, :])
        log_mid = jnp.where(is_fwd, dec_i[1:2, :], dec_j[1:2, :])
        entering = (st * jnp.exp(log_mid)).astype(_BF16)
        spt_ref[i, :, 0:GLA_DK] = entering[:, 0:GLA_DK]
        spt_ref[j, :, GLA_DK:SLAB] = entering[:, GLA_DK:SLAB]
        inc = jnp.where(is_fwd, ut_ref[i], ut_ref[j])
        return st * jnp.exp(log_decay) + inc

    sfin_ref[...] = lax.fori_loop(0, nc, step, s0_ref[...])


def _scan(ut, dec, s0, chunks_per_seq):
    n_chunks = ut.shape[0]
    bsz = n_chunks // chunks_per_seq
    return pl.pallas_call(
        _scan_kernel,
        grid=(bsz, GLA_HEADS),
        in_specs=[
            pl.BlockSpec((chunks_per_seq, None, GLA_DV, SLAB), lambda b, h: (b, h, 0, 0)),
            pl.BlockSpec((chunks_per_seq, 8, SLAB), lambda b, h: (b, 0, h)),
            pl.BlockSpec((None, None, GLA_DV, SLAB), lambda b, h: (b, h, 0, 0)),
        ],
        out_specs=[
            pl.BlockSpec((chunks_per_seq, None, GLA_DV, SLAB), lambda b, h: (b, h, 0, 0)),
            pl.BlockSpec((None, None, GLA_DV, SLAB), lambda b, h: (b, h, 0, 0)),
        ],
        out_shape=[
            jax.ShapeDtypeStruct((n_chunks, GLA_HEADS, GLA_DV, SLAB), _BF16),
            jax.ShapeDtypeStruct((bsz, GLA_HEADS, GLA_DV, SLAB), _F32),
        ],
        compiler_params=pltpu.CompilerParams(vmem_limit_bytes=VMEM_LIMIT_BYTES),
    )(ut, dec, s0)


def _mix_kernel(x_ref, mod_ref, n2w_ref, qe_ref, ki_ref, vt_ref, spt_ref, rc_ref, gnw_ref, cw_ref,
                pw_ref, ps_ref, wo_ref, w1_ref, w2_ref, fnw_ref, o_ref, y_scr, *, seg, final):
    tm = x_ref.shape[0]
    g1 = mod_ref[:, 2 * D_MODEL:3 * D_MODEL]
    sh2 = mod_ref[:, 3 * D_MODEL:4 * D_MODEL]
    sc2 = mod_ref[:, 4 * D_MODEL:5 * D_MODEL]
    g2 = mod_ref[:, 5 * D_MODEL:6 * D_MODEL]

    row = lax.broadcasted_iota(jnp.int32, (GLA_CHUNK, GLA_CHUNK), 0)
    col = lax.broadcasted_iota(jnp.int32, (GLA_CHUNK, GLA_CHUNK), 1)
    lane = lax.broadcasted_iota(jnp.int32, (GLA_CHUNK, SLAB), 1)
    fwd_lane = lane < GLA_DK
    gnw = gnw_ref[...]
    for c in range(tm // GLA_CHUNK):
        rows = slice(c * GLA_CHUNK, (c + 1) * GLA_CHUNK)
        for h in range(GLA_HEADS):
            lanes = slice(h * SLAB, (h + 1) * SLAB)
            qe = qe_ref[rows, lanes]
            ki = ki_ref[rows, lanes]
            zero = jnp.zeros_like(ki)
            keys = jnp.concatenate([jnp.where(fwd_lane, ki, zero), jnp.where(fwd_lane, zero, ki)], axis=0)
            a2 = lax.dot_general(qe, keys, _NT, preferred_element_type=_F32)
            am = jnp.where(col <= row, a2[:, 0:GLA_CHUNK], 0.0) + jnp.where(col >= row, a2[:, GLA_CHUNK:], 0.0)
            lhs = jnp.concatenate([am.astype(_BF16), qe], axis=1)
            rhs = jnp.concatenate([vt_ref[h * GLA_DV:(h + 1) * GLA_DV, rows], spt_ref[c, h]], axis=1)
            o = lax.dot_general(lhs, rhs, _NT, preferred_element_type=_F32)
            o = o * lax.rsqrt(jnp.mean(o * o, axis=-1, keepdims=True) + EPS) * gnw
            r = rc_ref[rows, h * GLA_DV:(h + 1) * GLA_DV].astype(_F32)
            y_scr[rows, h * GLA_DV:(h + 1) * GLA_DV] = (o * (r * _sigmoid(r))).astype(_BF16)

    pos = lax.broadcasted_iota(jnp.int32, (tm, 1), 0) % seg
    c0 = VAL_COLS
    cb = rc_ref[:, c0:c0 + CONV_WIDTH].astype(_F32)
    cc = rc_ref[:, c0 + CONV_WIDTH:c0 + 2 * CONV_WIDTH].astype(_F32)
    ch = rc_ref[:, c0 + 2 * CONV_WIDTH:c0 + 3 * CONV_WIDTH].astype(_F32)
    zc = cc * ch
    z_prev = jnp.where(pos >= 1, pltpu.roll(zc, 1, 0), 0.0)
    z_next = jnp.where(pos <= seg - 2, pltpu.roll(zc, tm - 1, 0), 0.0)
    yc = cb * (cw_ref[0:1, :] * z_prev + cw_ref[1:2, :] * zc + cw_ref[2:3, :] * z_next)
    y_scr[:, VAL_COLS:VAL_COLS + CONV_WIDTH] = yc.astype(_BF16)

    p0 = VAL_COLS + 3 * CONV_WIDTH
    lane128 = lax.broadcasted_iota(jnp.int32, (1, 2 * POOL_GROUP), 1)
    pooled = []
    for blk in range(POOL_WIDTH // (2 * POOL_GROUP)):
        u = rc_ref[:, p0 + blk * 2 * POOL_GROUP:p0 + (blk + 1) * 2 * POOL_GROUP].astype(_F32)
        h_lo, h_hi = POOL_HALF_WINDOWS[2 * blk], POOL_HALF_WINDOWS[2 * blk + 1]
        half = jnp.where(lane128 < POOL_GROUP, h_lo, h_hi)
        acc = jnp.zeros_like(u)
        for d in range(-h_hi, h_hi):
            shifted = u if d == 0 else pltpu.roll(u, (-d) % tm, 0)
            valid = (pos + d >= 0) & (pos + d < seg) & (d >= -half) & (d < half)
            acc = acc + jnp.where(valid, shifted, 0.0)
        cnt = (jnp.minimum(pos + half, seg) - jnp.maximum(pos - half, 0)).astype(_F32)
        pooled.append(acc / cnt - u)
    pool_in = jnp.concatenate(pooled, axis=1).astype(_BF16)
    yp = _dot(pool_in, pw_ref[...]) * ps_ref[...]
    y_scr[:, VAL_COLS + CONV_WIDTH:] = yp.astype(_BF16)

    x1 = x_ref[...] + g1 * _dot(y_scr[...], wo_ref[...])
    xn = x1 * lax.rsqrt(jnp.mean(x1 * x1, axis=-1, keepdims=True) + EPS) * n2w_ref[...]
    h2 = (xn * (1.0 + sc2) + sh2).astype(_BF16)
    acc = jnp.zeros((tm, D_MODEL), _F32)
    for f in range(D_FF // FF_CHUNK):
        cols = slice(f * FF_CHUNK, (f + 1) * FF_CHUNK)
        a = jnp.maximum(_dot(h2, w1_ref[:, cols]), 0.0)
        acc = acc + _dot((a * a).astype(_BF16), w2_ref[cols, :])
    out = x1 + g2 * acc
    if final:
        out = out * lax.rsqrt(jnp.mean(out * out, axis=-1, keepdims=True) + EPS) * fnw_ref[...]
    o_ref[...] = out


def _mix(x2d, mods3, mod_row_of_tile, n2w, qe, ki, vt, spt, rc, wts, fnw, tm, seg, final):
    tokens = x2d.shape[0]
    cpt = tm // GLA_CHUNK
    resident = lambda shape: pl.BlockSpec(shape, lambda j: (0,) * len(shape), pipeline_mode=pl.Buffered(1))
    return pl.pallas_call(
        functools.partial(_mix_kernel, seg=seg, final=final),
        grid=(tokens // tm,),
        in_specs=[
            pl.BlockSpec((tm, D_MODEL), lambda j: (j, 0)),
            pl.BlockSpec((None, 1, N_MOD * D_MODEL), lambda j: (mod_row_of_tile(j), 0, 0)),
            resident((1, D_MODEL)),
            pl.BlockSpec((tm, KEY_COLS), lambda j: (j, 0)),
            pl.BlockSpec((tm, KEY_COLS), lambda j: (j, 0)),
            pl.BlockSpec((VAL_COLS, tm), lambda j: (0, j)),
            pl.BlockSpec((cpt, GLA_HEADS, GLA_DV, SLAB), lambda j: (j, 0, 0, 0)),
            pl.BlockSpec((tm, RC_COLS), lambda j: (j, 0)),
            resident((1, GLA_DV)),
            resident((3, CONV_WIDTH)),
            resident((POOL_WIDTH, POOL_WIDTH)),
            resident((1, POOL_WIDTH)),
            resident((D_MODEL, D_MODEL)),
            resident((D_MODEL, D_FF)),
            resident((D_FF, D_MODEL)),
            resident((1, D_MODEL)),
        ],
        out_specs=pl.BlockSpec((tm, D_MODEL), lambda j: (j, 0)),
        out_shape=jax.ShapeDtypeStruct((tokens, D_MODEL), _F32),
        scratch_shapes=[pltpu.VMEM((tm, D_MODEL), _BF16)],
        compiler_params=pltpu.CompilerParams(vmem_limit_bytes=VMEM_LIMIT_BYTES),
    )(x2d, mods3, n2w, qe, ki, vt, spt, rc, wts["gnw"], wts["cw"], wts["pw"], wts["ps"],
      wts["wo"], wts["w1"], wts["w2"], fnw)


def _layer_weights(w_in, gate_w, gate_b, gla_norm_w, conv_w, pool_w, pool_scale, w_out, w_mlp1, w_mlp2):
    key_w = GLA_HEADS * GLA_DK
    q_off, k_off, v_off = 0, key_w, 2 * key_w
    g_off = v_off + VAL_COLS
    r_off = g_off + 2 * GATE_RANK

    zeros = jnp.zeros((GATE_RANK, GLA_HEADS, GLA_DK), _F32)
    gf = gate_w[0].reshape(GATE_RANK, GLA_HEADS, GLA_DK)
    gb = gate_w[1].reshape(GATE_RANK, GLA_HEADS, GLA_DK)
    gw = jnp.concatenate([jnp.concatenate([gf, zeros], -1), jnp.concatenate([zeros, gb], -1)], 0)
    gbias = jnp.concatenate([gate_b[0].reshape(GLA_HEADS, GLA_DK), gate_b[1].reshape(GLA_HEADS, GLA_DK)], -1)
    pw = jnp.zeros((POOL_WIDTH, POOL_WIDTH), _F32)
    for gi in range(POOL_WIDTH // POOL_GROUP):
        sl = slice(gi * POOL_GROUP, (gi + 1) * POOL_GROUP)
        pw = pw.at[sl, sl].set(pool_w[gi])
    return {
        "wq": w_in[:, q_off:k_off].astype(_BF16),
        "wk": w_in[:, k_off:v_off].astype(_BF16),
        "wz": w_in[:, g_off:r_off].astype(_BF16),
        "wvt": w_in[:, v_off:g_off].T.astype(_BF16),
        "wrc": w_in[:, r_off:].astype(_BF16),
        "gw": gw.reshape(2 * GATE_RANK, KEY_COLS).astype(_BF16),
        "gb": gbias.reshape(1, KEY_COLS),
        "gnw": gla_norm_w.reshape(1, GLA_DV),
        "cw": conv_w,
        "pw": pw.astype(_BF16),
        "ps": pool_scale.reshape(1, POOL_WIDTH),
        "wo": w_out.astype(_BF16),
        "w1": w_mlp1.astype(_BF16),
        "w2": w_mlp2.astype(_BF16),
    }


def kernel(x, c, ctx, c_ctx, w_mod, b_mod, norm1_w, norm2_w, w_in, gla_gate_w, gla_gate_b, gla_norm_w, conv_w, pool_w, pool_scale, w_out, w_mlp1, w_mlp2, final_norm_w):
    bsz, n, _ = x.shape
    ctx_len = ctx.shape[1]
    depth = w_in.shape[0]
    assert bsz + 1 <= MOD_ROWS and n % 512 == 0 and ctx_len % GLA_CHUNK == 0
    lat_tm, ctx_tm = 512, ctx_len
    ctx_row = bsz

    cv = jnp.zeros((MOD_ROWS, D_MODEL), _F32).at[:bsz].set(c).at[ctx_row].set(c_ctx)
    mods = _adaln(cv, w_mod, b_mod)

    xl = x.reshape(bsz * n, D_MODEL)
    xc = ctx.reshape(bsz * ctx_len, D_MODEL)
    lat_row = lambda j: j // (n // lat_tm)
    ctx_mod_row = lambda j: ctx_row
    fnw = final_norm_w.reshape(1, D_MODEL)
    s_zero = jnp.zeros((bsz, GLA_HEADS, GLA_DV, SLAB), _F32)

    for l in range(depth):
        last = l == depth - 1
        wts = _layer_weights(w_in[l], gla_gate_w[l], gla_gate_b[l], gla_norm_w[l], conv_w[l], pool_w[l],
                             pool_scale[l], w_out[l], w_mlp1[l], w_mlp2[l])
        mods3 = mods[l].reshape(MOD_ROWS, 1, N_MOD * D_MODEL)
        n1w = norm1_w[l].reshape(1, D_MODEL)
        n2w = norm2_w[l].reshape(1, D_MODEL)

        cqe, cki, cvt, cut, cdec, crc = _inproj(xc, mods3, ctx_mod_row, n1w, wts, ctx_tm)
        cspt, s_ctx = _scan(cut, cdec, s_zero, ctx_len // GLA_CHUNK)
        if not last:
            xc = _mix(xc, mods3, ctx_mod_row, n2w, cqe, cki, cvt, cspt, crc, wts, fnw, ctx_tm, ctx_len, False)

        qe, ki, vt, ut, dec, rc = _inproj(xl, mods3, lambda j: j // (n // 1024), n1w, wts, 1024)
        spt, _ = _scan(ut, dec, s_ctx, n // GLA_CHUNK)
        xl = _mix(xl, mods3, lat_row, n2w, qe, ki, vt, spt, rc, wts, fnw, lat_tm, GRID_W, last)
    return xl.reshape(bsz, n, D_MODEL)
```

```python
import functools

import jax
import jax.numpy as jnp
from jax import lax
from jax.experimental import pallas as pl
from jax.experimental.pallas import tpu as pltpu

D_MODEL = 1024
GLA_HEADS = 4
GLA_DK = 64
GLA_DV = 128
SLAB = 2 * GLA_DK
KEY_COLS = GLA_HEADS * SLAB
VAL_COLS = GLA_HEADS * GLA_DV
GATE_RANK = 16
GATE_TAU = 16.0
GLA_CHUNK = 128
SUB_ROWS = 256
LATENT_IN_TILE = 1024
LATENT_MIX_TILE = 512
MIX_SUB_ROWS = 512
CONV_WIDTH = 256
POOL_WIDTH = 256
POOL_GROUP = 64
POOL_HALF_WINDOWS = (1, 2, 4, 8)
D_FF = 4096
FF_CHUNK = 1024
N_MOD = 6
EPS = 1e-6
GRID_W = 64
RC_COLS = VAL_COLS + 3 * CONV_WIDTH + POOL_WIDTH
WQ_COLS = slice(0, GLA_HEADS * GLA_DK)
WK_COLS = slice(WQ_COLS.stop, WQ_COLS.stop + GLA_HEADS * GLA_DK)
WRC_COLS = slice(WK_COLS.stop, WK_COLS.stop + RC_COLS)
WZ_COLS = slice(WRC_COLS.stop, WRC_COLS.stop + 2 * GATE_RANK)
WIN_COLS = WZ_COLS.stop
SCAN_UNROLL = 2
MOD_ROWS = 8
ADALN_COLS = 1536
VMEM_LIMIT_BYTES = 56 * 1024 * 1024

_NT = (((1,), (1,)), ((), ()))
_BF16 = jnp.bfloat16
_F32 = jnp.float32


def _dot(a, b):
    return jnp.dot(a, b, preferred_element_type=_F32)


def _sigmoid(x):
    return 1.0 / (1.0 + jnp.exp(-x))


def _adaln_kernel(cv_ref, w_ref, b_ref, o_ref):
    cv = cv_ref[...]
    s = (cv * _sigmoid(cv)).astype(_BF16)
    o_ref[...] = _dot(s, w_ref[...].astype(_BF16)) + b_ref[...]


def _adaln(cv, w_mod, b_mod):
    depth = w_mod.shape[0]
    cols = w_mod.shape[2]
    return pl.pallas_call(
        _adaln_kernel,
        grid=(depth, cols // ADALN_COLS),
        in_specs=[
            pl.BlockSpec((MOD_ROWS, D_MODEL), lambda l, j: (0, 0)),
            pl.BlockSpec((None, D_MODEL, ADALN_COLS), lambda l, j: (l, 0, j)),
            pl.BlockSpec((None, 1, ADALN_COLS), lambda l, j: (l, 0, j)),
        ],
        out_specs=pl.BlockSpec((None, MOD_ROWS, ADALN_COLS), lambda l, j: (l, 0, j)),
        out_shape=jax.ShapeDtypeStruct((depth, MOD_ROWS, cols), _F32),
        compiler_params=pltpu.CompilerParams(vmem_limit_bytes=VMEM_LIMIT_BYTES),
    )(cv, w_mod, b_mod.reshape(depth, 1, cols))


def _per_direction(t):
    lane = lax.broadcasted_iota(jnp.int32, (1, SLAB), 1)
    first = lane < GLA_DK
    slabs = []
    for p in range(GLA_HEADS // 2):
        pair = t[:, p * SLAB:(p + 1) * SLAB]
        swapped = pltpu.roll(pair, GLA_DK, 1)
        slabs += [jnp.where(first, pair, swapped), jnp.where(first, swapped, pair)]
    return jnp.concatenate(slabs, axis=1)


def _inproj_kernel(x_ref, mod_ref, n1w_ref, win_ref, wvt_ref, gw_ref, gb_ref,
                   qe_ref, ki_ref, vt_ref, ut_ref, dec_ref, rc_ref):
    tm = x_ref.shape[0]
    wq_ref = win_ref.at[:, WQ_COLS]
    wk_ref = win_ref.at[:, WK_COLS]
    wrc_ref = win_ref.at[:, WRC_COLS]
    wz_ref = win_ref.at[:, WZ_COLS]
    sh1 = mod_ref[:, 0:D_MODEL]
    sc1 = mod_ref[:, D_MODEL:2 * D_MODEL]
    row = lax.broadcasted_iota(jnp.int32, (GLA_CHUNK, GLA_CHUNK), 0)
    col = lax.broadcasted_iota(jnp.int32, (GLA_CHUNK, GLA_CHUNK), 1)
    tri = (col <= row).astype(_BF16)
    lane = lax.broadcasted_iota(jnp.int32, (1, KEY_COLS), 1)
    is_fwd = (lane % SLAB) < GLA_DK
    mid = GLA_CHUNK // 2

    n_sub = tm // SUB_ROWS
    cps = SUB_ROWS // GLA_CHUNK
    sub_rows = [slice(s * SUB_ROWS, (s + 1) * SUB_ROWS) for s in range(n_sub)]
    hbs, gates, q2s, k2s = [], [], [], []
    for s in range(n_sub):
        x = x_ref[sub_rows[s], :]
        xn = x * lax.rsqrt(jnp.mean(x * x, axis=-1, keepdims=True) + EPS) * n1w_ref[...]
        hbs.append((xn * (1.0 + sc1) + sh1).astype(_BF16))
    for s in range(n_sub):
        z = _dot(hbs[s], wz_ref[...])
        pre = _dot(z.astype(_BF16), gw_ref[...]) + gb_ref[...]
        g = (jnp.minimum(pre, 0.0) - jnp.log(1.0 + jnp.exp(-jnp.abs(pre)))) * (1.0 / GATE_TAU)
        g_hi = g.astype(_BF16)
        gates.append((g, g_hi, (g - g_hi.astype(_F32)).astype(_BF16)))
        q2s.append(_per_direction(_dot(hbs[s], wq_ref[...]) * (GLA_DK ** -0.5)))
        k2s.append(_per_direction(_dot(hbs[s], wk_ref[...])))

    prefixes = []
    for s in range(n_sub):
        g, g_hi, g_lo = gates[s]
        for cl in range(cps):
            lrows = slice(cl * GLA_CHUNK, (cl + 1) * GLA_CHUNK)
            prefixes.append(_dot(tri, g_hi[lrows]) + _dot(tri, g_lo[lrows]))

    vtbs = []
    for s in range(n_sub):
        vtb = lax.dot_general(wvt_ref[...], hbs[s], _NT, preferred_element_type=_F32).astype(_BF16)
        vt_ref[:, sub_rows[s]] = vtb
        vtbs.append(vtb)
        rc_ref[sub_rows[s], :] = _dot(hbs[s], wrc_ref[...]).astype(_BF16)

    for s in range(n_sub):
        g = gates[s][0]
        for cl in range(cps):
            c = s * cps + cl
            lrows = slice(cl * GLA_CHUNK, (cl + 1) * GLA_CHUNK)
            rows = slice(c * GLA_CHUNK, (c + 1) * GLA_CHUNK)
            prefix = prefixes[c]
            total = prefix[GLA_CHUNK - 1:GLA_CHUNK, :]
            suffix = total - prefix + g[lrows]
            e = jnp.where(is_fwd, prefix, suffix)
            mvec = jnp.where(is_fwd, e[mid - 1:mid, :], e[mid:mid + 1, :])
            bm = e - mvec
            qe_ref[rows, :] = (q2s[s][lrows] * jnp.exp(bm)).astype(_BF16)
            ki_ref[rows, :] = (k2s[s][lrows] * jnp.exp(-bm)).astype(_BF16)
            kd = (k2s[s][lrows] * jnp.exp(total - e)).astype(_BF16)
            dec_ref[c] = jnp.concatenate([total, mvec, jnp.zeros((6, KEY_COLS), _F32)], axis=0)
            for h in range(GLA_HEADS):
                vt_h = vtbs[s][h * GLA_DV:(h + 1) * GLA_DV, lrows]
                ut_ref[c, h] = _dot(vt_h, kd[:, h * SLAB:(h + 1) * SLAB]).astype(ut_ref.dtype)


def _inproj(x2d, mods3, mod_row_of_tile, layer, wts, tm):
    tokens = x2d.shape[0]
    n_tiles = tokens // tm
    n_chunks = tokens // GLA_CHUNK
    cpt = tm // GLA_CHUNK
    of_layer = lambda shape: pl.BlockSpec((None,) + shape, lambda j: (layer,) + (0,) * len(shape))
    return pl.pallas_call(
        _inproj_kernel,
        grid=(n_tiles,),
        in_specs=[
            pl.BlockSpec((tm, D_MODEL), lambda j: (j, 0)),
            pl.BlockSpec((None, 1, N_MOD * D_MODEL), lambda j: (layer * MOD_ROWS + mod_row_of_tile(j), 0, 0)),
            of_layer((1, D_MODEL)),
            of_layer((D_MODEL, WIN_COLS)),
            of_layer((VAL_COLS, D_MODEL)),
            of_layer((2 * GATE_RANK, KEY_COLS)),
            of_layer((1, KEY_COLS)),
        ],
        out_specs=[
            pl.BlockSpec((tm, KEY_COLS), lambda j: (j, 0)),
            pl.BlockSpec((tm, KEY_COLS), lambda j: (j, 0)),
            pl.BlockSpec((VAL_COLS, tm), lambda j: (0, j)),
            pl.BlockSpec((cpt, GLA_HEADS, GLA_DV, SLAB), lambda j: (j, 0, 0, 0)),
            pl.BlockSpec((cpt, 8, KEY_COLS), lambda j: (j, 0, 0)),
            pl.BlockSpec((tm, RC_COLS), lambda j: (j, 0)),
        ],
        out_shape=[
            jax.ShapeDtypeStruct((tokens, KEY_COLS), _BF16),
            jax.ShapeDtypeStruct((tokens, KEY_COLS), _BF16),
            jax.ShapeDtypeStruct((VAL_COLS, tokens), _BF16),
            jax.ShapeDtypeStruct((n_chunks, GLA_HEADS, GLA_DV, SLAB), _BF16),
            jax.ShapeDtypeStruct((n_chunks, 8, KEY_COLS), _F32),
            jax.ShapeDtypeStruct((tokens, RC_COLS), _BF16),
        ],
        compiler_params=pltpu.CompilerParams(vmem_limit_bytes=VMEM_LIMIT_BYTES),
    )(x2d, mods3, wts["n1w"], wts["win"], wts["wvt"], wts["gw"], wts["gb"])


def _scan_kernel(ut_ref, dec_ref, s0_ref, spt_ref, sfin_ref):
    nc = ut_ref.shape[0]
    lane = lax.broadcasted_iota(jnp.int32, (1, SLAB), 1)
    is_fwd = lane < GLA_DK

    def step(i, st):
        j = nc - 1 - i
        dec_i = dec_ref[i]
        dec_j = dec_ref[j]
        log_decay = jnp.where(is_fwd, dec_i[0:1, :], dec_j[0:1, :])
        log_mid = jnp.where(is_fwd, dec_i[1:2, :], dec_j[1:2, :])
        entering = (st * jnp.exp(log_mid)).astype(_BF16)
        spt_ref[i, :, 0:GLA_DK] = entering[:, 0:GLA_DK]
        spt_ref[j, :, GLA_DK:SLAB] = entering[:, GLA_DK:SLAB]
        inc = jnp.where(is_fwd, ut_ref[i], ut_ref[j]).astype(_F32)
        return st * jnp.exp(log_decay) + inc

    sfin_ref[...] = lax.fori_loop(0, nc, step, s0_ref[...], unroll=SCAN_UNROLL)


def _scan(ut, dec, s0, chunks_per_seq):
    n_chunks = ut.shape[0]
    bsz = n_chunks // chunks_per_seq
    return pl.pallas_call(
        _scan_kernel,
        grid=(bsz, GLA_HEADS),
        in_specs=[
            pl.BlockSpec((chunks_per_seq, None, GLA_DV, SLAB), lambda b, h: (b, h, 0, 0)),
            pl.BlockSpec((chunks_per_seq, 8, SLAB), lambda b, h: (b, 0, h)),
            pl.BlockSpec((None, None, GLA_DV, SLAB), lambda b, h: (b, h, 0, 0)),
        ],
        out_specs=[
            pl.BlockSpec((chunks_per_seq, None, GLA_DV, SLAB), lambda b, h: (b, h, 0, 0)),
            pl.BlockSpec((None, None, GLA_DV, SLAB), lambda b, h: (b, h, 0, 0)),
        ],
        out_shape=[
            jax.ShapeDtypeStruct((n_chunks, GLA_HEADS, GLA_DV, SLAB), _BF16),
            jax.ShapeDtypeStruct((bsz, GLA_HEADS, GLA_DV, SLAB), _F32),
        ],
        compiler_params=pltpu.CompilerParams(vmem_limit_bytes=VMEM_LIMIT_BYTES),
    )(ut, dec, s0)


def _mix_kernel(x_ref, mod_ref, n2w_ref, qe_ref, ki_ref, vt_ref, spt_ref, rc_ref, gnw_ref, cw_ref,
                pw_ref, ps_ref, wo_ref, w1_ref, w2_ref, fnw_ref, o_ref, y_scr, *, seg, final):
    tm = x_ref.shape[0]
    sub = min(tm, max(MIX_SUB_ROWS, seg))
    g1 = mod_ref[:, 2 * D_MODEL:3 * D_MODEL]
    sh2 = mod_ref[:, 3 * D_MODEL:4 * D_MODEL]
    sc2 = mod_ref[:, 4 * D_MODEL:5 * D_MODEL]
    g2 = mod_ref[:, 5 * D_MODEL:6 * D_MODEL]
    row = lax.broadcasted_iota(jnp.int32, (GLA_CHUNK, GLA_CHUNK), 0)
    col = lax.broadcasted_iota(jnp.int32, (GLA_CHUNK, GLA_CHUNK), 1)
    lane = lax.broadcasted_iota(jnp.int32, (GLA_CHUNK, SLAB), 1)
    fwd_lane = lane < GLA_DK
    gnw = gnw_ref[...]
    pos = lax.broadcasted_iota(jnp.int32, (sub, 1), 0) % seg
    lane128 = lax.broadcasted_iota(jnp.int32, (1, 2 * POOL_GROUP), 1)

    for s in range(tm // sub):
        srows = slice(s * sub, (s + 1) * sub)

        for c in range(s * (sub // GLA_CHUNK), (s + 1) * (sub // GLA_CHUNK)):
            rows = slice(c * GLA_CHUNK, (c + 1) * GLA_CHUNK)
            for h in range(GLA_HEADS):
                lanes = slice(h * SLAB, (h + 1) * SLAB)
                qe = qe_ref[rows, lanes]
                ki = ki_ref[rows, lanes]
                zero = jnp.zeros_like(ki)
                keys = jnp.concatenate([jnp.where(fwd_lane, ki, zero), jnp.where(fwd_lane, zero, ki)], axis=0)
                a2 = lax.dot_general(qe, keys, _NT, preferred_element_type=_F32)
                am = jnp.where(col <= row, a2[:, 0:GLA_CHUNK], 0.0) + jnp.where(col >= row, a2[:, GLA_CHUNK:], 0.0)
                lhs = jnp.concatenate([am.astype(_BF16), qe], axis=1)
                rhs = jnp.concatenate([vt_ref[h * GLA_DV:(h + 1) * GLA_DV, rows], spt_ref[c, h]], axis=1)
                o = lax.dot_general(lhs, rhs, _NT, preferred_element_type=_F32)
                o = o * lax.rsqrt(jnp.mean(o * o, axis=-1, keepdims=True) + EPS) * gnw
                r = rc_ref[rows, h * GLA_DV:(h + 1) * GLA_DV].astype(_F32)
                y_scr[rows, h * GLA_DV:(h + 1) * GLA_DV] = (o * (r * _sigmoid(r))).astype(_BF16)

        c0 = VAL_COLS
        cb = rc_ref[srows, c0:c0 + CONV_WIDTH].astype(_F32)
        cc = rc_ref[srows, c0 + CONV_WIDTH:c0 + 2 * CONV_WIDTH].astype(_F32)
        ch = rc_ref[srows, c0 + 2 * CONV_WIDTH:c0 + 3 * CONV_WIDTH].astype(_F32)
        zc = cc * ch
        z_prev = jnp.where(pos >= 1, pltpu.roll(zc, 1, 0), 0.0)
        z_next = jnp.where(pos <= seg - 2, pltpu.roll(zc, sub - 1, 0), 0.0)
        yc = cb * (cw_ref[0:1, :] * z_prev + cw_ref[1:2, :] * zc + cw_ref[2:3, :] * z_next)
        y_scr[srows, VAL_COLS:VAL_COLS + CONV_WIDTH] = yc.astype(_BF16)

        p0 = VAL_COLS + 3 * CONV_WIDTH
        pooled = []
        for blk in range(POOL_WIDTH // (2 * POOL_GROUP)):
            u = rc_ref[srows, p0 + blk * 2 * POOL_GROUP:p0 + (blk + 1) * 2 * POOL_GROUP].astype(_F32)
            h_lo, h_hi = POOL_HALF_WINDOWS[2 * blk], POOL_HALF_WINDOWS[2 * blk + 1]
            half = jnp.where(lane128 < POOL_GROUP, h_lo, h_hi)
            acc = jnp.zeros_like(u)
            for d in range(-h_hi, h_hi):
                shifted = u if d == 0 else pltpu.roll(u, (-d) % sub, 0)
                valid = (pos + d >= 0) & (pos + d < seg) & (d >= -half) & (d < half)
                acc = acc + jnp.where(valid, shifted, 0.0)
            cnt = (jnp.minimum(pos + half, seg) - jnp.maximum(pos - half, 0)).astype(_F32)
            pooled.append(acc / cnt - u)
        pool_in = jnp.concatenate(pooled, axis=1).astype(_BF16)
        yp = _dot(pool_in, pw_ref[...]) * ps_ref[...]
        y_scr[srows, VAL_COLS + CONV_WIDTH:] = yp.astype(_BF16)

        x1 = x_ref[srows, :] + g1 * _dot(y_scr[srows, :], wo_ref[...])
        xn = x1 * lax.rsqrt(jnp.mean(x1 * x1, axis=-1, keepdims=True) + EPS) * n2w_ref[...]
        h2 = (xn * (1.0 + sc2) + sh2).astype(_BF16)
        acc = jnp.zeros((sub, D_MODEL), _F32)
        for f in range(D_FF // FF_CHUNK):
            cols = slice(f * FF_CHUNK, (f + 1) * FF_CHUNK)
            a = jnp.maximum(_dot(h2, w1_ref[:, cols]), 0.0)
            acc = acc + _dot((a * a).astype(_BF16), w2_ref[cols, :])
        out = x1 + g2 * acc
        if final:
            out = out * lax.rsqrt(jnp.mean(out * out, axis=-1, keepdims=True) + EPS) * fnw_ref[...]
        o_ref[srows, :] = out


def _mix(x2d, mods3, mod_row_of_tile, layer, qe, ki, vt, spt, rc, wts, tm, seg, final):
    tokens = x2d.shape[0]
    cpt = tm // GLA_CHUNK
    of_layer = lambda shape: pl.BlockSpec((None,) + shape, lambda j: (layer,) + (0,) * len(shape),
                                          pipeline_mode=pl.Buffered(1))
    return pl.pallas_call(
        functools.partial(_mix_kernel, seg=seg, final=final),
        grid=(tokens // tm,),
        in_specs=[
            pl.BlockSpec((tm, D_MODEL), lambda j: (j, 0)),
            pl.BlockSpec((None, 1, N_MOD * D_MODEL), lambda j: (layer * MOD_ROWS + mod_row_of_tile(j), 0, 0)),
            of_layer((1, D_MODEL)),
            pl.BlockSpec((tm, KEY_COLS), lambda j: (j, 0)),
            pl.BlockSpec((tm, KEY_COLS), lambda j: (j, 0)),
            pl.BlockSpec((VAL_COLS, tm), lambda j: (0, j)),
            pl.BlockSpec((cpt, GLA_HEADS, GLA_DV, SLAB), lambda j: (j, 0, 0, 0)),
            pl.BlockSpec((tm, RC_COLS), lambda j: (j, 0)),
            of_layer((1, GLA_DV)),
            of_layer((3, CONV_WIDTH)),
            of_layer((POOL_WIDTH, POOL_WIDTH)),
            of_layer((1, POOL_WIDTH)),
            of_layer((D_MODEL, D_MODEL)),
            of_layer((D_MODEL, D_FF)),
            of_layer((D_FF, D_MODEL)),
            pl.BlockSpec((1, D_MODEL), lambda j: (0, 0), pipeline_mode=pl.Buffered(1)),
        ],
        out_specs=pl.BlockSpec((tm, D_MODEL), lambda j: (j, 0)),
        out_shape=jax.ShapeDtypeStruct((tokens, D_MODEL), _F32),
        scratch_shapes=[pltpu.VMEM((tm, D_MODEL), _BF16)],
        compiler_params=pltpu.CompilerParams(vmem_limit_bytes=VMEM_LIMIT_BYTES),
    )(x2d, mods3, wts["n2w"], qe, ki, vt, spt, rc, wts["gnw"], wts["cw"], wts["pw"], wts["ps"],
      wts["wo"], wts["w1"], wts["w2"], wts["fnw"])


def _prepare_weights(norm1_w, norm2_w, w_in, gate_w, gate_b, gla_norm_w, conv_w, pool_w, pool_scale, w_out,
                     w_mlp1, w_mlp2, final_norm_w):
    depth = w_in.shape[0]
    key_w = GLA_HEADS * GLA_DK
    q_off, k_off, v_off = 0, key_w, 2 * key_w
    g_off = v_off + VAL_COLS
    r_off = g_off + 2 * GATE_RANK
    win = jnp.concatenate([w_in[:, :, q_off:v_off], w_in[:, :, r_off:], w_in[:, :, g_off:r_off]], axis=-1)

    zeros = jnp.zeros((depth, GATE_RANK, GLA_HEADS, GLA_DK), _F32)
    gf = gate_w[:, 0].reshape(depth, GATE_RANK, GLA_HEADS, GLA_DK)
    gb = gate_w[:, 1].reshape(depth, GATE_RANK, GLA_HEADS, GLA_DK)
    gw = jnp.concatenate([jnp.concatenate([gf, zeros], -1), jnp.concatenate([zeros, gb], -1)], 1)
    gbias = jnp.concatenate([gate_b[:, 0].reshape(depth, GLA_HEADS, GLA_DK),
                             gate_b[:, 1].reshape(depth, GLA_HEADS, GLA_DK)], -1)
    pw = jnp.zeros((depth, POOL_WIDTH, POOL_WIDTH), _F32)
    for gi in range(POOL_WIDTH // POOL_GROUP):
        sl = slice(gi * POOL_GROUP, (gi + 1) * POOL_GROUP)
        pw = pw.at[:, sl, sl].set(pool_w[:, gi])
    return {
        "n1w": norm1_w.reshape(depth, 1, D_MODEL),
        "n2w": norm2_w.reshape(depth, 1, D_MODEL),
        "win": win.astype(_BF16),
        "wvt": jnp.swapaxes(w_in[:, :, v_off:g_off], 1, 2).astype(_BF16),
        "gw": gw.reshape(depth, 2 * GATE_RANK, KEY_COLS).astype(_BF16),
        "gb": gbias.reshape(depth, 1, KEY_COLS),
        "gnw": gla_norm_w.reshape(depth, 1, GLA_DV),
        "cw": conv_w,
        "pw": pw.astype(_BF16),
        "ps": pool_scale.reshape(depth, 1, POOL_WIDTH),
        "wo": w_out.astype(_BF16),
        "w1": w_mlp1.astype(_BF16),
        "w2": w_mlp2.astype(_BF16),
        "fnw": final_norm_w.reshape(1, D_MODEL),
    }


def kernel(x, c, ctx, c_ctx, w_mod, b_mod, norm1_w, norm2_w, w_in, gla_gate_w, gla_gate_b, gla_norm_w, conv_w, pool_w, pool_scale, w_out, w_mlp1, w_mlp2, final_norm_w):
    bsz, n, _ = x.shape
    ctx_len = ctx.shape[1]
    depth = w_in.shape[0]
    assert bsz + 1 <= MOD_ROWS and n % LATENT_IN_TILE == 0 and ctx_len % SUB_ROWS == 0 and ctx_len >= GRID_W
    ctx_tile = bsz * ctx_len
    ctx_row = bsz

    cv = jnp.concatenate([c, c_ctx[None, :], jnp.zeros((MOD_ROWS - bsz - 1, D_MODEL), _F32)], axis=0)
    mods = _adaln(cv, w_mod, b_mod)
    mods3 = mods.reshape(depth * MOD_ROWS, 1, N_MOD * D_MODEL)
    wts = _prepare_weights(norm1_w, norm2_w, w_in, gla_gate_w, gla_gate_b, gla_norm_w, conv_w, pool_w, pool_scale,
                           w_out, w_mlp1, w_mlp2, final_norm_w)

    xl = x.reshape(bsz * n, D_MODEL)
    xc = ctx.reshape(bsz * ctx_len, D_MODEL)
    lat_in_row = lambda j: j // (n // LATENT_IN_TILE)
    lat_mix_row = lambda j: j // (n // LATENT_MIX_TILE)
    ctx_mod_row = lambda j: ctx_row
    s_zero = jnp.zeros((bsz, GLA_HEADS, GLA_DV, SLAB), _F32)

    for l in range(depth):
        last = l == depth - 1
        cqe, cki, cvt, cut, cdec, crc = _inproj(xc, mods3, ctx_mod_row, l, wts, ctx_tile)
        cspt, s_ctx = _scan(cut, cdec, s_zero, ctx_len // GLA_CHUNK)
        if not last:
            xc = _mix(xc, mods3, ctx_mod_row, l, cqe, cki, cvt, cspt, crc, wts, ctx_tile, ctx_len, False)

        qe, ki, vt, ut, dec, rc = _inproj(xl, mods3, lat_in_row, l, wts, LATENT_IN_TILE)
        spt, _ = _scan(ut, dec, s_ctx, n // GLA_CHUNK)
        xl = _mix(xl, mods3, lat_mix_row, l, qe, ki, vt, spt, rc, wts, LATENT_MIX_TILE, GRID_W, last)
    return xl.reshape(bsz, n, D_MODEL)
```

```python
import functools

import jax
import jax.numpy as jnp
from jax import lax
from jax.experimental import pallas as pl
from jax.experimental.pallas import tpu as pltpu

D_MODEL = 1024
GLA_HEADS = 4
GLA_DK = 64
GLA_DV = 128
SLAB = 2 * GLA_DK
KEY_COLS = GLA_HEADS * SLAB
VAL_COLS = GLA_HEADS * GLA_DV
GATE_RANK = 16
GATE_TAU = 16.0
LOG2_E = 1.4426950408889634
GLA_CHUNK = 128
SUB_ROWS = 256
LATENT_IN_TILE = 1024
LATENT_MIX_TILE = 512
CONV_WIDTH = 256
POOL_WIDTH = 256
POOL_GROUP = 64
POOL_HALF_WINDOWS = (1, 2, 4, 8)
D_FF = 4096
FF_CHUNK = 1024
N_MOD = 6
EPS = 1e-6
GRID_W = 64
RC_COLS = VAL_COLS + 3 * CONV_WIDTH + POOL_WIDTH
RY_COLS = VAL_COLS + CONV_WIDTH + POOL_WIDTH
WQ_COLS = slice(0, GLA_HEADS * GLA_DK)
WK_COLS = slice(WQ_COLS.stop, WQ_COLS.stop + GLA_HEADS * GLA_DK)
WRC_COLS = slice(WK_COLS.stop, WK_COLS.stop + RC_COLS)
WZ_COLS = slice(WRC_COLS.stop, WRC_COLS.stop + 2 * GATE_RANK)
WIN_COLS = WZ_COLS.stop
SCAN_UNROLL = 2
MOD_ROWS = 8
ADALN_COLS = 1536
VMEM_LIMIT_BYTES = 56 * 1024 * 1024

_NT = (((1,), (1,)), ((), ()))
_BF16 = jnp.bfloat16
_F32 = jnp.float32


def _dot(a, b):
    return jnp.dot(a, b, preferred_element_type=_F32)


def _sigmoid(x):
    return 1.0 / (1.0 + jnp.exp(-x))


def _adaln_kernel(cv_ref, w_ref, b_ref, o_ref):
    cv = cv_ref[...]
    s = (cv * _sigmoid(cv)).astype(_BF16)
    o_ref[...] = _dot(s, w_ref[...].astype(_BF16)) + b_ref[...]


def _adaln(cv, w_mod, b_mod):
    depth = w_mod.shape[0]
    cols = w_mod.shape[2]
    return pl.pallas_call(
        _adaln_kernel,
        grid=(depth, cols // ADALN_COLS),
        in_specs=[
            pl.BlockSpec((MOD_ROWS, D_MODEL), lambda l, j: (0, 0)),
            pl.BlockSpec((None, D_MODEL, ADALN_COLS), lambda l, j: (l, 0, j)),
            pl.BlockSpec((None, 1, ADALN_COLS), lambda l, j: (l, 0, j)),
        ],
        out_specs=pl.BlockSpec((None, MOD_ROWS, ADALN_COLS), lambda l, j: (l, 0, j)),
        out_shape=jax.ShapeDtypeStruct((depth, MOD_ROWS, cols), _F32),
        compiler_params=pltpu.CompilerParams(vmem_limit_bytes=VMEM_LIMIT_BYTES),
    )(cv, w_mod, b_mod.reshape(depth, 1, cols))


def _per_direction(t):
    lane = lax.broadcasted_iota(jnp.int32, (1, SLAB), 1)
    first = lane < GLA_DK
    slabs = []
    for p in range(GLA_HEADS // 2):
        pair = t[:, p * SLAB:(p + 1) * SLAB]
        swapped = pltpu.roll(pair, GLA_DK, 1)
        slabs += [jnp.where(first, pair, swapped), jnp.where(first, swapped, pair)]
    return jnp.concatenate(slabs, axis=1)


def _conv_and_pool(rc, cw_ref, pw_ref, ps_ref, seg):
    rows = rc.shape[0]
    pos = lax.broadcasted_iota(jnp.int32, (rows, 1), 0) % seg
    c0 = VAL_COLS
    cb = rc[:, c0:c0 + CONV_WIDTH]
    zc = rc[:, c0 + CONV_WIDTH:c0 + 2 * CONV_WIDTH] * rc[:, c0 + 2 * CONV_WIDTH:c0 + 3 * CONV_WIDTH]
    z_prev = jnp.where(pos >= 1, pltpu.roll(zc, 1, 0), 0.0)
    z_next = jnp.where(pos <= seg - 2, pltpu.roll(zc, rows - 1, 0), 0.0)
    yc = cb * (cw_ref[0:1, :] * z_prev + cw_ref[1:2, :] * zc + cw_ref[2:3, :] * z_next)

    p0 = VAL_COLS + 3 * CONV_WIDTH
    lane128 = lax.broadcasted_iota(jnp.int32, (1, 2 * POOL_GROUP), 1)
    pooled = []
    for blk in range(POOL_WIDTH // (2 * POOL_GROUP)):
        u = rc[:, p0 + blk * 2 * POOL_GROUP:p0 + (blk + 1) * 2 * POOL_GROUP]
        h_lo, h_hi = POOL_HALF_WINDOWS[2 * blk], POOL_HALF_WINDOWS[2 * blk + 1]
        half = jnp.where(lane128 < POOL_GROUP, h_lo, h_hi)
        acc = jnp.zeros_like(u)
        for d in range(-h_hi, h_hi):
            shifted = u if d == 0 else pltpu.roll(u, (-d) % rows, 0)
            valid = (pos + d >= 0) & (pos + d < seg) & (d >= -half) & (d < half)
            acc = acc + jnp.where(valid, shifted, 0.0)
        cnt = (jnp.minimum(pos + half, seg) - jnp.maximum(pos - half, 0)).astype(_F32)
        pooled.append(acc / cnt - u)
    pool_in = jnp.concatenate(pooled, axis=1).astype(_BF16)
    return yc, _dot(pool_in, pw_ref[...]) * ps_ref[...]


def _inproj_kernel(x_ref, mod_ref, n1w_ref, win_ref, wvt_ref, gw_ref, gb_ref, cw_ref, pw_ref, ps_ref,
                   qe_ref, ki_ref, vt_ref, ut_ref, dec_ref, ry_ref, *, seg):
    tm = x_ref.shape[0]
    wq_ref = win_ref.at[:, WQ_COLS]
    wk_ref = win_ref.at[:, WK_COLS]
    wrc_ref = win_ref.at[:, WRC_COLS]
    wz_ref = win_ref.at[:, WZ_COLS]
    sh1 = mod_ref[:, 0:D_MODEL]
    sc1 = mod_ref[:, D_MODEL:2 * D_MODEL]
    row = lax.broadcasted_iota(jnp.int32, (GLA_CHUNK, GLA_CHUNK), 0)
    col = lax.broadcasted_iota(jnp.int32, (GLA_CHUNK, GLA_CHUNK), 1)
    tri = (col <= row).astype(_BF16)
    lane = lax.broadcasted_iota(jnp.int32, (1, KEY_COLS), 1)
    is_fwd = (lane % SLAB) < GLA_DK
    mid = GLA_CHUNK // 2

    n_sub = tm // SUB_ROWS
    cps = SUB_ROWS // GLA_CHUNK
    sub_rows = [slice(s * SUB_ROWS, (s + 1) * SUB_ROWS) for s in range(n_sub)]
    hbs, gates, q2s, k2s = [], [], [], []
    for s in range(n_sub):
        x = x_ref[sub_rows[s], :]
        xn = x * lax.rsqrt(jnp.mean(x * x, axis=-1, keepdims=True) + EPS) * n1w_ref[...]
        hbs.append((xn * (1.0 + sc1) + sh1).astype(_BF16))
    for s in range(n_sub):
        z = _dot(hbs[s], wz_ref[...])
        pre = _dot(z.astype(_BF16), gw_ref[...]) + gb_ref[...]
        g = (jnp.minimum(pre, 0.0) * LOG2_E - jnp.log2(1.0 + jnp.exp2(jnp.abs(pre) * (-LOG2_E)))) * (1.0 / GATE_TAU)
        g_hi = g.astype(_BF16)
        gates.append((g, g_hi, (g - g_hi.astype(_F32)).astype(_BF16)))
        q2s.append(_per_direction(_dot(hbs[s], wq_ref[...]) * (GLA_DK ** -0.5)))
        k2s.append(_per_direction(_dot(hbs[s], wk_ref[...])))

    prefixes = []
    for s in range(n_sub):
        g, g_hi, g_lo = gates[s]
        for cl in range(cps):
            lrows = slice(cl * GLA_CHUNK, (cl + 1) * GLA_CHUNK)
            prefixes.append(_dot(tri, g_hi[lrows]) + _dot(tri, g_lo[lrows]))

    vtbs = []
    for s in range(n_sub):
        vtb = lax.dot_general(wvt_ref[...], hbs[s], _NT, preferred_element_type=_F32).astype(_BF16)
        vt_ref[:, sub_rows[s]] = vtb
        vtbs.append(vtb)
        rc = _dot(hbs[s], wrc_ref[...])
        yc, yp = _conv_and_pool(rc, cw_ref, pw_ref, ps_ref, seg)
        ry_ref[sub_rows[s], :] = jnp.concatenate([rc[:, 0:VAL_COLS], yc, yp], axis=1).astype(_BF16)

    for s in range(n_sub):
        g = gates[s][0]
        for cl in range(cps):
            c = s * cps + cl
            lrows = slice(cl * GLA_CHUNK, (cl + 1) * GLA_CHUNK)
            rows = slice(c * GLA_CHUNK, (c + 1) * GLA_CHUNK)
            prefix = prefixes[c]
            total = prefix[GLA_CHUNK - 1:GLA_CHUNK, :]
            suffix = total - prefix + g[lrows]
            e = jnp.where(is_fwd, prefix, suffix)
            mvec = jnp.where(is_fwd, e[mid - 1:mid, :], e[mid:mid + 1, :])
            bm = e - mvec
            qe_ref[rows, :] = (q2s[s][lrows] * jnp.exp2(bm)).astype(_BF16)
            ki_ref[rows, :] = (k2s[s][lrows] * jnp.exp2(-bm)).astype(_BF16)
            kd = (k2s[s][lrows] * jnp.exp2(total - e)).astype(_BF16)
            dec_ref[c] = jnp.concatenate([total, mvec, jnp.zeros((6, KEY_COLS), _F32)], axis=0)
            for h in range(GLA_HEADS):
                vt_h = vtbs[s][h * GLA_DV:(h + 1) * GLA_DV, lrows]
                ut_ref[c, h] = _dot(vt_h, kd[:, h * SLAB:(h + 1) * SLAB]).astype(ut_ref.dtype)


def _inproj(x2d, mods3, mod_row_of_tile, layer, wts, tm, seg):
    tokens = x2d.shape[0]
    n_tiles = tokens // tm
    n_chunks = tokens // GLA_CHUNK
    cpt = tm // GLA_CHUNK
    assert SUB_ROWS % seg == 0
    of_layer = lambda shape: pl.BlockSpec((None,) + shape, lambda j: (layer,) + (0,) * len(shape))
    return pl.pallas_call(
        functools.partial(_inproj_kernel, seg=seg),
        grid=(n_tiles,),
        in_specs=[
            pl.BlockSpec((tm, D_MODEL), lambda j: (j, 0)),
            pl.BlockSpec((None, 1, N_MOD * D_MODEL), lambda j: (layer * MOD_ROWS + mod_row_of_tile(j), 0, 0)),
            of_layer((1, D_MODEL)),
            of_layer((D_MODEL, WIN_COLS)),
            of_layer((VAL_COLS, D_MODEL)),
            of_layer((2 * GATE_RANK, KEY_COLS)),
            of_layer((1, KEY_COLS)),
            of_layer((3, CONV_WIDTH)),
            of_layer((POOL_WIDTH, POOL_WIDTH)),
            of_layer((1, POOL_WIDTH)),
        ],
        out_specs=[
            pl.BlockSpec((tm, KEY_COLS), lambda j: (j, 0)),
            pl.BlockSpec((tm, KEY_COLS), lambda j: (j, 0)),
            pl.BlockSpec((VAL_COLS, tm), lambda j: (0, j)),
            pl.BlockSpec((cpt, GLA_HEADS, GLA_DV, SLAB), lambda j: (j, 0, 0, 0)),
            pl.BlockSpec((cpt, 8, KEY_COLS), lambda j: (j, 0, 0)),
            pl.BlockSpec((tm, RY_COLS), lambda j: (j, 0)),
        ],
        out_shape=[
            jax.ShapeDtypeStruct((tokens, KEY_COLS), _BF16),
            jax.ShapeDtypeStruct((tokens, KEY_COLS), _BF16),
            jax.ShapeDtypeStruct((VAL_COLS, tokens), _BF16),
            jax.ShapeDtypeStruct((n_chunks, GLA_HEADS, GLA_DV, SLAB), _BF16),
            jax.ShapeDtypeStruct((n_chunks, 8, KEY_COLS), _F32),
            jax.ShapeDtypeStruct((tokens, RY_COLS), _BF16),
        ],
        compiler_params=pltpu.CompilerParams(vmem_limit_bytes=VMEM_LIMIT_BYTES),
    )(x2d, mods3, wts["n1w"], wts["win"], wts["wvt"], wts["gw"], wts["gb"], wts["cw"], wts["pw"], wts["ps"])


def _scan_kernel(ut_ref, dec_ref, s0_ref, spt_ref, sfin_ref):
    nc = ut_ref.shape[0]
    lane = lax.broadcasted_iota(jnp.int32, (1, SLAB), 1)
    is_fwd = lane < GLA_DK

    def step(i, st):
        j = nc - 1 - i
        dec_i = dec_ref[i]
        dec_j = dec_ref[j]
        log_decay = jnp.where(is_fwd, dec_i[0:1, :], dec_j[0:1, :])
        log_mid = jnp.where(is_fwd, dec_i[1:2, :], dec_j[1:2, :])
        entering = (st * jnp.exp2(log_mid)).astype(_BF16)
        spt_ref[i, :, 0:GLA_DK] = entering[:, 0:GLA_DK]
        spt_ref[j, :, GLA_DK:SLAB] = entering[:, GLA_DK:SLAB]
        inc = jnp.where(is_fwd, ut_ref[i], ut_ref[j]).astype(_F32)
        return st * jnp.exp2(log_decay) + inc

    sfin_ref[...] = lax.fori_loop(0, nc, step, s0_ref[...], unroll=SCAN_UNROLL)


def _scan(ut, dec, s0, chunks_per_seq):
    n_chunks = ut.shape[0]
    bsz = n_chunks // chunks_per_seq
    return pl.pallas_call(
        _scan_kernel,
        grid=(bsz, GLA_HEADS),
        in_specs=[
            pl.BlockSpec((chunks_per_seq, None, GLA_DV, SLAB), lambda b, h: (b, h, 0, 0)),
            pl.BlockSpec((chunks_per_seq, 8, SLAB), lambda b, h: (b, 0, h)),
            pl.BlockSpec((None, None, GLA_DV, SLAB), lambda b, h: (b, h, 0, 0)),
        ],
        out_specs=[
            pl.BlockSpec((chunks_per_seq, None, GLA_DV, SLAB), lambda b, h: (b, h, 0, 0)),
            pl.BlockSpec((None, None, GLA_DV, SLAB), lambda b, h: (b, h, 0, 0)),
        ],
        out_shape=[
            jax.ShapeDtypeStruct((n_chunks, GLA_HEADS, GLA_DV, SLAB), _BF16),
            jax.ShapeDtypeStruct((bsz, GLA_HEADS, GLA_DV, SLAB), _F32),
        ],
        compiler_params=pltpu.CompilerParams(vmem_limit_bytes=VMEM_LIMIT_BYTES),
    )(ut, dec, s0)


def _mix_kernel(x_ref, mod_ref, n2w_ref, qe_ref, ki_ref, vt_ref, spt_ref, ry_ref, gnw_ref,
                wo_ref, w1_ref, w2_ref, fnw_ref, o_ref, y_scr, *, final):
    tm = x_ref.shape[0]
    g1 = mod_ref[:, 2 * D_MODEL:3 * D_MODEL]
    sh2 = mod_ref[:, 3 * D_MODEL:4 * D_MODEL]
    sc2 = mod_ref[:, 4 * D_MODEL:5 * D_MODEL]
    g2 = mod_ref[:, 5 * D_MODEL:6 * D_MODEL]
    row = lax.broadcasted_iota(jnp.int32, (GLA_CHUNK, GLA_CHUNK), 0)
    col = lax.broadcasted_iota(jnp.int32, (GLA_CHUNK, GLA_CHUNK), 1)
    lane = lax.broadcasted_iota(jnp.int32, (GLA_CHUNK, SLAB), 1)
    fwd_lane = lane < GLA_DK
    gnw = gnw_ref[...]

    for c in range(tm // GLA_CHUNK):
        rows = slice(c * GLA_CHUNK, (c + 1) * GLA_CHUNK)
        for h in range(GLA_HEADS):
            lanes = slice(h * SLAB, (h + 1) * SLAB)
            qe = qe_ref[rows, lanes]
            ki = ki_ref[rows, lanes]
            zero = jnp.zeros_like(ki)
            keys = jnp.concatenate([jnp.where(fwd_lane, ki, zero), jnp.where(fwd_lane, zero, ki)], axis=0)
            a2 = lax.dot_general(qe, keys, _NT, preferred_element_type=_F32)
            am = jnp.where(col <= row, a2[:, 0:GLA_CHUNK], 0.0) + jnp.where(col >= row, a2[:, GLA_CHUNK:], 0.0)
            lhs = jnp.concatenate([am.astype(_BF16), qe], axis=1)
            rhs = jnp.concatenate([vt_ref[h * GLA_DV:(h + 1) * GLA_DV, rows], spt_ref[c, h]], axis=1)
            o = lax.dot_general(lhs, rhs, _NT, preferred_element_type=_F32)
            o = o * lax.rsqrt(jnp.mean(o * o, axis=-1, keepdims=True) + EPS) * gnw
            r = ry_ref[rows, h * GLA_DV:(h + 1) * GLA_DV].astype(_F32)
            y_scr[rows, h * GLA_DV:(h + 1) * GLA_DV] = (o * (r * _sigmoid(r))).astype(_BF16)

    mixed = _dot(y_scr[...], wo_ref[0:VAL_COLS, :]) + _dot(ry_ref[:, VAL_COLS:RY_COLS], wo_ref[VAL_COLS:D_MODEL, :])
    x1 = x_ref[...] + g1 * mixed
    xn = x1 * lax.rsqrt(jnp.mean(x1 * x1, axis=-1, keepdims=True) + EPS) * n2w_ref[...]
    h2 = (xn * (1.0 + sc2) + sh2).astype(_BF16)
    acc = jnp.zeros((tm, D_MODEL), _F32)
    for f in range(D_FF // FF_CHUNK):
        cols = slice(f * FF_CHUNK, (f + 1) * FF_CHUNK)
        a = jnp.maximum(_dot(h2, w1_ref[:, cols]), 0.0)
        acc = acc + _dot((a * a).astype(_BF16), w2_ref[cols, :])
    out = x1 + g2 * acc
    if final:
        out = out * lax.rsqrt(jnp.mean(out * out, axis=-1, keepdims=True) + EPS) * fnw_ref[...]
    o_ref[...] = out


def _mix(x2d, mods3, mod_row_of_tile, layer, qe, ki, vt, spt, ry, wts, tm, final):
    tokens = x2d.shape[0]
    cpt = tm // GLA_CHUNK
    of_layer = lambda shape: pl.BlockSpec((None,) + shape, lambda j: (layer,) + (0,) * len(shape),
                                          pipeline_mode=pl.Buffered(1))
    return pl.pallas_call(
        functools.partial(_mix_kernel, final=final),
        grid=(tokens // tm,),
        in_specs=[
            pl.BlockSpec((tm, D_MODEL), lambda j: (j, 0)),
            pl.BlockSpec((None, 1, N_MOD * D_MODEL), lambda j: (layer * MOD_ROWS + mod_row_of_tile(j), 0, 0)),
            of_layer((1, D_MODEL)),
            pl.BlockSpec((tm, KEY_COLS), lambda j: (j, 0)),
            pl.BlockSpec((tm, KEY_COLS), lambda j: (j, 0)),
            pl.BlockSpec((VAL_COLS, tm), lambda j: (0, j)),
            pl.BlockSpec((cpt, GLA_HEADS, GLA_DV, SLAB), lambda j: (j, 0, 0, 0)),
            pl.BlockSpec((tm, RY_COLS), lambda j: (j, 0)),
            of_layer((1, GLA_DV)),
            of_layer((D_MODEL, D_MODEL)),
            of_layer((D_MODEL, D_FF)),
            of_layer((D_FF, D_MODEL)),
            pl.BlockSpec((1, D_MODEL), lambda j: (0, 0), pipeline_mode=pl.Buffered(1)),
        ],
        out_specs=pl.BlockSpec((tm, D_MODEL), lambda j: (j, 0)),
        out_shape=jax.ShapeDtypeStruct((tokens, D_MODEL), _F32),
        scratch_shapes=[pltpu.VMEM((tm, VAL_COLS), _BF16)],
        compiler_params=pltpu.CompilerParams(vmem_limit_bytes=VMEM_LIMIT_BYTES),
    )(x2d, mods3, wts["n2w"], qe, ki, vt, spt, ry, wts["gnw"], wts["wo"], wts["w1"], wts["w2"], wts["fnw"])


def _prepare_weights(norm1_w, norm2_w, w_in, gate_w, gate_b, gla_norm_w, conv_w, pool_w, pool_scale, w_out,
                     w_mlp1, w_mlp2, final_norm_w):
    depth = w_in.shape[0]
    key_w = GLA_HEADS * GLA_DK
    q_off, k_off, v_off = 0, key_w, 2 * key_w
    g_off = v_off + VAL_COLS
    r_off = g_off + 2 * GATE_RANK
    win = jnp.concatenate([w_in[:, :, q_off:v_off], w_in[:, :, r_off:], w_in[:, :, g_off:r_off]], axis=-1)

    zeros = jnp.zeros((depth, GATE_RANK, GLA_HEADS, GLA_DK), _F32)
    gf = gate_w[:, 0].reshape(depth, GATE_RANK, GLA_HEADS, GLA_DK)
    gb = gate_w[:, 1].reshape(depth, GATE_RANK, GLA_HEADS, GLA_DK)
    gw = jnp.concatenate([jnp.concatenate([gf, zeros], -1), jnp.concatenate([zeros, gb], -1)], 1)
    gbias = jnp.concatenate([gate_b[:, 0].reshape(depth, GLA_HEADS, GLA_DK),
                             gate_b[:, 1].reshape(depth, GLA_HEADS, GLA_DK)], -1)
    pw = jnp.zeros((depth, POOL_WIDTH, POOL_WIDTH), _F32)
    for gi in range(POOL_WIDTH // POOL_GROUP):
        sl = slice(gi * POOL_GROUP, (gi + 1) * POOL_GROUP)
        pw = pw.at[:, sl, sl].set(pool_w[:, gi])
    return {
        "n1w": norm1_w.reshape(depth, 1, D_MODEL),
        "n2w": norm2_w.reshape(depth, 1, D_MODEL),
        "win": win.astype(_BF16),
        "wvt": jnp.swapaxes(w_in[:, :, v_off:g_off], 1, 2).astype(_BF16),
        "gw": gw.reshape(depth, 2 * GATE_RANK, KEY_COLS).astype(_BF16),
        "gb": gbias.reshape(depth, 1, KEY_COLS),
        "gnw": gla_norm_w.reshape(depth, 1, GLA_DV),
        "cw": conv_w,
        "pw": pw.astype(_BF16),
        "ps": pool_scale.reshape(depth, 1, POOL_WIDTH),
        "wo": w_out.astype(_BF16),
        "w1": w_mlp1.astype(_BF16),
        "w2": w_mlp2.astype(_BF16),
        "fnw": final_norm_w.reshape(1, D_MODEL),
    }


def kernel(x, c, ctx, c_ctx, w_mod, b_mod, norm1_w, norm2_w, w_in, gla_gate_w, gla_gate_b, gla_norm_w, conv_w, pool_w, pool_scale, w_out, w_mlp1, w_mlp2, final_norm_w):
    bsz, n, _ = x.shape
    ctx_len = ctx.shape[1]
    depth = w_in.shape[0]
    assert bsz + 1 <= MOD_ROWS and n % LATENT_IN_TILE == 0 and SUB_ROWS % ctx_len == 0
    ctx_tile = bsz * ctx_len
    ctx_row = bsz

    cv = jnp.concatenate([c, c_ctx[None, :], jnp.zeros((MOD_ROWS - bsz - 1, D_MODEL), _F32)], axis=0)
    mods = _adaln(cv, w_mod, b_mod)
    mods3 = mods.reshape(depth * MOD_ROWS, 1, N_MOD * D_MODEL)
    wts = _prepare_weights(norm1_w, norm2_w, w_in, gla_gate_w, gla_gate_b, gla_norm_w, conv_w, pool_w, pool_scale,
                           w_out, w_mlp1, w_mlp2, final_norm_w)

    xl = x.reshape(bsz * n, D_MODEL)
    xc = ctx.reshape(bsz * ctx_len, D_MODEL)
    lat_in_row = lambda j: j // (n // LATENT_IN_TILE)
    lat_mix_row = lambda j: j // (n // LATENT_MIX_TILE)
    ctx_mod_row = lambda j: ctx_row
    s_zero = jnp.zeros((bsz, GLA_HEADS, GLA_DV, SLAB), _F32)

    for l in range(depth):
        last = l == depth - 1
        cqe, cki, cvt, cut, cdec, cry = _inproj(xc, mods3, ctx_mod_row, l, wts, ctx_tile, ctx_len)
        cspt, s_ctx = _scan(cut, cdec, s_zero, ctx_len // GLA_CHUNK)
        if not last:
            xc = _mix(xc, mods3, ctx_mod_row, l, cqe, cki, cvt, cspt, cry, wts, ctx_tile, False)

        qe, ki, vt, ut, dec, ry = _inproj(xl, mods3, lat_in_row, l, wts, LATENT_IN_TILE, GRID_W)
        spt, _ = _scan(ut, dec, s_ctx, n // GLA_CHUNK)
        xl = _mix(xl, mods3, lat_mix_row, l, qe, ki, vt, spt, ry, wts, LATENT_MIX_TILE, last)
    return xl.reshape(bsz, n, D_MODEL)
```

```python
import functools

import jax
import jax.numpy as jnp
from jax import lax
from jax.experimental import pallas as pl
from jax.experimental.pallas import tpu as pltpu

D_MODEL = 1024
GLA_HEADS = 4
GLA_DK = 64
GLA_DV = 128
SLAB = 2 * GLA_DK
KEY_COLS = GLA_HEADS * SLAB
VAL_COLS = GLA_HEADS * GLA_DV
GATE_RANK = 16
GATE_TAU = 16.0
LOG2_E = 1.4426950408889634
GLA_CHUNK = 128
SUB_ROWS = 256
LATENT_IN_TILE = 1024
LATENT_MIX_TILE = 512
CONV_WIDTH = 256
POOL_WIDTH = 256
POOL_GROUP = 64
POOL_HALF_WINDOWS = (1, 2, 4, 8)
D_FF = 4096
FF_CHUNK = 1024
N_MOD = 6
EPS = 1e-6
GRID_W = 64
RC_COLS = VAL_COLS + 3 * CONV_WIDTH + POOL_WIDTH
RY_COLS = VAL_COLS + CONV_WIDTH + POOL_WIDTH
WQ_COLS = slice(0, GLA_HEADS * GLA_DK)
WK_COLS = slice(WQ_COLS.stop, WQ_COLS.stop + GLA_HEADS * GLA_DK)
WRC_COLS = slice(WK_COLS.stop, WK_COLS.stop + RC_COLS)
WZ_COLS = slice(WRC_COLS.stop, WRC_COLS.stop + 2 * GATE_RANK)
WIN_COLS = WZ_COLS.stop
SCAN_UNROLL = 2
SCAN_HEADS = 2
MOD_ROWS = 8
ADALN_COLS = 1536
VMEM_LIMIT_BYTES = 56 * 1024 * 1024

_NT = (((1,), (1,)), ((), ()))
_BF16 = jnp.bfloat16
_F32 = jnp.float32


def _dot(a, b):
    return jnp.dot(a, b, preferred_element_type=_F32)


def _sigmoid(x):
    return 1.0 / (1.0 + jnp.exp(-x))


def _adaln_kernel(cv_ref, w_ref, b_ref, o_ref):
    cv = cv_ref[...]
    s = (cv * _sigmoid(cv)).astype(_BF16)
    o_ref[...] = _dot(s, w_ref[...].astype(_BF16)) + b_ref[...]


def _adaln(cv, w_mod, b_mod):
    depth = w_mod.shape[0]
    cols = w_mod.shape[2]
    return pl.pallas_call(
        _adaln_kernel,
        grid=(depth, cols // ADALN_COLS),
        in_specs=[
            pl.BlockSpec((MOD_ROWS, D_MODEL), lambda l, j: (0, 0)),
            pl.BlockSpec((None, D_MODEL, ADALN_COLS), lambda l, j: (l, 0, j)),
            pl.BlockSpec((None, 1, ADALN_COLS), lambda l, j: (l, 0, j)),
        ],
        out_specs=pl.BlockSpec((None, MOD_ROWS, ADALN_COLS), lambda l, j: (l, 0, j)),
        out_shape=jax.ShapeDtypeStruct((depth, MOD_ROWS, cols), _F32),
        compiler_params=pltpu.CompilerParams(vmem_limit_bytes=VMEM_LIMIT_BYTES),
    )(cv, w_mod, b_mod.reshape(depth, 1, cols))


def _per_direction(t):
    lane = lax.broadcasted_iota(jnp.int32, (1, SLAB), 1)
    first = lane < GLA_DK
    slabs = []
    for p in range(GLA_HEADS // 2):
        pair = t[:, p * SLAB:(p + 1) * SLAB]
        swapped = pltpu.roll(pair, GLA_DK, 1)
        slabs += [jnp.where(first, pair, swapped), jnp.where(first, swapped, pair)]
    return jnp.concatenate(slabs, axis=1)


def _conv_and_pool(rc, cw_ref, pw_ref, ps_ref, seg):
    rows = rc.shape[0]
    pos = lax.broadcasted_iota(jnp.int32, (rows, 1), 0) % seg
    c0 = VAL_COLS
    cb = rc[:, c0:c0 + CONV_WIDTH]
    zc = rc[:, c0 + CONV_WIDTH:c0 + 2 * CONV_WIDTH] * rc[:, c0 + 2 * CONV_WIDTH:c0 + 3 * CONV_WIDTH]
    z_prev = jnp.where(pos >= 1, pltpu.roll(zc, 1, 0), 0.0)
    z_next = jnp.where(pos <= seg - 2, pltpu.roll(zc, rows - 1, 0), 0.0)
    yc = cb * (cw_ref[0:1, :] * z_prev + cw_ref[1:2, :] * zc + cw_ref[2:3, :] * z_next)

    p0 = VAL_COLS + 3 * CONV_WIDTH
    lane128 = lax.broadcasted_iota(jnp.int32, (1, 2 * POOL_GROUP), 1)
    pooled = []
    for blk in range(POOL_WIDTH // (2 * POOL_GROUP)):
        u = rc[:, p0 + blk * 2 * POOL_GROUP:p0 + (blk + 1) * 2 * POOL_GROUP]
        h_lo, h_hi = POOL_HALF_WINDOWS[2 * blk], POOL_HALF_WINDOWS[2 * blk + 1]
        half = jnp.where(lane128 < POOL_GROUP, h_lo, h_hi)
        acc = jnp.zeros_like(u)
        for d in range(-h_hi, h_hi):
            shifted = u if d == 0 else pltpu.roll(u, (-d) % rows, 0)
            valid = (pos + d >= 0) & (pos + d < seg) & (d >= -half) & (d < half)
            acc = acc + jnp.where(valid, shifted, 0.0)
        cnt = (jnp.minimum(pos + half, seg) - jnp.maximum(pos - half, 0)).astype(_F32)
        pooled.append(acc / cnt - u)
    pool_in = jnp.concatenate(pooled, axis=1).astype(_BF16)
    return yc, _dot(pool_in, pw_ref[...]) * ps_ref[...]


def _inproj_kernel(x_ref, mod_ref, n1w_ref, win_ref, wvt_ref, gw_ref, gb_ref, cw_ref, pw_ref, ps_ref,
                   qe_ref, ki_ref, vt_ref, ut_ref, dec_ref, ry_ref, *, seg):
    tm = x_ref.shape[0]
    wq_ref = win_ref.at[:, WQ_COLS]
    wk_ref = win_ref.at[:, WK_COLS]
    wrc_ref = win_ref.at[:, WRC_COLS]
    wz_ref = win_ref.at[:, WZ_COLS]
    sh1 = mod_ref[:, 0:D_MODEL]
    scale1 = n1w_ref[...] * (1.0 + mod_ref[:, D_MODEL:2 * D_MODEL])
    row = lax.broadcasted_iota(jnp.int32, (GLA_CHUNK, GLA_CHUNK), 0)
    col = lax.broadcasted_iota(jnp.int32, (GLA_CHUNK, GLA_CHUNK), 1)
    tri = (col <= row).astype(_BF16)
    lane = lax.broadcasted_iota(jnp.int32, (1, KEY_COLS), 1)
    is_fwd = (lane % SLAB) < GLA_DK
    mid = GLA_CHUNK // 2

    n_sub = tm // SUB_ROWS
    cps = SUB_ROWS // GLA_CHUNK
    sub_rows = [slice(s * SUB_ROWS, (s + 1) * SUB_ROWS) for s in range(n_sub)]
    hbs, gates, q2s, k2s = [], [], [], []
    for s in range(n_sub):
        x = x_ref[sub_rows[s], :]
        xn = x * lax.rsqrt(jnp.mean(x * x, axis=-1, keepdims=True) + EPS)
        hbs.append((xn * scale1 + sh1).astype(_BF16))
    for s in range(n_sub):
        z = _dot(hbs[s], wz_ref[...])
        pre = _dot(z.astype(_BF16), gw_ref[...]) + gb_ref[...]
        g = (jnp.minimum(pre, 0.0) * LOG2_E - jnp.log2(1.0 + jnp.exp2(jnp.abs(pre) * (-LOG2_E)))) * (1.0 / GATE_TAU)
        g_hi = g.astype(_BF16)
        gates.append((g, g_hi, (g - g_hi.astype(_F32)).astype(_BF16)))
        q2s.append(_per_direction(_dot(hbs[s], wq_ref[...]) * (GLA_DK ** -0.5)))
        k2s.append(_per_direction(_dot(hbs[s], wk_ref[...])))

    prefixes = []
    for s in range(n_sub):
        g, g_hi, g_lo = gates[s]
        for cl in range(cps):
            lrows = slice(cl * GLA_CHUNK, (cl + 1) * GLA_CHUNK)
            prefixes.append(_dot(tri, g_hi[lrows]) + _dot(tri, g_lo[lrows]))

    vtbs = []
    for s in range(n_sub):
        vtb = lax.dot_general(wvt_ref[...], hbs[s], _NT, preferred_element_type=_F32).astype(_BF16)
        vt_ref[:, sub_rows[s]] = vtb
        vtbs.append(vtb)
        rc = _dot(hbs[s], wrc_ref[...])
        yc, yp = _conv_and_pool(rc, cw_ref, pw_ref, ps_ref, seg)
        ry_ref[sub_rows[s], 0:VAL_COLS] = rc[:, 0:VAL_COLS].astype(_BF16)
        ry_ref[sub_rows[s], VAL_COLS:VAL_COLS + CONV_WIDTH] = yc.astype(_BF16)
        ry_ref[sub_rows[s], VAL_COLS + CONV_WIDTH:RY_COLS] = yp.astype(_BF16)

    for s in range(n_sub):
        g = gates[s][0]
        for cl in range(cps):
            c = s * cps + cl
            lrows = slice(cl * GLA_CHUNK, (cl + 1) * GLA_CHUNK)
            rows = slice(c * GLA_CHUNK, (c + 1) * GLA_CHUNK)
            prefix = prefixes[c]
            total = prefix[GLA_CHUNK - 1:GLA_CHUNK, :]
            suffix = total - prefix + g[lrows]
            e = jnp.where(is_fwd, prefix, suffix)
            mvec = jnp.where(is_fwd, e[mid - 1:mid, :], e[mid:mid + 1, :])
            bm = e - mvec
            qe_ref[rows, :] = (q2s[s][lrows] * jnp.exp2(bm)).astype(_BF16)
            ki_ref[rows, :] = (k2s[s][lrows] * jnp.exp2(-bm)).astype(_BF16)
            kd = (k2s[s][lrows] * jnp.exp2(total - e)).astype(_BF16)
            dec_ref[c] = jnp.concatenate([total, mvec, jnp.zeros((6, KEY_COLS), _F32)], axis=0)
            for h in range(GLA_HEADS):
                vt_h = vtbs[s][h * GLA_DV:(h + 1) * GLA_DV, lrows]
                ut_ref[c, h] = _dot(vt_h, kd[:, h * SLAB:(h + 1) * SLAB]).astype(ut_ref.dtype)


def _inproj(x2d, mods3, mod_row_of_tile, layer, wts, tm, seg):
    tokens = x2d.shape[0]
    n_tiles = tokens // tm
    n_chunks = tokens // GLA_CHUNK
    cpt = tm // GLA_CHUNK
    assert SUB_ROWS % seg == 0
    of_layer = lambda shape: pl.BlockSpec((None,) + shape, lambda j: (layer,) + (0,) * len(shape))
    return pl.pallas_call(
        functools.partial(_inproj_kernel, seg=seg),
        grid=(n_tiles,),
        in_specs=[
            pl.BlockSpec((tm, D_MODEL), lambda j: (j, 0)),
            pl.BlockSpec((None, 1, N_MOD * D_MODEL), lambda j: (layer * MOD_ROWS + mod_row_of_tile(j), 0, 0)),
            of_layer((1, D_MODEL)),
            of_layer((D_MODEL, WIN_COLS)),
            of_layer((VAL_COLS, D_MODEL)),
            of_layer((2 * GATE_RANK, KEY_COLS)),
            of_layer((1, KEY_COLS)),
            of_layer((3, CONV_WIDTH)),
            of_layer((POOL_WIDTH, POOL_WIDTH)),
            of_layer((1, POOL_WIDTH)),
        ],
        out_specs=[
            pl.BlockSpec((tm, KEY_COLS), lambda j: (j, 0)),
            pl.BlockSpec((tm, KEY_COLS), lambda j: (j, 0)),
            pl.BlockSpec((VAL_COLS, tm), lambda j: (0, j)),
            pl.BlockSpec((cpt, GLA_HEADS, GLA_DV, SLAB), lambda j: (j, 0, 0, 0)),
            pl.BlockSpec((cpt, 8, KEY_COLS), lambda j: (j, 0, 0)),
            pl.BlockSpec((tm, RY_COLS), lambda j: (j, 0)),
        ],
        out_shape=[
            jax.ShapeDtypeStruct((tokens, KEY_COLS), _BF16),
            jax.ShapeDtypeStruct((tokens, KEY_COLS), _BF16),
            jax.ShapeDtypeStruct((VAL_COLS, tokens), _BF16),
            jax.ShapeDtypeStruct((n_chunks, GLA_HEADS, GLA_DV, SLAB), _BF16),
            jax.ShapeDtypeStruct((n_chunks, 8, KEY_COLS), _F32),
            jax.ShapeDtypeStruct((tokens, RY_COLS), _BF16),
        ],
        compiler_params=pltpu.CompilerParams(vmem_limit_bytes=VMEM_LIMIT_BYTES),
    )(x2d, mods3, wts["n1w"], wts["win"], wts["wvt"], wts["gw"], wts["gb"], wts["cw"], wts["pw"], wts["ps"])


def _scan_kernel(ut_ref, dec_ref, s0_ref, spt_ref, sfin_ref):
    nc = ut_ref.shape[0]
    lane = lax.broadcasted_iota(jnp.int32, (1, SLAB), 1)
    is_fwd = lane < GLA_DK

    def step(i, states):
        j = nc - 1 - i
        new_states = []
        for h in range(SCAN_HEADS):
            lanes = slice(h * SLAB, (h + 1) * SLAB)
            log_decay = jnp.where(is_fwd, dec_ref[i, 0:1, lanes], dec_ref[j, 0:1, lanes])
            log_mid = jnp.where(is_fwd, dec_ref[i, 1:2, lanes], dec_ref[j, 1:2, lanes])
            entering = (states[h] * jnp.exp2(log_mid)).astype(_BF16)
            spt_ref[i, h, :, 0:GLA_DK] = entering[:, 0:GLA_DK]
            spt_ref[j, h, :, GLA_DK:SLAB] = entering[:, GLA_DK:SLAB]
            inc = jnp.where(is_fwd, ut_ref[i, h], ut_ref[j, h]).astype(_F32)
            new_states.append(states[h] * jnp.exp2(log_decay) + inc)
        return tuple(new_states)

    final = lax.fori_loop(0, nc, step, tuple(s0_ref[h] for h in range(SCAN_HEADS)), unroll=SCAN_UNROLL)
    for h in range(SCAN_HEADS):
        sfin_ref[h] = final[h]


def _scan(ut, dec, s0, chunks_per_seq):
    n_chunks = ut.shape[0]
    bsz = n_chunks // chunks_per_seq
    return pl.pallas_call(
        _scan_kernel,
        grid=(bsz, GLA_HEADS // SCAN_HEADS),
        in_specs=[
            pl.BlockSpec((chunks_per_seq, SCAN_HEADS, GLA_DV, SLAB), lambda b, g: (b, g, 0, 0)),
            pl.BlockSpec((chunks_per_seq, 8, SCAN_HEADS * SLAB), lambda b, g: (b, 0, g)),
            pl.BlockSpec((None, SCAN_HEADS, GLA_DV, SLAB), lambda b, g: (b, g, 0, 0)),
        ],
        out_specs=[
            pl.BlockSpec((chunks_per_seq, SCAN_HEADS, GLA_DV, SLAB), lambda b, g: (b, g, 0, 0)),
            pl.BlockSpec((None, SCAN_HEADS, GLA_DV, SLAB), lambda b, g: (b, g, 0, 0)),
        ],
        out_shape=[
            jax.ShapeDtypeStruct((n_chunks, GLA_HEADS, GLA_DV, SLAB), _BF16),
            jax.ShapeDtypeStruct((bsz, GLA_HEADS, GLA_DV, SLAB), _F32),
        ],
        compiler_params=pltpu.CompilerParams(vmem_limit_bytes=VMEM_LIMIT_BYTES),
    )(ut, dec, s0)


def _mix_kernel(x_ref, mod_ref, n2w_ref, qe_ref, ki_ref, vt_ref, spt_ref, ry_ref, gnw_ref,
                wo_ref, w1_ref, w2_ref, fnw_ref, o_ref, y_scr, *, final):
    tm = x_ref.shape[0]
    g1 = mod_ref[:, 2 * D_MODEL:3 * D_MODEL]
    sh2 = mod_ref[:, 3 * D_MODEL:4 * D_MODEL]
    scale2 = n2w_ref[...] * (1.0 + mod_ref[:, 4 * D_MODEL:5 * D_MODEL])
    g2 = mod_ref[:, 5 * D_MODEL:6 * D_MODEL]
    row = lax.broadcasted_iota(jnp.int32, (GLA_CHUNK, GLA_CHUNK), 0)
    col = lax.broadcasted_iota(jnp.int32, (GLA_CHUNK, GLA_CHUNK), 1)
    lane = lax.broadcasted_iota(jnp.int32, (GLA_CHUNK, SLAB), 1)
    fwd_lane = lane < GLA_DK
    gnw = gnw_ref[...]

    for c in range(tm // GLA_CHUNK):
        rows = slice(c * GLA_CHUNK, (c + 1) * GLA_CHUNK)
        for h in range(GLA_HEADS):
            lanes = slice(h * SLAB, (h + 1) * SLAB)
            qe = qe_ref[rows, lanes]
            ki = ki_ref[rows, lanes]
            zero = jnp.zeros_like(ki)
            keys = jnp.concatenate([jnp.where(fwd_lane, ki, zero), jnp.where(fwd_lane, zero, ki)], axis=0)
            a2 = lax.dot_general(qe, keys, _NT, preferred_element_type=_F32)
            am = jnp.where(col <= row, a2[:, 0:GLA_CHUNK], 0.0) + jnp.where(col >= row, a2[:, GLA_CHUNK:], 0.0)
            lhs = jnp.concatenate([am.astype(_BF16), qe], axis=1)
            rhs = jnp.concatenate([vt_ref[h * GLA_DV:(h + 1) * GLA_DV, rows], spt_ref[c, h]], axis=1)
            o = lax.dot_general(lhs, rhs, _NT, preferred_element_type=_F32)
            o = o * lax.rsqrt(jnp.mean(o * o, axis=-1, keepdims=True) + EPS) * gnw
            r = ry_ref[rows, h * GLA_DV:(h + 1) * GLA_DV].astype(_F32)
            y_scr[rows, h * GLA_DV:(h + 1) * GLA_DV] = (o * (r * _sigmoid(r))).astype(_BF16)

    mixed = _dot(y_scr[...], wo_ref[0:VAL_COLS, :]) + _dot(ry_ref[:, VAL_COLS:RY_COLS], wo_ref[VAL_COLS:D_MODEL, :])
    x1 = x_ref[...] + g1 * mixed
    xn = x1 * lax.rsqrt(jnp.mean(x1 * x1, axis=-1, keepdims=True) + EPS)
    h2 = (xn * scale2 + sh2).astype(_BF16)
    acc = jnp.zeros((tm, D_MODEL), _F32)
    for f in range(D_FF // FF_CHUNK):
        cols = slice(f * FF_CHUNK, (f + 1) * FF_CHUNK)
        a = jnp.maximum(_dot(h2, w1_ref[:, cols]), 0.0)
        acc = acc + _dot((a * a).astype(_BF16), w2_ref[cols, :])
    out = x1 + g2 * acc
    if final:
        out = out * lax.rsqrt(jnp.mean(out * out, axis=-1, keepdims=True) + EPS) * fnw_ref[...]
    o_ref[...] = out


def _mix(x2d, mods3, mod_row_of_tile, layer, qe, ki, vt, spt, ry, wts, tm, final):
    tokens = x2d.shape[0]
    cpt = tm // GLA_CHUNK
    of_layer = lambda shape: pl.BlockSpec((None,) + shape, lambda j: (layer,) + (0,) * len(shape),
                                          pipeline_mode=pl.Buffered(1))
    return pl.pallas_call(
        functools.partial(_mix_kernel, final=final),
        grid=(tokens // tm,),
        in_specs=[
            pl.BlockSpec((tm, D_MODEL), lambda j: (j, 0)),
            pl.BlockSpec((None, 1, N_MOD * D_MODEL), lambda j: (layer * MOD_ROWS + mod_row_of_tile(j), 0, 0)),
            of_layer((1, D_MODEL)),
            pl.BlockSpec((tm, KEY_COLS), lambda j: (j, 0)),
            pl.BlockSpec((tm, KEY_COLS), lambda j: (j, 0)),
            pl.BlockSpec((VAL_COLS, tm), lambda j: (0, j)),
            pl.BlockSpec((cpt, GLA_HEADS, GLA_DV, SLAB), lambda j: (j, 0, 0, 0)),
            pl.BlockSpec((tm, RY_COLS), lambda j: (j, 0)),
            of_layer((1, GLA_DV)),
            of_layer((D_MODEL, D_MODEL)),
            of_layer((D_MODEL, D_FF)),
            of_layer((D_FF, D_MODEL)),
            pl.BlockSpec((1, D_MODEL), lambda j: (0, 0), pipeline_mode=pl.Buffered(1)),
        ],
        out_specs=pl.BlockSpec((tm, D_MODEL), lambda j: (j, 0)),
        out_shape=jax.ShapeDtypeStruct((tokens, D_MODEL), _F32),
        scratch_shapes=[pltpu.VMEM((tm, VAL_COLS), _BF16)],
        compiler_params=pltpu.CompilerParams(vmem_limit_bytes=VMEM_LIMIT_BYTES),
    )(x2d, mods3, wts["n2w"], qe, ki, vt, spt, ry, wts["gnw"], wts["wo"], wts["w1"], wts["w2"], wts["fnw"])


def _prepare_weights(norm1_w, norm2_w, w_in, gate_w, gate_b, gla_norm_w, conv_w, pool_w, pool_scale, w_out,
                     w_mlp1, w_mlp2, final_norm_w):
    depth = w_in.shape[0]
    key_w = GLA_HEADS * GLA_DK
    q_off, k_off, v_off = 0, key_w, 2 * key_w
    g_off = v_off + VAL_COLS
    r_off = g_off + 2 * GATE_RANK
    win = jnp.concatenate([w_in[:, :, q_off:v_off], w_in[:, :, r_off:], w_in[:, :, g_off:r_off]], axis=-1)

    zeros = jnp.zeros((depth, GATE_RANK, GLA_HEADS, GLA_DK), _F32)
    gf = gate_w[:, 0].reshape(depth, GATE_RANK, GLA_HEADS, GLA_DK)
    gb = gate_w[:, 1].reshape(depth, GATE_RANK, GLA_HEADS, GLA_DK)
    gw = jnp.concatenate([jnp.concatenate([gf, zeros], -1), jnp.concatenate([zeros, gb], -1)], 1)
    gbias = jnp.concatenate([gate_b[:, 0].reshape(depth, GLA_HEADS, GLA_DK),
                             gate_b[:, 1].reshape(depth, GLA_HEADS, GLA_DK)], -1)
    pw = jnp.zeros((depth, POOL_WIDTH, POOL_WIDTH), _F32)
    for gi in range(POOL_WIDTH // POOL_GROUP):
        sl = slice(gi * POOL_GROUP, (gi + 1) * POOL_GROUP)
        pw = pw.at[:, sl, sl].set(pool_w[:, gi])
    return {
        "n1w": norm1_w.reshape(depth, 1, D_MODEL),
        "n2w": norm2_w.reshape(depth, 1, D_MODEL),
        "win": win.astype(_BF16),
        "wvt": jnp.swapaxes(w_in[:, :, v_off:g_off], 1, 2).astype(_BF16),
        "gw": gw.reshape(depth, 2 * GATE_RANK, KEY_COLS).astype(_BF16),
        "gb": gbias.reshape(depth, 1, KEY_COLS),
        "gnw": gla_norm_w.reshape(depth, 1, GLA_DV),
        "cw": conv_w,
        "pw": pw.astype(_BF16),
        "ps": pool_scale.reshape(depth, 1, POOL_WIDTH),
        "wo": w_out.astype(_BF16),
        "w1": w_mlp1.astype(_BF16),
        "w2": w_mlp2.astype(_BF16),
        "fnw": final_norm_w.reshape(1, D_MODEL),
    }


def kernel(x, c, ctx, c_ctx, w_mod, b_mod, norm1_w, norm2_w, w_in, gla_gate_w, gla_gate_b, gla_norm_w, conv_w, pool_w, pool_scale, w_out, w_mlp1, w_mlp2, final_norm_w):
    bsz, n, _ = x.shape
    ctx_len = ctx.shape[1]
    depth = w_in.shape[0]
    assert bsz + 1 <= MOD_ROWS and n % LATENT_IN_TILE == 0 and SUB_ROWS % ctx_len == 0
    ctx_tile = bsz * ctx_len
    ctx_row = bsz

    cv = jnp.concatenate([c, c_ctx[None, :], jnp.zeros((MOD_ROWS - bsz - 1, D_MODEL), _F32)], axis=0)
    mods = _adaln(cv, w_mod, b_mod)
    mods3 = mods.reshape(depth * MOD_ROWS, 1, N_MOD * D_MODEL)
    wts = _prepare_weights(norm1_w, norm2_w, w_in, gla_gate_w, gla_gate_b, gla_norm_w, conv_w, pool_w, pool_scale,
                           w_out, w_mlp1, w_mlp2, final_norm_w)

    xl = x.reshape(bsz * n, D_MODEL)
    xc = ctx.reshape(bsz * ctx_len, D_MODEL)
    lat_in_row = lambda j: j // (n // LATENT_IN_TILE)
    lat_mix_row = lambda j: j // (n // LATENT_MIX_TILE)
    ctx_mod_row = lambda j: ctx_row
    s_zero = jnp.zeros((bsz, GLA_HEADS, GLA_DV, SLAB), _F32)

    for l in range(depth):
        last = l == depth - 1
        cqe, cki, cvt, cut, cdec, cry = _inproj(xc, mods3, ctx_mod_row, l, wts, ctx_tile, ctx_len)
        cspt, s_ctx = _scan(cut, cdec, s_zero, ctx_len // GLA_CHUNK)
        if not last:
            xc = _mix(xc, mods3, ctx_mod_row, l, cqe, cki, cvt, cspt, cry, wts, ctx_tile, False)

        qe, ki, vt, ut, dec, ry = _inproj(xl, mods3, lat_in_row, l, wts, LATENT_IN_TILE, GRID_W)
        spt, _ = _scan(ut, dec, s_ctx, n // GLA_CHUNK)
        xl = _mix(xl, mods3, lat_mix_row, l, qe, ki, vt, spt, ry, wts, LATENT_MIX_TILE, last)
    return xl.reshape(bsz, n, D_MODEL)
```

```python
import functools

import jax
import jax.numpy as jnp
from jax import lax
from jax.experimental import pallas as pl
from jax.experimental.pallas import tpu as pltpu

D_MODEL = 1024
GLA_HEADS = 4
GLA_DK = 64
GLA_DV = 128
SLAB = 2 * GLA_DK
KEY_COLS = GLA_HEADS * SLAB
VAL_COLS = GLA_HEADS * GLA_DV
GATE_RANK = 16
GATE_TAU = 16.0
LOG2_E = 1.4426950408889634
GLA_CHUNK = 128
SUB_ROWS = 256
LATENT_IN_TILE = 1024
LATENT_MIX_TILE = 512
CONV_WIDTH = 256
POOL_WIDTH = 256
POOL_GROUP = 64
POOL_HALF_WINDOWS = (1, 2, 4, 8)
D_FF = 4096
FF_CHUNK = 1024
N_MOD = 6
EPS = 1e-6
GRID_W = 64
RC_COLS = VAL_COLS + 3 * CONV_WIDTH + POOL_WIDTH
RY_COLS = VAL_COLS + CONV_WIDTH + POOL_WIDTH
WZ_COLS = slice(0, 2 * GATE_RANK)
WQ_COLS = slice(128, 128 + GLA_HEADS * GLA_DK)
WK_COLS = slice(WQ_COLS.stop, WQ_COLS.stop + GLA_HEADS * GLA_DK)
WRC_COLS = slice(WK_COLS.stop, WK_COLS.stop + RC_COLS)
WIN_COLS = WRC_COLS.stop
SCAN_UNROLL = 2
SCAN_HEADS = 2
MOD_ROWS = 8
ADALN_COLS = 1536
VMEM_LIMIT_BYTES = 56 * 1024 * 1024

_NT = (((1,), (1,)), ((), ()))
_BF16 = jnp.bfloat16
_F32 = jnp.float32


def _dot(a, b):
    return jnp.dot(a, b, preferred_element_type=_F32)


def _sigmoid(x):
    return 1.0 / (1.0 + jnp.exp(-x))


def _adaln_kernel(cv_ref, w_ref, b_ref, o_ref):
    cv = cv_ref[...]
    s = (cv * _sigmoid(cv)).astype(_BF16)
    o_ref[...] = _dot(s, w_ref[...].astype(_BF16)) + b_ref[...]


def _adaln(cv, w_mod, b_mod):
    depth = w_mod.shape[0]
    cols = w_mod.shape[2]
    return pl.pallas_call(
        _adaln_kernel,
        grid=(depth, cols // ADALN_COLS),
        in_specs=[
            pl.BlockSpec((MOD_ROWS, D_MODEL), lambda l, j: (0, 0)),
            pl.BlockSpec((None, D_MODEL, ADALN_COLS), lambda l, j: (l, 0, j)),
            pl.BlockSpec((None, 1, ADALN_COLS), lambda l, j: (l, 0, j)),
        ],
        out_specs=pl.BlockSpec((None, MOD_ROWS, ADALN_COLS), lambda l, j: (l, 0, j)),
        out_shape=jax.ShapeDtypeStruct((depth, MOD_ROWS, cols), _F32),
        compiler_params=pltpu.CompilerParams(vmem_limit_bytes=VMEM_LIMIT_BYTES),
    )(cv, w_mod, b_mod.reshape(depth, 1, cols))


def _per_direction(t):
    lane = lax.broadcasted_iota(jnp.int32, (1, SLAB), 1)
    first = lane < GLA_DK
    slabs = []
    for p in range(GLA_HEADS // 2):
        pair = t[:, p * SLAB:(p + 1) * SLAB]
        swapped = pltpu.roll(pair, GLA_DK, 1)
        slabs += [jnp.where(first, pair, swapped), jnp.where(first, swapped, pair)]
    return jnp.concatenate(slabs, axis=1)


def _conv_and_pool(rc, cw_ref, pw_ref, ps_ref, seg):
    rows = rc.shape[0]
    pos = lax.broadcasted_iota(jnp.int32, (rows, 1), 0) % seg
    c0 = 0
    cb = rc[:, c0:c0 + CONV_WIDTH]
    zc = rc[:, c0 + CONV_WIDTH:c0 + 2 * CONV_WIDTH] * rc[:, c0 + 2 * CONV_WIDTH:c0 + 3 * CONV_WIDTH]
    z_prev = jnp.where(pos >= 1, pltpu.roll(zc, 1, 0), 0.0)
    z_next = jnp.where(pos <= seg - 2, pltpu.roll(zc, rows - 1, 0), 0.0)
    yc = cb * (cw_ref[0:1, :] * z_prev + cw_ref[1:2, :] * zc + cw_ref[2:3, :] * z_next)

    p0 = 3 * CONV_WIDTH
    lane128 = lax.broadcasted_iota(jnp.int32, (1, 2 * POOL_GROUP), 1)
    pooled = []
    for blk in range(POOL_WIDTH // (2 * POOL_GROUP)):
        u = rc[:, p0 + blk * 2 * POOL_GROUP:p0 + (blk + 1) * 2 * POOL_GROUP]
        h_lo, h_hi = POOL_HALF_WINDOWS[2 * blk], POOL_HALF_WINDOWS[2 * blk + 1]
        half = jnp.where(lane128 < POOL_GROUP, h_lo, h_hi)
        acc = jnp.zeros_like(u)
        for d in range(-h_hi, h_hi):
            shifted = u if d == 0 else pltpu.roll(u, (-d) % rows, 0)
            valid = (pos + d >= 0) & (pos + d < seg) & (d >= -half) & (d < half)
            acc = acc + jnp.where(valid, shifted, 0.0)
        cnt = (jnp.minimum(pos + half, seg) - jnp.maximum(pos - half, 0)).astype(_F32)
        pooled.append(acc / cnt - u)
    pool_in = jnp.concatenate(pooled, axis=1).astype(_BF16)
    return yc, _dot(pool_in, pw_ref[...]) * ps_ref[...]


def _inproj_kernel(x_ref, mod_ref, n1w_ref, win_ref, wvt_ref, gw_ref, gb_ref, cw_ref, pw_ref, ps_ref,
                   qe_ref, ki_ref, vt_ref, ut_ref, dec_ref, ry_ref, *, seg):
    tm = x_ref.shape[0]
    sh1 = mod_ref[:, 0:D_MODEL]
    scale1 = n1w_ref[...] * (1.0 + mod_ref[:, D_MODEL:2 * D_MODEL])
    row = lax.broadcasted_iota(jnp.int32, (GLA_CHUNK, GLA_CHUNK), 0)
    col = lax.broadcasted_iota(jnp.int32, (GLA_CHUNK, GLA_CHUNK), 1)
    tri = (col <= row).astype(_BF16)
    lane = lax.broadcasted_iota(jnp.int32, (1, KEY_COLS), 1)
    is_fwd = (lane % SLAB) < GLA_DK
    mid = GLA_CHUNK // 2

    n_sub = tm // SUB_ROWS
    cps = SUB_ROWS // GLA_CHUNK
    sub_rows = [slice(s * SUB_ROWS, (s + 1) * SUB_ROWS) for s in range(n_sub)]
    hbs, projs, gates, q2s, k2s = [], [], [], [], []
    for s in range(n_sub):
        x = x_ref[sub_rows[s], :]
        xn = x * lax.rsqrt(jnp.mean(x * x, axis=-1, keepdims=True) + EPS)
        hbs.append((xn * scale1 + sh1).astype(_BF16))
    for s in range(n_sub):
        proj = _dot(hbs[s], win_ref[...])
        projs.append(proj)
        pre = _dot(proj[:, WZ_COLS].astype(_BF16), gw_ref[...]) + gb_ref[...]
        g = (jnp.minimum(pre, 0.0) * LOG2_E - jnp.log2(1.0 + jnp.exp2(jnp.abs(pre) * (-LOG2_E)))) * (1.0 / GATE_TAU)
        gates.append(g.astype(_BF16))
        q2s.append(_per_direction(proj[:, WQ_COLS] * (GLA_DK ** -0.5)))
        k2s.append(_per_direction(proj[:, WK_COLS]))

    prefixes = []
    for s in range(n_sub):
        for cl in range(cps):
            lrows = slice(cl * GLA_CHUNK, (cl + 1) * GLA_CHUNK)
            prefixes.append(_dot(tri, gates[s][lrows]))

    vtbs = []
    for s in range(n_sub):
        vtb = lax.dot_general(wvt_ref[...], hbs[s], _NT, preferred_element_type=_F32).astype(_BF16)
        vt_ref[:, sub_rows[s]] = vtb
        vtbs.append(vtb)
        rc = projs[s][:, WRC_COLS]
        yc, yp = _conv_and_pool(rc[:, VAL_COLS:RC_COLS], cw_ref, pw_ref, ps_ref, seg)
        ry_ref[sub_rows[s], 0:VAL_COLS] = rc[:, 0:VAL_COLS].astype(_BF16)
        ry_ref[sub_rows[s], VAL_COLS:VAL_COLS + CONV_WIDTH] = yc.astype(_BF16)
        ry_ref[sub_rows[s], VAL_COLS + CONV_WIDTH:RY_COLS] = yp.astype(_BF16)

    for s in range(n_sub):
        g = gates[s].astype(_F32)
        for cl in range(cps):
            c = s * cps + cl
            lrows = slice(cl * GLA_CHUNK, (cl + 1) * GLA_CHUNK)
            rows = slice(c * GLA_CHUNK, (c + 1) * GLA_CHUNK)
            prefix = prefixes[c]
            total = prefix[GLA_CHUNK - 1:GLA_CHUNK, :]
            suffix = total - prefix + g[lrows]
            e = jnp.where(is_fwd, prefix, suffix)
            mvec = jnp.where(is_fwd, e[mid - 1:mid, :], e[mid:mid + 1, :])
            bm = e - mvec
            qe_ref[rows, :] = (q2s[s][lrows] * jnp.exp2(bm)).astype(_BF16)
            ki_ref[rows, :] = (k2s[s][lrows] * jnp.exp2(-bm)).astype(_BF16)
            kd = (k2s[s][lrows] * jnp.exp2(total - e)).astype(_BF16)
            dec_ref[c] = jnp.concatenate([total, mvec, jnp.zeros((6, KEY_COLS), _F32)], axis=0)
            for h in range(GLA_HEADS):
                vt_h = vtbs[s][h * GLA_DV:(h + 1) * GLA_DV, lrows]
                ut_ref[c, h] = _dot(vt_h, kd[:, h * SLAB:(h + 1) * SLAB]).astype(ut_ref.dtype)


def _inproj(x2d, mods3, mod_row_of_tile, layer, wts, tm, seg):
    tokens = x2d.shape[0]
    n_tiles = tokens // tm
    n_chunks = tokens // GLA_CHUNK
    cpt = tm // GLA_CHUNK
    assert SUB_ROWS % seg == 0
    of_layer = lambda shape: pl.BlockSpec((None,) + shape, lambda j: (layer,) + (0,) * len(shape))
    return pl.pallas_call(
        functools.partial(_inproj_kernel, seg=seg),
        grid=(n_tiles,),
        in_specs=[
            pl.BlockSpec((tm, D_MODEL), lambda j: (j, 0)),
            pl.BlockSpec((None, 1, N_MOD * D_MODEL), lambda j: (layer * MOD_ROWS + mod_row_of_tile(j), 0, 0)),
            of_layer((1, D_MODEL)),
            of_layer((D_MODEL, WIN_COLS)),
            of_layer((VAL_COLS, D_MODEL)),
            of_layer((2 * GATE_RANK, KEY_COLS)),
            of_layer((1, KEY_COLS)),
            of_layer((3, CONV_WIDTH)),
            of_layer((POOL_WIDTH, POOL_WIDTH)),
            of_layer((1, POOL_WIDTH)),
        ],
        out_specs=[
            pl.BlockSpec((tm, KEY_COLS), lambda j: (j, 0)),
            pl.BlockSpec((tm, KEY_COLS), lambda j: (j, 0)),
            pl.BlockSpec((VAL_COLS, tm), lambda j: (0, j)),
            pl.BlockSpec((cpt, GLA_HEADS, GLA_DV, SLAB), lambda j: (j, 0, 0, 0)),
            pl.BlockSpec((cpt, 8, KEY_COLS), lambda j: (j, 0, 0)),
            pl.BlockSpec((tm, RY_COLS), lambda j: (j, 0)),
        ],
        out_shape=[
            jax.ShapeDtypeStruct((tokens, KEY_COLS), _BF16),
            jax.ShapeDtypeStruct((tokens, KEY_COLS), _BF16),
            jax.ShapeDtypeStruct((VAL_COLS, tokens), _BF16),
            jax.ShapeDtypeStruct((n_chunks, GLA_HEADS, GLA_DV, SLAB), _BF16),
            jax.ShapeDtypeStruct((n_chunks, 8, KEY_COLS), _F32),
            jax.ShapeDtypeStruct((tokens, RY_COLS), _BF16),
        ],
        compiler_params=pltpu.CompilerParams(vmem_limit_bytes=VMEM_LIMIT_BYTES),
    )(x2d, mods3, wts["n1w"], wts["win"], wts["wvt"], wts["gw"], wts["gb"], wts["cw"], wts["pw"], wts["ps"])


def _scan_kernel(ut_ref, dec_ref, s0_ref, spt_ref, sfin_ref):
    nc = ut_ref.shape[0]
    lane = lax.broadcasted_iota(jnp.int32, (1, SLAB), 1)
    is_fwd = lane < GLA_DK

    def step(i, states):
        j = nc - 1 - i
        new_states = []
        for h in range(SCAN_HEADS):
            lanes = slice(h * SLAB, (h + 1) * SLAB)
            log_decay = jnp.where(is_fwd, dec_ref[i, 0:1, lanes], dec_ref[j, 0:1, lanes])
            log_mid = jnp.where(is_fwd, dec_ref[i, 1:2, lanes], dec_ref[j, 1:2, lanes])
            entering = (states[h] * jnp.exp2(log_mid)).astype(_BF16)
            spt_ref[i, h, :, 0:GLA_DK] = entering[:, 0:GLA_DK]
            spt_ref[j, h, :, GLA_DK:SLAB] = entering[:, GLA_DK:SLAB]
            inc = jnp.where(is_fwd, ut_ref[i, h], ut_ref[j, h]).astype(_F32)
            new_states.append(states[h] * jnp.exp2(log_decay) + inc)
        return tuple(new_states)

    final = lax.fori_loop(0, nc, step, tuple(s0_ref[h] for h in range(SCAN_HEADS)), unroll=SCAN_UNROLL)
    for h in range(SCAN_HEADS):
        sfin_ref[h] = final[h]


def _scan(ut, dec, s0, chunks_per_seq):
    n_chunks = ut.shape[0]
    bsz = n_chunks // chunks_per_seq
    return pl.pallas_call(
        _scan_kernel,
        grid=(bsz, GLA_HEADS // SCAN_HEADS),
        in_specs=[
            pl.BlockSpec((chunks_per_seq, SCAN_HEADS, GLA_DV, SLAB), lambda b, g: (b, g, 0, 0)),
            pl.BlockSpec((chunks_per_seq, 8, SCAN_HEADS * SLAB), lambda b, g: (b, 0, g)),
            pl.BlockSpec((None, SCAN_HEADS, GLA_DV, SLAB), lambda b, g: (b, g, 0, 0)),
        ],
        out_specs=[
            pl.BlockSpec((chunks_per_seq, SCAN_HEADS, GLA_DV, SLAB), lambda b, g: (b, g, 0, 0)),
            pl.BlockSpec((None, SCAN_HEADS, GLA_DV, SLAB), lambda b, g: (b, g, 0, 0)),
        ],
        out_shape=[
            jax.ShapeDtypeStruct((n_chunks, GLA_HEADS, GLA_DV, SLAB), _BF16),
            jax.ShapeDtypeStruct((bsz, GLA_HEADS, GLA_DV, SLAB), _F32),
        ],
        compiler_params=pltpu.CompilerParams(vmem_limit_bytes=VMEM_LIMIT_BYTES),
    )(ut, dec, s0)


def _mix_kernel(x_ref, mod_ref, n2w_ref, qe_ref, ki_ref, vt_ref, spt_ref, ry_ref, gnw_ref,
                wo_ref, w1_ref, w2_ref, fnw_ref, o_ref, y_scr, *, final):
    tm = x_ref.shape[0]
    g1 = mod_ref[:, 2 * D_MODEL:3 * D_MODEL]
    sh2 = mod_ref[:, 3 * D_MODEL:4 * D_MODEL]
    scale2 = n2w_ref[...] * (1.0 + mod_ref[:, 4 * D_MODEL:5 * D_MODEL])
    g2 = mod_ref[:, 5 * D_MODEL:6 * D_MODEL]
    row = lax.broadcasted_iota(jnp.int32, (GLA_CHUNK, GLA_CHUNK), 0)
    col = lax.broadcasted_iota(jnp.int32, (GLA_CHUNK, GLA_CHUNK), 1)
    lane = lax.broadcasted_iota(jnp.int32, (GLA_CHUNK, SLAB), 1)
    fwd_lane = lane < GLA_DK
    gnw = gnw_ref[...]

    for c in range(tm // GLA_CHUNK):
        rows = slice(c * GLA_CHUNK, (c + 1) * GLA_CHUNK)
        for h in range(GLA_HEADS):
            lanes = slice(h * SLAB, (h + 1) * SLAB)
            qe = qe_ref[rows, lanes]
            ki = ki_ref[rows, lanes]
            zero = jnp.zeros_like(ki)
            keys = jnp.concatenate([jnp.where(fwd_lane, ki, zero), jnp.where(fwd_lane, zero, ki)], axis=0)
            a2 = lax.dot_general(qe, keys, _NT, preferred_element_type=_F32)
            am = jnp.where(col <= row, a2[:, 0:GLA_CHUNK], 0.0) + jnp.where(col >= row, a2[:, GLA_CHUNK:], 0.0)
            lhs = jnp.concatenate([am.astype(_BF16), qe], axis=1)
            rhs = jnp.concatenate([vt_ref[h * GLA_DV:(h + 1) * GLA_DV, rows], spt_ref[c, h]], axis=1)
            o = lax.dot_general(lhs, rhs, _NT, preferred_element_type=_F32)
            o = o * lax.rsqrt(jnp.mean(o * o, axis=-1, keepdims=True) + EPS) * gnw
            r = ry_ref[rows, h * GLA_DV:(h + 1) * GLA_DV].astype(_F32)
            y_scr[rows, h * GLA_DV:(h + 1) * GLA_DV] = (o * (r * _sigmoid(r))).astype(_BF16)

    mixed = _dot(y_scr[...], wo_ref[0:VAL_COLS, :]) + _dot(ry_ref[:, VAL_COLS:RY_COLS], wo_ref[VAL_COLS:D_MODEL, :])
    x1 = x_ref[...] + g1 * mixed
    xn = x1 * lax.rsqrt(jnp.mean(x1 * x1, axis=-1, keepdims=True) + EPS)
    h2 = (xn * scale2 + sh2).astype(_BF16)
    acc = jnp.zeros((tm, D_MODEL), _F32)
    for f in range(D_FF // FF_CHUNK):
        cols = slice(f * FF_CHUNK, (f + 1) * FF_CHUNK)
        a = jnp.maximum(_dot(h2, w1_ref[:, cols]), 0.0)
        acc = acc + _dot((a * a).astype(_BF16), w2_ref[cols, :])
    out = x1 + g2 * acc
    if final:
        out = out * lax.rsqrt(jnp.mean(out * out, axis=-1, keepdims=True) + EPS) * fnw_ref[...]
    o_ref[...] = out


def _mix(x2d, mods3, mod_row_of_tile, layer, qe, ki, vt, spt, ry, wts, tm, final):
    tokens = x2d.shape[0]
    cpt = tm // GLA_CHUNK
    of_layer = lambda shape: pl.BlockSpec((None,) + shape, lambda j: (layer,) + (0,) * len(shape),
                                          pipeline_mode=pl.Buffered(1))
    return pl.pallas_call(
        functools.partial(_mix_kernel, final=final),
        grid=(tokens // tm,),
        in_specs=[
            pl.BlockSpec((tm, D_MODEL), lambda j: (j, 0)),
            pl.BlockSpec((None, 1, N_MOD * D_MODEL), lambda j: (layer * MOD_ROWS + mod_row_of_tile(j), 0, 0)),
            of_layer((1, D_MODEL)),
            pl.BlockSpec((tm, KEY_COLS), lambda j: (j, 0)),
            pl.BlockSpec((tm, KEY_COLS), lambda j: (j, 0)),
            pl.BlockSpec((VAL_COLS, tm), lambda j: (0, j)),
            pl.BlockSpec((cpt, GLA_HEADS, GLA_DV, SLAB), lambda j: (j, 0, 0, 0)),
            pl.BlockSpec((tm, RY_COLS), lambda j: (j, 0)),
            of_layer((1, GLA_DV)),
            of_layer((D_MODEL, D_MODEL)),
            of_layer((D_MODEL, D_FF)),
            of_layer((D_FF, D_MODEL)),
            pl.BlockSpec((1, D_MODEL), lambda j: (0, 0), pipeline_mode=pl.Buffered(1)),
        ],
        out_specs=pl.BlockSpec((tm, D_MODEL), lambda j: (j, 0)),
        out_shape=jax.ShapeDtypeStruct((tokens, D_MODEL), _F32),
        scratch_shapes=[pltpu.VMEM((tm, VAL_COLS), _BF16)],
        compiler_params=pltpu.CompilerParams(vmem_limit_bytes=VMEM_LIMIT_BYTES),
    )(x2d, mods3, wts["n2w"], qe, ki, vt, spt, ry, wts["gnw"], wts["wo"], wts["w1"], wts["w2"], wts["fnw"])


def _prepare_weights(norm1_w, norm2_w, w_in, gate_w, gate_b, gla_norm_w, conv_w, pool_w, pool_scale, w_out,
                     w_mlp1, w_mlp2, final_norm_w):
    depth = w_in.shape[0]
    key_w = GLA_HEADS * GLA_DK
    q_off, k_off, v_off = 0, key_w, 2 * key_w
    g_off = v_off + VAL_COLS
    r_off = g_off + 2 * GATE_RANK
    z_pad = jnp.zeros((depth, D_MODEL, WQ_COLS.start - WZ_COLS.stop), _F32)
    win = jnp.concatenate([w_in[:, :, g_off:r_off], z_pad, w_in[:, :, q_off:v_off], w_in[:, :, r_off:]], axis=-1)

    zeros = jnp.zeros((depth, GATE_RANK, GLA_HEADS, GLA_DK), _F32)
    gf = gate_w[:, 0].reshape(depth, GATE_RANK, GLA_HEADS, GLA_DK)
    gb = gate_w[:, 1].reshape(depth, GATE_RANK, GLA_HEADS, GLA_DK)
    gw = jnp.concatenate([jnp.concatenate([gf, zeros], -1), jnp.concatenate([zeros, gb], -1)], 1)
    gbias = jnp.concatenate([gate_b[:, 0].reshape(depth, GLA_HEADS, GLA_DK),
                             gate_b[:, 1].reshape(depth, GLA_HEADS, GLA_DK)], -1)
    pw = jnp.zeros((depth, POOL_WIDTH, POOL_WIDTH), _F32)
    for gi in range(POOL_WIDTH // POOL_GROUP):
        sl = slice(gi * POOL_GROUP, (gi + 1) * POOL_GROUP)
        pw = pw.at[:, sl, sl].set(pool_w[:, gi])
    return {
        "n1w": norm1_w.reshape(depth, 1, D_MODEL),
        "n2w": norm2_w.reshape(depth, 1, D_MODEL),
        "win": win.astype(_BF16),
        "wvt": jnp.swapaxes(w_in[:, :, v_off:g_off], 1, 2).astype(_BF16),
        "gw": gw.reshape(depth, 2 * GATE_RANK, KEY_COLS).astype(_BF16),
        "gb": gbias.reshape(depth, 1, KEY_COLS),
        "gnw": gla_norm_w.reshape(depth, 1, GLA_DV),
        "cw": conv_w,
        "pw": pw.astype(_BF16),
        "ps": pool_scale.reshape(depth, 1, POOL_WIDTH),
        "wo": w_out.astype(_BF16),
        "w1": w_mlp1.astype(_BF16),
        "w2": w_mlp2.astype(_BF16),
        "fnw": final_norm_w.reshape(1, D_MODEL),
    }


def kernel(x, c, ctx, c_ctx, w_mod, b_mod, norm1_w, norm2_w, w_in, gla_gate_w, gla_gate_b, gla_norm_w, conv_w, pool_w, pool_scale, w_out, w_mlp1, w_mlp2, final_norm_w):
    bsz, n, _ = x.shape
    ctx_len = ctx.shape[1]
    depth = w_in.shape[0]
    assert bsz + 1 <= MOD_ROWS and n % LATENT_IN_TILE == 0 and SUB_ROWS % ctx_len == 0
    ctx_tile = bsz * ctx_len
    ctx_row = bsz

    cv = jnp.concatenate([c, c_ctx[None, :], jnp.zeros((MOD_ROWS - bsz - 1, D_MODEL), _F32)], axis=0)
    mods = _adaln(cv, w_mod, b_mod)
    mods3 = mods.reshape(depth * MOD_ROWS, 1, N_MOD * D_MODEL)
    wts = _prepare_weights(norm1_w, norm2_w, w_in, gla_gate_w, gla_gate_b, gla_norm_w, conv_w, pool_w, pool_scale,
                           w_out, w_mlp1, w_mlp2, final_norm_w)

    xl = x.reshape(bsz * n, D_MODEL)
    xc = ctx.reshape(bsz * ctx_len, D_MODEL)
    lat_in_row = lambda j: j // (n // LATENT_IN_TILE)
    lat_mix_row = lambda j: j // (n // LATENT_MIX_TILE)
    ctx_mod_row = lambda j: ctx_row
    s_zero = jnp.zeros((bsz, GLA_HEADS, GLA_DV, SLAB), _F32)

    for l in range(depth):
        last = l == depth - 1
        cqe, cki, cvt, cut, cdec, cry = _inproj(xc, mods3, ctx_mod_row, l, wts, ctx_tile, ctx_len)
        cspt, s_ctx = _scan(cut, cdec, s_zero, ctx_len // GLA_CHUNK)
        if not last:
            xc = _mix(xc, mods3, ctx_mod_row, l, cqe, cki, cvt, cspt, cry, wts, ctx_tile, False)

        qe, ki, vt, ut, dec, ry = _inproj(xl, mods3, lat_in_row, l, wts, LATENT_IN_TILE, GRID_W)
        spt, _ = _scan(ut, dec, s_ctx, n // GLA_CHUNK)
        xl = _mix(xl, mods3, lat_mix_row, l, qe, ki, vt, spt, ry, wts, LATENT_MIX_TILE, last)
    return xl.reshape(bsz, n, D_MODEL)
```

```python
import functools

import jax
import jax.numpy as jnp
from jax import lax
from jax.experimental import pallas as pl
from jax.experimental.pallas import tpu as pltpu

D_MODEL = 1024
GLA_HEADS = 4
GLA_DK = 64
GLA_DV = 128
SLAB = 2 * GLA_DK
KEY_COLS = GLA_HEADS * SLAB
VAL_COLS = GLA_HEADS * GLA_DV
GATE_RANK = 16
GATE_TAU = 16.0
LOG2_E = 1.4426950408889634
GLA_CHUNK = 128
SUB_ROWS = 256
LATENT_IN_TILE = 1024
LATENT_MIX_TILE = 512
CONV_WIDTH = 256
POOL_WIDTH = 256
POOL_GROUP = 64
POOL_HALF_WINDOWS = (1, 2, 4, 8)
D_FF = 4096
FF_CHUNK = 1024
N_MOD = 6
EPS = 1e-6
GRID_W = 64
RC_COLS = VAL_COLS + 3 * CONV_WIDTH + POOL_WIDTH
RY_COLS = VAL_COLS + CONV_WIDTH + POOL_WIDTH
WZ_COLS = slice(0, 2 * GATE_RANK)
WQ_COLS = slice(128, 128 + GLA_HEADS * GLA_DK)
WK_COLS = slice(WQ_COLS.stop, WQ_COLS.stop + GLA_HEADS * GLA_DK)
WRC_COLS = slice(WK_COLS.stop, WK_COLS.stop + RC_COLS)
WIN_COLS = WRC_COLS.stop
SCAN_UNROLL = 2
SCAN_HEADS = 2
MOD_ROWS = 8
ADALN_COLS = 3072
VMEM_LIMIT_BYTES = 56 * 1024 * 1024

_NT = (((1,), (1,)), ((), ()))
_BF16 = jnp.bfloat16
_F32 = jnp.float32


def _dot(a, b):
    return jnp.dot(a, b, preferred_element_type=_F32)


def _sigmoid(x):
    return 1.0 / (1.0 + jnp.exp(-x))


def _adaln_kernel(cv_ref, w_ref, b_ref, o_ref):
    cv = cv_ref[...]
    s = (cv * _sigmoid(cv)).astype(_BF16)
    o_ref[...] = _dot(s, w_ref[...].astype(_BF16)) + b_ref[...]


def _adaln(cv, w_mod, b_mod):
    depth = w_mod.shape[0]
    cols = w_mod.shape[2]
    return pl.pallas_call(
        _adaln_kernel,
        grid=(depth, cols // ADALN_COLS),
        in_specs=[
            pl.BlockSpec((MOD_ROWS, D_MODEL), lambda l, j: (0, 0)),
            pl.BlockSpec((None, D_MODEL, ADALN_COLS), lambda l, j: (l, 0, j)),
            pl.BlockSpec((None, 1, ADALN_COLS), lambda l, j: (l, 0, j)),
        ],
        out_specs=pl.BlockSpec((None, MOD_ROWS, ADALN_COLS), lambda l, j: (l, 0, j)),
        out_shape=jax.ShapeDtypeStruct((depth, MOD_ROWS, cols), _F32),
        compiler_params=pltpu.CompilerParams(vmem_limit_bytes=VMEM_LIMIT_BYTES),
    )(cv, w_mod, b_mod.reshape(depth, 1, cols))


def _per_direction(t):
    lane = lax.broadcasted_iota(jnp.int32, (1, SLAB), 1)
    first = lane < GLA_DK
    slabs = []
    for p in range(GLA_HEADS // 2):
        pair = t[:, p * SLAB:(p + 1) * SLAB]
        swapped = pltpu.roll(pair, GLA_DK, 1)
        slabs += [jnp.where(first, pair, swapped), jnp.where(first, swapped, pair)]
    return jnp.concatenate(slabs, axis=1)


def _conv_and_pool(rc, cw_ref, pw_ref, ps_ref, seg):
    rows = rc.shape[0]
    pos = lax.broadcasted_iota(jnp.int32, (rows, 1), 0) % seg
    c0 = 0
    cb = rc[:, c0:c0 + CONV_WIDTH]
    zc = rc[:, c0 + CONV_WIDTH:c0 + 2 * CONV_WIDTH] * rc[:, c0 + 2 * CONV_WIDTH:c0 + 3 * CONV_WIDTH]
    z_prev = jnp.where(pos >= 1, pltpu.roll(zc, 1, 0), 0.0)
    z_next = jnp.where(pos <= seg - 2, pltpu.roll(zc, rows - 1, 0), 0.0)
    yc = cb * (cw_ref[0:1, :] * z_prev + cw_ref[1:2, :] * zc + cw_ref[2:3, :] * z_next)

    p0 = 3 * CONV_WIDTH
    lane128 = lax.broadcasted_iota(jnp.int32, (1, 2 * POOL_GROUP), 1)
    pooled = []
    for blk in range(POOL_WIDTH // (2 * POOL_GROUP)):
        u = rc[:, p0 + blk * 2 * POOL_GROUP:p0 + (blk + 1) * 2 * POOL_GROUP]
        h_lo, h_hi = POOL_HALF_WINDOWS[2 * blk], POOL_HALF_WINDOWS[2 * blk + 1]
        half = jnp.where(lane128 < POOL_GROUP, h_lo, h_hi)
        acc = jnp.zeros_like(u)
        for d in range(-h_hi, h_hi):
            shifted = u if d == 0 else pltpu.roll(u, (-d) % rows, 0)
            valid = (pos + d >= 0) & (pos + d < seg) & (d >= -half) & (d < half)
            acc = acc + jnp.where(valid, shifted, 0.0)
        cnt = (jnp.minimum(pos + half, seg) - jnp.maximum(pos - half, 0)).astype(_F32)
        pooled.append(acc / cnt - u)
    pool_in = jnp.concatenate(pooled, axis=1).astype(_BF16)
    return yc, _dot(pool_in, pw_ref[...]) * ps_ref[...]


def _inproj_kernel(x_ref, mod_ref, n1w_ref, win_ref, wvt_ref, gw_ref, gb_ref, cw_ref, pw_ref, ps_ref,
                   qe_ref, ki_ref, vt_ref, ut_ref, dec_ref, ry_ref, *, seg):
    vt_tile = vt_ref.shape[2]
    tm = x_ref.shape[0]
    sh1 = mod_ref[:, 0:D_MODEL]
    scale1 = n1w_ref[...] * (1.0 + mod_ref[:, D_MODEL:2 * D_MODEL])
    row = lax.broadcasted_iota(jnp.int32, (GLA_CHUNK, GLA_CHUNK), 0)
    col = lax.broadcasted_iota(jnp.int32, (GLA_CHUNK, GLA_CHUNK), 1)
    tri = (col <= row).astype(_BF16)
    lane = lax.broadcasted_iota(jnp.int32, (1, KEY_COLS), 1)
    is_fwd = (lane % SLAB) < GLA_DK
    mid = GLA_CHUNK // 2

    n_sub = tm // SUB_ROWS
    cps = SUB_ROWS // GLA_CHUNK
    sub_rows = [slice(s * SUB_ROWS, (s + 1) * SUB_ROWS) for s in range(n_sub)]
    hbs, projs, gates, q2s, k2s = [], [], [], [], []
    for s in range(n_sub):
        x = x_ref[sub_rows[s], :]
        xn = x * lax.rsqrt(jnp.mean(x * x, axis=-1, keepdims=True) + EPS)
        hbs.append((xn * scale1 + sh1).astype(_BF16))
    for s in range(n_sub):
        proj = _dot(hbs[s], win_ref[...])
        projs.append(proj)
        pre = _dot(proj[:, WZ_COLS].astype(_BF16), gw_ref[...]) + gb_ref[...]
        g = (jnp.minimum(pre, 0.0) * LOG2_E - jnp.log2(1.0 + jnp.exp2(jnp.abs(pre) * (-LOG2_E)))) * (1.0 / GATE_TAU)
        gates.append(g.astype(_BF16))
        q2s.append(_per_direction(proj[:, WQ_COLS] * (GLA_DK ** -0.5)))
        k2s.append(_per_direction(proj[:, WK_COLS]))

    prefixes = []
    for s in range(n_sub):
        for cl in range(cps):
            lrows = slice(cl * GLA_CHUNK, (cl + 1) * GLA_CHUNK)
            prefixes.append(_dot(tri, gates[s][lrows]))

    vtbs = []
    for s in range(n_sub):
        vtb = lax.dot_general(wvt_ref[...], hbs[s], _NT, preferred_element_type=_F32).astype(_BF16)
        first = s * SUB_ROWS
        vt_ref[first // vt_tile, :, first % vt_tile:first % vt_tile + SUB_ROWS] = vtb
        vtbs.append(vtb)
        rc = projs[s][:, WRC_COLS]
        yc, yp = _conv_and_pool(rc[:, VAL_COLS:RC_COLS], cw_ref, pw_ref, ps_ref, seg)
        ry_ref[sub_rows[s], 0:VAL_COLS] = rc[:, 0:VAL_COLS].astype(_BF16)
        ry_ref[sub_rows[s], VAL_COLS:VAL_COLS + CONV_WIDTH] = yc.astype(_BF16)
        ry_ref[sub_rows[s], VAL_COLS + CONV_WIDTH:RY_COLS] = yp.astype(_BF16)

    for s in range(n_sub):
        g = gates[s].astype(_F32)
        for cl in range(cps):
            c = s * cps + cl
            lrows = slice(cl * GLA_CHUNK, (cl + 1) * GLA_CHUNK)
            rows = slice(c * GLA_CHUNK, (c + 1) * GLA_CHUNK)
            prefix = prefixes[c]
            total = prefix[GLA_CHUNK - 1:GLA_CHUNK, :]
            suffix = total - prefix + g[lrows]
            e = jnp.where(is_fwd, prefix, suffix)
            mvec = jnp.where(is_fwd, e[mid - 1:mid, :], e[mid:mid + 1, :])
            bm = e - mvec
            qe_ref[rows, :] = (q2s[s][lrows] * jnp.exp2(bm)).astype(_BF16)
            ki_ref[rows, :] = (k2s[s][lrows] * jnp.exp2(-bm)).astype(_BF16)
            kd = (k2s[s][lrows] * jnp.exp2(total - e)).astype(_BF16)
            dec_ref[c] = jnp.concatenate([total, mvec, jnp.zeros((6, KEY_COLS), _F32)], axis=0)
            for h in range(GLA_HEADS):
                vt_h = vtbs[s][h * GLA_DV:(h + 1) * GLA_DV, lrows]
                ut_ref[c, h] = _dot(vt_h, kd[:, h * SLAB:(h + 1) * SLAB]).astype(ut_ref.dtype)


def _inproj(x2d, mods3, mod_row_of_tile, layer, wts, tm, seg, vt_tile):
    tokens = x2d.shape[0]
    n_tiles = tokens // tm
    n_chunks = tokens // GLA_CHUNK
    cpt = tm // GLA_CHUNK
    assert SUB_ROWS % seg == 0
    assert tm % vt_tile == 0 and vt_tile % SUB_ROWS == 0
    of_layer = lambda shape: pl.BlockSpec((None,) + shape, lambda j: (layer,) + (0,) * len(shape))
    return pl.pallas_call(
        functools.partial(_inproj_kernel, seg=seg),
        grid=(n_tiles,),
        in_specs=[
            pl.BlockSpec((tm, D_MODEL), lambda j: (j, 0)),
            pl.BlockSpec((None, 1, N_MOD * D_MODEL), lambda j: (layer * MOD_ROWS + mod_row_of_tile(j), 0, 0)),
            of_layer((1, D_MODEL)),
            of_layer((D_MODEL, WIN_COLS)),
            of_layer((VAL_COLS, D_MODEL)),
            of_layer((2 * GATE_RANK, KEY_COLS)),
            of_layer((1, KEY_COLS)),
            of_layer((3, CONV_WIDTH)),
            of_layer((POOL_WIDTH, POOL_WIDTH)),
            of_layer((1, POOL_WIDTH)),
        ],
        out_specs=[
            pl.BlockSpec((tm, KEY_COLS), lambda j: (j, 0)),
            pl.BlockSpec((tm, KEY_COLS), lambda j: (j, 0)),
            pl.BlockSpec((tm // vt_tile, VAL_COLS, vt_tile), lambda j: (j, 0, 0)),
            pl.BlockSpec((cpt, GLA_HEADS, GLA_DV, SLAB), lambda j: (j, 0, 0, 0)),
            pl.BlockSpec((cpt, 8, KEY_COLS), lambda j: (j, 0, 0)),
            pl.BlockSpec((tm, RY_COLS), lambda j: (j, 0)),
        ],
        out_shape=[
            jax.ShapeDtypeStruct((tokens, KEY_COLS), _BF16),
            jax.ShapeDtypeStruct((tokens, KEY_COLS), _BF16),
            jax.ShapeDtypeStruct((tokens // vt_tile, VAL_COLS, vt_tile), _BF16),
            jax.ShapeDtypeStruct((n_chunks, GLA_HEADS, GLA_DV, SLAB), _BF16),
            jax.ShapeDtypeStruct((n_chunks, 8, KEY_COLS), _F32),
            jax.ShapeDtypeStruct((tokens, RY_COLS), _BF16),
        ],
        compiler_params=pltpu.CompilerParams(vmem_limit_bytes=VMEM_LIMIT_BYTES),
    )(x2d, mods3, wts["n1w"], wts["win"], wts["wvt"], wts["gw"], wts["gb"], wts["cw"], wts["pw"], wts["ps"])


def _scan_kernel(ut_ref, dec_ref, s0_ref, spt_ref, sfin_ref):
    nc = ut_ref.shape[0]
    lane = lax.broadcasted_iota(jnp.int32, (1, SLAB), 1)
    is_fwd = lane < GLA_DK

    def step(i, states):
        j = nc - 1 - i
        new_states = []
        for h in range(SCAN_HEADS):
            lanes = slice(h * SLAB, (h + 1) * SLAB)
            log_decay = jnp.where(is_fwd, dec_ref[i, 0:1, lanes], dec_ref[j, 0:1, lanes])
            log_mid = jnp.where(is_fwd, dec_ref[i, 1:2, lanes], dec_ref[j, 1:2, lanes])
            entering = (states[h] * jnp.exp2(log_mid)).astype(_BF16)
            spt_ref[i, h, :, 0:GLA_DK] = entering[:, 0:GLA_DK]
            spt_ref[j, h, :, GLA_DK:SLAB] = entering[:, GLA_DK:SLAB]
            inc = jnp.where(is_fwd, ut_ref[i, h], ut_ref[j, h]).astype(_F32)
            new_states.append(states[h] * jnp.exp2(log_decay) + inc)
        return tuple(new_states)

    final = lax.fori_loop(0, nc, step, tuple(s0_ref[h] for h in range(SCAN_HEADS)), unroll=SCAN_UNROLL)
    for h in range(SCAN_HEADS):
        sfin_ref[h] = final[h]


def _scan(ut, dec, s0, chunks_per_seq):
    n_chunks = ut.shape[0]
    bsz = n_chunks // chunks_per_seq
    return pl.pallas_call(
        _scan_kernel,
        grid=(bsz, GLA_HEADS // SCAN_HEADS),
        in_specs=[
            pl.BlockSpec((chunks_per_seq, SCAN_HEADS, GLA_DV, SLAB), lambda b, g: (b, g, 0, 0)),
            pl.BlockSpec((chunks_per_seq, 8, SCAN_HEADS * SLAB), lambda b, g: (b, 0, g)),
            pl.BlockSpec((None, SCAN_HEADS, GLA_DV, SLAB), lambda b, g: (b, g, 0, 0)),
        ],
        out_specs=[
            pl.BlockSpec((chunks_per_seq, SCAN_HEADS, GLA_DV, SLAB), lambda b, g: (b, g, 0, 0)),
            pl.BlockSpec((None, SCAN_HEADS, GLA_DV, SLAB), lambda b, g: (b, g, 0, 0)),
        ],
        out_shape=[
            jax.ShapeDtypeStruct((n_chunks, GLA_HEADS, GLA_DV, SLAB), _BF16),
            jax.ShapeDtypeStruct((bsz, GLA_HEADS, GLA_DV, SLAB), _F32),
        ],
        compiler_params=pltpu.CompilerParams(vmem_limit_bytes=VMEM_LIMIT_BYTES),
    )(ut, dec, s0)


def _mix_kernel(x_ref, mod_ref, n2w_ref, qe_ref, ki_ref, vt_ref, spt_ref, ry_ref, gnw_ref,
                wo_ref, w1_ref, w2_ref, fnw_ref, o_ref, y_scr, *, final):
    tm = x_ref.shape[0]
    g1 = mod_ref[:, 2 * D_MODEL:3 * D_MODEL]
    sh2 = mod_ref[:, 3 * D_MODEL:4 * D_MODEL]
    scale2 = n2w_ref[...] * (1.0 + mod_ref[:, 4 * D_MODEL:5 * D_MODEL])
    g2 = mod_ref[:, 5 * D_MODEL:6 * D_MODEL]
    row = lax.broadcasted_iota(jnp.int32, (GLA_CHUNK, GLA_CHUNK), 0)
    col = lax.broadcasted_iota(jnp.int32, (GLA_CHUNK, GLA_CHUNK), 1)
    lane = lax.broadcasted_iota(jnp.int32, (GLA_CHUNK, SLAB), 1)
    fwd_lane = lane < GLA_DK
    gnw = gnw_ref[...]

    for c in range(tm // GLA_CHUNK):
        rows = slice(c * GLA_CHUNK, (c + 1) * GLA_CHUNK)
        for h in range(GLA_HEADS):
            lanes = slice(h * SLAB, (h + 1) * SLAB)
            qe = qe_ref[rows, lanes]
            ki = ki_ref[rows, lanes]
            zero = jnp.zeros_like(ki)
            keys = jnp.concatenate([jnp.where(fwd_lane, ki, zero), jnp.where(fwd_lane, zero, ki)], axis=0)
            a2 = lax.dot_general(qe, keys, _NT, preferred_element_type=_F32)
            am = jnp.where(col <= row, a2[:, 0:GLA_CHUNK], 0.0) + jnp.where(col >= row, a2[:, GLA_CHUNK:], 0.0)
            lhs = jnp.concatenate([am.astype(_BF16), qe], axis=1)
            rhs = jnp.concatenate([vt_ref[h * GLA_DV:(h + 1) * GLA_DV, rows], spt_ref[c, h]], axis=1)
            o = lax.dot_general(lhs, rhs, _NT, preferred_element_type=_F32)
            o = o * lax.rsqrt(jnp.mean(o * o, axis=-1, keepdims=True) + EPS) * gnw
            r = ry_ref[rows, h * GLA_DV:(h + 1) * GLA_DV].astype(_F32)
            y_scr[rows, h * GLA_DV:(h + 1) * GLA_DV] = (o * (r * _sigmoid(r))).astype(_BF16)

    mixed = _dot(y_scr[...], wo_ref[0:VAL_COLS, :]) + _dot(ry_ref[:, VAL_COLS:RY_COLS], wo_ref[VAL_COLS:D_MODEL, :])
    x1 = x_ref[...] + g1 * mixed
    xn = x1 * lax.rsqrt(jnp.mean(x1 * x1, axis=-1, keepdims=True) + EPS)
    h2 = (xn * scale2 + sh2).astype(_BF16)
    acc = jnp.zeros((tm, D_MODEL), _F32)
    for f in range(D_FF // FF_CHUNK):
        cols = slice(f * FF_CHUNK, (f + 1) * FF_CHUNK)
        a = jnp.maximum(_dot(h2, w1_ref[:, cols]), 0.0)
        acc = acc + _dot((a * a).astype(_BF16), w2_ref[cols, :])
    out = x1 + g2 * acc
    if final:
        out = out * lax.rsqrt(jnp.mean(out * out, axis=-1, keepdims=True) + EPS) * fnw_ref[...]
    o_ref[...] = out


def _mix(x2d, mods3, mod_row_of_tile, layer, qe, ki, vt, spt, ry, wts, tm, final):
    tokens = x2d.shape[0]
    cpt = tm // GLA_CHUNK
    of_layer = lambda shape: pl.BlockSpec((None,) + shape, lambda j: (layer,) + (0,) * len(shape),
                                          pipeline_mode=pl.Buffered(1))
    return pl.pallas_call(
        functools.partial(_mix_kernel, final=final),
        grid=(tokens // tm,),
        in_specs=[
            pl.BlockSpec((tm, D_MODEL), lambda j: (j, 0)),
            pl.BlockSpec((None, 1, N_MOD * D_MODEL), lambda j: (layer * MOD_ROWS + mod_row_of_tile(j), 0, 0)),
            of_layer((1, D_MODEL)),
            pl.BlockSpec((tm, KEY_COLS), lambda j: (j, 0)),
            pl.BlockSpec((tm, KEY_COLS), lambda j: (j, 0)),
            pl.BlockSpec((None, VAL_COLS, tm), lambda j: (j, 0, 0)),
            pl.BlockSpec((cpt, GLA_HEADS, GLA_DV, SLAB), lambda j: (j, 0, 0, 0)),
            pl.BlockSpec((tm, RY_COLS), lambda j: (j, 0)),
            of_layer((1, GLA_DV)),
            of_layer((D_MODEL, D_MODEL)),
            of_layer((D_MODEL, D_FF)),
            of_layer((D_FF, D_MODEL)),
            pl.BlockSpec((1, D_MODEL), lambda j: (0, 0), pipeline_mode=pl.Buffered(1)),
        ],
        out_specs=pl.BlockSpec((tm, D_MODEL), lambda j: (j, 0)),
        out_shape=jax.ShapeDtypeStruct((tokens, D_MODEL), _F32),
        scratch_shapes=[pltpu.VMEM((tm, VAL_COLS), _BF16)],
        compiler_params=pltpu.CompilerParams(vmem_limit_bytes=VMEM_LIMIT_BYTES),
    )(x2d, mods3, wts["n2w"], qe, ki, vt, spt, ry, wts["gnw"], wts["wo"], wts["w1"], wts["w2"], wts["fnw"])


def _prepare_weights(norm1_w, norm2_w, w_in, gate_w, gate_b, gla_norm_w, conv_w, pool_w, pool_scale, w_out,
                     w_mlp1, w_mlp2, final_norm_w):
    depth = w_in.shape[0]
    key_w = GLA_HEADS * GLA_DK
    q_off, k_off, v_off = 0, key_w, 2 * key_w
    g_off = v_off + VAL_COLS
    r_off = g_off + 2 * GATE_RANK
    z_pad = jnp.zeros((depth, D_MODEL, WQ_COLS.start - WZ_COLS.stop), _F32)
    win = jnp.concatenate([w_in[:, :, g_off:r_off], z_pad, w_in[:, :, q_off:v_off], w_in[:, :, r_off:]], axis=-1)

    zeros = jnp.zeros((depth, GATE_RANK, GLA_HEADS, GLA_DK), _F32)
    gf = gate_w[:, 0].reshape(depth, GATE_RANK, GLA_HEADS, GLA_DK)
    gb = gate_w[:, 1].reshape(depth, GATE_RANK, GLA_HEADS, GLA_DK)
    gw = jnp.concatenate([jnp.concatenate([gf, zeros], -1), jnp.concatenate([zeros, gb], -1)], 1)
    gbias = jnp.concatenate([gate_b[:, 0].reshape(depth, GLA_HEADS, GLA_DK),
                             gate_b[:, 1].reshape(depth, GLA_HEADS, GLA_DK)], -1)
    pw = jnp.zeros((depth, POOL_WIDTH, POOL_WIDTH), _F32)
    for gi in range(POOL_WIDTH // POOL_GROUP):
        sl = slice(gi * POOL_GROUP, (gi + 1) * POOL_GROUP)
        pw = pw.at[:, sl, sl].set(pool_w[:, gi])
    return {
        "n1w": norm1_w.reshape(depth, 1, D_MODEL),
        "n2w": norm2_w.reshape(depth, 1, D_MODEL),
        "win": win.astype(_BF16),
        "wvt": jnp.swapaxes(w_in[:, :, v_off:g_off], 1, 2).astype(_BF16),
        "gw": gw.reshape(depth, 2 * GATE_RANK, KEY_COLS).astype(_BF16),
        "gb": gbias.reshape(depth, 1, KEY_COLS),
        "gnw": gla_norm_w.reshape(depth, 1, GLA_DV),
        "cw": conv_w,
        "pw": pw.astype(_BF16),
        "ps": pool_scale.reshape(depth, 1, POOL_WIDTH),
        "wo": w_out.astype(_BF16),
        "w1": w_mlp1.astype(_BF16),
        "w2": w_mlp2.astype(_BF16),
        "fnw": final_norm_w.reshape(1, D_MODEL),
    }


def kernel(x, c, ctx, c_ctx, w_mod, b_mod, norm1_w, norm2_w, w_in, gla_gate_w, gla_gate_b, gla_norm_w, conv_w, pool_w, pool_scale, w_out, w_mlp1, w_mlp2, final_norm_w):
    bsz, n, _ = x.shape
    ctx_len = ctx.shape[1]
    depth = w_in.shape[0]
    assert bsz + 1 <= MOD_ROWS and n % LATENT_IN_TILE == 0 and SUB_ROWS % ctx_len == 0
    ctx_tile = bsz * ctx_len
    ctx_row = bsz

    cv = jnp.concatenate([c, c_ctx[None, :], jnp.zeros((MOD_ROWS - bsz - 1, D_MODEL), _F32)], axis=0)
    mods = _adaln(cv, w_mod, b_mod)
    mods3 = mods.reshape(depth * MOD_ROWS, 1, N_MOD * D_MODEL)
    wts = _prepare_weights(norm1_w, norm2_w, w_in, gla_gate_w, gla_gate_b, gla_norm_w, conv_w, pool_w, pool_scale,
                           w_out, w_mlp1, w_mlp2, final_norm_w)

    xl = x.reshape(bsz * n, D_MODEL)
    xc = ctx.reshape(bsz * ctx_len, D_MODEL)
    lat_in_row = lambda j: j // (n // LATENT_IN_TILE)
    lat_mix_row = lambda j: j // (n // LATENT_MIX_TILE)
    ctx_mod_row = lambda j: ctx_row
    s_zero = jnp.zeros((bsz, GLA_HEADS, GLA_DV, SLAB), _F32)

    for l in range(depth):
        last = l == depth - 1
        cqe, cki, cvt, cut, cdec, cry = _inproj(xc, mods3, ctx_mod_row, l, wts, ctx_tile, ctx_len, ctx_tile)
        cspt, s_ctx = _scan(cut, cdec, s_zero, ctx_len // GLA_CHUNK)
        if not last:
            xc = _mix(xc, mods3, ctx_mod_row, l, cqe, cki, cvt, cspt, cry, wts, ctx_tile, False)

        qe, ki, vt, ut, dec, ry = _inproj(xl, mods3, lat_in_row, l, wts, LATENT_IN_TILE, GRID_W, LATENT_MIX_TILE)
        spt, _ = _scan(ut, dec, s_ctx, n // GLA_CHUNK)
        xl = _mix(xl, mods3, lat_mix_row, l, qe, ki, vt, spt, ry, wts, LATENT_MIX_TILE, last)
    return xl.reshape(bsz, n, D_MODEL)
```

```python
import functools

import jax
import jax.numpy as jnp
from jax import lax
from jax.experimental import pallas as pl
from jax.experimental.pallas import tpu as pltpu

D_MODEL = 1024
GLA_HEADS = 4
GLA_DK = 64
GLA_DV = 128
SLAB = 2 * GLA_DK
KEY_COLS = GLA_HEADS * SLAB
VAL_COLS = GLA_HEADS * GLA_DV
GATE_RANK = 16
GATE_TAU = 16.0
LOG2_E = 1.4426950408889634
GLA_CHUNK = 128
SUB_ROWS = 256
LATENT_IN_TILE = 1024
LATENT_MIX_TILE = 512
CONV_WIDTH = 256
POOL_WIDTH = 256
POOL_GROUP = 64
POOL_HALF_WINDOWS = (1, 2, 4, 8)
D_FF = 4096
FF_CHUNK = 1024
N_MOD = 6
EPS = 1e-6
GRID_W = 64
RC_COLS = VAL_COLS + 3 * CONV_WIDTH + POOL_WIDTH
RY_COLS = VAL_COLS + CONV_WIDTH + POOL_WIDTH
TOK_QE = slice(0, KEY_COLS)
TOK_KI = slice(KEY_COLS, 2 * KEY_COLS)
TOK_RY = slice(2 * KEY_COLS, 2 * KEY_COLS + RY_COLS)
TOK_COLS = TOK_RY.stop
WZ_COLS = slice(0, 2 * GATE_RANK)
WQ_COLS = slice(128, 128 + GLA_HEADS * GLA_DK)
WK_COLS = slice(WQ_COLS.stop, WQ_COLS.stop + GLA_HEADS * GLA_DK)
WRC_COLS = slice(WK_COLS.stop, WK_COLS.stop + RC_COLS)
WIN_COLS = WRC_COLS.stop
SCAN_UNROLL = 2
SCAN_HEADS = 2
MOD_ROWS = 8
ADALN_COLS = 1536
VMEM_LIMIT_BYTES = 56 * 1024 * 1024

_NT = (((1,), (1,)), ((), ()))
_BF16 = jnp.bfloat16
_F32 = jnp.float32


def _dot(a, b):
    return jnp.dot(a, b, preferred_element_type=_F32)


def _sigmoid(x):
    return 1.0 / (1.0 + jnp.exp(-x))


def _adaln_kernel(cv_ref, w_ref, b_ref, o_ref):
    cv = cv_ref[...]
    s = (cv * _sigmoid(cv)).astype(_BF16)
    o_ref[...] = _dot(s, w_ref[...].astype(_BF16)) + b_ref[...]


def _adaln(cv, w_mod, b_mod):
    depth = w_mod.shape[0]
    cols = w_mod.shape[2]
    return pl.pallas_call(
        _adaln_kernel,
        grid=(depth, cols // ADALN_COLS),
        in_specs=[
            pl.BlockSpec((MOD_ROWS, D_MODEL), lambda l, j: (0, 0)),
            pl.BlockSpec((None, D_MODEL, ADALN_COLS), lambda l, j: (l, 0, j)),
            pl.BlockSpec((None, 1, ADALN_COLS), lambda l, j: (l, 0, j)),
        ],
        out_specs=pl.BlockSpec((None, MOD_ROWS, ADALN_COLS), lambda l, j: (l, 0, j)),
        out_shape=jax.ShapeDtypeStruct((depth, MOD_ROWS, cols), _F32),
        compiler_params=pltpu.CompilerParams(vmem_limit_bytes=VMEM_LIMIT_BYTES),
    )(cv, w_mod, b_mod.reshape(depth, 1, cols))


def _per_direction(t):
    lane = lax.broadcasted_iota(jnp.int32, (1, SLAB), 1)
    first = lane < GLA_DK
    slabs = []
    for p in range(GLA_HEADS // 2):
        pair = t[:, p * SLAB:(p + 1) * SLAB]
        swapped = pltpu.roll(pair, GLA_DK, 1)
        slabs += [jnp.where(first, pair, swapped), jnp.where(first, swapped, pair)]
    return jnp.concatenate(slabs, axis=1)


def _conv_and_pool(rc, cw_ref, pw_ref, ps_ref, seg):
    rows = rc.shape[0]
    pos = lax.broadcasted_iota(jnp.int32, (rows, 1), 0) % seg
    c0 = 0
    cb = rc[:, c0:c0 + CONV_WIDTH]
    zc = rc[:, c0 + CONV_WIDTH:c0 + 2 * CONV_WIDTH] * rc[:, c0 + 2 * CONV_WIDTH:c0 + 3 * CONV_WIDTH]
    z_prev = jnp.where(pos >= 1, pltpu.roll(zc, 1, 0), 0.0)
    z_next = jnp.where(pos <= seg - 2, pltpu.roll(zc, rows - 1, 0), 0.0)
    yc = cb * (cw_ref[0:1, :] * z_prev + cw_ref[1:2, :] * zc + cw_ref[2:3, :] * z_next)

    p0 = 3 * CONV_WIDTH
    lane128 = lax.broadcasted_iota(jnp.int32, (1, 2 * POOL_GROUP), 1)
    pooled = []
    for blk in range(POOL_WIDTH // (2 * POOL_GROUP)):
        u = rc[:, p0 + blk * 2 * POOL_GROUP:p0 + (blk + 1) * 2 * POOL_GROUP]
        h_lo, h_hi = POOL_HALF_WINDOWS[2 * blk], POOL_HALF_WINDOWS[2 * blk + 1]
        half = jnp.where(lane128 < POOL_GROUP, h_lo, h_hi)
        acc = jnp.zeros_like(u)
        for d in range(-h_hi, h_hi):
            shifted = u if d == 0 else pltpu.roll(u, (-d) % rows, 0)
            valid = (pos + d >= 0) & (pos + d < seg) & (d >= -half) & (d < half)
            acc = acc + jnp.where(valid, shifted, 0.0)
        cnt = (jnp.minimum(pos + half, seg) - jnp.maximum(pos - half, 0)).astype(_F32)
        pooled.append(acc / cnt - u)
    pool_in = jnp.concatenate(pooled, axis=1).astype(_BF16)
    return yc, _dot(pool_in, pw_ref[...]) * ps_ref[...]


def _inproj_kernel(x_ref, mod_ref, n1w_ref, win_ref, wvt_ref, gw_ref, gb_ref, cw_ref, pw_ref, ps_ref,
                   tok_ref, vt_ref, ut_ref, dec_ref, *, seg):
    qe_ref, ki_ref, ry_ref = tok_ref.at[:, TOK_QE], tok_ref.at[:, TOK_KI], tok_ref.at[:, TOK_RY]
    vt_tile = vt_ref.shape[2]
    tm = x_ref.shape[0]
    sh1 = mod_ref[:, 0:D_MODEL]
    scale1 = n1w_ref[...] * (1.0 + mod_ref[:, D_MODEL:2 * D_MODEL])
    row = lax.broadcasted_iota(jnp.int32, (GLA_CHUNK, GLA_CHUNK), 0)
    col = lax.broadcasted_iota(jnp.int32, (GLA_CHUNK, GLA_CHUNK), 1)
    tri = (col <= row).astype(_BF16)
    lane = lax.broadcasted_iota(jnp.int32, (1, KEY_COLS), 1)
    is_fwd = (lane % SLAB) < GLA_DK
    mid = GLA_CHUNK // 2

    n_sub = tm // SUB_ROWS
    cps = SUB_ROWS // GLA_CHUNK
    sub_rows = [slice(s * SUB_ROWS, (s + 1) * SUB_ROWS) for s in range(n_sub)]
    hbs, projs, gates, q2s, k2s = [], [], [], [], []
    for s in range(n_sub):
        x = x_ref[sub_rows[s], :]
        xn = x * lax.rsqrt(jnp.mean(x * x, axis=-1, keepdims=True) + EPS)
        hbs.append((xn * scale1 + sh1).astype(_BF16))
    for s in range(n_sub):
        proj = _dot(hbs[s], win_ref[...])
        projs.append(proj)
        pre = _dot(proj[:, WZ_COLS].astype(_BF16), gw_ref[...]) + gb_ref[...]
        g = (jnp.minimum(pre, 0.0) * LOG2_E - jnp.log2(1.0 + jnp.exp2(jnp.abs(pre) * (-LOG2_E)))) * (1.0 / GATE_TAU)
        gates.append(g.astype(_BF16))
        q2s.append(_per_direction(proj[:, WQ_COLS] * (GLA_DK ** -0.5)))
        k2s.append(_per_direction(proj[:, WK_COLS]))

    prefixes = []
    for s in range(n_sub):
        for cl in range(cps):
            lrows = slice(cl * GLA_CHUNK, (cl + 1) * GLA_CHUNK)
            prefixes.append(_dot(tri, gates[s][lrows]))

    vtbs = []
    for s in range(n_sub):
        vtb = lax.dot_general(wvt_ref[...], hbs[s], _NT, preferred_element_type=_F32).astype(_BF16)
        first = s * SUB_ROWS
        vt_ref[first // vt_tile, :, first % vt_tile:first % vt_tile + SUB_ROWS] = vtb
        vtbs.append(vtb)
        rc = projs[s][:, WRC_COLS]
        yc, yp = _conv_and_pool(rc[:, VAL_COLS:RC_COLS], cw_ref, pw_ref, ps_ref, seg)
        ry_ref[sub_rows[s], 0:VAL_COLS] = rc[:, 0:VAL_COLS].astype(_BF16)
        ry_ref[sub_rows[s], VAL_COLS:VAL_COLS + CONV_WIDTH] = yc.astype(_BF16)
        ry_ref[sub_rows[s], VAL_COLS + CONV_WIDTH:RY_COLS] = yp.astype(_BF16)

    for s in range(n_sub):
        g = gates[s].astype(_F32)
        for cl in range(cps):
            c = s * cps + cl
            lrows = slice(cl * GLA_CHUNK, (cl + 1) * GLA_CHUNK)
            rows = slice(c * GLA_CHUNK, (c + 1) * GLA_CHUNK)
            prefix = prefixes[c]
            total = prefix[GLA_CHUNK - 1:GLA_CHUNK, :]
            suffix = total - prefix + g[lrows]
            e = jnp.where(is_fwd, prefix, suffix)
            mvec = jnp.where(is_fwd, e[mid - 1:mid, :], e[mid:mid + 1, :])
            bm = e - mvec
            qe_ref[rows, :] = (q2s[s][lrows] * jnp.exp2(bm)).astype(_BF16)
            ki_ref[rows, :] = (k2s[s][lrows] * jnp.exp2(-bm)).astype(_BF16)
            kd = (k2s[s][lrows] * jnp.exp2(total - e)).astype(_BF16)
            dec_ref[c] = jnp.concatenate([total, mvec, jnp.zeros((6, KEY_COLS), _F32)], axis=0)
            for h in range(GLA_HEADS):
                vt_h = vtbs[s][h * GLA_DV:(h + 1) * GLA_DV, lrows]
                ut_ref[c, h] = _dot(vt_h, kd[:, h * SLAB:(h + 1) * SLAB]).astype(ut_ref.dtype)


def _inproj(x2d, mods3, mod_row_of_tile, layer, wts, tm, seg, vt_tile):
    tokens = x2d.shape[0]
    n_tiles = tokens // tm
    n_chunks = tokens // GLA_CHUNK
    cpt = tm // GLA_CHUNK
    assert SUB_ROWS % seg == 0
    assert tm % vt_tile == 0 and vt_tile % SUB_ROWS == 0
    of_layer = lambda shape: pl.BlockSpec((None,) + shape, lambda j: (layer,) + (0,) * len(shape))
    return pl.pallas_call(
        functools.partial(_inproj_kernel, seg=seg),
        grid=(n_tiles,),
        in_specs=[
            pl.BlockSpec((tm, D_MODEL), lambda j: (j, 0)),
            pl.BlockSpec((None, 1, N_MOD * D_MODEL), lambda j: (layer * MOD_ROWS + mod_row_of_tile(j), 0, 0)),
            of_layer((1, D_MODEL)),
            of_layer((D_MODEL, WIN_COLS)),
            of_layer((VAL_COLS, D_MODEL)),
            of_layer((2 * GATE_RANK, KEY_COLS)),
            of_layer((1, KEY_COLS)),
            of_layer((3, CONV_WIDTH)),
            of_layer((POOL_WIDTH, POOL_WIDTH)),
            of_layer((1, POOL_WIDTH)),
        ],
        out_specs=[
            pl.BlockSpec((tm, TOK_COLS), lambda j: (j, 0)),
            pl.BlockSpec((tm // vt_tile, VAL_COLS, vt_tile), lambda j: (j, 0, 0)),
            pl.BlockSpec((cpt, GLA_HEADS, GLA_DV, SLAB), lambda j: (j, 0, 0, 0)),
            pl.BlockSpec((cpt, 8, KEY_COLS), lambda j: (j, 0, 0)),
        ],
        out_shape=[
            jax.ShapeDtypeStruct((tokens, TOK_COLS), _BF16),
            jax.ShapeDtypeStruct((tokens // vt_tile, VAL_COLS, vt_tile), _BF16),
            jax.ShapeDtypeStruct((n_chunks, GLA_HEADS, GLA_DV, SLAB), _BF16),
            jax.ShapeDtypeStruct((n_chunks, 8, KEY_COLS), _F32),
        ],
        compiler_params=pltpu.CompilerParams(vmem_limit_bytes=VMEM_LIMIT_BYTES),
    )(x2d, mods3, wts["n1w"], wts["win"], wts["wvt"], wts["gw"], wts["gb"], wts["cw"], wts["pw"], wts["ps"])


def _scan_kernel(ut_ref, dec_ref, s0_ref, spt_ref, sfin_ref):
    nc = ut_ref.shape[0]
    lane = lax.broadcasted_iota(jnp.int32, (1, SLAB), 1)
    is_fwd = lane < GLA_DK

    def step(i, states):
        j = nc - 1 - i
        new_states = []
        for h in range(SCAN_HEADS):
            lanes = slice(h * SLAB, (h + 1) * SLAB)
            log_decay = jnp.where(is_fwd, dec_ref[i, 0:1, lanes], dec_ref[j, 0:1, lanes])
            log_mid = jnp.where(is_fwd, dec_ref[i, 1:2, lanes], dec_ref[j, 1:2, lanes])
            entering = (states[h] * jnp.exp2(log_mid)).astype(_BF16)
            spt_ref[i, h, :, 0:GLA_DK] = entering[:, 0:GLA_DK]
            spt_ref[j, h, :, GLA_DK:SLAB] = entering[:, GLA_DK:SLAB]
            inc = jnp.where(is_fwd, ut_ref[i, h], ut_ref[j, h]).astype(_F32)
            new_states.append(states[h] * jnp.exp2(log_decay) + inc)
        return tuple(new_states)

    final = lax.fori_loop(0, nc, step, tuple(s0_ref[h] for h in range(SCAN_HEADS)), unroll=SCAN_UNROLL)
    for h in range(SCAN_HEADS):
        sfin_ref[h] = final[h]


def _scan(ut, dec, s0, chunks_per_seq):
    n_chunks = ut.shape[0]
    bsz = n_chunks // chunks_per_seq
    return pl.pallas_call(
        _scan_kernel,
        grid=(bsz, GLA_HEADS // SCAN_HEADS),
        in_specs=[
            pl.BlockSpec((chunks_per_seq, SCAN_HEADS, GLA_DV, SLAB), lambda b, g: (b, g, 0, 0)),
            pl.BlockSpec((chunks_per_seq, 8, SCAN_HEADS * SLAB), lambda b, g: (b, 0, g)),
            pl.BlockSpec((None, SCAN_HEADS, GLA_DV, SLAB), lambda b, g: (b, g, 0, 0)),
        ],
        out_specs=[
            pl.BlockSpec((chunks_per_seq, SCAN_HEADS, GLA_DV, SLAB), lambda b, g: (b, g, 0, 0)),
            pl.BlockSpec((None, SCAN_HEADS, GLA_DV, SLAB), lambda b, g: (b, g, 0, 0)),
        ],
        out_shape=[
            jax.ShapeDtypeStruct((n_chunks, GLA_HEADS, GLA_DV, SLAB), _BF16),
            jax.ShapeDtypeStruct((bsz, GLA_HEADS, GLA_DV, SLAB), _F32),
        ],
        compiler_params=pltpu.CompilerParams(vmem_limit_bytes=VMEM_LIMIT_BYTES),
    )(ut, dec, s0)


def _mix_kernel(x_ref, mod_ref, n2w_ref, tok_ref, vt_ref, spt_ref, gnw_ref,
                wo_ref, w1_ref, w2_ref, fnw_ref, o_ref, y_scr, *, final):
    tm = x_ref.shape[0]
    qe_ref, ki_ref, ry_ref = tok_ref.at[:, TOK_QE], tok_ref.at[:, TOK_KI], tok_ref.at[:, TOK_RY]
    g1 = mod_ref[:, 2 * D_MODEL:3 * D_MODEL]
    sh2 = mod_ref[:, 3 * D_MODEL:4 * D_MODEL]
    scale2 = n2w_ref[...] * (1.0 + mod_ref[:, 4 * D_MODEL:5 * D_MODEL])
    g2 = mod_ref[:, 5 * D_MODEL:6 * D_MODEL]
    row = lax.broadcasted_iota(jnp.int32, (GLA_CHUNK, GLA_CHUNK), 0)
    col = lax.broadcasted_iota(jnp.int32, (GLA_CHUNK, GLA_CHUNK), 1)
    lane = lax.broadcasted_iota(jnp.int32, (GLA_CHUNK, SLAB), 1)
    fwd_lane = lane < GLA_DK
    gnw = gnw_ref[...]

    for c in range(tm // GLA_CHUNK):
        rows = slice(c * GLA_CHUNK, (c + 1) * GLA_CHUNK)
        for h in range(GLA_HEADS):
            lanes = slice(h * SLAB, (h + 1) * SLAB)
            qe = qe_ref[rows, lanes]
            ki = ki_ref[rows, lanes]
            zero = jnp.zeros_like(ki)
            keys = jnp.concatenate([jnp.where(fwd_lane, ki, zero), jnp.where(fwd_lane, zero, ki)], axis=0)
            a2 = lax.dot_general(qe, keys, _NT, preferred_element_type=_F32)
            am = jnp.where(col <= row, a2[:, 0:GLA_CHUNK], 0.0) + jnp.where(col >= row, a2[:, GLA_CHUNK:], 0.0)
            lhs = jnp.concatenate([am.astype(_BF16), qe], axis=1)
            rhs = jnp.concatenate([vt_ref[h * GLA_DV:(h + 1) * GLA_DV, rows], spt_ref[c, h]], axis=1)
            o = lax.dot_general(lhs, rhs, _NT, preferred_element_type=_F32)
            o = o * lax.rsqrt(jnp.mean(o * o, axis=-1, keepdims=True) + EPS) * gnw
            r = ry_ref[rows, h * GLA_DV:(h + 1) * GLA_DV].astype(_F32)
            y_scr[rows, h * GLA_DV:(h + 1) * GLA_DV] = (o * (r * _sigmoid(r))).astype(_BF16)

    mixed = _dot(y_scr[...], wo_ref[0:VAL_COLS, :]) + _dot(ry_ref[:, VAL_COLS:RY_COLS], wo_ref[VAL_COLS:D_MODEL, :])
    x1 = x_ref[...] + g1 * mixed
    xn = x1 * lax.rsqrt(jnp.mean(x1 * x1, axis=-1, keepdims=True) + EPS)
    h2 = (xn * scale2 + sh2).astype(_BF16)
    acc = jnp.zeros((tm, D_MODEL), _F32)
    for f in range(D_FF // FF_CHUNK):
        cols = slice(f * FF_CHUNK, (f + 1) * FF_CHUNK)
        a = jnp.maximum(_dot(h2, w1_ref[:, cols]), 0.0)
        acc = acc + _dot((a * a).astype(_BF16), w2_ref[cols, :])
    out = x1 + g2 * acc
    if final:
        out = out * lax.rsqrt(jnp.mean(out * out, axis=-1, keepdims=True) + EPS) * fnw_ref[...]
    o_ref[...] = out


def _mix(x2d, mods3, mod_row_of_tile, layer, tok, vt, spt, wts, tm, final):
    tokens = x2d.shape[0]
    cpt = tm // GLA_CHUNK
    of_layer = lambda shape: pl.BlockSpec((None,) + shape, lambda j: (layer,) + (0,) * len(shape),
                                          pipeline_mode=pl.Buffered(1))
    return pl.pallas_call(
        functools.partial(_mix_kernel, final=final),
        grid=(tokens // tm,),
        in_specs=[
            pl.BlockSpec((tm, D_MODEL), lambda j: (j, 0)),
            pl.BlockSpec((None, 1, N_MOD * D_MODEL), lambda j: (layer * MOD_ROWS + mod_row_of_tile(j), 0, 0)),
            of_layer((1, D_MODEL)),
            pl.BlockSpec((tm, TOK_COLS), lambda j: (j, 0)),
            pl.BlockSpec((None, VAL_COLS, tm), lambda j: (j, 0, 0)),
            pl.BlockSpec((cpt, GLA_HEADS, GLA_DV, SLAB), lambda j: (j, 0, 0, 0)),
            of_layer((1, GLA_DV)),
            of_layer((D_MODEL, D_MODEL)),
            of_layer((D_MODEL, D_FF)),
            of_layer((D_FF, D_MODEL)),
            pl.BlockSpec((1, D_MODEL), lambda j: (0, 0), pipeline_mode=pl.Buffered(1)),
        ],
        out_specs=pl.BlockSpec((tm, D_MODEL), lambda j: (j, 0)),
        out_shape=jax.ShapeDtypeStruct((tokens, D_MODEL), _F32),
        scratch_shapes=[pltpu.VMEM((tm, VAL_COLS), _BF16)],
        compiler_params=pltpu.CompilerParams(vmem_limit_bytes=VMEM_LIMIT_BYTES),
    )(x2d, mods3, wts["n2w"], tok, vt, spt, wts["gnw"], wts["wo"], wts["w1"], wts["w2"], wts["fnw"])


def _prepare_weights(norm1_w, norm2_w, w_in, gate_w, gate_b, gla_norm_w, conv_w, pool_w, pool_scale, w_out,
                     w_mlp1, w_mlp2, final_norm_w):
    depth = w_in.shape[0]
    key_w = GLA_HEADS * GLA_DK
    q_off, k_off, v_off = 0, key_w, 2 * key_w
    g_off = v_off + VAL_COLS
    r_off = g_off + 2 * GATE_RANK
    z_pad = jnp.zeros((depth, D_MODEL, WQ_COLS.start - WZ_COLS.stop), _F32)
    win = jnp.concatenate([w_in[:, :, g_off:r_off], z_pad, w_in[:, :, q_off:v_off], w_in[:, :, r_off:]], axis=-1)

    zeros = jnp.zeros((depth, GATE_RANK, GLA_HEADS, GLA_DK), _F32)
    gf = gate_w[:, 0].reshape(depth, GATE_RANK, GLA_HEADS, GLA_DK)
    gb = gate_w[:, 1].reshape(depth, GATE_RANK, GLA_HEADS, GLA_DK)
    gw = jnp.concatenate([jnp.concatenate([gf, zeros], -1), jnp.concatenate([zeros, gb], -1)], 1)
    gbias = jnp.concatenate([gate_b[:, 0].reshape(depth, GLA_HEADS, GLA_DK),
                             gate_b[:, 1].reshape(depth, GLA_HEADS, GLA_DK)], -1)
    pw = jnp.zeros((depth, POOL_WIDTH, POOL_WIDTH), _F32)
    for gi in range(POOL_WIDTH // POOL_GROUP):
        sl = slice(gi * POOL_GROUP, (gi + 1) * POOL_GROUP)
        pw = pw.at[:, sl, sl].set(pool_w[:, gi])
    return {
        "n1w": norm1_w.reshape(depth, 1, D_MODEL),
        "n2w": norm2_w.reshape(depth, 1, D_MODEL),
        "win": win.astype(_BF16),
        "wvt": jnp.swapaxes(w_in[:, :, v_off:g_off], 1, 2).astype(_BF16),
        "gw": gw.reshape(depth, 2 * GATE_RANK, KEY_COLS).astype(_BF16),
        "gb": gbias.reshape(depth, 1, KEY_COLS),
        "gnw": gla_norm_w.reshape(depth, 1, GLA_DV),
        "cw": conv_w,
        "pw": pw.astype(_BF16),
        "ps": pool_scale.reshape(depth, 1, POOL_WIDTH),
        "wo": w_out.astype(_BF16),
        "w1": w_mlp1.astype(_BF16),
        "w2": w_mlp2.astype(_BF16),
        "fnw": final_norm_w.reshape(1, D_MODEL),
    }


def kernel(x, c, ctx, c_ctx, w_mod, b_mod, norm1_w, norm2_w, w_in, gla_gate_w, gla_gate_b, gla_norm_w, conv_w, pool_w, pool_scale, w_out, w_mlp1, w_mlp2, final_norm_w):
    bsz, n, _ = x.shape
    ctx_len = ctx.shape[1]
    depth = w_in.shape[0]
    assert bsz + 1 <= MOD_ROWS and n % LATENT_IN_TILE == 0 and SUB_ROWS % ctx_len == 0
    ctx_tile = bsz * ctx_len
    ctx_row = bsz

    cv = jnp.concatenate([c, c_ctx[None, :], jnp.zeros((MOD_ROWS - bsz - 1, D_MODEL), _F32)], axis=0)
    mods = _adaln(cv, w_mod, b_mod)
    mods3 = mods.reshape(depth * MOD_ROWS, 1, N_MOD * D_MODEL)
    wts = _prepare_weights(norm1_w, norm2_w, w_in, gla_gate_w, gla_gate_b, gla_norm_w, conv_w, pool_w, pool_scale,
                           w_out, w_mlp1, w_mlp2, final_norm_w)

    xl = x.reshape(bsz * n, D_MODEL)
    xc = ctx.reshape(bsz * ctx_len, D_MODEL)
    lat_in_row = lambda j: j // (n // LATENT_IN_TILE)
    lat_mix_row = lambda j: j // (n // LATENT_MIX_TILE)
    ctx_mod_row = lambda j: ctx_row
    s_zero = jnp.zeros((bsz, GLA_HEADS, GLA_DV, SLAB), _F32)

    for l in range(depth):
        last = l == depth - 1
        ctok, cvt, cut, cdec = _inproj(xc, mods3, ctx_mod_row, l, wts, ctx_tile, ctx_len, ctx_tile)
        cspt, s_ctx = _scan(cut, cdec, s_zero, ctx_len // GLA_CHUNK)
        if not last:
            xc = _mix(xc, mods3, ctx_mod_row, l, ctok, cvt, cspt, wts, ctx_tile, False)

        tok, vt, ut, dec = _inproj(xl, mods3, lat_in_row, l, wts, LATENT_IN_TILE, GRID_W, LATENT_MIX_TILE)
        spt, _ = _scan(ut, dec, s_ctx, n // GLA_CHUNK)
        xl = _mix(xl, mods3, lat_mix_row, l, tok, vt, spt, wts, LATENT_MIX_TILE, last)
    return xl.reshape(bsz, n, D_MODEL)
```

```python
import functools

import jax
import jax.numpy as jnp
from jax import lax
from jax.experimental import pallas as pl
from jax.experimental.pallas import tpu as pltpu

D_MODEL = 1024
GLA_HEADS = 4
GLA_DK = 64
GLA_DV = 128
SLAB = 2 * GLA_DK
KEY_COLS = GLA_HEADS * SLAB
VAL_COLS = GLA_HEADS * GLA_DV
GATE_RANK = 16
GATE_TAU = 16.0
LOG2_E = 1.4426950408889634
GLA_CHUNK = 128
SUB_ROWS = 256
LATENT_IN_TILE = 1024
LATENT_MIX_TILE = 512
CONV_WIDTH = 256
POOL_WIDTH = 256
POOL_GROUP = 64
POOL_HALF_WINDOWS = (1, 2, 4, 8)
D_FF = 4096
FF_CHUNK = 1024
N_MOD = 6
EPS = 1e-6
GRID_W = 64
RC_COLS = VAL_COLS + 3 * CONV_WIDTH + POOL_WIDTH
RY_COLS = VAL_COLS + CONV_WIDTH + POOL_WIDTH
TOK_QE = slice(0, KEY_COLS)
TOK_KI = slice(KEY_COLS, 2 * KEY_COLS)
TOK_RY = slice(2 * KEY_COLS, 2 * KEY_COLS + RY_COLS)
TOK_COLS = TOK_RY.stop
WZ_COLS = slice(0, 2 * GATE_RANK)
WQ_COLS = slice(128, 128 + GLA_HEADS * GLA_DK)
WK_COLS = slice(WQ_COLS.stop, WQ_COLS.stop + GLA_HEADS * GLA_DK)
WRC_COLS = slice(WK_COLS.stop, WK_COLS.stop + RC_COLS)
WIN_COLS = WRC_COLS.stop
SCAN_UNROLL = 2
SCAN_HEADS = 2
MOD_ROWS = 8
ADALN_COLS = 1536
VMEM_LIMIT_BYTES = 56 * 1024 * 1024

_NT = (((1,), (1,)), ((), ()))
_BF16 = jnp.bfloat16
_F32 = jnp.float32


def _dot(a, b):
    return jnp.dot(a, b, preferred_element_type=_F32)


def _sigmoid(x):
    return 1.0 / (1.0 + jnp.exp(-x))


def _adaln_kernel(cv_ref, w_ref, b_ref, o_ref):
    cv = cv_ref[...]
    s = (cv * _sigmoid(cv)).astype(_BF16)
    o_ref[...] = _dot(s, w_ref[...].astype(_BF16)) + b_ref[...]


def _adaln(cv, w_mod, b_mod):
    depth = w_mod.shape[0]
    cols = w_mod.shape[2]
    return pl.pallas_call(
        _adaln_kernel,
        grid=(depth, cols // ADALN_COLS),
        in_specs=[
            pl.BlockSpec((MOD_ROWS, D_MODEL), lambda l, j: (0, 0)),
            pl.BlockSpec((None, D_MODEL, ADALN_COLS), lambda l, j: (l, 0, j)),
            pl.BlockSpec((None, 1, ADALN_COLS), lambda l, j: (l, 0, j)),
        ],
        out_specs=pl.BlockSpec((None, MOD_ROWS, ADALN_COLS), lambda l, j: (l, 0, j)),
        out_shape=jax.ShapeDtypeStruct((depth, MOD_ROWS, cols), _F32),
        compiler_params=pltpu.CompilerParams(vmem_limit_bytes=VMEM_LIMIT_BYTES),
    )(cv, w_mod, b_mod.reshape(depth, 1, cols))


def _per_direction(t):
    lane = lax.broadcasted_iota(jnp.int32, (1, SLAB), 1)
    first = lane < GLA_DK
    slabs = []
    for p in range(GLA_HEADS // 2):
        pair = t[:, p * SLAB:(p + 1) * SLAB]
        swapped = pltpu.roll(pair, GLA_DK, 1)
        slabs += [jnp.where(first, pair, swapped), jnp.where(first, swapped, pair)]
    return jnp.concatenate(slabs, axis=1)


def _conv_and_pool(rc, cw_ref, pw_ref, ps_ref, seg):
    rows = rc.shape[0]
    pos = lax.broadcasted_iota(jnp.int32, (rows, 1), 0) % seg
    c0 = 0
    cb = rc[:, c0:c0 + CONV_WIDTH]
    zc = rc[:, c0 + CONV_WIDTH:c0 + 2 * CONV_WIDTH] * rc[:, c0 + 2 * CONV_WIDTH:c0 + 3 * CONV_WIDTH]
    z_prev = jnp.where(pos >= 1, pltpu.roll(zc, 1, 0), 0.0)
    z_next = jnp.where(pos <= seg - 2, pltpu.roll(zc, rows - 1, 0), 0.0)
    yc = cb * (cw_ref[0:1, :] * z_prev + cw_ref[1:2, :] * zc + cw_ref[2:3, :] * z_next)

    p0 = 3 * CONV_WIDTH
    lane128 = lax.broadcasted_iota(jnp.int32, (1, 2 * POOL_GROUP), 1)
    pooled = []
    for blk in range(POOL_WIDTH // (2 * POOL_GROUP)):
        u = rc[:, p0 + blk * 2 * POOL_GROUP:p0 + (blk + 1) * 2 * POOL_GROUP]
        h_lo, h_hi = POOL_HALF_WINDOWS[2 * blk], POOL_HALF_WINDOWS[2 * blk + 1]
        half = jnp.where(lane128 < POOL_GROUP, h_lo, h_hi)
        acc = jnp.zeros_like(u)
        for d in range(-h_hi, h_hi):
            shifted = u if d == 0 else pltpu.roll(u, (-d) % rows, 0)
            valid = (pos + d >= 0) & (pos + d < seg) & (d >= -half) & (d < half)
            acc = acc + jnp.where(valid, shifted, 0.0)
        cnt = (jnp.minimum(pos + half, seg) - jnp.maximum(pos - half, 0)).astype(_F32)
        pooled.append(acc / cnt - u)
    pool_in = jnp.concatenate(pooled, axis=1).astype(_BF16)
    return yc, _dot(pool_in, pw_ref[...]) * ps_ref[...]


def _inproj_kernel(x_ref, mod_ref, n1w_ref, win_ref, wvt_ref, gw_ref, gb_ref, cw_ref, pw_ref, ps_ref,
                   tok_ref, vt_ref, ut_ref, dec_ref, *, seg):
    qe_ref, ki_ref, ry_ref = tok_ref.at[:, TOK_QE], tok_ref.at[:, TOK_KI], tok_ref.at[:, TOK_RY]
    vt_tile = vt_ref.shape[2]
    tm = x_ref.shape[0]
    sh1 = mod_ref[:, 0:D_MODEL]
    scale1 = n1w_ref[...] * (1.0 + mod_ref[:, D_MODEL:2 * D_MODEL])
    row = lax.broadcasted_iota(jnp.int32, (GLA_CHUNK, GLA_CHUNK), 0)
    col = lax.broadcasted_iota(jnp.int32, (GLA_CHUNK, GLA_CHUNK), 1)
    tri = (col <= row).astype(_BF16)
    lane = lax.broadcasted_iota(jnp.int32, (1, KEY_COLS), 1)
    is_fwd = (lane % SLAB) < GLA_DK
    mid = GLA_CHUNK // 2

    n_sub = tm // SUB_ROWS
    cps = SUB_ROWS // GLA_CHUNK
    sub_rows = [slice(s * SUB_ROWS, (s + 1) * SUB_ROWS) for s in range(n_sub)]
    hbs, projs, gates, q2s, k2s = [], [], [], [], []
    for s in range(n_sub):
        x = x_ref[sub_rows[s], :]
        xn = x * lax.rsqrt(jnp.mean(x * x, axis=-1, keepdims=True) + EPS)
        hbs.append((xn * scale1 + sh1).astype(_BF16))
    for s in range(n_sub):
        proj = _dot(hbs[s], win_ref[...])
        projs.append(proj)
        pre = _dot(proj[:, WZ_COLS].astype(_BF16), gw_ref[...]) + gb_ref[...]
        g = (jnp.minimum(pre, 0.0) * LOG2_E - jnp.log2(1.0 + jnp.exp2(jnp.abs(pre) * (-LOG2_E)))) * (1.0 / GATE_TAU)
        gates.append(g.astype(_BF16))
        q2s.append(_per_direction(proj[:, WQ_COLS] * (GLA_DK ** -0.5)))
        k2s.append(_per_direction(proj[:, WK_COLS]))

    prefixes = []
    for s in range(n_sub):
        for cl in range(cps):
            lrows = slice(cl * GLA_CHUNK, (cl + 1) * GLA_CHUNK)
            prefixes.append(_dot(tri, gates[s][lrows]))

    vtbs = []
    for s in range(n_sub):
        vtb = lax.dot_general(wvt_ref[...], hbs[s], _NT, preferred_element_type=_F32).astype(_BF16)
        first = s * SUB_ROWS
        vt_ref[first // vt_tile, :, first % vt_tile:first % vt_tile + SUB_ROWS] = vtb
        vtbs.append(vtb)
        rc = projs[s][:, WRC_COLS]
        yc, yp = _conv_and_pool(rc[:, VAL_COLS:RC_COLS], cw_ref, pw_ref, ps_ref, seg)
        ry_ref[sub_rows[s], 0:VAL_COLS] = rc[:, 0:VAL_COLS].astype(_BF16)
        ry_ref[sub_rows[s], VAL_COLS:VAL_COLS + CONV_WIDTH] = yc.astype(_BF16)
        ry_ref[sub_rows[s], VAL_COLS + CONV_WIDTH:RY_COLS] = yp.astype(_BF16)

    for s in range(n_sub):
        g = gates[s].astype(_F32)
        for cl in range(cps):
            c = s * cps + cl
            lrows = slice(cl * GLA_CHUNK, (cl + 1) * GLA_CHUNK)
            rows = slice(c * GLA_CHUNK, (c + 1) * GLA_CHUNK)
            prefix = prefixes[c]
            total = prefix[GLA_CHUNK - 1:GLA_CHUNK, :]
            suffix = total - prefix + g[lrows]
            e = jnp.where(is_fwd, prefix, suffix)
            mvec = jnp.where(is_fwd, e[mid - 1:mid, :], e[mid:mid + 1, :])
            bm = e - mvec
            qe_ref[rows, :] = (q2s[s][lrows] * jnp.exp2(bm)).astype(_BF16)
            ki_ref[rows, :] = (k2s[s][lrows] * jnp.exp2(-bm)).astype(_BF16)
            kd = (k2s[s][lrows] * jnp.exp2(total - e)).astype(_BF16)
            dec_ref[c] = jnp.concatenate([total, mvec, jnp.zeros((6, KEY_COLS), _F32)], axis=0)
            for h in range(GLA_HEADS):
                vt_h = vtbs[s][h * GLA_DV:(h + 1) * GLA_DV, lrows]
                ut_ref[c, h] = _dot(vt_h, kd[:, h * SLAB:(h + 1) * SLAB]).astype(ut_ref.dtype)


def _inproj(x2d, mods3, mod_row_of_tile, layer, wts, tm, seg, vt_tile):
    tokens = x2d.shape[0]
    n_tiles = tokens // tm
    n_chunks = tokens // GLA_CHUNK
    cpt = tm // GLA_CHUNK
    assert SUB_ROWS % seg == 0
    assert tm % vt_tile == 0 and vt_tile % SUB_ROWS == 0
    of_layer = lambda shape: pl.BlockSpec((None,) + shape, lambda j: (layer,) + (0,) * len(shape))
    return pl.pallas_call(
        functools.partial(_inproj_kernel, seg=seg),
        grid=(n_tiles,),
        in_specs=[
            pl.BlockSpec((tm, D_MODEL), lambda j: (j, 0)),
            pl.BlockSpec((None, 1, N_MOD * D_MODEL), lambda j: (layer * MOD_ROWS + mod_row_of_tile(j), 0, 0)),
            of_layer((1, D_MODEL)),
            of_layer((D_MODEL, WIN_COLS)),
            of_layer((VAL_COLS, D_MODEL)),
            of_layer((2 * GATE_RANK, KEY_COLS)),
            of_layer((1, KEY_COLS)),
            of_layer((3, CONV_WIDTH)),
            of_layer((POOL_WIDTH, POOL_WIDTH)),
            of_layer((1, POOL_WIDTH)),
        ],
        out_specs=[
            pl.BlockSpec((tm, TOK_COLS), lambda j: (j, 0)),
            pl.BlockSpec((tm // vt_tile, VAL_COLS, vt_tile), lambda j: (j, 0, 0)),
            pl.BlockSpec((cpt, GLA_HEADS, GLA_DV, SLAB), lambda j: (j, 0, 0, 0)),
            pl.BlockSpec((cpt, 8, KEY_COLS), lambda j: (j, 0, 0)),
        ],
        out_shape=[
            jax.ShapeDtypeStruct((tokens, TOK_COLS), _BF16),
            jax.ShapeDtypeStruct((tokens // vt_tile, VAL_COLS, vt_tile), _BF16),
            jax.ShapeDtypeStruct((n_chunks, GLA_HEADS, GLA_DV, SLAB), _BF16),
            jax.ShapeDtypeStruct((n_chunks, 8, KEY_COLS), _F32),
        ],
        compiler_params=pltpu.CompilerParams(vmem_limit_bytes=VMEM_LIMIT_BYTES),
    )(x2d, mods3, wts["n1w"], wts["win"], wts["wvt"], wts["gw"], wts["gb"], wts["cw"], wts["pw"], wts["ps"])


def _scan_chunks(ut_ref, dec_ref, states, spt_ref, nc, chunk0=0, head0=0):
    lane = lax.broadcasted_iota(jnp.int32, (1, SLAB), 1)
    is_fwd = lane < GLA_DK

    def step(t, states):
        i = chunk0 + t
        j = chunk0 + nc - 1 - t
        new_states = []
        for s in range(SCAN_HEADS):
            h = head0 + s
            lanes = slice(h * SLAB, (h + 1) * SLAB)
            log_decay = jnp.where(is_fwd, dec_ref[i, 0:1, lanes], dec_ref[j, 0:1, lanes])
            log_mid = jnp.where(is_fwd, dec_ref[i, 1:2, lanes], dec_ref[j, 1:2, lanes])
            entering = (states[s] * jnp.exp2(log_mid)).astype(_BF16)
            spt_ref[i, h, :, 0:GLA_DK] = entering[:, 0:GLA_DK]
            spt_ref[j, h, :, GLA_DK:SLAB] = entering[:, GLA_DK:SLAB]
            inc = jnp.where(is_fwd, ut_ref[i, h], ut_ref[j, h]).astype(_F32)
            new_states.append(states[s] * jnp.exp2(log_decay) + inc)
        return tuple(new_states)

    return lax.fori_loop(0, nc, step, tuple(states), unroll=SCAN_UNROLL)


def _scan_kernel(ut_ref, dec_ref, s0_ref, spt_ref, sfin_ref):
    final = _scan_chunks(ut_ref, dec_ref, [s0_ref[h] for h in range(SCAN_HEADS)], spt_ref, ut_ref.shape[0])
    for h in range(SCAN_HEADS):
        sfin_ref[h] = final[h]


def _scan(ut, dec, s0, chunks_per_seq):
    n_chunks = ut.shape[0]
    bsz = n_chunks // chunks_per_seq
    return pl.pallas_call(
        _scan_kernel,
        grid=(bsz, GLA_HEADS // SCAN_HEADS),
        in_specs=[
            pl.BlockSpec((chunks_per_seq, SCAN_HEADS, GLA_DV, SLAB), lambda b, g: (b, g, 0, 0)),
            pl.BlockSpec((chunks_per_seq, 8, SCAN_HEADS * SLAB), lambda b, g: (b, 0, g)),
            pl.BlockSpec((None, SCAN_HEADS, GLA_DV, SLAB), lambda b, g: (b, g, 0, 0)),
        ],
        out_specs=[
            pl.BlockSpec((chunks_per_seq, SCAN_HEADS, GLA_DV, SLAB), lambda b, g: (b, g, 0, 0)),
            pl.BlockSpec((None, SCAN_HEADS, GLA_DV, SLAB), lambda b, g: (b, g, 0, 0)),
        ],
        out_shape=[
            jax.ShapeDtypeStruct((n_chunks, GLA_HEADS, GLA_DV, SLAB), _BF16),
            jax.ShapeDtypeStruct((bsz, GLA_HEADS, GLA_DV, SLAB), _F32),
        ],
        compiler_params=pltpu.CompilerParams(vmem_limit_bytes=VMEM_LIMIT_BYTES),
    )(ut, dec, s0)


def _mix_kernel(x_ref, mod_ref, n2w_ref, tok_ref, vt_ref, spt_ref, gnw_ref,
                wo_ref, w1_ref, w2_ref, fnw_ref, o_ref, y_scr, *, final):
    tm = x_ref.shape[0]
    qe_ref, ki_ref, ry_ref = tok_ref.at[:, TOK_QE], tok_ref.at[:, TOK_KI], tok_ref.at[:, TOK_RY]
    g1 = mod_ref[:, 2 * D_MODEL:3 * D_MODEL]
    sh2 = mod_ref[:, 3 * D_MODEL:4 * D_MODEL]
    scale2 = n2w_ref[...] * (1.0 + mod_ref[:, 4 * D_MODEL:5 * D_MODEL])
    g2 = mod_ref[:, 5 * D_MODEL:6 * D_MODEL]
    row = lax.broadcasted_iota(jnp.int32, (GLA_CHUNK, GLA_CHUNK), 0)
    col = lax.broadcasted_iota(jnp.int32, (GLA_CHUNK, GLA_CHUNK), 1)
    lane = lax.broadcasted_iota(jnp.int32, (GLA_CHUNK, SLAB), 1)
    fwd_lane = lane < GLA_DK
    gnw = gnw_ref[...]

    for c in range(tm // GLA_CHUNK):
        rows = slice(c * GLA_CHUNK, (c + 1) * GLA_CHUNK)
        for h in range(GLA_HEADS):
            lanes = slice(h * SLAB, (h + 1) * SLAB)
            qe = qe_ref[rows, lanes]
            ki = ki_ref[rows, lanes]
            zero = jnp.zeros_like(ki)
            keys = jnp.concatenate([jnp.where(fwd_lane, ki, zero), jnp.where(fwd_lane, zero, ki)], axis=0)
            a2 = lax.dot_general(qe, keys, _NT, preferred_element_type=_F32)
            am = jnp.where(col <= row, a2[:, 0:GLA_CHUNK], 0.0) + jnp.where(col >= row, a2[:, GLA_CHUNK:], 0.0)
            lhs = jnp.concatenate([am.astype(_BF16), qe], axis=1)
            rhs = jnp.concatenate([vt_ref[h * GLA_DV:(h + 1) * GLA_DV, rows], spt_ref[c, h]], axis=1)
            o = lax.dot_general(lhs, rhs, _NT, preferred_element_type=_F32)
            o = o * lax.rsqrt(jnp.mean(o * o, axis=-1, keepdims=True) + EPS) * gnw
            r = ry_ref[rows, h * GLA_DV:(h + 1) * GLA_DV].astype(_F32)
            y_scr[rows, h * GLA_DV:(h + 1) * GLA_DV] = (o * (r * _sigmoid(r))).astype(_BF16)

    mixed = _dot(y_scr[...], wo_ref[0:VAL_COLS, :]) + _dot(ry_ref[:, VAL_COLS:RY_COLS], wo_ref[VAL_COLS:D_MODEL, :])
    x1 = x_ref[...] + g1 * mixed
    xn = x1 * lax.rsqrt(jnp.mean(x1 * x1, axis=-1, keepdims=True) + EPS)
    h2 = (xn * scale2 + sh2).astype(_BF16)
    acc = jnp.zeros((tm, D_MODEL), _F32)
    for f in range(D_FF // FF_CHUNK):
        cols = slice(f * FF_CHUNK, (f + 1) * FF_CHUNK)
        a = jnp.maximum(_dot(h2, w1_ref[:, cols]), 0.0)
        acc = acc + _dot((a * a).astype(_BF16), w2_ref[cols, :])
    out = x1 + g2 * acc
    if final:
        out = out * lax.rsqrt(jnp.mean(out * out, axis=-1, keepdims=True) + EPS) * fnw_ref[...]
    o_ref[...] = out


def _mix(x2d, mods3, mod_row_of_tile, layer, tok, vt, spt, wts, tm, final):
    tokens = x2d.shape[0]
    cpt = tm // GLA_CHUNK
    of_layer = lambda shape: pl.BlockSpec((None,) + shape, lambda j: (layer,) + (0,) * len(shape),
                                          pipeline_mode=pl.Buffered(1))
    return pl.pallas_call(
        functools.partial(_mix_kernel, final=final),
        grid=(tokens // tm,),
        in_specs=[
            pl.BlockSpec((tm, D_MODEL), lambda j: (j, 0)),
            pl.BlockSpec((None, 1, N_MOD * D_MODEL), lambda j: (layer * MOD_ROWS + mod_row_of_tile(j), 0, 0)),
            of_layer((1, D_MODEL)),
            pl.BlockSpec((tm, TOK_COLS), lambda j: (j, 0)),
            pl.BlockSpec((None, VAL_COLS, tm), lambda j: (j, 0, 0)),
            pl.BlockSpec((cpt, GLA_HEADS, GLA_DV, SLAB), lambda j: (j, 0, 0, 0)),
            of_layer((1, GLA_DV)),
            of_layer((D_MODEL, D_MODEL)),
            of_layer((D_MODEL, D_FF)),
            of_layer((D_FF, D_MODEL)),
            pl.BlockSpec((1, D_MODEL), lambda j: (0, 0), pipeline_mode=pl.Buffered(1)),
        ],
        out_specs=pl.BlockSpec((tm, D_MODEL), lambda j: (j, 0)),
        out_shape=jax.ShapeDtypeStruct((tokens, D_MODEL), _F32),
        scratch_shapes=[pltpu.VMEM((tm, VAL_COLS), _BF16)],
        compiler_params=pltpu.CompilerParams(vmem_limit_bytes=VMEM_LIMIT_BYTES),
    )(x2d, mods3, wts["n2w"], tok, vt, spt, wts["gnw"], wts["wo"], wts["w1"], wts["w2"], wts["fnw"])


def _context_kernel(*refs, seg, chunks_per_seq, with_mix):
    n_in = 10
    x_ref, mod_ref = refs[0], refs[1]
    inproj_in = refs[0:n_in]
    if with_mix:
        n2w_ref, gnw_ref, wo_ref, w1_ref, w2_ref, fnw_ref = refs[n_in:n_in + 6]
        sfin_ref, o_ref = refs[n_in + 6:n_in + 8]
        tok_scr, vt_scr, ut_scr, dec_scr, spt_scr, y_scr = refs[n_in + 8:]
    else:
        sfin_ref = refs[n_in]
        tok_scr, vt_scr, ut_scr, dec_scr, spt_scr = refs[n_in + 1:]

    _inproj_kernel(*inproj_in, tok_scr, vt_scr, ut_scr, dec_scr, seg=seg)

    sfin_ref[...] = jnp.zeros(sfin_ref.shape, _F32)
    for b in range(ut_scr.shape[0] // chunks_per_seq):
        for head0 in range(0, GLA_HEADS, SCAN_HEADS):
            start = [sfin_ref[b, head0 + s] for s in range(SCAN_HEADS)]
            final = _scan_chunks(ut_scr, dec_scr, start, spt_scr, chunks_per_seq,
                                 chunk0=b * chunks_per_seq, head0=head0)
            for s in range(SCAN_HEADS):
                sfin_ref[b, head0 + s] = final[s]

    if with_mix:
        _mix_kernel(x_ref, mod_ref, n2w_ref, tok_scr, vt_scr.at[0], spt_scr, gnw_ref, wo_ref, w1_ref, w2_ref, fnw_ref,
                    o_ref, y_scr, final=False)


def _context_layer(xc, mods3, mod_row, layer, wts, ctx_len, with_mix):
    rows = xc.shape[0]
    bsz = rows // ctx_len
    n_chunks = rows // GLA_CHUNK
    assert rows % SUB_ROWS == 0 and SUB_ROWS % ctx_len == 0
    of_layer = lambda shape: pl.BlockSpec((None,) + shape, lambda j: (layer,) + (0,) * len(shape),
                                          pipeline_mode=pl.Buffered(1))
    in_specs = [
        pl.BlockSpec((rows, D_MODEL), lambda j: (0, 0)),
        pl.BlockSpec((None, 1, N_MOD * D_MODEL), lambda j: (layer * MOD_ROWS + mod_row, 0, 0)),
        of_layer((1, D_MODEL)),
        of_layer((D_MODEL, WIN_COLS)),
        of_layer((VAL_COLS, D_MODEL)),
        of_layer((2 * GATE_RANK, KEY_COLS)),
        of_layer((1, KEY_COLS)),
        of_layer((3, CONV_WIDTH)),
        of_layer((POOL_WIDTH, POOL_WIDTH)),
        of_layer((1, POOL_WIDTH)),
    ]
    args = [xc, mods3, wts["n1w"], wts["win"], wts["wvt"], wts["gw"], wts["gb"], wts["cw"], wts["pw"], wts["ps"]]
    out_specs = [pl.BlockSpec((bsz, GLA_HEADS, GLA_DV, SLAB), lambda j: (0, 0, 0, 0))]
    out_shape = [jax.ShapeDtypeStruct((bsz, GLA_HEADS, GLA_DV, SLAB), _F32)]
    scratch = [
        pltpu.VMEM((rows, TOK_COLS), _BF16),
        pltpu.VMEM((1, VAL_COLS, rows), _BF16),
        pltpu.VMEM((n_chunks, GLA_HEADS, GLA_DV, SLAB), _BF16),
        pltpu.VMEM((n_chunks, 8, KEY_COLS), _F32),
        pltpu.VMEM((n_chunks, GLA_HEADS, GLA_DV, SLAB), _BF16),
    ]
    if with_mix:
        in_specs += [of_layer((1, D_MODEL)), of_layer((1, GLA_DV)), of_layer((D_MODEL, D_MODEL)),
                     of_layer((D_MODEL, D_FF)), of_layer((D_FF, D_MODEL)),
                     pl.BlockSpec((1, D_MODEL), lambda j: (0, 0), pipeline_mode=pl.Buffered(1))]
        args += [wts["n2w"], wts["gnw"], wts["wo"], wts["w1"], wts["w2"], wts["fnw"]]
        out_specs.append(pl.BlockSpec((rows, D_MODEL), lambda j: (0, 0)))
        out_shape.append(jax.ShapeDtypeStruct((rows, D_MODEL), _F32))
        scratch.append(pltpu.VMEM((rows, VAL_COLS), _BF16))
    outs = pl.pallas_call(
        functools.partial(_context_kernel, seg=ctx_len, chunks_per_seq=ctx_len // GLA_CHUNK, with_mix=with_mix),
        grid=(1,),
        in_specs=in_specs,
        out_specs=out_specs,
        out_shape=out_shape,
        scratch_shapes=scratch,
        compiler_params=pltpu.CompilerParams(vmem_limit_bytes=VMEM_LIMIT_BYTES),
    )(*args)
    return (outs[0], outs[1]) if with_mix else (outs[0], None)


def _prepare_weights(norm1_w, norm2_w, w_in, gate_w, gate_b, gla_norm_w, conv_w, pool_w, pool_scale, w_out,
                     w_mlp1, w_mlp2, final_norm_w):
    depth = w_in.shape[0]
    key_w = GLA_HEADS * GLA_DK
    q_off, k_off, v_off = 0, key_w, 2 * key_w
    g_off = v_off + VAL_COLS
    r_off = g_off + 2 * GATE_RANK
    z_pad = jnp.zeros((depth, D_MODEL, WQ_COLS.start - WZ_COLS.stop), _F32)
    win = jnp.concatenate([w_in[:, :, g_off:r_off], z_pad, w_in[:, :, q_off:v_off], w_in[:, :, r_off:]], axis=-1)

    zeros = jnp.zeros((depth, GATE_RANK, GLA_HEADS, GLA_DK), _F32)
    gf = gate_w[:, 0].reshape(depth, GATE_RANK, GLA_HEADS, GLA_DK)
    gb = gate_w[:, 1].reshape(depth, GATE_RANK, GLA_HEADS, GLA_DK)
    gw = jnp.concatenate([jnp.concatenate([gf, zeros], -1), jnp.concatenate([zeros, gb], -1)], 1)
    gbias = jnp.concatenate([gate_b[:, 0].reshape(depth, GLA_HEADS, GLA_DK),
                             gate_b[:, 1].reshape(depth, GLA_HEADS, GLA_DK)], -1)
    pw = jnp.zeros((depth, POOL_WIDTH, POOL_WIDTH), _F32)
    for gi in range(POOL_WIDTH // POOL_GROUP):
        sl = slice(gi * POOL_GROUP, (gi + 1) * POOL_GROUP)
        pw = pw.at[:, sl, sl].set(pool_w[:, gi])
    return {
        "n1w": norm1_w.reshape(depth, 1, D_MODEL),
        "n2w": norm2_w.reshape(depth, 1, D_MODEL),
        "win": win.astype(_BF16),
        "wvt": jnp.swapaxes(w_in[:, :, v_off:g_off], 1, 2).astype(_BF16),
        "gw": gw.reshape(depth, 2 * GATE_RANK, KEY_COLS).astype(_BF16),
        "gb": gbias.reshape(depth, 1, KEY_COLS),
        "gnw": gla_norm_w.reshape(depth, 1, GLA_DV),
        "cw": conv_w,
        "pw": pw.astype(_BF16),
        "ps": pool_scale.reshape(depth, 1, POOL_WIDTH),
        "wo": w_out.astype(_BF16),
        "w1": w_mlp1.astype(_BF16),
        "w2": w_mlp2.astype(_BF16),
        "fnw": final_norm_w.reshape(1, D_MODEL),
    }


def kernel(x, c, ctx, c_ctx, w_mod, b_mod, norm1_w, norm2_w, w_in, gla_gate_w, gla_gate_b, gla_norm_w, conv_w, pool_w, pool_scale, w_out, w_mlp1, w_mlp2, final_norm_w):
    bsz, n, _ = x.shape
    ctx_len = ctx.shape[1]
    depth = w_in.shape[0]
    assert bsz + 1 <= MOD_ROWS and n % LATENT_IN_TILE == 0 and SUB_ROWS % ctx_len == 0
    ctx_row = bsz

    cv = jnp.concatenate([c, c_ctx[None, :], jnp.zeros((MOD_ROWS - bsz - 1, D_MODEL), _F32)], axis=0)
    mods = _adaln(cv, w_mod, b_mod)
    mods3 = mods.reshape(depth * MOD_ROWS, 1, N_MOD * D_MODEL)
    wts = _prepare_weights(norm1_w, norm2_w, w_in, gla_gate_w, gla_gate_b, gla_norm_w, conv_w, pool_w, pool_scale,
                           w_out, w_mlp1, w_mlp2, final_norm_w)

    xl = x.reshape(bsz * n, D_MODEL)
    xc = ctx.reshape(bsz * ctx_len, D_MODEL)
    lat_in_row = lambda j: j // (n // LATENT_IN_TILE)
    lat_mix_row = lambda j: j // (n // LATENT_MIX_TILE)

    for l in range(depth):
        last = l == depth - 1
        s_ctx, xc = _context_layer(xc, mods3, ctx_row, l, wts, ctx_len, with_mix=not last)

        tok, vt, ut, dec = _inproj(xl, mods3, lat_in_row, l, wts, LATENT_IN_TILE, GRID_W, LATENT_MIX_TILE)
        spt, _ = _scan(ut, dec, s_ctx, n // GLA_CHUNK)
        xl = _mix(xl, mods3, lat_mix_row, l, tok, vt, spt, wts, LATENT_MIX_TILE, last)
    return xl.reshape(bsz, n, D_MODEL)
```

```python
import functools

import jax
import jax.numpy as jnp
from jax import lax
from jax.experimental import pallas as pl
from jax.experimental.pallas import tpu as pltpu

D_MODEL = 1024
GLA_HEADS = 4
GLA_DK = 64
GLA_DV = 128
SLAB = 2 * GLA_DK
KEY_COLS = GLA_HEADS * SLAB
VAL_COLS = GLA_HEADS * GLA_DV
GATE_RANK = 16
GATE_TAU = 16.0
LOG2_E = 1.4426950408889634
GLA_CHUNK = 128
SUB_ROWS = 256
LATENT_IN_TILE = 1024
LATENT_MIX_TILE = 512
CONV_WIDTH = 256
POOL_WIDTH = 256
POOL_GROUP = 64
POOL_HALF_WINDOWS = (1, 2, 4, 8)
D_FF = 4096
FF_CHUNK = 1024
N_MOD = 6
EPS = 1e-6
GRID_W = 64
RC_COLS = VAL_COLS + 3 * CONV_WIDTH + POOL_WIDTH
RY_COLS = VAL_COLS + CONV_WIDTH + POOL_WIDTH
TOK_QE = slice(0, KEY_COLS)
TOK_KI = slice(KEY_COLS, 2 * KEY_COLS)
TOK_RY = slice(2 * KEY_COLS, 2 * KEY_COLS + RY_COLS)
TOK_COLS = TOK_RY.stop
WZ_COLS = slice(0, 2 * GATE_RANK)
WQ_COLS = slice(128, 128 + GLA_HEADS * GLA_DK)
WK_COLS = slice(WQ_COLS.stop, WQ_COLS.stop + GLA_HEADS * GLA_DK)
WRC_COLS = slice(WK_COLS.stop, WK_COLS.stop + RC_COLS)
WIN_COLS = WRC_COLS.stop
SCAN_UNROLL = 2
SCAN_HEADS = 2
MOD_ROWS = 8
ADALN_COLS = 1536
VMEM_LIMIT_BYTES = 56 * 1024 * 1024
STAGE_SLOTS = 2

_NT = (((1,), (1,)), ((), ()))
_BF16 = jnp.bfloat16
_F32 = jnp.float32


def _dot(a, b):
    return jnp.dot(a, b, preferred_element_type=_F32)


def _sigmoid(x):
    return 1.0 / (1.0 + jnp.exp(-x))


def _adaln_kernel(cv_ref, w_ref, b_ref, o_ref):
    cv = cv_ref[...]
    s = (cv * _sigmoid(cv)).astype(_BF16)
    o_ref[...] = _dot(s, w_ref[...].astype(_BF16)) + b_ref[...]


def _adaln(cv, w_mod, b_mod):
    depth = w_mod.shape[0]
    cols = w_mod.shape[2]
    return pl.pallas_call(
        _adaln_kernel,
        grid=(depth, cols // ADALN_COLS),
        in_specs=[
            pl.BlockSpec((MOD_ROWS, D_MODEL), lambda l, j: (0, 0)),
            pl.BlockSpec((None, D_MODEL, ADALN_COLS), lambda l, j: (l, 0, j)),
            pl.BlockSpec((None, 1, ADALN_COLS), lambda l, j: (l, 0, j)),
        ],
        out_specs=pl.BlockSpec((None, MOD_ROWS, ADALN_COLS), lambda l, j: (l, 0, j)),
        out_shape=jax.ShapeDtypeStruct((depth, MOD_ROWS, cols), _F32),
        compiler_params=pltpu.CompilerParams(vmem_limit_bytes=VMEM_LIMIT_BYTES),
    )(cv, w_mod, b_mod.reshape(depth, 1, cols))


def _per_direction(t):
    lane = lax.broadcasted_iota(jnp.int32, (1, SLAB), 1)
    first = lane < GLA_DK
    slabs = []
    for p in range(GLA_HEADS // 2):
        pair = t[:, p * SLAB:(p + 1) * SLAB]
        swapped = pltpu.roll(pair, GLA_DK, 1)
        slabs += [jnp.where(first, pair, swapped), jnp.where(first, swapped, pair)]
    return jnp.concatenate(slabs, axis=1)


def _conv_and_pool(rc, cw_ref, pw_ref, ps_ref, seg):
    rows = rc.shape[0]
    pos = lax.broadcasted_iota(jnp.int32, (rows, 1), 0) % seg
    c0 = 0
    cb = rc[:, c0:c0 + CONV_WIDTH]
    zc = rc[:, c0 + CONV_WIDTH:c0 + 2 * CONV_WIDTH] * rc[:, c0 + 2 * CONV_WIDTH:c0 + 3 * CONV_WIDTH]
    z_prev = jnp.where(pos >= 1, pltpu.roll(zc, 1, 0), 0.0)
    z_next = jnp.where(pos <= seg - 2, pltpu.roll(zc, rows - 1, 0), 0.0)
    yc = cb * (cw_ref[0:1, :] * z_prev + cw_ref[1:2, :] * zc + cw_ref[2:3, :] * z_next)

    p0 = 3 * CONV_WIDTH
    lane128 = lax.broadcasted_iota(jnp.int32, (1, 2 * POOL_GROUP), 1)
    pooled = []
    for blk in range(POOL_WIDTH // (2 * POOL_GROUP)):
        u = rc[:, p0 + blk * 2 * POOL_GROUP:p0 + (blk + 1) * 2 * POOL_GROUP]
        h_lo, h_hi = POOL_HALF_WINDOWS[2 * blk], POOL_HALF_WINDOWS[2 * blk + 1]
        half = jnp.where(lane128 < POOL_GROUP, h_lo, h_hi)
        acc = jnp.zeros_like(u)
        for d in range(-h_hi, h_hi):
            shifted = u if d == 0 else pltpu.roll(u, (-d) % rows, 0)
            valid = (pos + d >= 0) & (pos + d < seg) & (d >= -half) & (d < half)
            acc = acc + jnp.where(valid, shifted, 0.0)
        cnt = (jnp.minimum(pos + half, seg) - jnp.maximum(pos - half, 0)).astype(_F32)
        pooled.append(acc / cnt - u)
    pool_in = jnp.concatenate(pooled, axis=1).astype(_BF16)
    return yc, _dot(pool_in, pw_ref[...]) * ps_ref[...]


def _inproj_kernel(x_ref, mod_ref, n1w_ref, win_ref, wvt_ref, gw_ref, gb_ref, cw_ref, pw_ref, ps_ref,
                   tok_ref, vt_ref, ut_ref, dec_ref, *, seg):
    qe_ref, ki_ref, ry_ref = tok_ref.at[:, TOK_QE], tok_ref.at[:, TOK_KI], tok_ref.at[:, TOK_RY]
    vt_tile = vt_ref.shape[2]
    tm = x_ref.shape[0]
    sh1 = mod_ref[:, 0:D_MODEL]
    scale1 = n1w_ref[...] * (1.0 + mod_ref[:, D_MODEL:2 * D_MODEL])
    row = lax.broadcasted_iota(jnp.int32, (GLA_CHUNK, GLA_CHUNK), 0)
    col = lax.broadcasted_iota(jnp.int32, (GLA_CHUNK, GLA_CHUNK), 1)
    tri = (col <= row).astype(_BF16)
    lane = lax.broadcasted_iota(jnp.int32, (1, KEY_COLS), 1)
    is_fwd = (lane % SLAB) < GLA_DK
    mid = GLA_CHUNK // 2

    n_sub = tm // SUB_ROWS
    cps = SUB_ROWS // GLA_CHUNK
    sub_rows = [slice(s * SUB_ROWS, (s + 1) * SUB_ROWS) for s in range(n_sub)]
    hbs, projs, gates, q2s, k2s = [], [], [], [], []
    for s in range(n_sub):
        x = x_ref[sub_rows[s], :]
        xn = x * lax.rsqrt(jnp.mean(x * x, axis=-1, keepdims=True) + EPS)
        hbs.append((xn * scale1 + sh1).astype(_BF16))
    for s in range(n_sub):
        proj = _dot(hbs[s], win_ref[...])
        projs.append(proj)
        pre = _dot(proj[:, WZ_COLS].astype(_BF16), gw_ref[...]) + gb_ref[...]
        g = (jnp.minimum(pre, 0.0) * LOG2_E - jnp.log2(1.0 + jnp.exp2(jnp.abs(pre) * (-LOG2_E)))) * (1.0 / GATE_TAU)
        gates.append(g.astype(_BF16))
        q2s.append(_per_direction(proj[:, WQ_COLS] * (GLA_DK ** -0.5)))
        k2s.append(_per_direction(proj[:, WK_COLS]))

    prefixes = []
    for s in range(n_sub):
        for cl in range(cps):
            lrows = slice(cl * GLA_CHUNK, (cl + 1) * GLA_CHUNK)
            prefixes.append(_dot(tri, gates[s][lrows]))

    vtbs = []
    for s in range(n_sub):
        vtb = lax.dot_general(wvt_ref[...], hbs[s], _NT, preferred_element_type=_F32).astype(_BF16)
        first = s * SUB_ROWS
        vt_ref[first // vt_tile, :, first % vt_tile:first % vt_tile + SUB_ROWS] = vtb
        vtbs.append(vtb)
        rc = projs[s][:, WRC_COLS]
        yc, yp = _conv_and_pool(rc[:, VAL_COLS:RC_COLS], cw_ref, pw_ref, ps_ref, seg)
        ry_ref[sub_rows[s], 0:VAL_COLS] = rc[:, 0:VAL_COLS].astype(_BF16)
        ry_ref[sub_rows[s], VAL_COLS:VAL_COLS + CONV_WIDTH] = yc.astype(_BF16)
        ry_ref[sub_rows[s], VAL_COLS + CONV_WIDTH:RY_COLS] = yp.astype(_BF16)

    for s in range(n_sub):
        g = gates[s].astype(_F32)
        for cl in range(cps):
            c = s * cps + cl
            lrows = slice(cl * GLA_CHUNK, (cl + 1) * GLA_CHUNK)
            rows = slice(c * GLA_CHUNK, (c + 1) * GLA_CHUNK)
            prefix = prefixes[c]
            total = prefix[GLA_CHUNK - 1:GLA_CHUNK, :]
            suffix = total - prefix + g[lrows]
            e = jnp.where(is_fwd, prefix, suffix)
            mvec = jnp.where(is_fwd, e[mid - 1:mid, :], e[mid:mid + 1, :])
            bm = e - mvec
            qe_ref[rows, :] = (q2s[s][lrows] * jnp.exp2(bm)).astype(_BF16)
            ki_ref[rows, :] = (k2s[s][lrows] * jnp.exp2(-bm)).astype(_BF16)
            kd = (k2s[s][lrows] * jnp.exp2(total - e)).astype(_BF16)
            dec_ref[c] = jnp.concatenate([total, mvec, jnp.zeros((6, KEY_COLS), _F32)], axis=0)
            for h in range(GLA_HEADS):
                vt_h = vtbs[s][h * GLA_DV:(h + 1) * GLA_DV, lrows]
                ut_ref[c, h] = _dot(vt_h, kd[:, h * SLAB:(h + 1) * SLAB]).astype(ut_ref.dtype)


def _inproj(x2d, mods3, mod_row_of_tile, layer, wts, tm, seg, vt_tile):
    tokens = x2d.shape[0]
    n_tiles = tokens // tm
    n_chunks = tokens // GLA_CHUNK
    cpt = tm // GLA_CHUNK
    assert SUB_ROWS % seg == 0
    assert tm % vt_tile == 0 and vt_tile % SUB_ROWS == 0
    of_layer = lambda shape: pl.BlockSpec((None,) + shape, lambda j: (layer,) + (0,) * len(shape))
    return pl.pallas_call(
        functools.partial(_inproj_kernel, seg=seg),
        grid=(n_tiles,),
        in_specs=[
            pl.BlockSpec((tm, D_MODEL), lambda j: (j, 0)),
            pl.BlockSpec((None, 1, N_MOD * D_MODEL), lambda j: (layer * MOD_ROWS + mod_row_of_tile(j), 0, 0)),
            of_layer((1, D_MODEL)),
            of_layer((D_MODEL, WIN_COLS)),
            of_layer((VAL_COLS, D_MODEL)),
            of_layer((2 * GATE_RANK, KEY_COLS)),
            of_layer((1, KEY_COLS)),
            of_layer((3, CONV_WIDTH)),
            of_layer((POOL_WIDTH, POOL_WIDTH)),
            of_layer((1, POOL_WIDTH)),
        ],
        out_specs=[
            pl.BlockSpec((tm, TOK_COLS), lambda j: (j, 0)),
            pl.BlockSpec((tm // vt_tile, VAL_COLS, vt_tile), lambda j: (j, 0, 0)),
            pl.BlockSpec((cpt, GLA_HEADS, GLA_DV, SLAB), lambda j: (j, 0, 0, 0)),
            pl.BlockSpec((cpt, 8, KEY_COLS), lambda j: (j, 0, 0)),
        ],
        out_shape=[
            jax.ShapeDtypeStruct((tokens, TOK_COLS), _BF16),
            jax.ShapeDtypeStruct((tokens // vt_tile, VAL_COLS, vt_tile), _BF16),
            jax.ShapeDtypeStruct((n_chunks, GLA_HEADS, GLA_DV, SLAB), _BF16),
            jax.ShapeDtypeStruct((n_chunks, 8, KEY_COLS), _F32),
        ],
        compiler_params=pltpu.CompilerParams(vmem_limit_bytes=VMEM_LIMIT_BYTES),
    )(x2d, mods3, wts["n1w"], wts["win"], wts["wvt"], wts["gw"], wts["gb"], wts["cw"], wts["pw"], wts["ps"])


def _scan_chunks(ut_ref, dec_ref, states, spt_ref, nc, chunk0=0, head0=0):
    lane = lax.broadcasted_iota(jnp.int32, (1, SLAB), 1)
    is_fwd = lane < GLA_DK

    def step(t, states):
        i = chunk0 + t
        j = chunk0 + nc - 1 - t
        new_states = []
        for s in range(SCAN_HEADS):
            h = head0 + s
            lanes = slice(h * SLAB, (h + 1) * SLAB)
            log_decay = jnp.where(is_fwd, dec_ref[i, 0:1, lanes], dec_ref[j, 0:1, lanes])
            log_mid = jnp.where(is_fwd, dec_ref[i, 1:2, lanes], dec_ref[j, 1:2, lanes])
            entering = (states[s] * jnp.exp2(log_mid)).astype(_BF16)
            spt_ref[i, h, :, 0:GLA_DK] = entering[:, 0:GLA_DK]
            spt_ref[j, h, :, GLA_DK:SLAB] = entering[:, GLA_DK:SLAB]
            inc = jnp.where(is_fwd, ut_ref[i, h], ut_ref[j, h]).astype(_F32)
            new_states.append(states[s] * jnp.exp2(log_decay) + inc)
        return tuple(new_states)

    return lax.fori_loop(0, nc, step, tuple(states), unroll=SCAN_UNROLL)


def _scan_kernel(ut_ref, dec_ref, s0_ref, spt_ref, sfin_ref):
    final = _scan_chunks(ut_ref, dec_ref, [s0_ref[h] for h in range(SCAN_HEADS)], spt_ref, ut_ref.shape[0])
    for h in range(SCAN_HEADS):
        sfin_ref[h] = final[h]


def _scan(ut, dec, s0, chunks_per_seq):
    n_chunks = ut.shape[0]
    bsz = n_chunks // chunks_per_seq
    return pl.pallas_call(
        _scan_kernel,
        grid=(bsz, GLA_HEADS // SCAN_HEADS),
        in_specs=[
            pl.BlockSpec((chunks_per_seq, SCAN_HEADS, GLA_DV, SLAB), lambda b, g: (b, g, 0, 0)),
            pl.BlockSpec((chunks_per_seq, 8, SCAN_HEADS * SLAB), lambda b, g: (b, 0, g)),
            pl.BlockSpec((None, SCAN_HEADS, GLA_DV, SLAB), lambda b, g: (b, g, 0, 0)),
        ],
        out_specs=[
            pl.BlockSpec((chunks_per_seq, SCAN_HEADS, GLA_DV, SLAB), lambda b, g: (b, g, 0, 0)),
            pl.BlockSpec((None, SCAN_HEADS, GLA_DV, SLAB), lambda b, g: (b, g, 0, 0)),
        ],
        out_shape=[
            jax.ShapeDtypeStruct((n_chunks, GLA_HEADS, GLA_DV, SLAB), _BF16),
            jax.ShapeDtypeStruct((bsz, GLA_HEADS, GLA_DV, SLAB), _F32),
        ],
        compiler_params=pltpu.CompilerParams(vmem_limit_bytes=VMEM_LIMIT_BYTES),
    )(ut, dec, s0)


def _stage_mix_weights(layer, wo_hbm, w1_hbm, w2_hbm, wo_ref, w1_ref, w2_ref, stage, sems):
    pieces = [(wo_hbm.at[layer], wo_ref)]
    for f in range(D_FF // D_MODEL):
        cols = slice(f * D_MODEL, (f + 1) * D_MODEL)
        pieces.append((w1_hbm.at[layer, :, cols], w1_ref.at[:, cols]))
        pieces.append((w2_hbm.at[layer, cols, :], w2_ref.at[cols, :]))
    copies = [pltpu.make_async_copy(src, stage.at[k % STAGE_SLOTS], sems.at[k % STAGE_SLOTS])
              for k, (src, _) in enumerate(pieces)]
    for k in range(STAGE_SLOTS):
        copies[k].start()
    for k, (_, dst) in enumerate(pieces):
        copies[k].wait()
        dst[...] = stage[k % STAGE_SLOTS].astype(_BF16)
        if k + STAGE_SLOTS < len(pieces):
            copies[k + STAGE_SLOTS].start()


def _mix_kernel(x_ref, mod_ref, n2w_ref, tok_ref, vt_ref, spt_ref, gnw_ref, wo_hbm, w1_hbm, w2_hbm, fnw_ref,
                o_ref, y_scr, wo_ref, w1_ref, w2_ref, stage, sems, *, layer, final):
    @pl.when(pl.program_id(0) == 0)
    def _():
        _stage_mix_weights(layer, wo_hbm, w1_hbm, w2_hbm, wo_ref, w1_ref, w2_ref, stage, sems)

    tm = x_ref.shape[0]
    qe_ref, ki_ref, ry_ref = tok_ref.at[:, TOK_QE], tok_ref.at[:, TOK_KI], tok_ref.at[:, TOK_RY]
    g1 = mod_ref[:, 2 * D_MODEL:3 * D_MODEL]
    sh2 = mod_ref[:, 3 * D_MODEL:4 * D_MODEL]
    scale2 = n2w_ref[...] * (1.0 + mod_ref[:, 4 * D_MODEL:5 * D_MODEL])
    g2 = mod_ref[:, 5 * D_MODEL:6 * D_MODEL]
    row = lax.broadcasted_iota(jnp.int32, (GLA_CHUNK, GLA_CHUNK), 0)
    col = lax.broadcasted_iota(jnp.int32, (GLA_CHUNK, GLA_CHUNK), 1)
    lane = lax.broadcasted_iota(jnp.int32, (GLA_CHUNK, SLAB), 1)
    fwd_lane = lane < GLA_DK
    gnw = gnw_ref[...]

    for c in range(tm // GLA_CHUNK):
        rows = slice(c * GLA_CHUNK, (c + 1) * GLA_CHUNK)
        for h in range(GLA_HEADS):
            lanes = slice(h * SLAB, (h + 1) * SLAB)
            qe = qe_ref[rows, lanes]
            ki = ki_ref[rows, lanes]
            zero = jnp.zeros_like(ki)
            keys = jnp.concatenate([jnp.where(fwd_lane, ki, zero), jnp.where(fwd_lane, zero, ki)], axis=0)
            a2 = lax.dot_general(qe, keys, _NT, preferred_element_type=_F32)
            am = jnp.where(col <= row, a2[:, 0:GLA_CHUNK], 0.0) + jnp.where(col >= row, a2[:, GLA_CHUNK:], 0.0)
            lhs = jnp.concatenate([am.astype(_BF16), qe], axis=1)
            rhs = jnp.concatenate([vt_ref[h * GLA_DV:(h + 1) * GLA_DV, rows], spt_ref[c, h]], axis=1)
            o = lax.dot_general(lhs, rhs, _NT, preferred_element_type=_F32)
            o = o * lax.rsqrt(jnp.mean(o * o, axis=-1, keepdims=True) + EPS) * gnw
            r = ry_ref[rows, h * GLA_DV:(h + 1) * GLA_DV].astype(_F32)
            y_scr[rows, h * GLA_DV:(h + 1) * GLA_DV] = (o * (r * _sigmoid(r))).astype(_BF16)

    mixed = _dot(y_scr[...], wo_ref[0:VAL_COLS, :]) + _dot(ry_ref[:, VAL_COLS:RY_COLS], wo_ref[VAL_COLS:D_MODEL, :])
    x1 = x_ref[...] + g1 * mixed
    xn = x1 * lax.rsqrt(jnp.mean(x1 * x1, axis=-1, keepdims=True) + EPS)
    h2 = (xn * scale2 + sh2).astype(_BF16)
    acc = jnp.zeros((tm, D_MODEL), _F32)
    for f in range(D_FF // FF_CHUNK):
        cols = slice(f * FF_CHUNK, (f + 1) * FF_CHUNK)
        a = jnp.maximum(_dot(h2, w1_ref[:, cols]), 0.0)
        acc = acc + _dot((a * a).astype(_BF16), w2_ref[cols, :])
    out = x1 + g2 * acc
    if final:
        out = out * lax.rsqrt(jnp.mean(out * out, axis=-1, keepdims=True) + EPS) * fnw_ref[...]
    o_ref[...] = out


def _mix_weight_scratch():
    return [pltpu.VMEM((D_MODEL, D_MODEL), _BF16), pltpu.VMEM((D_MODEL, D_FF), _BF16), pltpu.VMEM((D_FF, D_MODEL), _BF16),
            pltpu.VMEM((STAGE_SLOTS, D_MODEL, D_MODEL), _F32), pltpu.SemaphoreType.DMA((STAGE_SLOTS,))]


def _mix(x2d, mods3, mod_row_of_tile, layer, tok, vt, spt, wts, tm, final):
    tokens = x2d.shape[0]
    cpt = tm // GLA_CHUNK
    of_layer = lambda shape: pl.BlockSpec((None,) + shape, lambda j: (layer,) + (0,) * len(shape),
                                          pipeline_mode=pl.Buffered(1))
    return pl.pallas_call(
        functools.partial(_mix_kernel, layer=layer, final=final),
        grid=(tokens // tm,),
        in_specs=[
            pl.BlockSpec((tm, D_MODEL), lambda j: (j, 0)),
            pl.BlockSpec((None, 1, N_MOD * D_MODEL), lambda j: (layer * MOD_ROWS + mod_row_of_tile(j), 0, 0)),
            of_layer((1, D_MODEL)),
            pl.BlockSpec((tm, TOK_COLS), lambda j: (j, 0)),
            pl.BlockSpec((None, VAL_COLS, tm), lambda j: (j, 0, 0)),
            pl.BlockSpec((cpt, GLA_HEADS, GLA_DV, SLAB), lambda j: (j, 0, 0, 0)),
            of_layer((1, GLA_DV)),
            pl.BlockSpec(memory_space=pl.ANY),
            pl.BlockSpec(memory_space=pl.ANY),
            pl.BlockSpec(memory_space=pl.ANY),
            pl.BlockSpec((1, D_MODEL), lambda j: (0, 0), pipeline_mode=pl.Buffered(1)),
        ],
        out_specs=pl.BlockSpec((tm, D_MODEL), lambda j: (j, 0)),
        out_shape=jax.ShapeDtypeStruct((tokens, D_MODEL), _F32),
        scratch_shapes=[pltpu.VMEM((tm, VAL_COLS), _BF16)] + _mix_weight_scratch(),
        compiler_params=pltpu.CompilerParams(vmem_limit_bytes=VMEM_LIMIT_BYTES),
    )(x2d, mods3, wts["n2w"], tok, vt, spt, wts["gnw"], wts["wo"], wts["w1"], wts["w2"], wts["fnw"])


def _context_kernel(*refs, layer, seg, chunks_per_seq, with_mix):
    n_in = 10
    x_ref, mod_ref = refs[0], refs[1]
    inproj_in = refs[0:n_in]
    if with_mix:
        n2w_ref, gnw_ref, wo_hbm, w1_hbm, w2_hbm, fnw_ref = refs[n_in:n_in + 6]
        sfin_ref, o_ref = refs[n_in + 6:n_in + 8]
        tok_scr, vt_scr, ut_scr, dec_scr, spt_scr, y_scr = refs[n_in + 8:n_in + 14]
        mix_weight_scr = refs[n_in + 14:]
    else:
        sfin_ref = refs[n_in]
        tok_scr, vt_scr, ut_scr, dec_scr, spt_scr = refs[n_in + 1:]

    _inproj_kernel(*inproj_in, tok_scr, vt_scr, ut_scr, dec_scr, seg=seg)

    sfin_ref[...] = jnp.zeros(sfin_ref.shape, _F32)
    for b in range(ut_scr.shape[0] // chunks_per_seq):
        for head0 in range(0, GLA_HEADS, SCAN_HEADS):
            start = [sfin_ref[b, head0 + s] for s in range(SCAN_HEADS)]
            final = _scan_chunks(ut_scr, dec_scr, start, spt_scr, chunks_per_seq,
                                 chunk0=b * chunks_per_seq, head0=head0)
            for s in range(SCAN_HEADS):
                sfin_ref[b, head0 + s] = final[s]

    if with_mix:
        _mix_kernel(x_ref, mod_ref, n2w_ref, tok_scr, vt_scr.at[0], spt_scr, gnw_ref, wo_hbm, w1_hbm, w2_hbm, fnw_ref,
                    o_ref, y_scr, *mix_weight_scr, layer=layer, final=False)


def _context_layer(xc, mods3, mod_row, layer, wts, ctx_len, with_mix):
    rows = xc.shape[0]
    bsz = rows // ctx_len
    n_chunks = rows // GLA_CHUNK
    assert rows % SUB_ROWS == 0 and SUB_ROWS % ctx_len == 0
    of_layer = lambda shape: pl.BlockSpec((None,) + shape, lambda j: (layer,) + (0,) * len(shape),
                                          pipeline_mode=pl.Buffered(1))
    in_specs = [
        pl.BlockSpec((rows, D_MODEL), lambda j: (0, 0)),
        pl.BlockSpec((None, 1, N_MOD * D_MODEL), lambda j: (layer * MOD_ROWS + mod_row, 0, 0)),
        of_layer((1, D_MODEL)),
        of_layer((D_MODEL, WIN_COLS)),
        of_layer((VAL_COLS, D_MODEL)),
        of_layer((2 * GATE_RANK, KEY_COLS)),
        of_layer((1, KEY_COLS)),
        of_layer((3, CONV_WIDTH)),
        of_layer((POOL_WIDTH, POOL_WIDTH)),
        of_layer((1, POOL_WIDTH)),
    ]
    args = [xc, mods3, wts["n1w"], wts["win"], wts["wvt"], wts["gw"], wts["gb"], wts["cw"], wts["pw"], wts["ps"]]
    out_specs = [pl.BlockSpec((bsz, GLA_HEADS, GLA_DV, SLAB), lambda j: (0, 0, 0, 0))]
    out_shape = [jax.ShapeDtypeStruct((bsz, GLA_HEADS, GLA_DV, SLAB), _F32)]
    scratch = [
        pltpu.VMEM((rows, TOK_COLS), _BF16),
        pltpu.VMEM((1, VAL_COLS, rows), _BF16),
        pltpu.VMEM((n_chunks, GLA_HEADS, GLA_DV, SLAB), _BF16),
        pltpu.VMEM((n_chunks, 8, KEY_COLS), _F32),
        pltpu.VMEM((n_chunks, GLA_HEADS, GLA_DV, SLAB), _BF16),
    ]
    if with_mix:
        in_specs += [of_layer((1, D_MODEL)), of_layer((1, GLA_DV))] + [pl.BlockSpec(memory_space=pl.ANY)] * 3
        in_specs += [pl.BlockSpec((1, D_MODEL), lambda j: (0, 0), pipeline_mode=pl.Buffered(1))]
        args += [wts["n2w"], wts["gnw"], wts["wo"], wts["w1"], wts["w2"], wts["fnw"]]
        out_specs.append(pl.BlockSpec((rows, D_MODEL), lambda j: (0, 0)))
        out_shape.append(jax.ShapeDtypeStruct((rows, D_MODEL), _F32))
        scratch += [pltpu.VMEM((rows, VAL_COLS), _BF16)] + _mix_weight_scratch()
    outs = pl.pallas_call(
        functools.partial(_context_kernel, layer=layer, seg=ctx_len, chunks_per_seq=ctx_len // GLA_CHUNK,
                          with_mix=with_mix),
        grid=(1,),
        in_specs=in_specs,
        out_specs=out_specs,
        out_shape=out_shape,
        scratch_shapes=scratch,
        compiler_params=pltpu.CompilerParams(vmem_limit_bytes=VMEM_LIMIT_BYTES),
    )(*args)
    return (outs[0], outs[1]) if with_mix else (outs[0], None)


def _prepare_weights(norm1_w, norm2_w, w_in, gate_w, gate_b, gla_norm_w, conv_w, pool_w, pool_scale, w_out,
                     w_mlp1, w_mlp2, final_norm_w):
    depth = w_in.shape[0]
    key_w = GLA_HEADS * GLA_DK
    q_off, k_off, v_off = 0, key_w, 2 * key_w
    g_off = v_off + VAL_COLS
    r_off = g_off + 2 * GATE_RANK
    z_pad = jnp.zeros((depth, D_MODEL, WQ_COLS.start - WZ_COLS.stop), _F32)
    win = jnp.concatenate([w_in[:, :, g_off:r_off], z_pad, w_in[:, :, q_off:v_off], w_in[:, :, r_off:]], axis=-1)

    zeros = jnp.zeros((depth, GATE_RANK, GLA_HEADS, GLA_DK), _F32)
    gf = gate_w[:, 0].reshape(depth, GATE_RANK, GLA_HEADS, GLA_DK)
    gb = gate_w[:, 1].reshape(depth, GATE_RANK, GLA_HEADS, GLA_DK)
    gw = jnp.concatenate([jnp.concatenate([gf, zeros], -1), jnp.concatenate([zeros, gb], -1)], 1)
    gbias = jnp.concatenate([gate_b[:, 0].reshape(depth, GLA_HEADS, GLA_DK),
                             gate_b[:, 1].reshape(depth, GLA_HEADS, GLA_DK)], -1)
    pw = jnp.zeros((depth, POOL_WIDTH, POOL_WIDTH), _F32)
    for gi in range(POOL_WIDTH // POOL_GROUP):
        sl = slice(gi * POOL_GROUP, (gi + 1) * POOL_GROUP)
        pw = pw.at[:, sl, sl].set(pool_w[:, gi])
    return {
        "n1w": norm1_w.reshape(depth, 1, D_MODEL),
        "n2w": norm2_w.reshape(depth, 1, D_MODEL),
        "win": win.astype(_BF16),
        "wvt": jnp.swapaxes(w_in[:, :, v_off:g_off], 1, 2).astype(_BF16),
        "gw": gw.reshape(depth, 2 * GATE_RANK, KEY_COLS).astype(_BF16),
        "gb": gbias.reshape(depth, 1, KEY_COLS),
        "gnw": gla_norm_w.reshape(depth, 1, GLA_DV),
        "cw": conv_w,
        "pw": pw.astype(_BF16),
        "ps": pool_scale.reshape(depth, 1, POOL_WIDTH),
        "wo": w_out,
        "w1": w_mlp1,
        "w2": w_mlp2,
        "fnw": final_norm_w.reshape(1, D_MODEL),
    }


def kernel(x, c, ctx, c_ctx, w_mod, b_mod, norm1_w, norm2_w, w_in, gla_gate_w, gla_gate_b, gla_norm_w, conv_w, pool_w, pool_scale, w_out, w_mlp1, w_mlp2, final_norm_w):
    bsz, n, _ = x.shape
    ctx_len = ctx.shape[1]
    depth = w_in.shape[0]
    assert bsz + 1 <= MOD_ROWS and n % LATENT_IN_TILE == 0 and SUB_ROWS % ctx_len == 0
    ctx_row = bsz

    cv = jnp.concatenate([c, c_ctx[None, :], jnp.zeros((MOD_ROWS - bsz - 1, D_MODEL), _F32)], axis=0)
    mods = _adaln(cv, w_mod, b_mod)
    mods3 = mods.reshape(depth * MOD_ROWS, 1, N_MOD * D_MODEL)
    wts = _prepare_weights(norm1_w, norm2_w, w_in, gla_gate_w, gla_gate_b, gla_norm_w, conv_w, pool_w, pool_scale,
                           w_out, w_mlp1, w_mlp2, final_norm_w)

    xl = x.reshape(bsz * n, D_MODEL)
    xc = ctx.reshape(bsz * ctx_len, D_MODEL)
    lat_in_row = lambda j: j // (n // LATENT_IN_TILE)
    lat_mix_row = lambda j: j // (n // LATENT_MIX_TILE)

    for l in range(depth):
        last = l == depth - 1
        s_ctx, xc = _context_layer(xc, mods3, ctx_row, l, wts, ctx_len, with_mix=not last)

        tok, vt, ut, dec = _inproj(xl, mods3, lat_in_row, l, wts, LATENT_IN_TILE, GRID_W, LATENT_MIX_TILE)
        spt, _ = _scan(ut, dec, s_ctx, n // GLA_CHUNK)
        xl = _mix(xl, mods3, lat_mix_row, l, tok, vt, spt, wts, LATENT_MIX_TILE, last)
    return xl.reshape(bsz, n, D_MODEL)
```

```python
import functools

import jax
import jax.numpy as jnp
from jax import lax
from jax.experimental import pallas as pl
from jax.experimental.pallas import tpu as pltpu

D_MODEL = 1024
GLA_HEADS = 4
GLA_DK = 64
GLA_DV = 128
SLAB = 2 * GLA_DK
KEY_COLS = GLA_HEADS * SLAB
VAL_COLS = GLA_HEADS * GLA_DV
GATE_RANK = 16
GATE_TAU = 16.0
LOG2_E = 1.4426950408889634
GLA_CHUNK = 128
SUB_ROWS = 256
LATENT_IN_TILE = 1024
LATENT_MIX_TILE = 512
CONV_WIDTH = 256
POOL_WIDTH = 256
POOL_GROUP = 64
POOL_HALF_WINDOWS = (1, 2, 4, 8)
D_FF = 4096
FF_CHUNK = 1024
N_MOD = 6
EPS = 1e-6
GRID_W = 64
RC_COLS = VAL_COLS + 3 * CONV_WIDTH + POOL_WIDTH
RY_COLS = VAL_COLS + CONV_WIDTH + POOL_WIDTH
TOK_QE = slice(0, KEY_COLS)
TOK_KI = slice(KEY_COLS, 2 * KEY_COLS)
TOK_RY = slice(2 * KEY_COLS, 2 * KEY_COLS + RY_COLS)
TOK_COLS = TOK_RY.stop
WZ_COLS = slice(0, 2 * GATE_RANK)
WQ_COLS = slice(128, 128 + GLA_HEADS * GLA_DK)
WK_COLS = slice(WQ_COLS.stop, WQ_COLS.stop + GLA_HEADS * GLA_DK)
WRC_COLS = slice(WK_COLS.stop, WK_COLS.stop + RC_COLS)
WIN_COLS = WRC_COLS.stop
SCAN_UNROLL = 2
SCAN_HEADS = 2
MOD_ROWS = 8
ADALN_COLS = 1536
VMEM_LIMIT_BYTES = 56 * 1024 * 1024
STAGE_SLOTS = 2

_NT = (((1,), (1,)), ((), ()))
_BF16 = jnp.bfloat16
_F32 = jnp.float32


def _dot(a, b):
    return jnp.dot(a, b, preferred_element_type=_F32)


def _sigmoid(x):
    return 1.0 / (1.0 + jnp.exp(-x))


def _adaln_kernel(cv_ref, w_ref, b_ref, o_ref):
    cv = cv_ref[...]
    s = (cv * _sigmoid(cv)).astype(_BF16)
    o_ref[...] = _dot(s, w_ref[...].astype(_BF16)) + b_ref[...]


def _adaln(cv, w_mod, b_mod):
    depth = w_mod.shape[0]
    cols = w_mod.shape[2]
    return pl.pallas_call(
        _adaln_kernel,
        grid=(depth, cols // ADALN_COLS),
        in_specs=[
            pl.BlockSpec((MOD_ROWS, D_MODEL), lambda l, j: (0, 0)),
            pl.BlockSpec((None, D_MODEL, ADALN_COLS), lambda l, j: (l, 0, j)),
            pl.BlockSpec((None, 1, ADALN_COLS), lambda l, j: (l, 0, j)),
        ],
        out_specs=pl.BlockSpec((None, MOD_ROWS, ADALN_COLS), lambda l, j: (l, 0, j)),
        out_shape=jax.ShapeDtypeStruct((depth, MOD_ROWS, cols), _F32),
        compiler_params=pltpu.CompilerParams(vmem_limit_bytes=VMEM_LIMIT_BYTES),
    )(cv, w_mod, b_mod.reshape(depth, 1, cols))


def _per_direction(t):
    lane = lax.broadcasted_iota(jnp.int32, (1, SLAB), 1)
    first = lane < GLA_DK
    slabs = []
    for p in range(GLA_HEADS // 2):
        pair = t[:, p * SLAB:(p + 1) * SLAB]
        swapped = pltpu.roll(pair, GLA_DK, 1)
        slabs += [jnp.where(first, pair, swapped), jnp.where(first, swapped, pair)]
    return jnp.concatenate(slabs, axis=1)


def _conv_and_pool(rc, cw_ref, pw_ref, ps_ref, seg):
    rows = rc.shape[0]
    pos = lax.broadcasted_iota(jnp.int32, (rows, 1), 0) % seg
    c0 = 0
    cb = rc[:, c0:c0 + CONV_WIDTH]
    zc = rc[:, c0 + CONV_WIDTH:c0 + 2 * CONV_WIDTH] * rc[:, c0 + 2 * CONV_WIDTH:c0 + 3 * CONV_WIDTH]
    z_prev = jnp.where(pos >= 1, pltpu.roll(zc, 1, 0), 0.0)
    z_next = jnp.where(pos <= seg - 2, pltpu.roll(zc, rows - 1, 0), 0.0)
    yc = cb * (cw_ref[0:1, :] * z_prev + cw_ref[1:2, :] * zc + cw_ref[2:3, :] * z_next)

    p0 = 3 * CONV_WIDTH
    lane128 = lax.broadcasted_iota(jnp.int32, (1, 2 * POOL_GROUP), 1)
    pooled = []
    for blk in range(POOL_WIDTH // (2 * POOL_GROUP)):
        u = rc[:, p0 + blk * 2 * POOL_GROUP:p0 + (blk + 1) * 2 * POOL_GROUP]
        h_lo, h_hi = POOL_HALF_WINDOWS[2 * blk], POOL_HALF_WINDOWS[2 * blk + 1]
        half = jnp.where(lane128 < POOL_GROUP, h_lo, h_hi)
        acc = jnp.zeros_like(u)
        for d in range(-h_hi, h_hi):
            shifted = u if d == 0 else pltpu.roll(u, (-d) % rows, 0)
            valid = (pos + d >= 0) & (pos + d < seg) & (d >= -half) & (d < half)
            acc = acc + jnp.where(valid, shifted, 0.0)
        cnt = (jnp.minimum(pos + half, seg) - jnp.maximum(pos - half, 0)).astype(_F32)
        pooled.append(acc / cnt - u)
    pool_in = jnp.concatenate(pooled, axis=1).astype(_BF16)
    return yc, _dot(pool_in, pw_ref[...]) * ps_ref[...]


def _inproj_kernel(x_ref, mod_ref, n1w_ref, win_ref, wvt_ref, gw_ref, gb_ref, cw_ref, pw_ref, ps_ref,
                   tok_ref, vt_ref, ut_ref, dec_ref, *, seg):
    qe_ref, ki_ref, ry_ref = tok_ref.at[:, TOK_QE], tok_ref.at[:, TOK_KI], tok_ref.at[:, TOK_RY]
    vt_tile = vt_ref.shape[2]
    tm = x_ref.shape[0]
    sh1 = mod_ref[:, 0:D_MODEL]
    scale1 = n1w_ref[...] * (1.0 + mod_ref[:, D_MODEL:2 * D_MODEL])
    row = lax.broadcasted_iota(jnp.int32, (GLA_CHUNK, GLA_CHUNK), 0)
    col = lax.broadcasted_iota(jnp.int32, (GLA_CHUNK, GLA_CHUNK), 1)
    tri = (col <= row).astype(_BF16)
    lane = lax.broadcasted_iota(jnp.int32, (1, KEY_COLS), 1)
    is_fwd = (lane % SLAB) < GLA_DK
    mid = GLA_CHUNK // 2

    n_sub = tm // SUB_ROWS
    cps = SUB_ROWS // GLA_CHUNK
    sub_rows = [slice(s * SUB_ROWS, (s + 1) * SUB_ROWS) for s in range(n_sub)]
    hbs, projs, gates, q2s, k2s = [], [], [], [], []
    for s in range(n_sub):
        x = x_ref[sub_rows[s], :]
        xn = x * lax.rsqrt(jnp.mean(x * x, axis=-1, keepdims=True) + EPS)
        hbs.append((xn * scale1 + sh1).astype(_BF16))
    for s in range(n_sub):
        proj = _dot(hbs[s], win_ref[...])
        projs.append(proj)
        pre = _dot(proj[:, WZ_COLS].astype(_BF16), gw_ref[...]) + gb_ref[...]
        g = (jnp.minimum(pre, 0.0) * LOG2_E - jnp.log2(1.0 + jnp.exp2(jnp.abs(pre) * (-LOG2_E)))) * (1.0 / GATE_TAU)
        gates.append(g.astype(_BF16))
        q2s.append(_per_direction(proj[:, WQ_COLS] * (GLA_DK ** -0.5)))
        k2s.append(_per_direction(proj[:, WK_COLS]))

    prefixes = []
    for s in range(n_sub):
        for cl in range(cps):
            lrows = slice(cl * GLA_CHUNK, (cl + 1) * GLA_CHUNK)
            prefixes.append(_dot(tri, gates[s][lrows]))

    vtbs = []
    for s in range(n_sub):
        vtb = lax.dot_general(wvt_ref[...], hbs[s], _NT, preferred_element_type=_F32).astype(_BF16)
        first = s * SUB_ROWS
        vt_ref[first // vt_tile, :, first % vt_tile:first % vt_tile + SUB_ROWS] = vtb
        vtbs.append(vtb)
        rc = projs[s][:, WRC_COLS]
        yc, yp = _conv_and_pool(rc[:, VAL_COLS:RC_COLS], cw_ref, pw_ref, ps_ref, seg)
        ry_ref[sub_rows[s], 0:VAL_COLS] = rc[:, 0:VAL_COLS].astype(_BF16)
        ry_ref[sub_rows[s], VAL_COLS:VAL_COLS + CONV_WIDTH] = yc.astype(_BF16)
        ry_ref[sub_rows[s], VAL_COLS + CONV_WIDTH:RY_COLS] = yp.astype(_BF16)

    for s in range(n_sub):
        g = gates[s].astype(_F32)
        for cl in range(cps):
            c = s * cps + cl
            lrows = slice(cl * GLA_CHUNK, (cl + 1) * GLA_CHUNK)
            rows = slice(c * GLA_CHUNK, (c + 1) * GLA_CHUNK)
            prefix = prefixes[c]
            total = prefix[GLA_CHUNK - 1:GLA_CHUNK, :]
            suffix = total - prefix + g[lrows]
            e = jnp.where(is_fwd, prefix, suffix)
            mvec = jnp.where(is_fwd, e[mid - 1:mid, :], e[mid:mid + 1, :])
            bm = e - mvec
            qe_ref[rows, :] = (q2s[s][lrows] * jnp.exp2(bm)).astype(_BF16)
            ki_ref[rows, :] = (k2s[s][lrows] * jnp.exp2(-bm)).astype(_BF16)
            kd = (k2s[s][lrows] * jnp.exp2(total - e)).astype(_BF16)
            dec_ref[c] = jnp.concatenate([total, mvec, jnp.zeros((6, KEY_COLS), _F32)], axis=0)
            for h in range(GLA_HEADS):
                vt_h = vtbs[s][h * GLA_DV:(h + 1) * GLA_DV, lrows]
                ut_ref[c, h] = _dot(vt_h, kd[:, h * SLAB:(h + 1) * SLAB]).astype(ut_ref.dtype)


def _inproj(x2d, mods3, mod_row_of_tile, layer, wts, tm, seg, vt_tile):
    tokens = x2d.shape[0]
    n_tiles = tokens // tm
    n_chunks = tokens // GLA_CHUNK
    cpt = tm // GLA_CHUNK
    assert SUB_ROWS % seg == 0
    assert tm % vt_tile == 0 and vt_tile % SUB_ROWS == 0
    of_layer = lambda shape: pl.BlockSpec((None,) + shape, lambda j: (layer,) + (0,) * len(shape))
    return pl.pallas_call(
        functools.partial(_inproj_kernel, seg=seg),
        grid=(n_tiles,),
        in_specs=[
            pl.BlockSpec((tm, D_MODEL), lambda j: (j, 0)),
            pl.BlockSpec((None, 1, N_MOD * D_MODEL), lambda j: (layer * MOD_ROWS + mod_row_of_tile(j), 0, 0)),
            of_layer((1, D_MODEL)),
            of_layer((D_MODEL, WIN_COLS)),
            of_layer((VAL_COLS, D_MODEL)),
            of_layer((2 * GATE_RANK, KEY_COLS)),
            of_layer((1, KEY_COLS)),
            of_layer((3, CONV_WIDTH)),
            of_layer((POOL_WIDTH, POOL_WIDTH)),
            of_layer((1, POOL_WIDTH)),
        ],
        out_specs=[
            pl.BlockSpec((tm, TOK_COLS), lambda j: (j, 0)),
            pl.BlockSpec((tm // vt_tile, VAL_COLS, vt_tile), lambda j: (j, 0, 0)),
            pl.BlockSpec((cpt, GLA_HEADS, GLA_DV, SLAB), lambda j: (j, 0, 0, 0)),
            pl.BlockSpec((cpt, 8, KEY_COLS), lambda j: (j, 0, 0)),
        ],
        out_shape=[
            jax.ShapeDtypeStruct((tokens, TOK_COLS), _BF16),
            jax.ShapeDtypeStruct((tokens // vt_tile, VAL_COLS, vt_tile), _BF16),
            jax.ShapeDtypeStruct((n_chunks, GLA_HEADS, GLA_DV, SLAB), _BF16),
            jax.ShapeDtypeStruct((n_chunks, 8, KEY_COLS), _F32),
        ],
        compiler_params=pltpu.CompilerParams(vmem_limit_bytes=VMEM_LIMIT_BYTES),
    )(x2d, mods3, wts["n1w"], wts["win"], wts["wvt"], wts["gw"], wts["gb"], wts["cw"], wts["pw"], wts["ps"])


def _scan_chunks(ut_ref, dec_ref, states, spt_ref, nc, chunk0=0, head0=0):
    lane = lax.broadcasted_iota(jnp.int32, (1, SLAB), 1)
    is_fwd = lane < GLA_DK

    def step(t, states):
        i = chunk0 + t
        j = chunk0 + nc - 1 - t
        new_states = []
        for s in range(SCAN_HEADS):
            h = head0 + s
            lanes = slice(h * SLAB, (h + 1) * SLAB)
            log_decay = jnp.where(is_fwd, dec_ref[i, 0:1, lanes], dec_ref[j, 0:1, lanes])
            log_mid = jnp.where(is_fwd, dec_ref[i, 1:2, lanes], dec_ref[j, 1:2, lanes])
            entering = (states[s] * jnp.exp2(log_mid)).astype(_BF16)
            spt_ref[i, h, :, 0:GLA_DK] = entering[:, 0:GLA_DK]
            spt_ref[j, h, :, GLA_DK:SLAB] = entering[:, GLA_DK:SLAB]
            inc = jnp.where(is_fwd, ut_ref[i, h], ut_ref[j, h]).astype(_F32)
            new_states.append(states[s] * jnp.exp2(log_decay) + inc)
        return tuple(new_states)

    return lax.fori_loop(0, nc, step, tuple(states), unroll=SCAN_UNROLL)


def _scan_kernel(ut_ref, dec_ref, s0_ref, spt_ref, sfin_ref):
    final = _scan_chunks(ut_ref, dec_ref, [s0_ref[h] for h in range(SCAN_HEADS)], spt_ref, ut_ref.shape[0])
    for h in range(SCAN_HEADS):
        sfin_ref[h] = final[h]


def _scan(ut, dec, s0, chunks_per_seq):
    n_chunks = ut.shape[0]
    bsz = n_chunks // chunks_per_seq
    return pl.pallas_call(
        _scan_kernel,
        grid=(bsz, GLA_HEADS // SCAN_HEADS),
        in_specs=[
            pl.BlockSpec((chunks_per_seq, SCAN_HEADS, GLA_DV, SLAB), lambda b, g: (b, g, 0, 0)),
            pl.BlockSpec((chunks_per_seq, 8, SCAN_HEADS * SLAB), lambda b, g: (b, 0, g)),
            pl.BlockSpec((None, SCAN_HEADS, GLA_DV, SLAB), lambda b, g: (b, g, 0, 0)),
        ],
        out_specs=[
            pl.BlockSpec((chunks_per_seq, SCAN_HEADS, GLA_DV, SLAB), lambda b, g: (b, g, 0, 0)),
            pl.BlockSpec((None, SCAN_HEADS, GLA_DV, SLAB), lambda b, g: (b, g, 0, 0)),
        ],
        out_shape=[
            jax.ShapeDtypeStruct((n_chunks, GLA_HEADS, GLA_DV, SLAB), _BF16),
            jax.ShapeDtypeStruct((bsz, GLA_HEADS, GLA_DV, SLAB), _F32),
        ],
        compiler_params=pltpu.CompilerParams(vmem_limit_bytes=VMEM_LIMIT_BYTES),
    )(ut, dec, s0)


def _mix_weight_stager(layer, wo_hbm, w1_hbm, w2_hbm, wo_ref, w1_ref, w2_ref, stage, sems):
    pieces = [(wo_hbm.at[layer], wo_ref)]
    for i in range(D_FF // D_MODEL):
        cols = slice(i * D_MODEL, (i + 1) * D_MODEL)
        pieces.append((w1_hbm.at[layer, :, cols], w1_ref.at[:, cols]))
        pieces.append((w2_hbm.at[layer, cols, :], w2_ref.at[cols, :]))
    copies = [pltpu.make_async_copy(src, stage.at[k % STAGE_SLOTS], sems.at[k % STAGE_SLOTS])
              for k, (src, _) in enumerate(pieces)]
    ready = [None]

    def need(n):
        if ready[0] is None:
            for k in range(STAGE_SLOTS):
                copies[k].start()
            ready[0] = 0
        for k in range(ready[0], min(n, len(pieces))):
            copies[k].wait()
            pieces[k][1][...] = stage[k % STAGE_SLOTS].astype(_BF16)
            if k + STAGE_SLOTS < len(pieces):
                copies[k + STAGE_SLOTS].start()
            ready[0] = k + 1

    return need


def _mix_body(x_ref, mod_ref, n2w_ref, tok_ref, vt_ref, spt_ref, gnw_ref, fnw_ref, o_ref, y_scr, wo_ref, w1_ref, w2_ref,
              need, final):
    tm = x_ref.shape[0]
    qe_ref, ki_ref, ry_ref = tok_ref.at[:, TOK_QE], tok_ref.at[:, TOK_KI], tok_ref.at[:, TOK_RY]
    g1 = mod_ref[:, 2 * D_MODEL:3 * D_MODEL]
    sh2 = mod_ref[:, 3 * D_MODEL:4 * D_MODEL]
    scale2 = n2w_ref[...] * (1.0 + mod_ref[:, 4 * D_MODEL:5 * D_MODEL])
    g2 = mod_ref[:, 5 * D_MODEL:6 * D_MODEL]
    row = lax.broadcasted_iota(jnp.int32, (GLA_CHUNK, GLA_CHUNK), 0)
    col = lax.broadcasted_iota(jnp.int32, (GLA_CHUNK, GLA_CHUNK), 1)
    lane = lax.broadcasted_iota(jnp.int32, (GLA_CHUNK, SLAB), 1)
    fwd_lane = lane < GLA_DK
    gnw = gnw_ref[...]
    need(0)

    for c in range(tm // GLA_CHUNK):
        rows = slice(c * GLA_CHUNK, (c + 1) * GLA_CHUNK)
        for h in range(GLA_HEADS):
            lanes = slice(h * SLAB, (h + 1) * SLAB)
            qe = qe_ref[rows, lanes]
            ki = ki_ref[rows, lanes]
            zero = jnp.zeros_like(ki)
            keys = jnp.concatenate([jnp.where(fwd_lane, ki, zero), jnp.where(fwd_lane, zero, ki)], axis=0)
            a2 = lax.dot_general(qe, keys, _NT, preferred_element_type=_F32)
            am = jnp.where(col <= row, a2[:, 0:GLA_CHUNK], 0.0) + jnp.where(col >= row, a2[:, GLA_CHUNK:], 0.0)
            lhs = jnp.concatenate([am.astype(_BF16), qe], axis=1)
            rhs = jnp.concatenate([vt_ref[h * GLA_DV:(h + 1) * GLA_DV, rows], spt_ref[c, h]], axis=1)
            o = lax.dot_general(lhs, rhs, _NT, preferred_element_type=_F32)
            o = o * lax.rsqrt(jnp.mean(o * o, axis=-1, keepdims=True) + EPS) * gnw
            r = ry_ref[rows, h * GLA_DV:(h + 1) * GLA_DV].astype(_F32)
            y_scr[rows, h * GLA_DV:(h + 1) * GLA_DV] = (o * (r * _sigmoid(r))).astype(_BF16)
        need(c + 1)

    need(1)
    mixed = _dot(y_scr[...], wo_ref[0:VAL_COLS, :]) + _dot(ry_ref[:, VAL_COLS:RY_COLS], wo_ref[VAL_COLS:D_MODEL, :])
    x1 = x_ref[...] + g1 * mixed
    xn = x1 * lax.rsqrt(jnp.mean(x1 * x1, axis=-1, keepdims=True) + EPS)
    h2 = (xn * scale2 + sh2).astype(_BF16)
    acc = jnp.zeros((tm, D_MODEL), _F32)
    for f in range(D_FF // FF_CHUNK):
        cols = slice(f * FF_CHUNK, (f + 1) * FF_CHUNK)
        need(1 + 2 * pl.cdiv(cols.stop, D_MODEL))
        a = jnp.maximum(_dot(h2, w1_ref[:, cols]), 0.0)
        acc = acc + _dot((a * a).astype(_BF16), w2_ref[cols, :])
    out = x1 + g2 * acc
    if final:
        out = out * lax.rsqrt(jnp.mean(out * out, axis=-1, keepdims=True) + EPS) * fnw_ref[...]
    o_ref[...] = out


def _mix_kernel(x_ref, mod_ref, n2w_ref, tok_ref, vt_ref, spt_ref, gnw_ref, wo_hbm, w1_hbm, w2_hbm, fnw_ref,
                o_ref, y_scr, wo_ref, w1_ref, w2_ref, stage, sems, *, layer, final, single_step):
    body = functools.partial(_mix_body, x_ref, mod_ref, n2w_ref, tok_ref, vt_ref, spt_ref, gnw_ref, fnw_ref, o_ref, y_scr,
                             wo_ref, w1_ref, w2_ref, final=final)

    def staging_step():
        body(need=_mix_weight_stager(layer, wo_hbm, w1_hbm, w2_hbm, wo_ref, w1_ref, w2_ref, stage, sems))

    if single_step:
        staging_step()
    else:
        first = pl.program_id(0) == 0
        pl.when(first)(staging_step)
        pl.when(jnp.logical_not(first))(functools.partial(body, need=lambda n: None))


def _mix_weight_scratch():
    return [pltpu.VMEM((D_MODEL, D_MODEL), _BF16), pltpu.VMEM((D_MODEL, D_FF), _BF16), pltpu.VMEM((D_FF, D_MODEL), _BF16),
            pltpu.VMEM((STAGE_SLOTS, D_MODEL, D_MODEL), _F32), pltpu.SemaphoreType.DMA((STAGE_SLOTS,))]


def _mix(x2d, mods3, mod_row_of_tile, layer, tok, vt, spt, wts, tm, final):
    tokens = x2d.shape[0]
    cpt = tm // GLA_CHUNK
    of_layer = lambda shape: pl.BlockSpec((None,) + shape, lambda j: (layer,) + (0,) * len(shape),
                                          pipeline_mode=pl.Buffered(1))
    return pl.pallas_call(
        functools.partial(_mix_kernel, layer=layer, final=final, single_step=tokens == tm),
        grid=(tokens // tm,),
        in_specs=[
            pl.BlockSpec((tm, D_MODEL), lambda j: (j, 0)),
            pl.BlockSpec((None, 1, N_MOD * D_MODEL), lambda j: (layer * MOD_ROWS + mod_row_of_tile(j), 0, 0)),
            of_layer((1, D_MODEL)),
            pl.BlockSpec((tm, TOK_COLS), lambda j: (j, 0)),
            pl.BlockSpec((None, VAL_COLS, tm), lambda j: (j, 0, 0)),
            pl.BlockSpec((cpt, GLA_HEADS, GLA_DV, SLAB), lambda j: (j, 0, 0, 0)),
            of_layer((1, GLA_DV)),
            pl.BlockSpec(memory_space=pl.ANY),
            pl.BlockSpec(memory_space=pl.ANY),
            pl.BlockSpec(memory_space=pl.ANY),
            pl.BlockSpec((1, D_MODEL), lambda j: (0, 0), pipeline_mode=pl.Buffered(1)),
        ],
        out_specs=pl.BlockSpec((tm, D_MODEL), lambda j: (j, 0)),
        out_shape=jax.ShapeDtypeStruct((tokens, D_MODEL), _F32),
        scratch_shapes=[pltpu.VMEM((tm, VAL_COLS), _BF16)] + _mix_weight_scratch(),
        compiler_params=pltpu.CompilerParams(vmem_limit_bytes=VMEM_LIMIT_BYTES),
    )(x2d, mods3, wts["n2w"], tok, vt, spt, wts["gnw"], wts["wo"], wts["w1"], wts["w2"], wts["fnw"])


def _context_kernel(*refs, layer, seg, chunks_per_seq, with_mix):
    n_in = 10
    x_ref, mod_ref = refs[0], refs[1]
    inproj_in = refs[0:n_in]
    if with_mix:
        n2w_ref, gnw_ref, wo_hbm, w1_hbm, w2_hbm, fnw_ref = refs[n_in:n_in + 6]
        sfin_ref, o_ref = refs[n_in + 6:n_in + 8]
        tok_scr, vt_scr, ut_scr, dec_scr, spt_scr, y_scr = refs[n_in + 8:n_in + 14]
        mix_weight_scr = refs[n_in + 14:]
    else:
        sfin_ref = refs[n_in]
        tok_scr, vt_scr, ut_scr, dec_scr, spt_scr = refs[n_in + 1:]

    _inproj_kernel(*inproj_in, tok_scr, vt_scr, ut_scr, dec_scr, seg=seg)

    sfin_ref[...] = jnp.zeros(sfin_ref.shape, _F32)
    for b in range(ut_scr.shape[0] // chunks_per_seq):
        for head0 in range(0, GLA_HEADS, SCAN_HEADS):
            start = [sfin_ref[b, head0 + s] for s in range(SCAN_HEADS)]
            final = _scan_chunks(ut_scr, dec_scr, start, spt_scr, chunks_per_seq,
                                 chunk0=b * chunks_per_seq, head0=head0)
            for s in range(SCAN_HEADS):
                sfin_ref[b, head0 + s] = final[s]

    if with_mix:
        _mix_kernel(x_ref, mod_ref, n2w_ref, tok_scr, vt_scr.at[0], spt_scr, gnw_ref, wo_hbm, w1_hbm, w2_hbm, fnw_ref,
                    o_ref, y_scr, *mix_weight_scr, layer=layer, final=False, single_step=True)


def _context_layer(xc, mods3, mod_row, layer, wts, ctx_len, with_mix):
    rows = xc.shape[0]
    bsz = rows // ctx_len
    n_chunks = rows // GLA_CHUNK
    assert rows % SUB_ROWS == 0 and SUB_ROWS % ctx_len == 0
    of_layer = lambda shape: pl.BlockSpec((None,) + shape, lambda j: (layer,) + (0,) * len(shape),
                                          pipeline_mode=pl.Buffered(1))
    in_specs = [
        pl.BlockSpec((rows, D_MODEL), lambda j: (0, 0)),
        pl.BlockSpec((None, 1, N_MOD * D_MODEL), lambda j: (layer * MOD_ROWS + mod_row, 0, 0)),
        of_layer((1, D_MODEL)),
        of_layer((D_MODEL, WIN_COLS)),
        of_layer((VAL_COLS, D_MODEL)),
        of_layer((2 * GATE_RANK, KEY_COLS)),
        of_layer((1, KEY_COLS)),
        of_layer((3, CONV_WIDTH)),
        of_layer((POOL_WIDTH, POOL_WIDTH)),
        of_layer((1, POOL_WIDTH)),
    ]
    args = [xc, mods3, wts["n1w"], wts["win"], wts["wvt"], wts["gw"], wts["gb"], wts["cw"], wts["pw"], wts["ps"]]
    out_specs = [pl.BlockSpec((bsz, GLA_HEADS, GLA_DV, SLAB), lambda j: (0, 0, 0, 0))]
    out_shape = [jax.ShapeDtypeStruct((bsz, GLA_HEADS, GLA_DV, SLAB), _F32)]
    scratch = [
        pltpu.VMEM((rows, TOK_COLS), _BF16),
        pltpu.VMEM((1, VAL_COLS, rows), _BF16),
        pltpu.VMEM((n_chunks, GLA_HEADS, GLA_DV, SLAB), _BF16),
        pltpu.VMEM((n_chunks, 8, KEY_COLS), _F32),
        pltpu.VMEM((n_chunks, GLA_HEADS, GLA_DV, SLAB), _BF16),
    ]
    if with_mix:
        in_specs += [of_layer((1, D_MODEL)), of_layer((1, GLA_DV))] + [pl.BlockSpec(memory_space=pl.ANY)] * 3
        in_specs += [pl.BlockSpec((1, D_MODEL), lambda j: (0, 0), pipeline_mode=pl.Buffered(1))]
        args += [wts["n2w"], wts["gnw"], wts["wo"], wts["w1"], wts["w2"], wts["fnw"]]
        out_specs.append(pl.BlockSpec((rows, D_MODEL), lambda j: (0, 0)))
        out_shape.append(jax.ShapeDtypeStruct((rows, D_MODEL), _F32))
        scratch += [pltpu.VMEM((rows, VAL_COLS), _BF16)] + _mix_weight_scratch()
    outs = pl.pallas_call(
        functools.partial(_context_kernel, layer=layer, seg=ctx_len, chunks_per_seq=ctx_len // GLA_CHUNK,
                          with_mix=with_mix),
        grid=(1,),
        in_specs=in_specs,
        out_specs=out_specs,
        out_shape=out_shape,
        scratch_shapes=scratch,
        compiler_params=pltpu.CompilerParams(vmem_limit_bytes=VMEM_LIMIT_BYTES),
    )(*args)
    return (outs[0], outs[1]) if with_mix else (outs[0], None)


def _prepare_weights(norm1_w, norm2_w, w_in, gate_w, gate_b, gla_norm_w, conv_w, pool_w, pool_scale, w_out,
                     w_mlp1, w_mlp2, final_norm_w):
    depth = w_in.shape[0]
    key_w = GLA_HEADS * GLA_DK
    q_off, k_off, v_off = 0, key_w, 2 * key_w
    g_off = v_off + VAL_COLS
    r_off = g_off + 2 * GATE_RANK
    z_pad = jnp.zeros((depth, D_MODEL, WQ_COLS.start - WZ_COLS.stop), _F32)
    win = jnp.concatenate([w_in[:, :, g_off:r_off], z_pad, w_in[:, :, q_off:v_off], w_in[:, :, r_off:]], axis=-1)

    zeros = jnp.zeros((depth, GATE_RANK, GLA_HEADS, GLA_DK), _F32)
    gf = gate_w[:, 0].reshape(depth, GATE_RANK, GLA_HEADS, GLA_DK)
    gb = gate_w[:, 1].reshape(depth, GATE_RANK, GLA_HEADS, GLA_DK)
    gw = jnp.concatenate([jnp.concatenate([gf, zeros], -1), jnp.concatenate([zeros, gb], -1)], 1)
    gbias = jnp.concatenate([gate_b[:, 0].reshape(depth, GLA_HEADS, GLA_DK),
                             gate_b[:, 1].reshape(depth, GLA_HEADS, GLA_DK)], -1)
    pw = jnp.zeros((depth, POOL_WIDTH, POOL_WIDTH), _F32)
    for gi in range(POOL_WIDTH // POOL_GROUP):
        sl = slice(gi * POOL_GROUP, (gi + 1) * POOL_GROUP)
        pw = pw.at[:, sl, sl].set(pool_w[:, gi])
    return {
        "n1w": norm1_w.reshape(depth, 1, D_MODEL),
        "n2w": norm2_w.reshape(depth, 1, D_MODEL),
        "win": win.astype(_BF16),
        "wvt": jnp.swapaxes(w_in[:, :, v_off:g_off], 1, 2).astype(_BF16),
        "gw": gw.reshape(depth, 2 * GATE_RANK, KEY_COLS).astype(_BF16),
        "gb": gbias.reshape(depth, 1, KEY_COLS),
        "gnw": gla_norm_w.reshape(depth, 1, GLA_DV),
        "cw": conv_w,
        "pw": pw.astype(_BF16),
        "ps": pool_scale.reshape(depth, 1, POOL_WIDTH),
        "wo": w_out,
        "w1": w_mlp1,
        "w2": w_mlp2,
        "fnw": final_norm_w.reshape(1, D_MODEL),
    }


def kernel(x, c, ctx, c_ctx, w_mod, b_mod, norm1_w, norm2_w, w_in, gla_gate_w, gla_gate_b, gla_norm_w, conv_w, pool_w, pool_scale, w_out, w_mlp1, w_mlp2, final_norm_w):
    bsz, n, _ = x.shape
    ctx_len = ctx.shape[1]
    depth = w_in.shape[0]
    assert bsz + 1 <= MOD_ROWS and n % LATENT_IN_TILE == 0 and SUB_ROWS % ctx_len == 0
    ctx_row = bsz

    cv = jnp.concatenate([c, c_ctx[None, :], jnp.zeros((MOD_ROWS - bsz - 1, D_MODEL), _F32)], axis=0)
    mods = _adaln(cv, w_mod, b_mod)
    mods3 = mods.reshape(depth * MOD_ROWS, 1, N_MOD * D_MODEL)
    wts = _prepare_weights(norm1_w, norm2_w, w_in, gla_gate_w, gla_gate_b, gla_norm_w, conv_w, pool_w, pool_scale,
                           w_out, w_mlp1, w_mlp2, final_norm_w)

    xl = x.reshape(bsz * n, D_MODEL)
    xc = ctx.reshape(bsz * ctx_len, D_MODEL)
    lat_in_row = lambda j: j // (n // LATENT_IN_TILE)
    lat_mix_row = lambda j: j // (n // LATENT_MIX_TILE)

    for l in range(depth):
        last = l == depth - 1
        s_ctx, xc = _context_layer(xc, mods3, ctx_row, l, wts, ctx_len, with_mix=not last)

        tok, vt, ut, dec = _inproj(xl, mods3, lat_in_row, l, wts, LATENT_IN_TILE, GRID_W, LATENT_MIX_TILE)
        spt, _ = _scan(ut, dec, s_ctx, n // GLA_CHUNK)
        xl = _mix(xl, mods3, lat_mix_row, l, tok, vt, spt, wts, LATENT_MIX_TILE, last)
    return xl.reshape(bsz, n, D_MODEL)
```

```python
import functools

import jax
import jax.numpy as jnp
from jax import lax
from jax.experimental import pallas as pl
from jax.experimental.pallas import tpu as pltpu

D_MODEL = 1024
GLA_HEADS = 4
GLA_DK = 64
GLA_DV = 128
SLAB = 2 * GLA_DK
KEY_COLS = GLA_HEADS * SLAB
VAL_COLS = GLA_HEADS * GLA_DV
GATE_RANK = 16
GATE_TAU = 16.0
LOG2_E = 1.4426950408889634
GLA_CHUNK = 128
SUB_ROWS = 256
LATENT_IN_TILE = 1024
LATENT_MIX_TILE = 512
CONV_WIDTH = 256
POOL_WIDTH = 256
POOL_GROUP = 64
POOL_HALF_WINDOWS = (1, 2, 4, 8)
D_FF = 4096
FF_CHUNK = 1024
N_MOD = 6
EPS = 1e-6
GRID_W = 64
RC_COLS = VAL_COLS + 3 * CONV_WIDTH + POOL_WIDTH
RY_COLS = VAL_COLS + CONV_WIDTH + POOL_WIDTH
TOK_QE = slice(0, KEY_COLS)
TOK_KI = slice(KEY_COLS, 2 * KEY_COLS)
TOK_RY = slice(2 * KEY_COLS, 2 * KEY_COLS + RY_COLS)
TOK_COLS = TOK_RY.stop
WZ_COLS = slice(0, 2 * GATE_RANK)
WQ_COLS = slice(128, 128 + GLA_HEADS * GLA_DK)
WK_COLS = slice(WQ_COLS.stop, WQ_COLS.stop + GLA_HEADS * GLA_DK)
WRC_COLS = slice(WK_COLS.stop, WK_COLS.stop + RC_COLS)
WIN_COLS = WRC_COLS.stop
IN_QK = slice(0, 2 * GLA_HEADS * GLA_DK)
IN_V = slice(IN_QK.stop, IN_QK.stop + VAL_COLS)
IN_Z = slice(IN_V.stop, IN_V.stop + 2 * GATE_RANK)
IN_RC = slice(IN_Z.stop, IN_Z.stop + RC_COLS)
IN_COLS = IN_RC.stop
IN_STAGE_ROWS = 256
SCAN_UNROLL = 2
SCAN_HEADS = 2
MOD_ROWS = 8
ADALN_COLS = 1536
VMEM_LIMIT_BYTES = 56 * 1024 * 1024
STAGE_SLOTS = 2

_NT = (((1,), (1,)), ((), ()))
_BF16 = jnp.bfloat16
_F32 = jnp.float32


def _dot(a, b):
    return jnp.dot(a, b, preferred_element_type=_F32)


def _sigmoid(x):
    return 1.0 / (1.0 + jnp.exp(-x))


def _adaln_kernel(cv_ref, w_ref, b_ref, o_ref):
    cv = cv_ref[...]
    s = (cv * _sigmoid(cv)).astype(_BF16)
    o_ref[...] = _dot(s, w_ref[...].astype(_BF16)) + b_ref[...]


def _adaln(cv, w_mod, b_mod):
    depth = w_mod.shape[0]
    cols = w_mod.shape[2]
    return pl.pallas_call(
        _adaln_kernel,
        grid=(depth, cols // ADALN_COLS),
        in_specs=[
            pl.BlockSpec((MOD_ROWS, D_MODEL), lambda l, j: (0, 0)),
            pl.BlockSpec((None, D_MODEL, ADALN_COLS), lambda l, j: (l, 0, j)),
            pl.BlockSpec((None, 1, ADALN_COLS), lambda l, j: (l, 0, j)),
        ],
        out_specs=pl.BlockSpec((None, MOD_ROWS, ADALN_COLS), lambda l, j: (l, 0, j)),
        out_shape=jax.ShapeDtypeStruct((depth, MOD_ROWS, cols), _F32),
        compiler_params=pltpu.CompilerParams(vmem_limit_bytes=VMEM_LIMIT_BYTES),
    )(cv, w_mod, b_mod.reshape(depth, 1, cols))


def _per_direction(t):
    lane = lax.broadcasted_iota(jnp.int32, (1, SLAB), 1)
    first = lane < GLA_DK
    slabs = []
    for p in range(GLA_HEADS // 2):
        pair = t[:, p * SLAB:(p + 1) * SLAB]
        swapped = pltpu.roll(pair, GLA_DK, 1)
        slabs += [jnp.where(first, pair, swapped), jnp.where(first, swapped, pair)]
    return jnp.concatenate(slabs, axis=1)


def _conv_and_pool(rc, cw_ref, pw_ref, ps_ref, seg):
    rows = rc.shape[0]
    pos = lax.broadcasted_iota(jnp.int32, (rows, 1), 0) % seg
    c0 = 0
    cb = rc[:, c0:c0 + CONV_WIDTH]
    zc = rc[:, c0 + CONV_WIDTH:c0 + 2 * CONV_WIDTH] * rc[:, c0 + 2 * CONV_WIDTH:c0 + 3 * CONV_WIDTH]
    z_prev = jnp.where(pos >= 1, pltpu.roll(zc, 1, 0), 0.0)
    z_next = jnp.where(pos <= seg - 2, pltpu.roll(zc, rows - 1, 0), 0.0)
    yc = cb * (cw_ref[0:1, :] * z_prev + cw_ref[1:2, :] * zc + cw_ref[2:3, :] * z_next)

    p0 = 3 * CONV_WIDTH
    lane128 = lax.broadcasted_iota(jnp.int32, (1, 2 * POOL_GROUP), 1)
    pooled = []
    for blk in range(POOL_WIDTH // (2 * POOL_GROUP)):
        u = rc[:, p0 + blk * 2 * POOL_GROUP:p0 + (blk + 1) * 2 * POOL_GROUP]
        h_lo, h_hi = POOL_HALF_WINDOWS[2 * blk], POOL_HALF_WINDOWS[2 * blk + 1]
        half = jnp.where(lane128 < POOL_GROUP, h_lo, h_hi)
        acc = jnp.zeros_like(u)
        for d in range(-h_hi, h_hi):
            shifted = u if d == 0 else pltpu.roll(u, (-d) % rows, 0)
            valid = (pos + d >= 0) & (pos + d < seg) & (d >= -half) & (d < half)
            acc = acc + jnp.where(valid, shifted, 0.0)
        cnt = (jnp.minimum(pos + half, seg) - jnp.maximum(pos - half, 0)).astype(_F32)
        pooled.append(acc / cnt - u)
    pool_in = jnp.concatenate(pooled, axis=1).astype(_BF16)
    return yc, _dot(pool_in, pw_ref[...]) * ps_ref[...]


def _inproj_kernel(x_ref, mod_ref, n1w_ref, win_ref, wvt_ref, gw_ref, gb_ref, cw_ref, pw_ref, ps_ref,
                   tok_ref, vt_ref, ut_ref, dec_ref, *, seg):
    qe_ref, ki_ref, ry_ref = tok_ref.at[:, TOK_QE], tok_ref.at[:, TOK_KI], tok_ref.at[:, TOK_RY]
    vt_tile = vt_ref.shape[2]
    tm = x_ref.shape[0]
    sh1 = mod_ref[:, 0:D_MODEL]
    scale1 = n1w_ref[...] * (1.0 + mod_ref[:, D_MODEL:2 * D_MODEL])
    row = lax.broadcasted_iota(jnp.int32, (GLA_CHUNK, GLA_CHUNK), 0)
    col = lax.broadcasted_iota(jnp.int32, (GLA_CHUNK, GLA_CHUNK), 1)
    tri = (col <= row).astype(_BF16)
    lane = lax.broadcasted_iota(jnp.int32, (1, KEY_COLS), 1)
    is_fwd = (lane % SLAB) < GLA_DK
    mid = GLA_CHUNK // 2

    n_sub = tm // SUB_ROWS
    cps = SUB_ROWS // GLA_CHUNK
    sub_rows = [slice(s * SUB_ROWS, (s + 1) * SUB_ROWS) for s in range(n_sub)]
    hbs, projs, gates, q2s, k2s = [], [], [], [], []
    for s in range(n_sub):
        x = x_ref[sub_rows[s], :]
        xn = x * lax.rsqrt(jnp.mean(x * x, axis=-1, keepdims=True) + EPS)
        hbs.append((xn * scale1 + sh1).astype(_BF16))
    for s in range(n_sub):
        proj = _dot(hbs[s], win_ref[...])
        projs.append(proj)
        pre = _dot(proj[:, WZ_COLS].astype(_BF16), gw_ref[...]) + gb_ref[...]
        g = (jnp.minimum(pre, 0.0) * LOG2_E - jnp.log2(1.0 + jnp.exp2(jnp.abs(pre) * (-LOG2_E)))) * (1.0 / GATE_TAU)
        gates.append(g.astype(_BF16))
        q2s.append(_per_direction(proj[:, WQ_COLS] * (GLA_DK ** -0.5)))
        k2s.append(_per_direction(proj[:, WK_COLS]))

    prefixes = []
    for s in range(n_sub):
        for cl in range(cps):
            lrows = slice(cl * GLA_CHUNK, (cl + 1) * GLA_CHUNK)
            prefixes.append(_dot(tri, gates[s][lrows]))

    vtbs = []
    for s in range(n_sub):
        vtb = lax.dot_general(wvt_ref[...], hbs[s], _NT, preferred_element_type=_F32).astype(_BF16)
        first = s * SUB_ROWS
        vt_ref[first // vt_tile, :, first % vt_tile:first % vt_tile + SUB_ROWS] = vtb
        vtbs.append(vtb)
        rc = projs[s][:, WRC_COLS]
        yc, yp = _conv_and_pool(rc[:, VAL_COLS:RC_COLS], cw_ref, pw_ref, ps_ref, seg)
        ry_ref[sub_rows[s], 0:VAL_COLS] = rc[:, 0:VAL_COLS].astype(_BF16)
        ry_ref[sub_rows[s], VAL_COLS:VAL_COLS + CONV_WIDTH] = yc.astype(_BF16)
        ry_ref[sub_rows[s], VAL_COLS + CONV_WIDTH:RY_COLS] = yp.astype(_BF16)

    for s in range(n_sub):
        g = gates[s].astype(_F32)
        for cl in range(cps):
            c = s * cps + cl
            lrows = slice(cl * GLA_CHUNK, (cl + 1) * GLA_CHUNK)
            rows = slice(c * GLA_CHUNK, (c + 1) * GLA_CHUNK)
            prefix = prefixes[c]
            total = prefix[GLA_CHUNK - 1:GLA_CHUNK, :]
            suffix = total - prefix + g[lrows]
            e = jnp.where(is_fwd, prefix, suffix)
            mvec = jnp.where(is_fwd, e[mid - 1:mid, :], e[mid:mid + 1, :])
            bm = e - mvec
            qe_ref[rows, :] = (q2s[s][lrows] * jnp.exp2(bm)).astype(_BF16)
            ki_ref[rows, :] = (k2s[s][lrows] * jnp.exp2(-bm)).astype(_BF16)
            kd = (k2s[s][lrows] * jnp.exp2(total - e)).astype(_BF16)
            dec_ref[c] = jnp.concatenate([total, mvec, jnp.zeros((6, KEY_COLS), _F32)], axis=0)
            for h in range(GLA_HEADS):
                vt_h = vtbs[s][h * GLA_DV:(h + 1) * GLA_DV, lrows]
                ut_ref[c, h] = _dot(vt_h, kd[:, h * SLAB:(h + 1) * SLAB]).astype(ut_ref.dtype)


def _inproj(x2d, mods3, mod_row_of_tile, layer, win, wts, tm, seg, vt_tile):
    tokens = x2d.shape[0]
    n_tiles = tokens // tm
    n_chunks = tokens // GLA_CHUNK
    cpt = tm // GLA_CHUNK
    assert SUB_ROWS % seg == 0
    assert tm % vt_tile == 0 and vt_tile % SUB_ROWS == 0
    of_layer = lambda shape: pl.BlockSpec((None,) + shape, lambda j: (layer,) + (0,) * len(shape))
    return pl.pallas_call(
        functools.partial(_inproj_kernel, seg=seg),
        grid=(n_tiles,),
        in_specs=[
            pl.BlockSpec((tm, D_MODEL), lambda j: (j, 0)),
            pl.BlockSpec((None, 1, N_MOD * D_MODEL), lambda j: (layer * MOD_ROWS + mod_row_of_tile(j), 0, 0)),
            of_layer((1, D_MODEL)),
            pl.BlockSpec((D_MODEL, WIN_COLS), lambda j: (0, 0)),
            of_layer((VAL_COLS, D_MODEL)),
            of_layer((2 * GATE_RANK, KEY_COLS)),
            of_layer((1, KEY_COLS)),
            of_layer((3, CONV_WIDTH)),
            of_layer((POOL_WIDTH, POOL_WIDTH)),
            of_layer((1, POOL_WIDTH)),
        ],
        out_specs=[
            pl.BlockSpec((tm, TOK_COLS), lambda j: (j, 0)),
            pl.BlockSpec((tm // vt_tile, VAL_COLS, vt_tile), lambda j: (j, 0, 0)),
            pl.BlockSpec((cpt, GLA_HEADS, GLA_DV, SLAB), lambda j: (j, 0, 0, 0)),
            pl.BlockSpec((cpt, 8, KEY_COLS), lambda j: (j, 0, 0)),
        ],
        out_shape=[
            jax.ShapeDtypeStruct((tokens, TOK_COLS), _BF16),
            jax.ShapeDtypeStruct((tokens // vt_tile, VAL_COLS, vt_tile), _BF16),
            jax.ShapeDtypeStruct((n_chunks, GLA_HEADS, GLA_DV, SLAB), _BF16),
            jax.ShapeDtypeStruct((n_chunks, 8, KEY_COLS), _F32),
        ],
        compiler_params=pltpu.CompilerParams(vmem_limit_bytes=VMEM_LIMIT_BYTES),
    )(x2d, mods3, wts["n1w"], win, wts["wvt"], wts["gw"], wts["gb"], wts["cw"], wts["pw"], wts["ps"])


def _scan_chunks(ut_ref, dec_ref, states, spt_ref, nc, chunk0=0, head0=0):
    lane = lax.broadcasted_iota(jnp.int32, (1, SLAB), 1)
    is_fwd = lane < GLA_DK

    def step(t, states):
        i = chunk0 + t
        j = chunk0 + nc - 1 - t
        new_states = []
        for s in range(SCAN_HEADS):
            h = head0 + s
            lanes = slice(h * SLAB, (h + 1) * SLAB)
            log_decay = jnp.where(is_fwd, dec_ref[i, 0:1, lanes], dec_ref[j, 0:1, lanes])
            log_mid = jnp.where(is_fwd, dec_ref[i, 1:2, lanes], dec_ref[j, 1:2, lanes])
            entering = (states[s] * jnp.exp2(log_mid)).astype(_BF16)
            spt_ref[i, h, :, 0:GLA_DK] = entering[:, 0:GLA_DK]
            spt_ref[j, h, :, GLA_DK:SLAB] = entering[:, GLA_DK:SLAB]
            inc = jnp.where(is_fwd, ut_ref[i, h], ut_ref[j, h]).astype(_F32)
            new_states.append(states[s] * jnp.exp2(log_decay) + inc)
        return tuple(new_states)

    return lax.fori_loop(0, nc, step, tuple(states), unroll=SCAN_UNROLL)


def _scan_kernel(ut_ref, dec_ref, s0_ref, spt_ref, sfin_ref):
    final = _scan_chunks(ut_ref, dec_ref, [s0_ref[h] for h in range(SCAN_HEADS)], spt_ref, ut_ref.shape[0])
    for h in range(SCAN_HEADS):
        sfin_ref[h] = final[h]


def _scan(ut, dec, s0, chunks_per_seq):
    n_chunks = ut.shape[0]
    bsz = n_chunks // chunks_per_seq
    return pl.pallas_call(
        _scan_kernel,
        grid=(bsz, GLA_HEADS // SCAN_HEADS),
        in_specs=[
            pl.BlockSpec((chunks_per_seq, SCAN_HEADS, GLA_DV, SLAB), lambda b, g: (b, g, 0, 0)),
            pl.BlockSpec((chunks_per_seq, 8, SCAN_HEADS * SLAB), lambda b, g: (b, 0, g)),
            pl.BlockSpec((None, SCAN_HEADS, GLA_DV, SLAB), lambda b, g: (b, g, 0, 0)),
        ],
        out_specs=[
            pl.BlockSpec((chunks_per_seq, SCAN_HEADS, GLA_DV, SLAB), lambda b, g: (b, g, 0, 0)),
            pl.BlockSpec((None, SCAN_HEADS, GLA_DV, SLAB), lambda b, g: (b, g, 0, 0)),
        ],
        out_shape=[
            jax.ShapeDtypeStruct((n_chunks, GLA_HEADS, GLA_DV, SLAB), _BF16),
            jax.ShapeDtypeStruct((bsz, GLA_HEADS, GLA_DV, SLAB), _F32),
        ],
        compiler_params=pltpu.CompilerParams(vmem_limit_bytes=VMEM_LIMIT_BYTES),
    )(ut, dec, s0)


def _mix_weight_stager(layer, wo_hbm, w1_hbm, w2_hbm, wo_ref, w1_ref, w2_ref, stage, sems):
    pieces = [(wo_hbm.at[layer], wo_ref)]
    for i in range(D_FF // D_MODEL):
        cols = slice(i * D_MODEL, (i + 1) * D_MODEL)
        pieces.append((w1_hbm.at[layer, :, cols], w1_ref.at[:, cols]))
        pieces.append((w2_hbm.at[layer, cols, :], w2_ref.at[cols, :]))
    copies = [pltpu.make_async_copy(src, stage.at[k % STAGE_SLOTS], sems.at[k % STAGE_SLOTS])
              for k, (src, _) in enumerate(pieces)]
    ready = [None]

    def need(n):
        if ready[0] is None:
            for k in range(STAGE_SLOTS):
                copies[k].start()
            ready[0] = 0
        for k in range(ready[0], min(n, len(pieces))):
            copies[k].wait()
            pieces[k][1][...] = stage[k % STAGE_SLOTS].astype(_BF16)
            if k + STAGE_SLOTS < len(pieces):
                copies[k + STAGE_SLOTS].start()
            ready[0] = k + 1

    return need


def _mix_body(x_ref, mod_ref, n2w_ref, tok_ref, vt_ref, spt_ref, gnw_ref, fnw_ref, o_ref, y_scr, wo_ref, w1_ref, w2_ref,
              need, final):
    tm = x_ref.shape[0]
    qe_ref, ki_ref, ry_ref = tok_ref.at[:, TOK_QE], tok_ref.at[:, TOK_KI], tok_ref.at[:, TOK_RY]
    g1 = mod_ref[:, 2 * D_MODEL:3 * D_MODEL]
    sh2 = mod_ref[:, 3 * D_MODEL:4 * D_MODEL]
    scale2 = n2w_ref[...] * (1.0 + mod_ref[:, 4 * D_MODEL:5 * D_MODEL])
    g2 = mod_ref[:, 5 * D_MODEL:6 * D_MODEL]
    row = lax.broadcasted_iota(jnp.int32, (GLA_CHUNK, GLA_CHUNK), 0)
    col = lax.broadcasted_iota(jnp.int32, (GLA_CHUNK, GLA_CHUNK), 1)
    lane = lax.broadcasted_iota(jnp.int32, (GLA_CHUNK, SLAB), 1)
    fwd_lane = lane < GLA_DK
    gnw = gnw_ref[...]
    need(0)

    for c in range(tm // GLA_CHUNK):
        rows = slice(c * GLA_CHUNK, (c + 1) * GLA_CHUNK)
        for h in range(GLA_HEADS):
            lanes = slice(h * SLAB, (h + 1) * SLAB)
            qe = qe_ref[rows, lanes]
            ki = ki_ref[rows, lanes]
            zero = jnp.zeros_like(ki)
            keys = jnp.concatenate([jnp.where(fwd_lane, ki, zero), jnp.where(fwd_lane, zero, ki)], axis=0)
            a2 = lax.dot_general(qe, keys, _NT, preferred_element_type=_F32)
            am = jnp.where(col <= row, a2[:, 0:GLA_CHUNK], 0.0) + jnp.where(col >= row, a2[:, GLA_CHUNK:], 0.0)
            lhs = jnp.concatenate([am.astype(_BF16), qe], axis=1)
            rhs = jnp.concatenate([vt_ref[h * GLA_DV:(h + 1) * GLA_DV, rows], spt_ref[c, h]], axis=1)
            o = lax.dot_general(lhs, rhs, _NT, preferred_element_type=_F32)
            o = o * lax.rsqrt(jnp.mean(o * o, axis=-1, keepdims=True) + EPS) * gnw
            r = ry_ref[rows, h * GLA_DV:(h + 1) * GLA_DV].astype(_F32)
            y_scr[rows, h * GLA_DV:(h + 1) * GLA_DV] = (o * (r * _sigmoid(r))).astype(_BF16)
        need(c + 1)

    need(1)
    mixed = _dot(y_scr[...], wo_ref[0:VAL_COLS, :]) + _dot(ry_ref[:, VAL_COLS:RY_COLS], wo_ref[VAL_COLS:D_MODEL, :])
    x1 = x_ref[...] + g1 * mixed
    xn = x1 * lax.rsqrt(jnp.mean(x1 * x1, axis=-1, keepdims=True) + EPS)
    h2 = (xn * scale2 + sh2).astype(_BF16)
    acc = jnp.zeros((tm, D_MODEL), _F32)
    for f in range(D_FF // FF_CHUNK):
        cols = slice(f * FF_CHUNK, (f + 1) * FF_CHUNK)
        need(1 + 2 * pl.cdiv(cols.stop, D_MODEL))
        a = jnp.maximum(_dot(h2, w1_ref[:, cols]), 0.0)
        acc = acc + _dot((a * a).astype(_BF16), w2_ref[cols, :])
    out = x1 + g2 * acc
    if final:
        out = out * lax.rsqrt(jnp.mean(out * out, axis=-1, keepdims=True) + EPS) * fnw_ref[...]
    o_ref[...] = out


def _mix_kernel(x_ref, mod_ref, n2w_ref, tok_ref, vt_ref, spt_ref, gnw_ref, wo_hbm, w1_hbm, w2_hbm, fnw_ref,
                o_ref, y_scr, wo_ref, w1_ref, w2_ref, stage, sems, *, layer, final, single_step):
    body = functools.partial(_mix_body, x_ref, mod_ref, n2w_ref, tok_ref, vt_ref, spt_ref, gnw_ref, fnw_ref, o_ref, y_scr,
                             wo_ref, w1_ref, w2_ref, final=final)

    def staging_step():
        body(need=_mix_weight_stager(layer, wo_hbm, w1_hbm, w2_hbm, wo_ref, w1_ref, w2_ref, stage, sems))

    if single_step:
        staging_step()
    else:
        first = pl.program_id(0) == 0
        pl.when(first)(staging_step)
        pl.when(jnp.logical_not(first))(functools.partial(body, need=lambda n: None))


def _mix_weight_scratch():
    return [pltpu.VMEM((D_MODEL, D_MODEL), _BF16), pltpu.VMEM((D_MODEL, D_FF), _BF16), pltpu.VMEM((D_FF, D_MODEL), _BF16),
            pltpu.VMEM((STAGE_SLOTS, D_MODEL, D_MODEL), _F32), pltpu.SemaphoreType.DMA((STAGE_SLOTS,))]


def _mix(x2d, mods3, mod_row_of_tile, layer, tok, vt, spt, wts, tm, final):
    tokens = x2d.shape[0]
    cpt = tm // GLA_CHUNK
    of_layer = lambda shape: pl.BlockSpec((None,) + shape, lambda j: (layer,) + (0,) * len(shape),
                                          pipeline_mode=pl.Buffered(1))
    return pl.pallas_call(
        functools.partial(_mix_kernel, layer=layer, final=final, single_step=tokens == tm),
        grid=(tokens // tm,),
        in_specs=[
            pl.BlockSpec((tm, D_MODEL), lambda j: (j, 0)),
            pl.BlockSpec((None, 1, N_MOD * D_MODEL), lambda j: (layer * MOD_ROWS + mod_row_of_tile(j), 0, 0)),
            of_layer((1, D_MODEL)),
            pl.BlockSpec((tm, TOK_COLS), lambda j: (j, 0)),
            pl.BlockSpec((None, VAL_COLS, tm), lambda j: (j, 0, 0)),
            pl.BlockSpec((cpt, GLA_HEADS, GLA_DV, SLAB), lambda j: (j, 0, 0, 0)),
            of_layer((1, GLA_DV)),
            pl.BlockSpec(memory_space=pl.ANY),
            pl.BlockSpec(memory_space=pl.ANY),
            pl.BlockSpec(memory_space=pl.ANY),
            pl.BlockSpec((1, D_MODEL), lambda j: (0, 0), pipeline_mode=pl.Buffered(1)),
        ],
        out_specs=pl.BlockSpec((tm, D_MODEL), lambda j: (j, 0)),
        out_shape=jax.ShapeDtypeStruct((tokens, D_MODEL), _F32),
        scratch_shapes=[pltpu.VMEM((tm, VAL_COLS), _BF16)] + _mix_weight_scratch(),
        compiler_params=pltpu.CompilerParams(vmem_limit_bytes=VMEM_LIMIT_BYTES),
    )(x2d, mods3, wts["n2w"], tok, vt, spt, wts["gnw"], wts["wo"], wts["w1"], wts["w2"], wts["fnw"])


def _stage_in_projection(layer, w_in_hbm, win_ref, stage, sems):
    n_pieces = D_MODEL // IN_STAGE_ROWS
    copies = [pltpu.make_async_copy(w_in_hbm.at[layer, k * IN_STAGE_ROWS:(k + 1) * IN_STAGE_ROWS, :],
                                    stage.at[k % STAGE_SLOTS], sems.at[k % STAGE_SLOTS]) for k in range(n_pieces)]
    for k in range(STAGE_SLOTS):
        copies[k].start()
    for k in range(n_pieces):
        rows = slice(k * IN_STAGE_ROWS, (k + 1) * IN_STAGE_ROWS)
        piece = stage.at[k % STAGE_SLOTS]
        copies[k].wait()
        win_ref[rows, 0:WQ_COLS.start] = piece[:, IN_Z.start:IN_Z.start + WQ_COLS.start].astype(_BF16)
        win_ref[rows, WQ_COLS.start:WK_COLS.stop] = piece[:, IN_QK].astype(_BF16)
        win_ref[rows, WRC_COLS] = piece[:, IN_RC].astype(_BF16)
        if k + STAGE_SLOTS < n_pieces:
            copies[k + STAGE_SLOTS].start()


def _context_kernel(*refs, layer, seg, chunks_per_seq, with_mix):
    n_in = 10
    x_ref, mod_ref, n1w_ref, w_in_hbm = refs[0:4]
    if with_mix:
        n2w_ref, gnw_ref, wo_hbm, w1_hbm, w2_hbm, fnw_ref = refs[n_in:n_in + 6]
        sfin_ref, win_ref, o_ref = refs[n_in + 6:n_in + 9]
        tok_scr, vt_scr, ut_scr, dec_scr, spt_scr, in_stage, in_sems, y_scr = refs[n_in + 9:n_in + 17]
        mix_weight_scr = refs[n_in + 17:]
    else:
        sfin_ref, win_ref = refs[n_in:n_in + 2]
        tok_scr, vt_scr, ut_scr, dec_scr, spt_scr, in_stage, in_sems = refs[n_in + 2:]

    _stage_in_projection(layer, w_in_hbm, win_ref, in_stage, in_sems)
    _inproj_kernel(x_ref, mod_ref, n1w_ref, win_ref, *refs[4:n_in], tok_scr, vt_scr, ut_scr, dec_scr, seg=seg)

    sfin_ref[...] = jnp.zeros(sfin_ref.shape, _F32)
    for b in range(ut_scr.shape[0] // chunks_per_seq):
        for head0 in range(0, GLA_HEADS, SCAN_HEADS):
            start = [sfin_ref[b, head0 + s] for s in range(SCAN_HEADS)]
            final = _scan_chunks(ut_scr, dec_scr, start, spt_scr, chunks_per_seq,
                                 chunk0=b * chunks_per_seq, head0=head0)
            for s in range(SCAN_HEADS):
                sfin_ref[b, head0 + s] = final[s]

    if with_mix:
        _mix_kernel(x_ref, mod_ref, n2w_ref, tok_scr, vt_scr.at[0], spt_scr, gnw_ref, wo_hbm, w1_hbm, w2_hbm, fnw_ref,
                    o_ref, y_scr, *mix_weight_scr, layer=layer, final=False, single_step=True)


def _context_layer(xc, mods3, mod_row, layer, wts, ctx_len, with_mix):
    rows = xc.shape[0]
    bsz = rows // ctx_len
    n_chunks = rows // GLA_CHUNK
    assert rows % SUB_ROWS == 0 and SUB_ROWS % ctx_len == 0
    of_layer = lambda shape: pl.BlockSpec((None,) + shape, lambda j: (layer,) + (0,) * len(shape),
                                          pipeline_mode=pl.Buffered(1))
    in_specs = [
        pl.BlockSpec((rows, D_MODEL), lambda j: (0, 0)),
        pl.BlockSpec((None, 1, N_MOD * D_MODEL), lambda j: (layer * MOD_ROWS + mod_row, 0, 0)),
        of_layer((1, D_MODEL)),
        pl.BlockSpec(memory_space=pl.ANY),
        of_layer((VAL_COLS, D_MODEL)),
        of_layer((2 * GATE_RANK, KEY_COLS)),
        of_layer((1, KEY_COLS)),
        of_layer((3, CONV_WIDTH)),
        of_layer((POOL_WIDTH, POOL_WIDTH)),
        of_layer((1, POOL_WIDTH)),
    ]
    args = [xc, mods3, wts["n1w"], wts["w_in"], wts["wvt"], wts["gw"], wts["gb"], wts["cw"], wts["pw"], wts["ps"]]
    out_specs = [pl.BlockSpec((bsz, GLA_HEADS, GLA_DV, SLAB), lambda j: (0, 0, 0, 0)),
                 pl.BlockSpec((D_MODEL, WIN_COLS), lambda j: (0, 0))]
    out_shape = [jax.ShapeDtypeStruct((bsz, GLA_HEADS, GLA_DV, SLAB), _F32),
                 jax.ShapeDtypeStruct((D_MODEL, WIN_COLS), _BF16)]
    scratch = [
        pltpu.VMEM((rows, TOK_COLS), _BF16),
        pltpu.VMEM((1, VAL_COLS, rows), _BF16),
        pltpu.VMEM((n_chunks, GLA_HEADS, GLA_DV, SLAB), _BF16),
        pltpu.VMEM((n_chunks, 8, KEY_COLS), _F32),
        pltpu.VMEM((n_chunks, GLA_HEADS, GLA_DV, SLAB), _BF16),
        pltpu.VMEM((STAGE_SLOTS, IN_STAGE_ROWS, IN_COLS), _F32),
        pltpu.SemaphoreType.DMA((STAGE_SLOTS,)),
    ]
    if with_mix:
        in_specs += [of_layer((1, D_MODEL)), of_layer((1, GLA_DV))] + [pl.BlockSpec(memory_space=pl.ANY)] * 3
        in_specs += [pl.BlockSpec((1, D_MODEL), lambda j: (0, 0), pipeline_mode=pl.Buffered(1))]
        args += [wts["n2w"], wts["gnw"], wts["wo"], wts["w1"], wts["w2"], wts["fnw"]]
        out_specs.append(pl.BlockSpec((rows, D_MODEL), lambda j: (0, 0)))
        out_shape.append(jax.ShapeDtypeStruct((rows, D_MODEL), _F32))
        scratch += [pltpu.VMEM((rows, VAL_COLS), _BF16)] + _mix_weight_scratch()
    outs = pl.pallas_call(
        functools.partial(_context_kernel, layer=layer, seg=ctx_len, chunks_per_seq=ctx_len // GLA_CHUNK,
                          with_mix=with_mix),
        grid=(1,),
        in_specs=in_specs,
        out_specs=out_specs,
        out_shape=out_shape,
        scratch_shapes=scratch,
        compiler_params=pltpu.CompilerParams(vmem_limit_bytes=VMEM_LIMIT_BYTES),
    )(*args)
    return tuple(outs) if with_mix else (outs[0], outs[1], None)


def _prepare_weights(norm1_w, norm2_w, w_in, gate_w, gate_b, gla_norm_w, conv_w, pool_w, pool_scale, w_out,
                     w_mlp1, w_mlp2, final_norm_w):
    depth = w_in.shape[0]
    assert w_in.shape[2] == IN_COLS
    zeros = jnp.zeros((depth, GATE_RANK, GLA_HEADS, GLA_DK), _F32)
    gf = gate_w[:, 0].reshape(depth, GATE_RANK, GLA_HEADS, GLA_DK)
    gb = gate_w[:, 1].reshape(depth, GATE_RANK, GLA_HEADS, GLA_DK)
    gw = jnp.concatenate([jnp.concatenate([gf, zeros], -1), jnp.concatenate([zeros, gb], -1)], 1)
    gbias = jnp.concatenate([gate_b[:, 0].reshape(depth, GLA_HEADS, GLA_DK),
                             gate_b[:, 1].reshape(depth, GLA_HEADS, GLA_DK)], -1)
    pw = jnp.zeros((depth, POOL_WIDTH, POOL_WIDTH), _F32)
    for gi in range(POOL_WIDTH // POOL_GROUP):
        sl = slice(gi * POOL_GROUP, (gi + 1) * POOL_GROUP)
        pw = pw.at[:, sl, sl].set(pool_w[:, gi])
    return {
        "n1w": norm1_w.reshape(depth, 1, D_MODEL),
        "n2w": norm2_w.reshape(depth, 1, D_MODEL),
        "w_in": w_in,
        "wvt": jnp.swapaxes(w_in[:, :, IN_V], 1, 2).astype(_BF16),
        "gw": gw.reshape(depth, 2 * GATE_RANK, KEY_COLS).astype(_BF16),
        "gb": gbias.reshape(depth, 1, KEY_COLS),
        "gnw": gla_norm_w.reshape(depth, 1, GLA_DV),
        "cw": conv_w,
        "pw": pw.astype(_BF16),
        "ps": pool_scale.reshape(depth, 1, POOL_WIDTH),
        "wo": w_out,
        "w1": w_mlp1,
        "w2": w_mlp2,
        "fnw": final_norm_w.reshape(1, D_MODEL),
    }


def kernel(x, c, ctx, c_ctx, w_mod, b_mod, norm1_w, norm2_w, w_in, gla_gate_w, gla_gate_b, gla_norm_w, conv_w, pool_w, pool_scale, w_out, w_mlp1, w_mlp2, final_norm_w):
    bsz, n, _ = x.shape
    ctx_len = ctx.shape[1]
    depth = w_in.shape[0]
    assert bsz + 1 <= MOD_ROWS and n % LATENT_IN_TILE == 0 and SUB_ROWS % ctx_len == 0
    ctx_row = bsz

    cv = jnp.concatenate([c, c_ctx[None, :], jnp.zeros((MOD_ROWS - bsz - 1, D_MODEL), _F32)], axis=0)
    mods = _adaln(cv, w_mod, b_mod)
    mods3 = mods.reshape(depth * MOD_ROWS, 1, N_MOD * D_MODEL)
    wts = _prepare_weights(norm1_w, norm2_w, w_in, gla_gate_w, gla_gate_b, gla_norm_w, conv_w, pool_w, pool_scale,
                           w_out, w_mlp1, w_mlp2, final_norm_w)

    xl = x.reshape(bsz * n, D_MODEL)
    xc = ctx.reshape(bsz * ctx_len, D_MODEL)
    lat_in_row = lambda j: j // (n // LATENT_IN_TILE)
    lat_mix_row = lambda j: j // (n // LATENT_MIX_TILE)

    for l in range(depth):
        last = l == depth - 1
        s_ctx, win, xc = _context_layer(xc, mods3, ctx_row, l, wts, ctx_len, with_mix=not last)

        tok, vt, ut, dec = _inproj(xl, mods3, lat_in_row, l, win, wts, LATENT_IN_TILE, GRID_W, LATENT_MIX_TILE)
        spt, _ = _scan(ut, dec, s_ctx, n // GLA_CHUNK)
        xl = _mix(xl, mods3, lat_mix_row, l, tok, vt, spt, wts, LATENT_MIX_TILE, last)
    return xl.reshape(bsz, n, D_MODEL)
```

```python
import functools

import jax
import jax.numpy as jnp
from jax import lax
from jax.experimental import pallas as pl
from jax.experimental.pallas import tpu as pltpu

D_MODEL = 1024
GLA_HEADS = 4
GLA_DK = 64
GLA_DV = 128
SLAB = 2 * GLA_DK
KEY_COLS = GLA_HEADS * SLAB
VAL_COLS = GLA_HEADS * GLA_DV
GATE_RANK = 16
GATE_TAU = 16.0
LOG2_E = 1.4426950408889634
GLA_CHUNK = 128
SUB_ROWS = 256
LATENT_IN_TILE = 1024
LATENT_MIX_TILE = 512
CONV_WIDTH = 256
POOL_WIDTH = 256
POOL_GROUP = 64
POOL_HALF_WINDOWS = (1, 2, 4, 8)
D_FF = 4096
FF_CHUNK = 1024
N_MOD = 6
EPS = 1e-6
GRID_W = 64
RC_COLS = VAL_COLS + 3 * CONV_WIDTH + POOL_WIDTH
RY_COLS = VAL_COLS + CONV_WIDTH + POOL_WIDTH
TOK_QE = slice(0, KEY_COLS)
TOK_KI = slice(KEY_COLS, 2 * KEY_COLS)
TOK_RY = slice(2 * KEY_COLS, 2 * KEY_COLS + RY_COLS)
TOK_COLS = TOK_RY.stop
WZ_COLS = slice(0, 2 * GATE_RANK)
WQ_COLS = slice(128, 128 + GLA_HEADS * GLA_DK)
WK_COLS = slice(WQ_COLS.stop, WQ_COLS.stop + GLA_HEADS * GLA_DK)
WRC_COLS = slice(WK_COLS.stop, WK_COLS.stop + RC_COLS)
WIN_COLS = WRC_COLS.stop
IN_QK = slice(0, 2 * GLA_HEADS * GLA_DK)
IN_V = slice(IN_QK.stop, IN_QK.stop + VAL_COLS)
IN_Z = slice(IN_V.stop, IN_V.stop + 2 * GATE_RANK)
IN_RC = slice(IN_Z.stop, IN_Z.stop + RC_COLS)
IN_COLS = IN_RC.stop
IN_STAGE_ROWS = 512
SCAN_UNROLL = 2
SCAN_HEADS = 2
MOD_ROWS = 8
ADALN_COLS = 1536
VMEM_LIMIT_BYTES = 56 * 1024 * 1024
STAGE_SLOTS = 2

_NT = (((1,), (1,)), ((), ()))
_BF16 = jnp.bfloat16
_F32 = jnp.float32


def _dot(a, b):
    return jnp.dot(a, b, preferred_element_type=_F32)


def _sigmoid(x):
    return 1.0 / (1.0 + jnp.exp(-x))


def _adaln_kernel(cv_ref, w_ref, b_ref, o_ref):
    cv = cv_ref[...]
    s = (cv * _sigmoid(cv)).astype(_BF16)
    o_ref[...] = _dot(s, w_ref[...].astype(_BF16)) + b_ref[...]


def _adaln(cv, w_mod, b_mod):
    depth = w_mod.shape[0]
    cols = w_mod.shape[2]
    return pl.pallas_call(
        _adaln_kernel,
        grid=(depth, cols // ADALN_COLS),
        in_specs=[
            pl.BlockSpec((MOD_ROWS, D_MODEL), lambda l, j: (0, 0)),
            pl.BlockSpec((None, D_MODEL, ADALN_COLS), lambda l, j: (l, 0, j)),
            pl.BlockSpec((None, 1, ADALN_COLS), lambda l, j: (l, 0, j)),
        ],
        out_specs=pl.BlockSpec((None, MOD_ROWS, ADALN_COLS), lambda l, j: (l, 0, j)),
        out_shape=jax.ShapeDtypeStruct((depth, MOD_ROWS, cols), _F32),
        compiler_params=pltpu.CompilerParams(vmem_limit_bytes=VMEM_LIMIT_BYTES),
    )(cv, w_mod, b_mod.reshape(depth, 1, cols))


def _per_direction(t):
    lane = lax.broadcasted_iota(jnp.int32, (1, SLAB), 1)
    first = lane < GLA_DK
    slabs = []
    for p in range(GLA_HEADS // 2):
        pair = t[:, p * SLAB:(p + 1) * SLAB]
        swapped = pltpu.roll(pair, GLA_DK, 1)
        slabs += [jnp.where(first, pair, swapped), jnp.where(first, swapped, pair)]
    return jnp.concatenate(slabs, axis=1)


def _conv_and_pool(rc, cw_ref, pw_ref, ps_ref, seg):
    rows = rc.shape[0]
    pos = lax.broadcasted_iota(jnp.int32, (rows, 1), 0) % seg
    c0 = 0
    cb = rc[:, c0:c0 + CONV_WIDTH]
    zc = rc[:, c0 + CONV_WIDTH:c0 + 2 * CONV_WIDTH] * rc[:, c0 + 2 * CONV_WIDTH:c0 + 3 * CONV_WIDTH]
    z_prev = jnp.where(pos >= 1, pltpu.roll(zc, 1, 0), 0.0)
    z_next = jnp.where(pos <= seg - 2, pltpu.roll(zc, rows - 1, 0), 0.0)
    yc = cb * (cw_ref[0:1, :] * z_prev + cw_ref[1:2, :] * zc + cw_ref[2:3, :] * z_next)

    p0 = 3 * CONV_WIDTH
    lane128 = lax.broadcasted_iota(jnp.int32, (1, 2 * POOL_GROUP), 1)
    pooled = []
    for blk in range(POOL_WIDTH // (2 * POOL_GROUP)):
        u = rc[:, p0 + blk * 2 * POOL_GROUP:p0 + (blk + 1) * 2 * POOL_GROUP]
        h_lo, h_hi = POOL_HALF_WINDOWS[2 * blk], POOL_HALF_WINDOWS[2 * blk + 1]
        half = jnp.where(lane128 < POOL_GROUP, h_lo, h_hi)
        acc = jnp.zeros_like(u)
        for d in range(-h_hi, h_hi):
            shifted = u if d == 0 else pltpu.roll(u, (-d) % rows, 0)
            valid = (pos + d >= 0) & (pos + d < seg) & (d >= -half) & (d < half)
            acc = acc + jnp.where(valid, shifted, 0.0)
        cnt = (jnp.minimum(pos + half, seg) - jnp.maximum(pos - half, 0)).astype(_F32)
        pooled.append(acc / cnt - u)
    pool_in = jnp.concatenate(pooled, axis=1).astype(_BF16)
    return yc, _dot(pool_in, pw_ref[...]) * ps_ref[...]


def _inproj_kernel(x_ref, mod_ref, n1w_ref, win_ref, wvt_ref, gw_ref, gb_ref, cw_ref, pw_ref, ps_ref,
                   tok_ref, vt_ref, ut_ref, dec_ref, *, seg):
    qe_ref, ki_ref, ry_ref = tok_ref.at[:, TOK_QE], tok_ref.at[:, TOK_KI], tok_ref.at[:, TOK_RY]
    vt_tile = vt_ref.shape[2]
    tm = x_ref.shape[0]
    sh1 = mod_ref[:, 0:D_MODEL]
    scale1 = n1w_ref[...] * (1.0 + mod_ref[:, D_MODEL:2 * D_MODEL])
    row = lax.broadcasted_iota(jnp.int32, (GLA_CHUNK, GLA_CHUNK), 0)
    col = lax.broadcasted_iota(jnp.int32, (GLA_CHUNK, GLA_CHUNK), 1)
    tri = (col <= row).astype(_BF16)
    lane = lax.broadcasted_iota(jnp.int32, (1, KEY_COLS), 1)
    is_fwd = (lane % SLAB) < GLA_DK
    mid = GLA_CHUNK // 2

    n_sub = tm // SUB_ROWS
    cps = SUB_ROWS // GLA_CHUNK
    sub_rows = [slice(s * SUB_ROWS, (s + 1) * SUB_ROWS) for s in range(n_sub)]
    hbs, projs, gates, q2s, k2s = [], [], [], [], []
    for s in range(n_sub):
        x = x_ref[sub_rows[s], :]
        xn = x * lax.rsqrt(jnp.mean(x * x, axis=-1, keepdims=True) + EPS)
        hbs.append((xn * scale1 + sh1).astype(_BF16))
    for s in range(n_sub):
        proj = lax.dot_general(hbs[s], win_ref[...], _NT, preferred_element_type=_F32)
        projs.append(proj)
        pre = _dot(proj[:, WZ_COLS].astype(_BF16), gw_ref[...]) + gb_ref[...]
        g = (jnp.minimum(pre, 0.0) * LOG2_E - jnp.log2(1.0 + jnp.exp2(jnp.abs(pre) * (-LOG2_E)))) * (1.0 / GATE_TAU)
        gates.append(g.astype(_BF16))
        q2s.append(_per_direction(proj[:, WQ_COLS] * (GLA_DK ** -0.5)))
        k2s.append(_per_direction(proj[:, WK_COLS]))

    prefixes = []
    for s in range(n_sub):
        for cl in range(cps):
            lrows = slice(cl * GLA_CHUNK, (cl + 1) * GLA_CHUNK)
            prefixes.append(_dot(tri, gates[s][lrows]))

    vtbs = []
    for s in range(n_sub):
        vtb = lax.dot_general(wvt_ref[...], hbs[s], _NT, preferred_element_type=_F32).astype(_BF16)
        first = s * SUB_ROWS
        vt_ref[first // vt_tile, :, first % vt_tile:first % vt_tile + SUB_ROWS] = vtb
        vtbs.append(vtb)
        rc = projs[s][:, WRC_COLS]
        yc, yp = _conv_and_pool(rc[:, VAL_COLS:RC_COLS], cw_ref, pw_ref, ps_ref, seg)
        ry_ref[sub_rows[s], 0:VAL_COLS] = rc[:, 0:VAL_COLS].astype(_BF16)
        ry_ref[sub_rows[s], VAL_COLS:VAL_COLS + CONV_WIDTH] = yc.astype(_BF16)
        ry_ref[sub_rows[s], VAL_COLS + CONV_WIDTH:RY_COLS] = yp.astype(_BF16)

    for s in range(n_sub):
        g = gates[s].astype(_F32)
        for cl in range(cps):
            c = s * cps + cl
            lrows = slice(cl * GLA_CHUNK, (cl + 1) * GLA_CHUNK)
            rows = slice(c * GLA_CHUNK, (c + 1) * GLA_CHUNK)
            prefix = prefixes[c]
            total = prefix[GLA_CHUNK - 1:GLA_CHUNK, :]
            suffix = total - prefix + g[lrows]
            e = jnp.where(is_fwd, prefix, suffix)
            mvec = jnp.where(is_fwd, e[mid - 1:mid, :], e[mid:mid + 1, :])
            bm = e - mvec
            qe_ref[rows, :] = (q2s[s][lrows] * jnp.exp2(bm)).astype(_BF16)
            ki_ref[rows, :] = (k2s[s][lrows] * jnp.exp2(-bm)).astype(_BF16)
            kd = (k2s[s][lrows] * jnp.exp2(total - e)).astype(_BF16)
            dec_ref[c] = jnp.concatenate([total, mvec, jnp.zeros((6, KEY_COLS), _F32)], axis=0)
            for h in range(GLA_HEADS):
                vt_h = vtbs[s][h * GLA_DV:(h + 1) * GLA_DV, lrows]
                ut_ref[c, h] = _dot(vt_h, kd[:, h * SLAB:(h + 1) * SLAB]).astype(ut_ref.dtype)


def _inproj(x2d, mods3, mod_row_of_tile, layer, win, wts, tm, seg, vt_tile):
    tokens = x2d.shape[0]
    n_tiles = tokens // tm
    n_chunks = tokens // GLA_CHUNK
    cpt = tm // GLA_CHUNK
    assert SUB_ROWS % seg == 0
    assert tm % vt_tile == 0 and vt_tile % SUB_ROWS == 0
    of_layer = lambda shape: pl.BlockSpec((None,) + shape, lambda j: (layer,) + (0,) * len(shape))
    return pl.pallas_call(
        functools.partial(_inproj_kernel, seg=seg),
        grid=(n_tiles,),
        in_specs=[
            pl.BlockSpec((tm, D_MODEL), lambda j: (j, 0)),
            pl.BlockSpec((None, 1, N_MOD * D_MODEL), lambda j: (layer * MOD_ROWS + mod_row_of_tile(j), 0, 0)),
            of_layer((1, D_MODEL)),
            pl.BlockSpec((WIN_COLS, D_MODEL), lambda j: (0, 0)),
            of_layer((VAL_COLS, D_MODEL)),
            of_layer((2 * GATE_RANK, KEY_COLS)),
            of_layer((1, KEY_COLS)),
            of_layer((3, CONV_WIDTH)),
            of_layer((POOL_WIDTH, POOL_WIDTH)),
            of_layer((1, POOL_WIDTH)),
        ],
        out_specs=[
            pl.BlockSpec((tm, TOK_COLS), lambda j: (j, 0)),
            pl.BlockSpec((tm // vt_tile, VAL_COLS, vt_tile), lambda j: (j, 0, 0)),
            pl.BlockSpec((cpt, GLA_HEADS, GLA_DV, SLAB), lambda j: (j, 0, 0, 0)),
            pl.BlockSpec((cpt, 8, KEY_COLS), lambda j: (j, 0, 0)),
        ],
        out_shape=[
            jax.ShapeDtypeStruct((tokens, TOK_COLS), _BF16),
            jax.ShapeDtypeStruct((tokens // vt_tile, VAL_COLS, vt_tile), _BF16),
            jax.ShapeDtypeStruct((n_chunks, GLA_HEADS, GLA_DV, SLAB), _BF16),
            jax.ShapeDtypeStruct((n_chunks, 8, KEY_COLS), _F32),
        ],
        compiler_params=pltpu.CompilerParams(vmem_limit_bytes=VMEM_LIMIT_BYTES),
    )(x2d, mods3, wts["n1w"], win, wts["wvt"], wts["gw"], wts["gb"], wts["cw"], wts["pw"], wts["ps"])


def _scan_chunks(ut_ref, dec_ref, states, spt_ref, nc, chunk0=0, head0=0):
    lane = lax.broadcasted_iota(jnp.int32, (1, SLAB), 1)
    is_fwd = lane < GLA_DK

    def step(t, states):
        i = chunk0 + t
        j = chunk0 + nc - 1 - t
        new_states = []
        for s in range(SCAN_HEADS):
            h = head0 + s
            lanes = slice(h * SLAB, (h + 1) * SLAB)
            log_decay = jnp.where(is_fwd, dec_ref[i, 0:1, lanes], dec_ref[j, 0:1, lanes])
            log_mid = jnp.where(is_fwd, dec_ref[i, 1:2, lanes], dec_ref[j, 1:2, lanes])
            entering = (states[s] * jnp.exp2(log_mid)).astype(_BF16)
            spt_ref[i, h, :, 0:GLA_DK] = entering[:, 0:GLA_DK]
            spt_ref[j, h, :, GLA_DK:SLAB] = entering[:, GLA_DK:SLAB]
            inc = jnp.where(is_fwd, ut_ref[i, h], ut_ref[j, h]).astype(_F32)
            new_states.append(states[s] * jnp.exp2(log_decay) + inc)
        return tuple(new_states)

    return lax.fori_loop(0, nc, step, tuple(states), unroll=SCAN_UNROLL)


def _scan_kernel(ut_ref, dec_ref, s0_ref, spt_ref, sfin_ref):
    final = _scan_chunks(ut_ref, dec_ref, [s0_ref[h] for h in range(SCAN_HEADS)], spt_ref, ut_ref.shape[0])
    for h in range(SCAN_HEADS):
        sfin_ref[h] = final[h]


def _scan(ut, dec, s0, chunks_per_seq):
    n_chunks = ut.shape[0]
    bsz = n_chunks // chunks_per_seq
    return pl.pallas_call(
        _scan_kernel,
        grid=(bsz, GLA_HEADS // SCAN_HEADS),
        in_specs=[
            pl.BlockSpec((chunks_per_seq, SCAN_HEADS, GLA_DV, SLAB), lambda b, g: (b, g, 0, 0)),
            pl.BlockSpec((chunks_per_seq, 8, SCAN_HEADS * SLAB), lambda b, g: (b, 0, g)),
            pl.BlockSpec((None, SCAN_HEADS, GLA_DV, SLAB), lambda b, g: (b, g, 0, 0)),
        ],
        out_specs=[
            pl.BlockSpec((chunks_per_seq, SCAN_HEADS, GLA_DV, SLAB), lambda b, g: (b, g, 0, 0)),
            pl.BlockSpec((None, SCAN_HEADS, GLA_DV, SLAB), lambda b, g: (b, g, 0, 0)),
        ],
        out_shape=[
            jax.ShapeDtypeStruct((n_chunks, GLA_HEADS, GLA_DV, SLAB), _BF16),
            jax.ShapeDtypeStruct((bsz, GLA_HEADS, GLA_DV, SLAB), _F32),
        ],
        compiler_params=pltpu.CompilerParams(vmem_limit_bytes=VMEM_LIMIT_BYTES),
    )(ut, dec, s0)


def _mix_weight_stager(layer, wo_hbm, w1_hbm, w2_hbm, wo_ref, w1_ref, w2_ref, stage, sems):
    pieces = [(wo_hbm.at[layer], wo_ref)]
    for i in range(D_FF // D_MODEL):
        cols = slice(i * D_MODEL, (i + 1) * D_MODEL)
        pieces.append((w1_hbm.at[layer, :, cols], w1_ref.at[:, cols]))
        pieces.append((w2_hbm.at[layer, cols, :], w2_ref.at[cols, :]))
    copies = [pltpu.make_async_copy(src, stage.at[k % STAGE_SLOTS], sems.at[k % STAGE_SLOTS])
              for k, (src, _) in enumerate(pieces)]
    ready = [None]

    def need(n):
        if ready[0] is None:
            for k in range(STAGE_SLOTS):
                copies[k].start()
            ready[0] = 0
        for k in range(ready[0], min(n, len(pieces))):
            copies[k].wait()
            pieces[k][1][...] = stage[k % STAGE_SLOTS].astype(_BF16)
            if k + STAGE_SLOTS < len(pieces):
                copies[k + STAGE_SLOTS].start()
            ready[0] = k + 1

    return need


def _mix_body(x_ref, mod_ref, n2w_ref, tok_ref, vt_ref, spt_ref, gnw_ref, fnw_ref, o_ref, y_scr, wo_ref, w1_ref, w2_ref,
              need, final):
    tm = x_ref.shape[0]
    qe_ref, ki_ref, ry_ref = tok_ref.at[:, TOK_QE], tok_ref.at[:, TOK_KI], tok_ref.at[:, TOK_RY]
    g1 = mod_ref[:, 2 * D_MODEL:3 * D_MODEL]
    sh2 = mod_ref[:, 3 * D_MODEL:4 * D_MODEL]
    scale2 = n2w_ref[...] * (1.0 + mod_ref[:, 4 * D_MODEL:5 * D_MODEL])
    g2 = mod_ref[:, 5 * D_MODEL:6 * D_MODEL]
    row = lax.broadcasted_iota(jnp.int32, (GLA_CHUNK, GLA_CHUNK), 0)
    col = lax.broadcasted_iota(jnp.int32, (GLA_CHUNK, GLA_CHUNK), 1)
    lane = lax.broadcasted_iota(jnp.int32, (GLA_CHUNK, SLAB), 1)
    fwd_lane = lane < GLA_DK
    gnw = gnw_ref[...]
    need(0)

    for c in range(tm // GLA_CHUNK):
        rows = slice(c * GLA_CHUNK, (c + 1) * GLA_CHUNK)
        for h in range(GLA_HEADS):
            lanes = slice(h * SLAB, (h + 1) * SLAB)
            qe = qe_ref[rows, lanes]
            ki = ki_ref[rows, lanes]
            zero = jnp.zeros_like(ki)
            keys = jnp.concatenate([jnp.where(fwd_lane, ki, zero), jnp.where(fwd_lane, zero, ki)], axis=0)
            a2 = lax.dot_general(qe, keys, _NT, preferred_element_type=_F32)
            am = jnp.where(col <= row, a2[:, 0:GLA_CHUNK], 0.0) + jnp.where(col >= row, a2[:, GLA_CHUNK:], 0.0)
            lhs = jnp.concatenate([am.astype(_BF16), qe], axis=1)
            rhs = jnp.concatenate([vt_ref[h * GLA_DV:(h + 1) * GLA_DV, rows], spt_ref[c, h]], axis=1)
            o = lax.dot_general(lhs, rhs, _NT, preferred_element_type=_F32)
            o = o * lax.rsqrt(jnp.mean(o * o, axis=-1, keepdims=True) + EPS) * gnw
            r = ry_ref[rows, h * GLA_DV:(h + 1) * GLA_DV].astype(_F32)
            y_scr[rows, h * GLA_DV:(h + 1) * GLA_DV] = (o * (r * _sigmoid(r))).astype(_BF16)
        need(c + 1)

    need(1)
    mixed = _dot(y_scr[...], wo_ref[0:VAL_COLS, :]) + _dot(ry_ref[:, VAL_COLS:RY_COLS], wo_ref[VAL_COLS:D_MODEL, :])
    x1 = x_ref[...] + g1 * mixed
    xn = x1 * lax.rsqrt(jnp.mean(x1 * x1, axis=-1, keepdims=True) + EPS)
    h2 = (xn * scale2 + sh2).astype(_BF16)
    acc = jnp.zeros((tm, D_MODEL), _F32)
    for f in range(D_FF // FF_CHUNK):
        cols = slice(f * FF_CHUNK, (f + 1) * FF_CHUNK)
        need(1 + 2 * pl.cdiv(cols.stop, D_MODEL))
        a = jnp.maximum(_dot(h2, w1_ref[:, cols]), 0.0)
        acc = acc + _dot((a * a).astype(_BF16), w2_ref[cols, :])
    out = x1 + g2 * acc
    if final:
        out = out * lax.rsqrt(jnp.mean(out * out, axis=-1, keepdims=True) + EPS) * fnw_ref[...]
    o_ref[...] = out


def _mix_kernel(x_ref, mod_ref, n2w_ref, tok_ref, vt_ref, spt_ref, gnw_ref, wo_hbm, w1_hbm, w2_hbm, fnw_ref,
                o_ref, y_scr, wo_ref, w1_ref, w2_ref, stage, sems, *, layer, final, single_step):
    body = functools.partial(_mix_body, x_ref, mod_ref, n2w_ref, tok_ref, vt_ref, spt_ref, gnw_ref, fnw_ref, o_ref, y_scr,
                             wo_ref, w1_ref, w2_ref, final=final)

    def staging_step():
        body(need=_mix_weight_stager(layer, wo_hbm, w1_hbm, w2_hbm, wo_ref, w1_ref, w2_ref, stage, sems))

    if single_step:
        staging_step()
    else:
        first = pl.program_id(0) == 0
        pl.when(first)(staging_step)
        pl.when(jnp.logical_not(first))(functools.partial(body, need=lambda n: None))


def _mix_weight_scratch():
    return [pltpu.VMEM((D_MODEL, D_MODEL), _BF16), pltpu.VMEM((D_MODEL, D_FF), _BF16), pltpu.VMEM((D_FF, D_MODEL), _BF16),
            pltpu.VMEM((STAGE_SLOTS, D_MODEL, D_MODEL), _F32), pltpu.SemaphoreType.DMA((STAGE_SLOTS,))]


def _mix(x2d, mods3, mod_row_of_tile, layer, tok, vt, spt, wts, tm, final):
    tokens = x2d.shape[0]
    cpt = tm // GLA_CHUNK
    of_layer = lambda shape: pl.BlockSpec((None,) + shape, lambda j: (layer,) + (0,) * len(shape),
                                          pipeline_mode=pl.Buffered(1))
    return pl.pallas_call(
        functools.partial(_mix_kernel, layer=layer, final=final, single_step=tokens == tm),
        grid=(tokens // tm,),
        in_specs=[
            pl.BlockSpec((tm, D_MODEL), lambda j: (j, 0)),
            pl.BlockSpec((None, 1, N_MOD * D_MODEL), lambda j: (layer * MOD_ROWS + mod_row_of_tile(j), 0, 0)),
            of_layer((1, D_MODEL)),
            pl.BlockSpec((tm, TOK_COLS), lambda j: (j, 0)),
            pl.BlockSpec((None, VAL_COLS, tm), lambda j: (j, 0, 0)),
            pl.BlockSpec((cpt, GLA_HEADS, GLA_DV, SLAB), lambda j: (j, 0, 0, 0)),
            of_layer((1, GLA_DV)),
            pl.BlockSpec(memory_space=pl.ANY),
            pl.BlockSpec(memory_space=pl.ANY),
            pl.BlockSpec(memory_space=pl.ANY),
            pl.BlockSpec((1, D_MODEL), lambda j: (0, 0), pipeline_mode=pl.Buffered(1)),
        ],
        out_specs=pl.BlockSpec((tm, D_MODEL), lambda j: (j, 0)),
        out_shape=jax.ShapeDtypeStruct((tokens, D_MODEL), _F32),
        scratch_shapes=[pltpu.VMEM((tm, VAL_COLS), _BF16)] + _mix_weight_scratch(),
        compiler_params=pltpu.CompilerParams(vmem_limit_bytes=VMEM_LIMIT_BYTES),
    )(x2d, mods3, wts["n2w"], tok, vt, spt, wts["gnw"], wts["wo"], wts["w1"], wts["w2"], wts["fnw"])


def _stage_in_projection(layer, w_in_t_hbm, win_ref, stage, sems):
    pieces = []
    for src, dst in ((IN_Z, WZ_COLS), (IN_QK, WQ_COLS), (IN_RC, WRC_COLS)):
        for off in range(0, src.stop - src.start, IN_STAGE_ROWS):
            pieces.append((src.start + off, dst.start + off, min(IN_STAGE_ROWS, src.stop - src.start - off)))
    copies = [pltpu.make_async_copy(w_in_t_hbm.at[layer, s0:s0 + n, :], stage.at[k % STAGE_SLOTS, 0:n, :],
                                    sems.at[k % STAGE_SLOTS]) for k, (s0, _, n) in enumerate(pieces)]
    for k in range(STAGE_SLOTS):
        copies[k].start()
    win_ref[WZ_COLS.stop:WQ_COLS.start, :] = jnp.zeros((WQ_COLS.start - WZ_COLS.stop, D_MODEL), _BF16)
    for k, (_, d0, n) in enumerate(pieces):
        copies[k].wait()
        win_ref[d0:d0 + n, :] = stage[k % STAGE_SLOTS, 0:n, :].astype(_BF16)
        if k + STAGE_SLOTS < len(pieces):
            copies[k + STAGE_SLOTS].start()


def _context_kernel(*refs, layer, seg, chunks_per_seq, with_mix):
    n_in = 10
    x_ref, mod_ref, n1w_ref, w_in_t_hbm = refs[0:4]
    if with_mix:
        n2w_ref, gnw_ref, wo_hbm, w1_hbm, w2_hbm, fnw_ref = refs[n_in:n_in + 6]
        sfin_ref, win_ref, o_ref = refs[n_in + 6:n_in + 9]
        tok_scr, vt_scr, ut_scr, dec_scr, spt_scr, in_stage, in_sems, y_scr = refs[n_in + 9:n_in + 17]
        mix_weight_scr = refs[n_in + 17:]
    else:
        sfin_ref, win_ref = refs[n_in:n_in + 2]
        tok_scr, vt_scr, ut_scr, dec_scr, spt_scr, in_stage, in_sems = refs[n_in + 2:]

    _stage_in_projection(layer, w_in_t_hbm, win_ref, in_stage, in_sems)
    _inproj_kernel(x_ref, mod_ref, n1w_ref, win_ref, *refs[4:n_in], tok_scr, vt_scr, ut_scr, dec_scr, seg=seg)

    sfin_ref[...] = jnp.zeros(sfin_ref.shape, _F32)
    for b in range(ut_scr.shape[0] // chunks_per_seq):
        for head0 in range(0, GLA_HEADS, SCAN_HEADS):
            start = [sfin_ref[b, head0 + s] for s in range(SCAN_HEADS)]
            final = _scan_chunks(ut_scr, dec_scr, start, spt_scr, chunks_per_seq,
                                 chunk0=b * chunks_per_seq, head0=head0)
            for s in range(SCAN_HEADS):
                sfin_ref[b, head0 + s] = final[s]

    if with_mix:
        _mix_kernel(x_ref, mod_ref, n2w_ref, tok_scr, vt_scr.at[0], spt_scr, gnw_ref, wo_hbm, w1_hbm, w2_hbm, fnw_ref,
                    o_ref, y_scr, *mix_weight_scr, layer=layer, final=False, single_step=True)


def _context_layer(xc, mods3, mod_row, layer, wts, ctx_len, with_mix):
    rows = xc.shape[0]
    bsz = rows // ctx_len
    n_chunks = rows // GLA_CHUNK
    assert rows % SUB_ROWS == 0 and SUB_ROWS % ctx_len == 0
    of_layer = lambda shape: pl.BlockSpec((None,) + shape, lambda j: (layer,) + (0,) * len(shape),
                                          pipeline_mode=pl.Buffered(1))
    in_specs = [
        pl.BlockSpec((rows, D_MODEL), lambda j: (0, 0)),
        pl.BlockSpec((None, 1, N_MOD * D_MODEL), lambda j: (layer * MOD_ROWS + mod_row, 0, 0)),
        of_layer((1, D_MODEL)),
        pl.BlockSpec(memory_space=pl.ANY),
        of_layer((VAL_COLS, D_MODEL)),
        of_layer((2 * GATE_RANK, KEY_COLS)),
        of_layer((1, KEY_COLS)),
        of_layer((3, CONV_WIDTH)),
        of_layer((POOL_WIDTH, POOL_WIDTH)),
        of_layer((1, POOL_WIDTH)),
    ]
    args = [xc, mods3, wts["n1w"], wts["w_in_t"], wts["wvt"], wts["gw"], wts["gb"], wts["cw"], wts["pw"], wts["ps"]]
    out_specs = [pl.BlockSpec((bsz, GLA_HEADS, GLA_DV, SLAB), lambda j: (0, 0, 0, 0)),
                 pl.BlockSpec((WIN_COLS, D_MODEL), lambda j: (0, 0))]
    out_shape = [jax.ShapeDtypeStruct((bsz, GLA_HEADS, GLA_DV, SLAB), _F32),
                 jax.ShapeDtypeStruct((WIN_COLS, D_MODEL), _BF16)]
    scratch = [
        pltpu.VMEM((rows, TOK_COLS), _BF16),
        pltpu.VMEM((1, VAL_COLS, rows), _BF16),
        pltpu.VMEM((n_chunks, GLA_HEADS, GLA_DV, SLAB), _BF16),
        pltpu.VMEM((n_chunks, 8, KEY_COLS), _F32),
        pltpu.VMEM((n_chunks, GLA_HEADS, GLA_DV, SLAB), _BF16),
        pltpu.VMEM((STAGE_SLOTS, IN_STAGE_ROWS, D_MODEL), _F32),
        pltpu.SemaphoreType.DMA((STAGE_SLOTS,)),
    ]
    if with_mix:
        in_specs += [of_layer((1, D_MODEL)), of_layer((1, GLA_DV))] + [pl.BlockSpec(memory_space=pl.ANY)] * 3
        in_specs += [pl.BlockSpec((1, D_MODEL), lambda j: (0, 0), pipeline_mode=pl.Buffered(1))]
        args += [wts["n2w"], wts["gnw"], wts["wo"], wts["w1"], wts["w2"], wts["fnw"]]
        out_specs.append(pl.BlockSpec((rows, D_MODEL), lambda j: (0, 0)))
        out_shape.append(jax.ShapeDtypeStruct((rows, D_MODEL), _F32))
        scratch += [pltpu.VMEM((rows, VAL_COLS), _BF16)] + _mix_weight_scratch()
    outs = pl.pallas_call(
        functools.partial(_context_kernel, layer=layer, seg=ctx_len, chunks_per_seq=ctx_len // GLA_CHUNK,
                          with_mix=with_mix),
        grid=(1,),
        in_specs=in_specs,
        out_specs=out_specs,
        out_shape=out_shape,
        scratch_shapes=scratch,
        compiler_params=pltpu.CompilerParams(vmem_limit_bytes=VMEM_LIMIT_BYTES),
    )(*args)
    return tuple(outs) if with_mix else (outs[0], outs[1], None)


def _prepare_weights(norm1_w, norm2_w, w_in, gate_w, gate_b, gla_norm_w, conv_w, pool_w, pool_scale, w_out,
                     w_mlp1, w_mlp2, final_norm_w):
    depth = w_in.shape[0]
    assert w_in.shape[2] == IN_COLS
    w_in_t = jnp.swapaxes(w_in, 1, 2)
    zeros = jnp.zeros((depth, GATE_RANK, GLA_HEADS, GLA_DK), _F32)
    gf = gate_w[:, 0].reshape(depth, GATE_RANK, GLA_HEADS, GLA_DK)
    gb = gate_w[:, 1].reshape(depth, GATE_RANK, GLA_HEADS, GLA_DK)
    gw = jnp.concatenate([jnp.concatenate([gf, zeros], -1), jnp.concatenate([zeros, gb], -1)], 1)
    gbias = jnp.concatenate([gate_b[:, 0].reshape(depth, GLA_HEADS, GLA_DK),
                             gate_b[:, 1].reshape(depth, GLA_HEADS, GLA_DK)], -1)
    pw = jnp.zeros((depth, POOL_WIDTH, POOL_WIDTH), _F32)
    for gi in range(POOL_WIDTH // POOL_GROUP):
        sl = slice(gi * POOL_GROUP, (gi + 1) * POOL_GROUP)
        pw = pw.at[:, sl, sl].set(pool_w[:, gi])
    return {
        "n1w": norm1_w.reshape(depth, 1, D_MODEL),
        "n2w": norm2_w.reshape(depth, 1, D_MODEL),
        "w_in_t": w_in_t,
        "wvt": w_in_t[:, IN_V, :].astype(_BF16),
        "gw": gw.reshape(depth, 2 * GATE_RANK, KEY_COLS).astype(_BF16),
        "gb": gbias.reshape(depth, 1, KEY_COLS),
        "gnw": gla_norm_w.reshape(depth, 1, GLA_DV),
        "cw": conv_w,
        "pw": pw.astype(_BF16),
        "ps": pool_scale.reshape(depth, 1, POOL_WIDTH),
        "wo": w_out,
        "w1": w_mlp1,
        "w2": w_mlp2,
        "fnw": final_norm_w.reshape(1, D_MODEL),
    }


def kernel(x, c, ctx, c_ctx, w_mod, b_mod, norm1_w, norm2_w, w_in, gla_gate_w, gla_gate_b, gla_norm_w, conv_w, pool_w, pool_scale, w_out, w_mlp1, w_mlp2, final_norm_w):
    bsz, n, _ = x.shape
    ctx_len = ctx.shape[1]
    depth = w_in.shape[0]
    assert bsz + 1 <= MOD_ROWS and n % LATENT_IN_TILE == 0 and SUB_ROWS % ctx_len == 0
    ctx_row = bsz

    cv = jnp.concatenate([c, c_ctx[None, :], jnp.zeros((MOD_ROWS - bsz - 1, D_MODEL), _F32)], axis=0)
    mods = _adaln(cv, w_mod, b_mod)
    mods3 = mods.reshape(depth * MOD_ROWS, 1, N_MOD * D_MODEL)
    wts = _prepare_weights(norm1_w, norm2_w, w_in, gla_gate_w, gla_gate_b, gla_norm_w, conv_w, pool_w, pool_scale,
                           w_out, w_mlp1, w_mlp2, final_norm_w)

    xl = x.reshape(bsz * n, D_MODEL)
    xc = ctx.reshape(bsz * ctx_len, D_MODEL)
    lat_in_row = lambda j: j // (n // LATENT_IN_TILE)
    lat_mix_row = lambda j: j // (n // LATENT_MIX_TILE)

    for l in range(depth):
        last = l == depth - 1
        s_ctx, win, xc = _context_layer(xc, mods3, ctx_row, l, wts, ctx_len, with_mix=not last)

        tok, vt, ut, dec = _inproj(xl, mods3, lat_in_row, l, win, wts, LATENT_IN_TILE, GRID_W, LATENT_MIX_TILE)
        spt, _ = _scan(ut, dec, s_ctx, n // GLA_CHUNK)
        xl = _mix(xl, mods3, lat_mix_row, l, tok, vt, spt, wts, LATENT_MIX_TILE, last)
    return xl.reshape(bsz, n, D_MODEL)
```

```python
import functools

import jax
import jax.numpy as jnp
from jax import lax
from jax.experimental import pallas as pl
from jax.experimental.pallas import tpu as pltpu

D_MODEL = 1024
GLA_HEADS = 4
GLA_DK = 64
GLA_DV = 128
SLAB = 2 * GLA_DK
KEY_COLS = GLA_HEADS * SLAB
VAL_COLS = GLA_HEADS * GLA_DV
GATE_RANK = 16
GATE_TAU = 16.0
LOG2_E = 1.4426950408889634
GLA_CHUNK = 128
SUB_ROWS = 256
LATENT_IN_TILE = 1024
LATENT_MIX_TILE = 512
CONV_WIDTH = 256
POOL_WIDTH = 256
POOL_GROUP = 64
POOL_HALF_WINDOWS = (1, 2, 4, 8)
D_FF = 4096
FF_CHUNK = 1024
N_MOD = 6
EPS = 1e-6
GRID_W = 64
RC_COLS = VAL_COLS + 3 * CONV_WIDTH + POOL_WIDTH
RY_COLS = VAL_COLS + CONV_WIDTH + POOL_WIDTH
TOK_QE = slice(0, KEY_COLS)
TOK_KI = slice(KEY_COLS, 2 * KEY_COLS)
TOK_RY = slice(2 * KEY_COLS, 2 * KEY_COLS + RY_COLS)
TOK_COLS = TOK_RY.stop
WZ_COLS = slice(0, 2 * GATE_RANK)
WQ_COLS = slice(128, 128 + GLA_HEADS * GLA_DK)
WK_COLS = slice(WQ_COLS.stop, WQ_COLS.stop + GLA_HEADS * GLA_DK)
WRC_COLS = slice(WK_COLS.stop, WK_COLS.stop + RC_COLS)
WIN_COLS = WRC_COLS.stop
IN_QK = slice(0, 2 * GLA_HEADS * GLA_DK)
IN_V = slice(IN_QK.stop, IN_QK.stop + VAL_COLS)
IN_Z = slice(IN_V.stop, IN_V.stop + 2 * GATE_RANK)
IN_RC = slice(IN_Z.stop, IN_Z.stop + RC_COLS)
IN_COLS = IN_RC.stop
IN_STAGE_ROWS = 512
SCAN_UNROLL = 2
SCAN_HEADS = 2
MOD_ROWS = 8
ADALN_COLS = 1536
VMEM_LIMIT_BYTES = 56 * 1024 * 1024
STAGE_SLOTS = 2

_NT = (((1,), (1,)), ((), ()))
_BF16 = jnp.bfloat16
_F32 = jnp.float32


def _dot(a, b):
    return jnp.dot(a, b, preferred_element_type=_F32)


def _sigmoid(x):
    return 1.0 / (1.0 + jnp.exp(-x))


def _adaln_kernel(cv_ref, w_ref, b_ref, o_ref):
    cv = cv_ref[...]
    s = (cv * _sigmoid(cv)).astype(_BF16)
    mod = _dot(s, w_ref[...].astype(_BF16)) + b_ref[pl.ds(pl.program_id(0), 1), :]
    for r in range(MOD_ROWS):
        o_ref[r] = mod[r:r + 1, :]


def _adaln(cv, w_mod, b_mod):
    depth = w_mod.shape[0]
    cols = w_mod.shape[2]
    return pl.pallas_call(
        _adaln_kernel,
        grid=(depth, cols // ADALN_COLS),
        in_specs=[
            pl.BlockSpec((MOD_ROWS, D_MODEL), lambda l, j: (0, 0)),
            pl.BlockSpec((None, D_MODEL, ADALN_COLS), lambda l, j: (l, 0, j)),
            pl.BlockSpec((depth, ADALN_COLS), lambda l, j: (0, j)),
        ],
        out_specs=pl.BlockSpec((MOD_ROWS, 1, ADALN_COLS), lambda l, j: (l, 0, j)),
        out_shape=jax.ShapeDtypeStruct((depth * MOD_ROWS, 1, cols), _F32),
        compiler_params=pltpu.CompilerParams(vmem_limit_bytes=VMEM_LIMIT_BYTES),
    )(cv, w_mod, b_mod)


def _per_direction(t):
    lane = lax.broadcasted_iota(jnp.int32, (1, SLAB), 1)
    first = lane < GLA_DK
    slabs = []
    for p in range(GLA_HEADS // 2):
        pair = t[:, p * SLAB:(p + 1) * SLAB]
        swapped = pltpu.roll(pair, GLA_DK, 1)
        slabs += [jnp.where(first, pair, swapped), jnp.where(first, swapped, pair)]
    return jnp.concatenate(slabs, axis=1)


def _conv_and_pool(rc, cw_ref, pw_ref, ps_ref, seg):
    rows = rc.shape[0]
    pos = lax.broadcasted_iota(jnp.int32, (rows, 1), 0) % seg
    c0 = 0
    cb = rc[:, c0:c0 + CONV_WIDTH]
    zc = rc[:, c0 + CONV_WIDTH:c0 + 2 * CONV_WIDTH] * rc[:, c0 + 2 * CONV_WIDTH:c0 + 3 * CONV_WIDTH]
    z_prev = jnp.where(pos >= 1, pltpu.roll(zc, 1, 0), 0.0)
    z_next = jnp.where(pos <= seg - 2, pltpu.roll(zc, rows - 1, 0), 0.0)
    yc = cb * (cw_ref[0:1, :] * z_prev + cw_ref[1:2, :] * zc + cw_ref[2:3, :] * z_next)

    p0 = 3 * CONV_WIDTH
    lane128 = lax.broadcasted_iota(jnp.int32, (1, 2 * POOL_GROUP), 1)
    pooled = []
    for blk in range(POOL_WIDTH // (2 * POOL_GROUP)):
        u = rc[:, p0 + blk * 2 * POOL_GROUP:p0 + (blk + 1) * 2 * POOL_GROUP]
        h_lo, h_hi = POOL_HALF_WINDOWS[2 * blk], POOL_HALF_WINDOWS[2 * blk + 1]
        half = jnp.where(lane128 < POOL_GROUP, h_lo, h_hi)
        acc = jnp.zeros_like(u)
        for d in range(-h_hi, h_hi):
            shifted = u if d == 0 else pltpu.roll(u, (-d) % rows, 0)
            valid = (pos + d >= 0) & (pos + d < seg) & (d >= -half) & (d < half)
            acc = acc + jnp.where(valid, shifted, 0.0)
        cnt = (jnp.minimum(pos + half, seg) - jnp.maximum(pos - half, 0)).astype(_F32)
        pooled.append(acc / cnt - u)
    pool_in = jnp.concatenate(pooled, axis=1).astype(_BF16)
    return yc, _dot(pool_in, pw_ref[...]) * ps_ref[...]


def _inproj_kernel(x_ref, mod_ref, n1w_ref, win_ref, wvt_ref, gw_ref, gb_ref, cw_ref, pw_ref, ps_ref,
                   tok_ref, vt_ref, ut_ref, dec_ref, *, seg):
    qe_ref, ki_ref, ry_ref = tok_ref.at[:, TOK_QE], tok_ref.at[:, TOK_KI], tok_ref.at[:, TOK_RY]
    vt_tile = vt_ref.shape[2]
    tm = x_ref.shape[0]
    sh1 = mod_ref[:, 0:D_MODEL]
    scale1 = n1w_ref[...] * (1.0 + mod_ref[:, D_MODEL:2 * D_MODEL])
    row = lax.broadcasted_iota(jnp.int32, (GLA_CHUNK, GLA_CHUNK), 0)
    col = lax.broadcasted_iota(jnp.int32, (GLA_CHUNK, GLA_CHUNK), 1)
    tri = (col <= row).astype(_BF16)
    lane = lax.broadcasted_iota(jnp.int32, (1, KEY_COLS), 1)
    is_fwd = (lane % SLAB) < GLA_DK
    mid = GLA_CHUNK // 2

    n_sub = tm // SUB_ROWS
    cps = SUB_ROWS // GLA_CHUNK
    sub_rows = [slice(s * SUB_ROWS, (s + 1) * SUB_ROWS) for s in range(n_sub)]
    hbs, projs, gates, q2s, k2s = [], [], [], [], []
    for s in range(n_sub):
        x = x_ref[sub_rows[s], :]
        xn = x * lax.rsqrt(jnp.mean(x * x, axis=-1, keepdims=True) + EPS)
        hbs.append((xn * scale1 + sh1).astype(_BF16))
    for s in range(n_sub):
        proj = lax.dot_general(hbs[s], win_ref[...], _NT, preferred_element_type=_F32)
        projs.append(proj)
        pre = _dot(proj[:, WZ_COLS].astype(_BF16), gw_ref[...]) + gb_ref[...]
        g = (jnp.minimum(pre, 0.0) * LOG2_E - jnp.log2(1.0 + jnp.exp2(jnp.abs(pre) * (-LOG2_E)))) * (1.0 / GATE_TAU)
        gates.append(g.astype(_BF16))
        q2s.append(_per_direction(proj[:, WQ_COLS] * (GLA_DK ** -0.5)))
        k2s.append(_per_direction(proj[:, WK_COLS]))

    prefixes = []
    for s in range(n_sub):
        for cl in range(cps):
            lrows = slice(cl * GLA_CHUNK, (cl + 1) * GLA_CHUNK)
            prefixes.append(_dot(tri, gates[s][lrows]))

    vtbs = []
    for s in range(n_sub):
        vtb = lax.dot_general(wvt_ref[...], hbs[s], _NT, preferred_element_type=_F32).astype(_BF16)
        first = s * SUB_ROWS
        vt_ref[first // vt_tile, :, first % vt_tile:first % vt_tile + SUB_ROWS] = vtb
        vtbs.append(vtb)
        rc = projs[s][:, WRC_COLS]
        yc, yp = _conv_and_pool(rc[:, VAL_COLS:RC_COLS], cw_ref, pw_ref, ps_ref, seg)
        ry_ref[sub_rows[s], 0:VAL_COLS] = rc[:, 0:VAL_COLS].astype(_BF16)
        ry_ref[sub_rows[s], VAL_COLS:VAL_COLS + CONV_WIDTH] = yc.astype(_BF16)
        ry_ref[sub_rows[s], VAL_COLS + CONV_WIDTH:RY_COLS] = yp.astype(_BF16)

    for s in range(n_sub):
        g = gates[s].astype(_F32)
        for cl in range(cps):
            c = s * cps + cl
            lrows = slice(cl * GLA_CHUNK, (cl + 1) * GLA_CHUNK)
            rows = slice(c * GLA_CHUNK, (c + 1) * GLA_CHUNK)
            prefix = prefixes[c]
            total = prefix[GLA_CHUNK - 1:GLA_CHUNK, :]
            suffix = total - prefix + g[lrows]
            e = jnp.where(is_fwd, prefix, suffix)
            mvec = jnp.where(is_fwd, e[mid - 1:mid, :], e[mid:mid + 1, :])
            bm = e - mvec
            qe_ref[rows, :] = (q2s[s][lrows] * jnp.exp2(bm)).astype(_BF16)
            ki_ref[rows, :] = (k2s[s][lrows] * jnp.exp2(-bm)).astype(_BF16)
            kd = (k2s[s][lrows] * jnp.exp2(total - e)).astype(_BF16)
            dec_ref[c] = jnp.concatenate([total, mvec, jnp.zeros((6, KEY_COLS), _F32)], axis=0)
            for h in range(GLA_HEADS):
                vt_h = vtbs[s][h * GLA_DV:(h + 1) * GLA_DV, lrows]
                ut_ref[c, h] = _dot(vt_h, kd[:, h * SLAB:(h + 1) * SLAB]).astype(ut_ref.dtype)


def _inproj(x2d, mods3, mod_row_of_tile, layer, win, wvt, wts, tm, seg, vt_tile):
    tokens = x2d.shape[0]
    n_tiles = tokens // tm
    n_chunks = tokens // GLA_CHUNK
    cpt = tm // GLA_CHUNK
    assert SUB_ROWS % seg == 0
    assert tm % vt_tile == 0 and vt_tile % SUB_ROWS == 0
    of_layer = lambda shape: pl.BlockSpec((None,) + shape, lambda j: (layer,) + (0,) * len(shape))
    return pl.pallas_call(
        functools.partial(_inproj_kernel, seg=seg),
        grid=(n_tiles,),
        in_specs=[
            pl.BlockSpec((tm, D_MODEL), lambda j: (j, 0)),
            pl.BlockSpec((None, 1, N_MOD * D_MODEL), lambda j: (layer * MOD_ROWS + mod_row_of_tile(j), 0, 0)),
            of_layer((1, D_MODEL)),
            pl.BlockSpec((WIN_COLS, D_MODEL), lambda j: (0, 0)),
            pl.BlockSpec((VAL_COLS, D_MODEL), lambda j: (0, 0)),
            of_layer((2 * GATE_RANK, KEY_COLS)),
            of_layer((1, KEY_COLS)),
            of_layer((3, CONV_WIDTH)),
            of_layer((POOL_WIDTH, POOL_WIDTH)),
            of_layer((1, POOL_WIDTH)),
        ],
        out_specs=[
            pl.BlockSpec((tm, TOK_COLS), lambda j: (j, 0)),
            pl.BlockSpec((tm // vt_tile, VAL_COLS, vt_tile), lambda j: (j, 0, 0)),
            pl.BlockSpec((cpt, GLA_HEADS, GLA_DV, SLAB), lambda j: (j, 0, 0, 0)),
            pl.BlockSpec((cpt, 8, KEY_COLS), lambda j: (j, 0, 0)),
        ],
        out_shape=[
            jax.ShapeDtypeStruct((tokens, TOK_COLS), _BF16),
            jax.ShapeDtypeStruct((tokens // vt_tile, VAL_COLS, vt_tile), _BF16),
            jax.ShapeDtypeStruct((n_chunks, GLA_HEADS, GLA_DV, SLAB), _BF16),
            jax.ShapeDtypeStruct((n_chunks, 8, KEY_COLS), _F32),
        ],
        compiler_params=pltpu.CompilerParams(vmem_limit_bytes=VMEM_LIMIT_BYTES),
    )(x2d, mods3, wts["n1w"], win, wvt, wts["gw"], wts["gb"], wts["cw"], wts["pw"], wts["ps"])


def _scan_chunks(ut_ref, dec_ref, states, spt_ref, nc, chunk0=0, head0=0):
    lane = lax.broadcasted_iota(jnp.int32, (1, SLAB), 1)
    is_fwd = lane < GLA_DK

    def step(t, states):
        i = chunk0 + t
        j = chunk0 + nc - 1 - t
        new_states = []
        for s in range(SCAN_HEADS):
            h = head0 + s
            lanes = slice(h * SLAB, (h + 1) * SLAB)
            log_decay = jnp.where(is_fwd, dec_ref[i, 0:1, lanes], dec_ref[j, 0:1, lanes])
            log_mid = jnp.where(is_fwd, dec_ref[i, 1:2, lanes], dec_ref[j, 1:2, lanes])
            entering = (states[s] * jnp.exp2(log_mid)).astype(_BF16)
            spt_ref[i, h, :, 0:GLA_DK] = entering[:, 0:GLA_DK]
            spt_ref[j, h, :, GLA_DK:SLAB] = entering[:, GLA_DK:SLAB]
            inc = jnp.where(is_fwd, ut_ref[i, h], ut_ref[j, h]).astype(_F32)
            new_states.append(states[s] * jnp.exp2(log_decay) + inc)
        return tuple(new_states)

    return lax.fori_loop(0, nc, step, tuple(states), unroll=SCAN_UNROLL)


def _scan_kernel(ut_ref, dec_ref, s0_ref, spt_ref, sfin_ref):
    final = _scan_chunks(ut_ref, dec_ref, [s0_ref[h] for h in range(SCAN_HEADS)], spt_ref, ut_ref.shape[0])
    for h in range(SCAN_HEADS):
        sfin_ref[h] = final[h]


def _scan(ut, dec, s0, chunks_per_seq):
    n_chunks = ut.shape[0]
    bsz = n_chunks // chunks_per_seq
    return pl.pallas_call(
        _scan_kernel,
        grid=(bsz, GLA_HEADS // SCAN_HEADS),
        in_specs=[
            pl.BlockSpec((chunks_per_seq, SCAN_HEADS, GLA_DV, SLAB), lambda b, g: (b, g, 0, 0)),
            pl.BlockSpec((chunks_per_seq, 8, SCAN_HEADS * SLAB), lambda b, g: (b, 0, g)),
            pl.BlockSpec((None, SCAN_HEADS, GLA_DV, SLAB), lambda b, g: (b, g, 0, 0)),
        ],
        out_specs=[
            pl.BlockSpec((chunks_per_seq, SCAN_HEADS, GLA_DV, SLAB), lambda b, g: (b, g, 0, 0)),
            pl.BlockSpec((None, SCAN_HEADS, GLA_DV, SLAB), lambda b, g: (b, g, 0, 0)),
        ],
        out_shape=[
            jax.ShapeDtypeStruct((n_chunks, GLA_HEADS, GLA_DV, SLAB), _BF16),
            jax.ShapeDtypeStruct((bsz, GLA_HEADS, GLA_DV, SLAB), _F32),
        ],
        compiler_params=pltpu.CompilerParams(vmem_limit_bytes=VMEM_LIMIT_BYTES),
    )(ut, dec, s0)


def _mix_weight_stager(layer, wo_hbm, w1_hbm, w2_hbm, wo_ref, w1_ref, w2_ref, stage, sems):
    pieces = [(wo_hbm.at[layer], wo_ref)]
    for i in range(D_FF // D_MODEL):
        cols = slice(i * D_MODEL, (i + 1) * D_MODEL)
        pieces.append((w1_hbm.at[layer, :, cols], w1_ref.at[:, cols]))
        pieces.append((w2_hbm.at[layer, cols, :], w2_ref.at[cols, :]))
    copies = [pltpu.make_async_copy(src, stage.at[k % STAGE_SLOTS], sems.at[k % STAGE_SLOTS])
              for k, (src, _) in enumerate(pieces)]
    ready = [None]

    def need(n):
        if ready[0] is None:
            for k in range(STAGE_SLOTS):
                copies[k].start()
            ready[0] = 0
        for k in range(ready[0], min(n, len(pieces))):
            copies[k].wait()
            pieces[k][1][...] = stage[k % STAGE_SLOTS].astype(_BF16)
            if k + STAGE_SLOTS < len(pieces):
                copies[k + STAGE_SLOTS].start()
            ready[0] = k + 1

    return need


def _mix_body(x_ref, mod_ref, n2w_ref, tok_ref, vt_ref, spt_ref, gnw_ref, fnw_ref, o_ref, y_scr, wo_ref, w1_ref, w2_ref,
              need, final):
    tm = x_ref.shape[0]
    qe_ref, ki_ref, ry_ref = tok_ref.at[:, TOK_QE], tok_ref.at[:, TOK_KI], tok_ref.at[:, TOK_RY]
    g1 = mod_ref[:, 2 * D_MODEL:3 * D_MODEL]
    sh2 = mod_ref[:, 3 * D_MODEL:4 * D_MODEL]
    scale2 = n2w_ref[...] * (1.0 + mod_ref[:, 4 * D_MODEL:5 * D_MODEL])
    g2 = mod_ref[:, 5 * D_MODEL:6 * D_MODEL]
    row = lax.broadcasted_iota(jnp.int32, (GLA_CHUNK, GLA_CHUNK), 0)
    col = lax.broadcasted_iota(jnp.int32, (GLA_CHUNK, GLA_CHUNK), 1)
    lane = lax.broadcasted_iota(jnp.int32, (GLA_CHUNK, SLAB), 1)
    fwd_lane = lane < GLA_DK
    gnw = gnw_ref[...]
    need(0)

    for c in range(tm // GLA_CHUNK):
        rows = slice(c * GLA_CHUNK, (c + 1) * GLA_CHUNK)
        for h in range(GLA_HEADS):
            lanes = slice(h * SLAB, (h + 1) * SLAB)
            qe = qe_ref[rows, lanes]
            ki = ki_ref[rows, lanes]
            zero = jnp.zeros_like(ki)
            keys = jnp.concatenate([jnp.where(fwd_lane, ki, zero), jnp.where(fwd_lane, zero, ki)], axis=0)
            a2 = lax.dot_general(qe, keys, _NT, preferred_element_type=_F32)
            am = jnp.where(col <= row, a2[:, 0:GLA_CHUNK], 0.0) + jnp.where(col >= row, a2[:, GLA_CHUNK:], 0.0)
            lhs = jnp.concatenate([am.astype(_BF16), qe], axis=1)
            rhs = jnp.concatenate([vt_ref[h * GLA_DV:(h + 1) * GLA_DV, rows], spt_ref[c, h]], axis=1)
            o = lax.dot_general(lhs, rhs, _NT, preferred_element_type=_F32)
            o = o * lax.rsqrt(jnp.mean(o * o, axis=-1, keepdims=True) + EPS) * gnw
            r = ry_ref[rows, h * GLA_DV:(h + 1) * GLA_DV].astype(_F32)
            y_scr[rows, h * GLA_DV:(h + 1) * GLA_DV] = (o * (r * _sigmoid(r))).astype(_BF16)
        need(c + 1)

    need(1)
    mixed = _dot(y_scr[...], wo_ref[0:VAL_COLS, :]) + _dot(ry_ref[:, VAL_COLS:RY_COLS], wo_ref[VAL_COLS:D_MODEL, :])
    x1 = x_ref[...] + g1 * mixed
    xn = x1 * lax.rsqrt(jnp.mean(x1 * x1, axis=-1, keepdims=True) + EPS)
    h2 = (xn * scale2 + sh2).astype(_BF16)
    acc = jnp.zeros((tm, D_MODEL), _F32)
    for f in range(D_FF // FF_CHUNK):
        cols = slice(f * FF_CHUNK, (f + 1) * FF_CHUNK)
        need(1 + 2 * pl.cdiv(cols.stop, D_MODEL))
        a = jnp.maximum(_dot(h2, w1_ref[:, cols]), 0.0)
        acc = acc + _dot((a * a).astype(_BF16), w2_ref[cols, :])
    out = x1 + g2 * acc
    if final:
        out = out * lax.rsqrt(jnp.mean(out * out, axis=-1, keepdims=True) + EPS) * fnw_ref[...]
    o_ref[...] = out


def _mix_kernel(x_ref, mod_ref, n2w_ref, tok_ref, vt_ref, spt_ref, gnw_ref, wo_hbm, w1_hbm, w2_hbm, fnw_ref,
                o_ref, y_scr, wo_ref, w1_ref, w2_ref, stage, sems, *, layer, final, single_step):
    body = functools.partial(_mix_body, x_ref, mod_ref, n2w_ref, tok_ref, vt_ref, spt_ref, gnw_ref, fnw_ref, o_ref, y_scr,
                             wo_ref, w1_ref, w2_ref, final=final)

    def staging_step():
        body(need=_mix_weight_stager(layer, wo_hbm, w1_hbm, w2_hbm, wo_ref, w1_ref, w2_ref, stage, sems))

    if single_step:
        staging_step()
    else:
        first = pl.program_id(0) == 0
        pl.when(first)(staging_step)
        pl.when(jnp.logical_not(first))(functools.partial(body, need=lambda n: None))


def _mix_weight_scratch():
    return [pltpu.VMEM((D_MODEL, D_MODEL), _BF16), pltpu.VMEM((D_MODEL, D_FF), _BF16), pltpu.VMEM((D_FF, D_MODEL), _BF16),
            pltpu.VMEM((STAGE_SLOTS, D_MODEL, D_MODEL), _F32), pltpu.SemaphoreType.DMA((STAGE_SLOTS,))]


def _mix(x2d, mods3, mod_row_of_tile, layer, tok, vt, spt, wts, tm, final):
    tokens = x2d.shape[0]
    cpt = tm // GLA_CHUNK
    of_layer = lambda shape: pl.BlockSpec((None,) + shape, lambda j: (layer,) + (0,) * len(shape),
                                          pipeline_mode=pl.Buffered(1))
    return pl.pallas_call(
        functools.partial(_mix_kernel, layer=layer, final=final, single_step=tokens == tm),
        grid=(tokens // tm,),
        in_specs=[
            pl.BlockSpec((tm, D_MODEL), lambda j: (j, 0)),
            pl.BlockSpec((None, 1, N_MOD * D_MODEL), lambda j: (layer * MOD_ROWS + mod_row_of_tile(j), 0, 0)),
            of_layer((1, D_MODEL)),
            pl.BlockSpec((tm, TOK_COLS), lambda j: (j, 0)),
            pl.BlockSpec((None, VAL_COLS, tm), lambda j: (j, 0, 0)),
            pl.BlockSpec((cpt, GLA_HEADS, GLA_DV, SLAB), lambda j: (j, 0, 0, 0)),
            of_layer((1, GLA_DV)),
            pl.BlockSpec(memory_space=pl.ANY),
            pl.BlockSpec(memory_space=pl.ANY),
            pl.BlockSpec(memory_space=pl.ANY),
            pl.BlockSpec((1, D_MODEL), lambda j: (0, 0), pipeline_mode=pl.Buffered(1)),
        ],
        out_specs=pl.BlockSpec((tm, D_MODEL), lambda j: (j, 0)),
        out_shape=jax.ShapeDtypeStruct((tokens, D_MODEL), _F32),
        scratch_shapes=[pltpu.VMEM((tm, VAL_COLS), _BF16)] + _mix_weight_scratch(),
        compiler_params=pltpu.CompilerParams(vmem_limit_bytes=VMEM_LIMIT_BYTES),
    )(x2d, mods3, wts["n2w"], tok, vt, spt, wts["gnw"], wts["wo"], wts["w1"], wts["w2"], wts["fnw"])


def _stage_in_projection(layer, w_in_t_hbm, win_ref, wvt_ref, stage, sems):
    pieces = []
    for src, dst_ref, dst in ((IN_Z, win_ref, WZ_COLS.start), (IN_QK, win_ref, WQ_COLS.start),
                              (IN_V, wvt_ref, 0), (IN_RC, win_ref, WRC_COLS.start)):
        for off in range(0, src.stop - src.start, IN_STAGE_ROWS):
            pieces.append((src.start + off, dst_ref, dst + off, min(IN_STAGE_ROWS, src.stop - src.start - off)))
    copies = [pltpu.make_async_copy(w_in_t_hbm.at[layer, s0:s0 + n, :], stage.at[k % STAGE_SLOTS, 0:n, :],
                                    sems.at[k % STAGE_SLOTS]) for k, (s0, _, _, n) in enumerate(pieces)]
    for k in range(STAGE_SLOTS):
        copies[k].start()
    win_ref[WZ_COLS.stop:WQ_COLS.start, :] = jnp.zeros((WQ_COLS.start - WZ_COLS.stop, D_MODEL), _BF16)
    for k, (_, dst_ref, d0, n) in enumerate(pieces):
        copies[k].wait()
        dst_ref[d0:d0 + n, :] = stage[k % STAGE_SLOTS, 0:n, :].astype(_BF16)
        if k + STAGE_SLOTS < len(pieces):
            copies[k + STAGE_SLOTS].start()


def _context_kernel(*refs, layer, seg, chunks_per_seq, with_mix):
    n_in = 9
    x_ref, mod_ref, n1w_ref, w_in_t_hbm = refs[0:4]
    if with_mix:
        n2w_ref, gnw_ref, wo_hbm, w1_hbm, w2_hbm, fnw_ref = refs[n_in:n_in + 6]
        sfin_ref, win_ref, wvt_ref, o_ref = refs[n_in + 6:n_in + 10]
        tok_scr, vt_scr, ut_scr, dec_scr, spt_scr, in_stage, in_sems, y_scr = refs[n_in + 10:n_in + 18]
        mix_weight_scr = refs[n_in + 18:]
    else:
        sfin_ref, win_ref, wvt_ref = refs[n_in:n_in + 3]
        tok_scr, vt_scr, ut_scr, dec_scr, spt_scr, in_stage, in_sems = refs[n_in + 3:]

    _stage_in_projection(layer, w_in_t_hbm, win_ref, wvt_ref, in_stage, in_sems)
    _inproj_kernel(x_ref, mod_ref, n1w_ref, win_ref, wvt_ref, *refs[4:n_in], tok_scr, vt_scr, ut_scr, dec_scr, seg=seg)

    sfin_ref[...] = jnp.zeros(sfin_ref.shape, _F32)
    for b in range(ut_scr.shape[0] // chunks_per_seq):
        for head0 in range(0, GLA_HEADS, SCAN_HEADS):
            start = [sfin_ref[b, head0 + s] for s in range(SCAN_HEADS)]
            final = _scan_chunks(ut_scr, dec_scr, start, spt_scr, chunks_per_seq,
                                 chunk0=b * chunks_per_seq, head0=head0)
            for s in range(SCAN_HEADS):
                sfin_ref[b, head0 + s] = final[s]

    if with_mix:
        _mix_kernel(x_ref, mod_ref, n2w_ref, tok_scr, vt_scr.at[0], spt_scr, gnw_ref, wo_hbm, w1_hbm, w2_hbm, fnw_ref,
                    o_ref, y_scr, *mix_weight_scr, layer=layer, final=False, single_step=True)


def _context_layer(xc, mods3, mod_row, layer, wts, ctx_len, with_mix):
    rows = xc.shape[0]
    bsz = rows // ctx_len
    n_chunks = rows // GLA_CHUNK
    assert rows % SUB_ROWS == 0 and SUB_ROWS % ctx_len == 0
    of_layer = lambda shape: pl.BlockSpec((None,) + shape, lambda j: (layer,) + (0,) * len(shape),
                                          pipeline_mode=pl.Buffered(1))
    in_specs = [
        pl.BlockSpec((rows, D_MODEL), lambda j: (0, 0)),
        pl.BlockSpec((None, 1, N_MOD * D_MODEL), lambda j: (layer * MOD_ROWS + mod_row, 0, 0)),
        of_layer((1, D_MODEL)),
        pl.BlockSpec(memory_space=pl.ANY),
        of_layer((2 * GATE_RANK, KEY_COLS)),
        of_layer((1, KEY_COLS)),
        of_layer((3, CONV_WIDTH)),
        of_layer((POOL_WIDTH, POOL_WIDTH)),
        of_layer((1, POOL_WIDTH)),
    ]
    args = [xc, mods3, wts["n1w"], wts["w_in_t"], wts["gw"], wts["gb"], wts["cw"], wts["pw"], wts["ps"]]
    out_specs = [pl.BlockSpec((bsz, GLA_HEADS, GLA_DV, SLAB), lambda j: (0, 0, 0, 0)),
                 pl.BlockSpec((WIN_COLS, D_MODEL), lambda j: (0, 0)),
                 pl.BlockSpec((VAL_COLS, D_MODEL), lambda j: (0, 0))]
    out_shape = [jax.ShapeDtypeStruct((bsz, GLA_HEADS, GLA_DV, SLAB), _F32),
                 jax.ShapeDtypeStruct((WIN_COLS, D_MODEL), _BF16),
                 jax.ShapeDtypeStruct((VAL_COLS, D_MODEL), _BF16)]
    scratch = [
        pltpu.VMEM((rows, TOK_COLS), _BF16),
        pltpu.VMEM((1, VAL_COLS, rows), _BF16),
        pltpu.VMEM((n_chunks, GLA_HEADS, GLA_DV, SLAB), _BF16),
        pltpu.VMEM((n_chunks, 8, KEY_COLS), _F32),
        pltpu.VMEM((n_chunks, GLA_HEADS, GLA_DV, SLAB), _BF16),
        pltpu.VMEM((STAGE_SLOTS, IN_STAGE_ROWS, D_MODEL), _F32),
        pltpu.SemaphoreType.DMA((STAGE_SLOTS,)),
    ]
    if with_mix:
        in_specs += [of_layer((1, D_MODEL)), of_layer((1, GLA_DV))] + [pl.BlockSpec(memory_space=pl.ANY)] * 3
        in_specs += [pl.BlockSpec((1, D_MODEL), lambda j: (0, 0), pipeline_mode=pl.Buffered(1))]
        args += [wts["n2w"], wts["gnw"], wts["wo"], wts["w1"], wts["w2"], wts["fnw"]]
        out_specs.append(pl.BlockSpec((rows, D_MODEL), lambda j: (0, 0)))
        out_shape.append(jax.ShapeDtypeStruct((rows, D_MODEL), _F32))
        scratch += [pltpu.VMEM((rows, VAL_COLS), _BF16)] + _mix_weight_scratch()
    outs = pl.pallas_call(
        functools.partial(_context_kernel, layer=layer, seg=ctx_len, chunks_per_seq=ctx_len // GLA_CHUNK,
                          with_mix=with_mix),
        grid=(1,),
        in_specs=in_specs,
        out_specs=out_specs,
        out_shape=out_shape,
        scratch_shapes=scratch,
        compiler_params=pltpu.CompilerParams(vmem_limit_bytes=VMEM_LIMIT_BYTES),
    )(*args)
    return tuple(outs) if with_mix else tuple(outs) + (None,)


def _prepare_weights(norm1_w, norm2_w, w_in, gate_w, gate_b, gla_norm_w, conv_w, pool_w, pool_scale, w_out,
                     w_mlp1, w_mlp2, final_norm_w):
    depth = w_in.shape[0]
    assert w_in.shape[2] == IN_COLS
    zeros = jnp.zeros((depth, GATE_RANK, GLA_HEADS, GLA_DK), _F32)
    gf = gate_w[:, 0].reshape(depth, GATE_RANK, GLA_HEADS, GLA_DK)
    gb = gate_w[:, 1].reshape(depth, GATE_RANK, GLA_HEADS, GLA_DK)
    gw = jnp.concatenate([jnp.concatenate([gf, zeros], -1), jnp.concatenate([zeros, gb], -1)], 1)
    gbias = jnp.concatenate([gate_b[:, 0].reshape(depth, GLA_HEADS, GLA_DK),
                             gate_b[:, 1].reshape(depth, GLA_HEADS, GLA_DK)], -1)
    n_groups = POOL_WIDTH // POOL_GROUP
    same_group = jnp.eye(n_groups, dtype=_F32)[None, :, None, :, None]
    pw = (pool_w[:, :, :, None, :] * same_group).reshape(depth, POOL_WIDTH, POOL_WIDTH)
    return {
        "n1w": norm1_w.reshape(depth, 1, D_MODEL),
        "n2w": norm2_w.reshape(depth, 1, D_MODEL),
        "w_in_t": jnp.swapaxes(w_in, 1, 2),
        "gw": gw.reshape(depth, 2 * GATE_RANK, KEY_COLS).astype(_BF16),
        "gb": gbias.reshape(depth, 1, KEY_COLS),
        "gnw": gla_norm_w.reshape(depth, 1, GLA_DV),
        "cw": conv_w,
        "pw": pw.astype(_BF16),
        "ps": pool_scale.reshape(depth, 1, POOL_WIDTH),
        "wo": w_out,
        "w1": w_mlp1,
        "w2": w_mlp2,
        "fnw": final_norm_w.reshape(1, D_MODEL),
    }


def kernel(x, c, ctx, c_ctx, w_mod, b_mod, norm1_w, norm2_w, w_in, gla_gate_w, gla_gate_b, gla_norm_w, conv_w, pool_w, pool_scale, w_out, w_mlp1, w_mlp2, final_norm_w):
    bsz, n, _ = x.shape
    ctx_len = ctx.shape[1]
    depth = w_in.shape[0]
    assert bsz + 1 <= MOD_ROWS and n % LATENT_IN_TILE == 0 and SUB_ROWS % ctx_len == 0
    ctx_row = bsz

    cv = jnp.concatenate([c, c_ctx[None, :], jnp.zeros((MOD_ROWS - bsz - 1, D_MODEL), _F32)], axis=0)
    mods3 = _adaln(cv, w_mod, b_mod)
    wts = _prepare_weights(norm1_w, norm2_w, w_in, gla_gate_w, gla_gate_b, gla_norm_w, conv_w, pool_w, pool_scale,
                           w_out, w_mlp1, w_mlp2, final_norm_w)

    xl = x.reshape(bsz * n, D_MODEL)
    xc = ctx.reshape(bsz * ctx_len, D_MODEL)
    lat_in_row = lambda j: j // (n // LATENT_IN_TILE)
    lat_mix_row = lambda j: j // (n // LATENT_MIX_TILE)

    for l in range(depth):
        last = l == depth - 1
        s_ctx, win, wvt, xc = _context_layer(xc, mods3, ctx_row, l, wts, ctx_len, with_mix=not last)

        tok, vt, ut, dec = _inproj(xl, mods3, lat_in_row, l, win, wvt, wts, LATENT_IN_TILE, GRID_W, LATENT_MIX_TILE)
        spt, _ = _scan(ut, dec, s_ctx, n // GLA_CHUNK)
        xl = _mix(xl, mods3, lat_mix_row, l, tok, vt, spt, wts, LATENT_MIX_TILE, last)
    return xl.reshape(bsz, n, D_MODEL)
```

```python
import functools

import jax
import jax.numpy as jnp
from jax import lax
from jax.experimental import pallas as pl
from jax.experimental.pallas import tpu as pltpu

D_MODEL = 1024
GLA_HEADS = 4
GLA_DK = 64
GLA_DV = 128
SLAB = 2 * GLA_DK
KEY_COLS = GLA_HEADS * SLAB
VAL_COLS = GLA_HEADS * GLA_DV
GATE_RANK = 16
GATE_TAU = 16.0
LOG2_E = 1.4426950408889634
GLA_CHUNK = 128
SUB_ROWS = 256
LATENT_IN_TILE = 1024
LATENT_MIX_TILE = 512
CONV_WIDTH = 256
POOL_WIDTH = 256
POOL_GROUP = 64
POOL_HALF_WINDOWS = (1, 2, 4, 8)
D_FF = 4096
FF_CHUNK = 1024
N_MOD = 6
EPS = 1e-6
GRID_W = 64
RC_COLS = VAL_COLS + 3 * CONV_WIDTH + POOL_WIDTH
RY_COLS = VAL_COLS + CONV_WIDTH + POOL_WIDTH
TOK_QE = slice(0, KEY_COLS)
TOK_KI = slice(KEY_COLS, 2 * KEY_COLS)
TOK_RY = slice(2 * KEY_COLS, 2 * KEY_COLS + RY_COLS)
TOK_COLS = TOK_RY.stop
WZ_COLS = slice(0, 2 * GATE_RANK)
WQ_COLS = slice(128, 128 + GLA_HEADS * GLA_DK)
WK_COLS = slice(WQ_COLS.stop, WQ_COLS.stop + GLA_HEADS * GLA_DK)
WRC_COLS = slice(WK_COLS.stop, WK_COLS.stop + RC_COLS)
WIN_COLS = WRC_COLS.stop
IN_QK = slice(0, 2 * GLA_HEADS * GLA_DK)
IN_V = slice(IN_QK.stop, IN_QK.stop + VAL_COLS)
IN_Z = slice(IN_V.stop, IN_V.stop + 2 * GATE_RANK)
IN_RC = slice(IN_Z.stop, IN_Z.stop + RC_COLS)
IN_COLS = IN_RC.stop
IN_STAGE_ROWS = 512
SCAN_UNROLL = 2
SCAN_HEADS = 2
MOD_ROWS = 8
ADALN_COLS = 1536
VMEM_LIMIT_BYTES = 56 * 1024 * 1024
STAGE_SLOTS = 2

_NT = (((1,), (1,)), ((), ()))
_BF16 = jnp.bfloat16
_F32 = jnp.float32


def _dot(a, b):
    return jnp.dot(a, b, preferred_element_type=_F32)


def _sigmoid(x):
    return 1.0 / (1.0 + jnp.exp(-x))


def _adaln_kernel(cv_ref, w_ref, b_ref, o_ref):
    cv = cv_ref[...]
    s = (cv * _sigmoid(cv)).astype(_BF16)
    mod = _dot(s, w_ref[...].astype(_BF16)) + b_ref[pl.ds(pl.program_id(0), 1), :]
    for r in range(MOD_ROWS):
        o_ref[r] = mod[r:r + 1, :]


def _adaln(cv, w_mod, b_mod):
    depth = w_mod.shape[0]
    cols = w_mod.shape[2]
    return pl.pallas_call(
        _adaln_kernel,
        grid=(depth, cols // ADALN_COLS),
        in_specs=[
            pl.BlockSpec((MOD_ROWS, D_MODEL), lambda l, j: (0, 0)),
            pl.BlockSpec((None, D_MODEL, ADALN_COLS), lambda l, j: (l, 0, j)),
            pl.BlockSpec((depth, ADALN_COLS), lambda l, j: (0, j)),
        ],
        out_specs=pl.BlockSpec((MOD_ROWS, 1, ADALN_COLS), lambda l, j: (l, 0, j)),
        out_shape=jax.ShapeDtypeStruct((depth * MOD_ROWS, 1, cols), _F32),
        compiler_params=pltpu.CompilerParams(vmem_limit_bytes=VMEM_LIMIT_BYTES),
    )(cv, w_mod, b_mod)


def _per_direction(t):
    lane = lax.broadcasted_iota(jnp.int32, (1, SLAB), 1)
    first = lane < GLA_DK
    slabs = []
    for p in range(GLA_HEADS // 2):
        pair = t[:, p * SLAB:(p + 1) * SLAB]
        swapped = pltpu.roll(pair, GLA_DK, 1)
        slabs += [jnp.where(first, pair, swapped), jnp.where(first, swapped, pair)]
    return jnp.concatenate(slabs, axis=1)


def _conv_and_pool(rc, cw_ref, pw_ref, ps_ref, seg):
    rows = rc.shape[0]
    pos = lax.broadcasted_iota(jnp.int32, (rows, 1), 0) % seg
    c0 = 0
    cb = rc[:, c0:c0 + CONV_WIDTH]
    zc = rc[:, c0 + CONV_WIDTH:c0 + 2 * CONV_WIDTH] * rc[:, c0 + 2 * CONV_WIDTH:c0 + 3 * CONV_WIDTH]
    z_prev = jnp.where(pos >= 1, pltpu.roll(zc, 1, 0), 0.0)
    z_next = jnp.where(pos <= seg - 2, pltpu.roll(zc, rows - 1, 0), 0.0)
    yc = cb * (cw_ref[0:1, :] * z_prev + cw_ref[1:2, :] * zc + cw_ref[2:3, :] * z_next)

    p0 = 3 * CONV_WIDTH
    lane128 = lax.broadcasted_iota(jnp.int32, (1, 2 * POOL_GROUP), 1)
    pooled = []
    for blk in range(POOL_WIDTH // (2 * POOL_GROUP)):
        u = rc[:, p0 + blk * 2 * POOL_GROUP:p0 + (blk + 1) * 2 * POOL_GROUP]
        h_lo, h_hi = POOL_HALF_WINDOWS[2 * blk], POOL_HALF_WINDOWS[2 * blk + 1]
        half = jnp.where(lane128 < POOL_GROUP, h_lo, h_hi)
        acc = jnp.zeros_like(u)
        for d in range(-h_hi, h_hi):
            shifted = u if d == 0 else pltpu.roll(u, (-d) % rows, 0)
            valid = (pos + d >= 0) & (pos + d < seg) & (d >= -half) & (d < half)
            acc = acc + jnp.where(valid, shifted, 0.0)
        cnt = (jnp.minimum(pos + half, seg) - jnp.maximum(pos - half, 0)).astype(_F32)
        pooled.append(acc / cnt - u)
    pool_in = jnp.concatenate(pooled, axis=1).astype(_BF16)
    return yc, _dot(pool_in, pw_ref[...]) * ps_ref[...]


def _inproj_kernel(x_ref, mod_ref, n1w_ref, win_ref, wvt_ref, gw_ref, gb_ref, cw_ref, pw_ref, ps_ref,
                   tok_ref, vt_ref, ut_ref, dec_ref, *, seg):
    qe_ref, ki_ref, ry_ref = tok_ref.at[:, TOK_QE], tok_ref.at[:, TOK_KI], tok_ref.at[:, TOK_RY]
    vt_tile = vt_ref.shape[2]
    tm = x_ref.shape[0]
    sh1 = mod_ref[:, 0:D_MODEL]
    scale1 = n1w_ref[...] * (1.0 + mod_ref[:, D_MODEL:2 * D_MODEL])
    row = lax.broadcasted_iota(jnp.int32, (GLA_CHUNK, GLA_CHUNK), 0)
    col = lax.broadcasted_iota(jnp.int32, (GLA_CHUNK, GLA_CHUNK), 1)
    scan_op = jnp.concatenate([(col <= row).astype(_BF16), (col >= row).astype(_BF16)], axis=1)
    lane = lax.broadcasted_iota(jnp.int32, (1, KEY_COLS), 1)
    is_fwd = (lane % SLAB) < GLA_DK
    mid = GLA_CHUNK // 2

    n_sub = tm // SUB_ROWS
    cps = SUB_ROWS // GLA_CHUNK
    sub_rows = [slice(s * SUB_ROWS, (s + 1) * SUB_ROWS) for s in range(n_sub)]
    hbs, projs, gates, q2s, k2s = [], [], [], [], []
    for s in range(n_sub):
        x = x_ref[sub_rows[s], :]
        xn = x * lax.rsqrt(jnp.mean(x * x, axis=-1, keepdims=True) + EPS)
        hbs.append((xn * scale1 + sh1).astype(_BF16))
    for s in range(n_sub):
        proj = lax.dot_general(hbs[s], win_ref[...], _NT, preferred_element_type=_F32)
        projs.append(proj)
        pre = _dot(proj[:, WZ_COLS].astype(_BF16), gw_ref[...]) + gb_ref[...]
        g = (jnp.minimum(pre, 0.0) * LOG2_E - jnp.log2(1.0 + jnp.exp2(jnp.abs(pre) * (-LOG2_E)))) * (1.0 / GATE_TAU)
        gates.append(g.astype(_BF16))
        q2s.append(_per_direction(proj[:, WQ_COLS] * (GLA_DK ** -0.5)))
        k2s.append(_per_direction(proj[:, WK_COLS]))

    decays = []
    for s in range(n_sub):
        for cl in range(cps):
            g = gates[s][cl * GLA_CHUNK:(cl + 1) * GLA_CHUNK]
            zero = jnp.zeros_like(g)
            by_direction = jnp.concatenate([jnp.where(is_fwd, g, zero), jnp.where(is_fwd, zero, g)], axis=0)
            decays.append(_dot(scan_op, by_direction))

    vtbs = []
    for s in range(n_sub):
        vtb = lax.dot_general(wvt_ref[...], hbs[s], _NT, preferred_element_type=_F32).astype(_BF16)
        first = s * SUB_ROWS
        vt_ref[first // vt_tile, :, first % vt_tile:first % vt_tile + SUB_ROWS] = vtb
        vtbs.append(vtb)
        rc = projs[s][:, WRC_COLS]
        yc, yp = _conv_and_pool(rc[:, VAL_COLS:RC_COLS], cw_ref, pw_ref, ps_ref, seg)
        ry_ref[sub_rows[s], 0:VAL_COLS] = rc[:, 0:VAL_COLS].astype(_BF16)
        ry_ref[sub_rows[s], VAL_COLS:VAL_COLS + CONV_WIDTH] = yc.astype(_BF16)
        ry_ref[sub_rows[s], VAL_COLS + CONV_WIDTH:RY_COLS] = yp.astype(_BF16)

    for s in range(n_sub):
        for cl in range(cps):
            c = s * cps + cl
            lrows = slice(cl * GLA_CHUNK, (cl + 1) * GLA_CHUNK)
            rows = slice(c * GLA_CHUNK, (c + 1) * GLA_CHUNK)
            e = decays[c]
            total = jnp.where(is_fwd, e[GLA_CHUNK - 1:GLA_CHUNK, :], e[0:1, :])
            mvec = jnp.where(is_fwd, e[mid - 1:mid, :], e[mid:mid + 1, :])
            bm = e - mvec
            qe_ref[rows, :] = (q2s[s][lrows] * jnp.exp2(bm)).astype(_BF16)
            k_mid = k2s[s][lrows] * jnp.exp2(-bm)
            ki_ref[rows, :] = k_mid.astype(_BF16)
            kd = (k_mid * jnp.exp2(total - mvec)).astype(_BF16)
            dec_ref[c] = jnp.concatenate([total, mvec, jnp.zeros((6, KEY_COLS), _F32)], axis=0)
            for h in range(GLA_HEADS):
                vt_h = vtbs[s][h * GLA_DV:(h + 1) * GLA_DV, lrows]
                ut_ref[c, h] = _dot(vt_h, kd[:, h * SLAB:(h + 1) * SLAB]).astype(ut_ref.dtype)


def _inproj(x2d, mods3, mod_row_of_tile, layer, win, wvt, wts, tm, seg, vt_tile):
    tokens = x2d.shape[0]
    n_tiles = tokens // tm
    n_chunks = tokens // GLA_CHUNK
    cpt = tm // GLA_CHUNK
    assert SUB_ROWS % seg == 0
    assert tm % vt_tile == 0 and vt_tile % SUB_ROWS == 0
    of_layer = lambda shape: pl.BlockSpec((None,) + shape, lambda j: (layer,) + (0,) * len(shape))
    return pl.pallas_call(
        functools.partial(_inproj_kernel, seg=seg),
        grid=(n_tiles,),
        in_specs=[
            pl.BlockSpec((tm, D_MODEL), lambda j: (j, 0)),
            pl.BlockSpec((None, 1, N_MOD * D_MODEL), lambda j: (layer * MOD_ROWS + mod_row_of_tile(j), 0, 0)),
            of_layer((1, D_MODEL)),
            pl.BlockSpec((WIN_COLS, D_MODEL), lambda j: (0, 0)),
            pl.BlockSpec((VAL_COLS, D_MODEL), lambda j: (0, 0)),
            of_layer((2 * GATE_RANK, KEY_COLS)),
            of_layer((1, KEY_COLS)),
            of_layer((3, CONV_WIDTH)),
            of_layer((POOL_WIDTH, POOL_WIDTH)),
            of_layer((1, POOL_WIDTH)),
        ],
        out_specs=[
            pl.BlockSpec((tm, TOK_COLS), lambda j: (j, 0)),
            pl.BlockSpec((tm // vt_tile, VAL_COLS, vt_tile), lambda j: (j, 0, 0)),
            pl.BlockSpec((cpt, GLA_HEADS, GLA_DV, SLAB), lambda j: (j, 0, 0, 0)),
            pl.BlockSpec((cpt, 8, KEY_COLS), lambda j: (j, 0, 0)),
        ],
        out_shape=[
            jax.ShapeDtypeStruct((tokens, TOK_COLS), _BF16),
            jax.ShapeDtypeStruct((tokens // vt_tile, VAL_COLS, vt_tile), _BF16),
            jax.ShapeDtypeStruct((n_chunks, GLA_HEADS, GLA_DV, SLAB), _BF16),
            jax.ShapeDtypeStruct((n_chunks, 8, KEY_COLS), _F32),
        ],
        compiler_params=pltpu.CompilerParams(vmem_limit_bytes=VMEM_LIMIT_BYTES),
    )(x2d, mods3, wts["n1w"], win, wvt, wts["gw"], wts["gb"], wts["cw"], wts["pw"], wts["ps"])


def _scan_chunks(ut_ref, dec_ref, states, spt_ref, nc, chunk0=0, head0=0):
    lane = lax.broadcasted_iota(jnp.int32, (1, SLAB), 1)
    is_fwd = lane < GLA_DK

    def step(t, states):
        i = chunk0 + t
        j = chunk0 + nc - 1 - t
        new_states = []
        for s in range(SCAN_HEADS):
            h = head0 + s
            lanes = slice(h * SLAB, (h + 1) * SLAB)
            log_decay = jnp.where(is_fwd, dec_ref[i, 0:1, lanes], dec_ref[j, 0:1, lanes])
            log_mid = jnp.where(is_fwd, dec_ref[i, 1:2, lanes], dec_ref[j, 1:2, lanes])
            entering = (states[s] * jnp.exp2(log_mid)).astype(_BF16)
            spt_ref[i, h, :, 0:GLA_DK] = entering[:, 0:GLA_DK]
            spt_ref[j, h, :, GLA_DK:SLAB] = entering[:, GLA_DK:SLAB]
            inc = jnp.where(is_fwd, ut_ref[i, h], ut_ref[j, h]).astype(_F32)
            new_states.append(states[s] * jnp.exp2(log_decay) + inc)
        return tuple(new_states)

    return lax.fori_loop(0, nc, step, tuple(states), unroll=SCAN_UNROLL)


def _scan_kernel(ut_ref, dec_ref, s0_ref, spt_ref, sfin_ref):
    final = _scan_chunks(ut_ref, dec_ref, [s0_ref[h] for h in range(SCAN_HEADS)], spt_ref, ut_ref.shape[0])
    for h in range(SCAN_HEADS):
        sfin_ref[h] = final[h]


def _scan(ut, dec, s0, chunks_per_seq):
    n_chunks = ut.shape[0]
    bsz = n_chunks // chunks_per_seq
    return pl.pallas_call(
        _scan_kernel,
        grid=(bsz, GLA_HEADS // SCAN_HEADS),
        in_specs=[
            pl.BlockSpec((chunks_per_seq, SCAN_HEADS, GLA_DV, SLAB), lambda b, g: (b, g, 0, 0)),
            pl.BlockSpec((chunks_per_seq, 8, SCAN_HEADS * SLAB), lambda b, g: (b, 0, g)),
            pl.BlockSpec((None, SCAN_HEADS, GLA_DV, SLAB), lambda b, g: (b, g, 0, 0)),
        ],
        out_specs=[
            pl.BlockSpec((chunks_per_seq, SCAN_HEADS, GLA_DV, SLAB), lambda b, g: (b, g, 0, 0)),
            pl.BlockSpec((None, SCAN_HEADS, GLA_DV, SLAB), lambda b, g: (b, g, 0, 0)),
        ],
        out_shape=[
            jax.ShapeDtypeStruct((n_chunks, GLA_HEADS, GLA_DV, SLAB), _BF16),
            jax.ShapeDtypeStruct((bsz, GLA_HEADS, GLA_DV, SLAB), _F32),
        ],
        compiler_params=pltpu.CompilerParams(vmem_limit_bytes=VMEM_LIMIT_BYTES),
    )(ut, dec, s0)


def _mix_weight_stager(layer, wo_hbm, w1_hbm, w2_hbm, wo_ref, w1_ref, w2_ref, stage, sems):
    pieces = [(wo_hbm.at[layer], wo_ref)]
    for i in range(D_FF // D_MODEL):
        cols = slice(i * D_MODEL, (i + 1) * D_MODEL)
        pieces.append((w1_hbm.at[layer, :, cols], w1_ref.at[:, cols]))
        pieces.append((w2_hbm.at[layer, cols, :], w2_ref.at[cols, :]))
    copies = [pltpu.make_async_copy(src, stage.at[k % STAGE_SLOTS], sems.at[k % STAGE_SLOTS])
              for k, (src, _) in enumerate(pieces)]
    ready = [None]

    def need(n):
        if ready[0] is None:
            for k in range(STAGE_SLOTS):
                copies[k].start()
            ready[0] = 0
        for k in range(ready[0], min(n, len(pieces))):
            copies[k].wait()
            pieces[k][1][...] = stage[k % STAGE_SLOTS].astype(_BF16)
            if k + STAGE_SLOTS < len(pieces):
                copies[k + STAGE_SLOTS].start()
            ready[0] = k + 1

    return need


def _mix_body(x_ref, mod_ref, n2w_ref, tok_ref, vt_ref, spt_ref, gnw_ref, fnw_ref, o_ref, y_scr, wo_ref, w1_ref, w2_ref,
              need, final):
    tm = x_ref.shape[0]
    qe_ref, ki_ref, ry_ref = tok_ref.at[:, TOK_QE], tok_ref.at[:, TOK_KI], tok_ref.at[:, TOK_RY]
    g1 = mod_ref[:, 2 * D_MODEL:3 * D_MODEL]
    sh2 = mod_ref[:, 3 * D_MODEL:4 * D_MODEL]
    scale2 = n2w_ref[...] * (1.0 + mod_ref[:, 4 * D_MODEL:5 * D_MODEL])
    g2 = mod_ref[:, 5 * D_MODEL:6 * D_MODEL]
    row = lax.broadcasted_iota(jnp.int32, (GLA_CHUNK, GLA_CHUNK), 0)
    col = lax.broadcasted_iota(jnp.int32, (GLA_CHUNK, GLA_CHUNK), 1)
    lane = lax.broadcasted_iota(jnp.int32, (GLA_CHUNK, SLAB), 1)
    fwd_lane = lane < GLA_DK
    gnw = gnw_ref[...]
    need(0)

    for c in range(tm // GLA_CHUNK):
        rows = slice(c * GLA_CHUNK, (c + 1) * GLA_CHUNK)
        for h in range(GLA_HEADS):
            lanes = slice(h * SLAB, (h + 1) * SLAB)
            qe = qe_ref[rows, lanes]
            ki = ki_ref[rows, lanes]
            zero = jnp.zeros_like(ki)
            keys = jnp.concatenate([jnp.where(fwd_lane, ki, zero), jnp.where(fwd_lane, zero, ki)], axis=0)
            a2 = lax.dot_general(qe, keys, _NT, preferred_element_type=_F32)
            am = jnp.where(col <= row, a2[:, 0:GLA_CHUNK], 0.0) + jnp.where(col >= row, a2[:, GLA_CHUNK:], 0.0)
            lhs = jnp.concatenate([am.astype(_BF16), qe], axis=1)
            rhs = jnp.concatenate([vt_ref[h * GLA_DV:(h + 1) * GLA_DV, rows], spt_ref[c, h]], axis=1)
            o = lax.dot_general(lhs, rhs, _NT, preferred_element_type=_F32)
            o = o * lax.rsqrt(jnp.mean(o * o, axis=-1, keepdims=True) + EPS) * gnw
            r = ry_ref[rows, h * GLA_DV:(h + 1) * GLA_DV].astype(_F32)
            y_scr[rows, h * GLA_DV:(h + 1) * GLA_DV] = (o * (r * _sigmoid(r))).astype(_BF16)
        need(c + 1)

    need(1)
    mixed = _dot(y_scr[...], wo_ref[0:VAL_COLS, :]) + _dot(ry_ref[:, VAL_COLS:RY_COLS], wo_ref[VAL_COLS:D_MODEL, :])
    x1 = x_ref[...] + g1 * mixed
    xn = x1 * lax.rsqrt(jnp.mean(x1 * x1, axis=-1, keepdims=True) + EPS)
    h2 = (xn * scale2 + sh2).astype(_BF16)
    acc = jnp.zeros((tm, D_MODEL), _F32)
    for f in range(D_FF // FF_CHUNK):
        cols = slice(f * FF_CHUNK, (f + 1) * FF_CHUNK)
        need(1 + 2 * pl.cdiv(cols.stop, D_MODEL))
        a = jnp.maximum(_dot(h2, w1_ref[:, cols]), 0.0)
        acc = acc + _dot((a * a).astype(_BF16), w2_ref[cols, :])
    out = x1 + g2 * acc
    if final:
        out = out * lax.rsqrt(jnp.mean(out * out, axis=-1, keepdims=True) + EPS) * fnw_ref[...]
    o_ref[...] = out


def _mix_kernel(x_ref, mod_ref, n2w_ref, tok_ref, vt_ref, spt_ref, gnw_ref, wo_hbm, w1_hbm, w2_hbm, fnw_ref,
                o_ref, y_scr, wo_ref, w1_ref, w2_ref, stage, sems, *, layer, final, single_step):
    body = functools.partial(_mix_body, x_ref, mod_ref, n2w_ref, tok_ref, vt_ref, spt_ref, gnw_ref, fnw_ref, o_ref, y_scr,
                             wo_ref, w1_ref, w2_ref, final=final)

    def staging_step():
        body(need=_mix_weight_stager(layer, wo_hbm, w1_hbm, w2_hbm, wo_ref, w1_ref, w2_ref, stage, sems))

    if single_step:
        staging_step()
    else:
        first = pl.program_id(0) == 0
        pl.when(first)(staging_step)
        pl.when(jnp.logical_not(first))(functools.partial(body, need=lambda n: None))


def _mix_weight_scratch():
    return [pltpu.VMEM((D_MODEL, D_MODEL), _BF16), pltpu.VMEM((D_MODEL, D_FF), _BF16), pltpu.VMEM((D_FF, D_MODEL), _BF16),
            pltpu.VMEM((STAGE_SLOTS, D_MODEL, D_MODEL), _F32), pltpu.SemaphoreType.DMA((STAGE_SLOTS,))]


def _mix(x2d, mods3, mod_row_of_tile, layer, tok, vt, spt, wts, tm, final):
    tokens = x2d.shape[0]
    cpt = tm // GLA_CHUNK
    of_layer = lambda shape: pl.BlockSpec((None,) + shape, lambda j: (layer,) + (0,) * len(shape),
                                          pipeline_mode=pl.Buffered(1))
    return pl.pallas_call(
        functools.partial(_mix_kernel, layer=layer, final=final, single_step=tokens == tm),
        grid=(tokens // tm,),
        in_specs=[
            pl.BlockSpec((tm, D_MODEL), lambda j: (j, 0)),
            pl.BlockSpec((None, 1, N_MOD * D_MODEL), lambda j: (layer * MOD_ROWS + mod_row_of_tile(j), 0, 0)),
            of_layer((1, D_MODEL)),
            pl.BlockSpec((tm, TOK_COLS), lambda j: (j, 0)),
            pl.BlockSpec((None, VAL_COLS, tm), lambda j: (j, 0, 0)),
            pl.BlockSpec((cpt, GLA_HEADS, GLA_DV, SLAB), lambda j: (j, 0, 0, 0)),
            of_layer((1, GLA_DV)),
            pl.BlockSpec(memory_space=pl.ANY),
            pl.BlockSpec(memory_space=pl.ANY),
            pl.BlockSpec(memory_space=pl.ANY),
            pl.BlockSpec((1, D_MODEL), lambda j: (0, 0), pipeline_mode=pl.Buffered(1)),
        ],
        out_specs=pl.BlockSpec((tm, D_MODEL), lambda j: (j, 0)),
        out_shape=jax.ShapeDtypeStruct((tokens, D_MODEL), _F32),
        scratch_shapes=[pltpu.VMEM((tm, VAL_COLS), _BF16)] + _mix_weight_scratch(),
        compiler_params=pltpu.CompilerParams(vmem_limit_bytes=VMEM_LIMIT_BYTES),
    )(x2d, mods3, wts["n2w"], tok, vt, spt, wts["gnw"], wts["wo"], wts["w1"], wts["w2"], wts["fnw"])


def _stage_in_projection(layer, w_in_t_hbm, win_ref, wvt_ref, stage, sems):
    pieces = []
    for src, dst_ref, dst in ((IN_Z, win_ref, WZ_COLS.start), (IN_QK, win_ref, WQ_COLS.start),
                              (IN_V, wvt_ref, 0), (IN_RC, win_ref, WRC_COLS.start)):
        for off in range(0, src.stop - src.start, IN_STAGE_ROWS):
            pieces.append((src.start + off, dst_ref, dst + off, min(IN_STAGE_ROWS, src.stop - src.start - off)))
    copies = [pltpu.make_async_copy(w_in_t_hbm.at[layer, s0:s0 + n, :], stage.at[k % STAGE_SLOTS, 0:n, :],
                                    sems.at[k % STAGE_SLOTS]) for k, (s0, _, _, n) in enumerate(pieces)]
    for k in range(STAGE_SLOTS):
        copies[k].start()
    win_ref[WZ_COLS.stop:WQ_COLS.start, :] = jnp.zeros((WQ_COLS.start - WZ_COLS.stop, D_MODEL), _BF16)
    for k, (_, dst_ref, d0, n) in enumerate(pieces):
        copies[k].wait()
        dst_ref[d0:d0 + n, :] = stage[k % STAGE_SLOTS, 0:n, :].astype(_BF16)
        if k + STAGE_SLOTS < len(pieces):
            copies[k + STAGE_SLOTS].start()


def _context_kernel(*refs, layer, seg, chunks_per_seq, with_mix):
    n_in = 9
    x_ref, mod_ref, n1w_ref, w_in_t_hbm = refs[0:4]
    if with_mix:
        n2w_ref, gnw_ref, wo_hbm, w1_hbm, w2_hbm, fnw_ref = refs[n_in:n_in + 6]
        sfin_ref, win_ref, wvt_ref, o_ref = refs[n_in + 6:n_in + 10]
        tok_scr, vt_scr, ut_scr, dec_scr, spt_scr, in_stage, in_sems, y_scr = refs[n_in + 10:n_in + 18]
        mix_weight_scr = refs[n_in + 18:]
    else:
        sfin_ref, win_ref, wvt_ref = refs[n_in:n_in + 3]
        tok_scr, vt_scr, ut_scr, dec_scr, spt_scr, in_stage, in_sems = refs[n_in + 3:]

    _stage_in_projection(layer, w_in_t_hbm, win_ref, wvt_ref, in_stage, in_sems)
    _inproj_kernel(x_ref, mod_ref, n1w_ref, win_ref, wvt_ref, *refs[4:n_in], tok_scr, vt_scr, ut_scr, dec_scr, seg=seg)

    sfin_ref[...] = jnp.zeros(sfin_ref.shape, _F32)
    for b in range(ut_scr.shape[0] // chunks_per_seq):
        for head0 in range(0, GLA_HEADS, SCAN_HEADS):
            start = [sfin_ref[b, head0 + s] for s in range(SCAN_HEADS)]
            final = _scan_chunks(ut_scr, dec_scr, start, spt_scr, chunks_per_seq,
                                 chunk0=b * chunks_per_seq, head0=head0)
            for s in range(SCAN_HEADS):
                sfin_ref[b, head0 + s] = final[s]

    if with_mix:
        _mix_kernel(x_ref, mod_ref, n2w_ref, tok_scr, vt_scr.at[0], spt_scr, gnw_ref, wo_hbm, w1_hbm, w2_hbm, fnw_ref,
                    o_ref, y_scr, *mix_weight_scr, layer=layer, final=False, single_step=True)


def _context_layer(xc, mods3, mod_row, layer, wts, ctx_len, with_mix):
    rows = xc.shape[0]
    bsz = rows // ctx_len
    n_chunks = rows // GLA_CHUNK
    assert rows % SUB_ROWS == 0 and SUB_ROWS % ctx_len == 0
    of_layer = lambda shape: pl.BlockSpec((None,) + shape, lambda j: (layer,) + (0,) * len(shape),
                                          pipeline_mode=pl.Buffered(1))
    in_specs = [
        pl.BlockSpec((rows, D_MODEL), lambda j: (0, 0)),
        pl.BlockSpec((None, 1, N_MOD * D_MODEL), lambda j: (layer * MOD_ROWS + mod_row, 0, 0)),
        of_layer((1, D_MODEL)),
        pl.BlockSpec(memory_space=pl.ANY),
        of_layer((2 * GATE_RANK, KEY_COLS)),
        of_layer((1, KEY_COLS)),
        of_layer((3, CONV_WIDTH)),
        of_layer((POOL_WIDTH, POOL_WIDTH)),
        of_layer((1, POOL_WIDTH)),
    ]
    args = [xc, mods3, wts["n1w"], wts["w_in_t"], wts["gw"], wts["gb"], wts["cw"], wts["pw"], wts["ps"]]
    out_specs = [pl.BlockSpec((bsz, GLA_HEADS, GLA_DV, SLAB), lambda j: (0, 0, 0, 0)),
                 pl.BlockSpec((WIN_COLS, D_MODEL), lambda j: (0, 0)),
                 pl.BlockSpec((VAL_COLS, D_MODEL), lambda j: (0, 0))]
    out_shape = [jax.ShapeDtypeStruct((bsz, GLA_HEADS, GLA_DV, SLAB), _F32),
                 jax.ShapeDtypeStruct((WIN_COLS, D_MODEL), _BF16),
                 jax.ShapeDtypeStruct((VAL_COLS, D_MODEL), _BF16)]
    scratch = [
        pltpu.VMEM((rows, TOK_COLS), _BF16),
        pltpu.VMEM((1, VAL_COLS, rows), _BF16),
        pltpu.VMEM((n_chunks, GLA_HEADS, GLA_DV, SLAB), _BF16),
        pltpu.VMEM((n_chunks, 8, KEY_COLS), _F32),
        pltpu.VMEM((n_chunks, GLA_HEADS, GLA_DV, SLAB), _BF16),
        pltpu.VMEM((STAGE_SLOTS, IN_STAGE_ROWS, D_MODEL), _F32),
        pltpu.SemaphoreType.DMA((STAGE_SLOTS,)),
    ]
    if with_mix:
        in_specs += [of_layer((1, D_MODEL)), of_layer((1, GLA_DV))] + [pl.BlockSpec(memory_space=pl.ANY)] * 3
        in_specs += [pl.BlockSpec((1, D_MODEL), lambda j: (0, 0), pipeline_mode=pl.Buffered(1))]
        args += [wts["n2w"], wts["gnw"], wts["wo"], wts["w1"], wts["w2"], wts["fnw"]]
        out_specs.append(pl.BlockSpec((rows, D_MODEL), lambda j: (0, 0)))
        out_shape.append(jax.ShapeDtypeStruct((rows, D_MODEL), _F32))
        scratch += [pltpu.VMEM((rows, VAL_COLS), _BF16)] + _mix_weight_scratch()
    outs = pl.pallas_call(
        functools.partial(_context_kernel, layer=layer, seg=ctx_len, chunks_per_seq=ctx_len // GLA_CHUNK,
                          with_mix=with_mix),
        grid=(1,),
        in_specs=in_specs,
        out_specs=out_specs,
        out_shape=out_shape,
        scratch_shapes=scratch,
        compiler_params=pltpu.CompilerParams(vmem_limit_bytes=VMEM_LIMIT_BYTES),
    )(*args)
    return tuple(outs) if with_mix else tuple(outs) + (None,)


def _prepare_weights(norm1_w, norm2_w, w_in, gate_w, gate_b, gla_norm_w, conv_w, pool_w, pool_scale, w_out,
                     w_mlp1, w_mlp2, final_norm_w):
    depth = w_in.shape[0]
    assert w_in.shape[2] == IN_COLS
    zeros = jnp.zeros((depth, GATE_RANK, GLA_HEADS, GLA_DK), _F32)
    gf = gate_w[:, 0].reshape(depth, GATE_RANK, GLA_HEADS, GLA_DK)
    gb = gate_w[:, 1].reshape(depth, GATE_RANK, GLA_HEADS, GLA_DK)
    gw = jnp.concatenate([jnp.concatenate([gf, zeros], -1), jnp.concatenate([zeros, gb], -1)], 1)
    gbias = jnp.concatenate([gate_b[:, 0].reshape(depth, GLA_HEADS, GLA_DK),
                             gate_b[:, 1].reshape(depth, GLA_HEADS, GLA_DK)], -1)
    n_groups = POOL_WIDTH // POOL_GROUP
    same_group = jnp.eye(n_groups, dtype=_F32)[None, :, None, :, None]
    pw = (pool_w[:, :, :, None, :] * same_group).reshape(depth, POOL_WIDTH, POOL_WIDTH)
    return {
        "n1w": norm1_w.reshape(depth, 1, D_MODEL),
        "n2w": norm2_w.reshape(depth, 1, D_MODEL),
        "w_in_t": jnp.swapaxes(w_in, 1, 2),
        "gw": gw.reshape(depth, 2 * GATE_RANK, KEY_COLS).astype(_BF16),
        "gb": gbias.reshape(depth, 1, KEY_COLS),
        "gnw": gla_norm_w.reshape(depth, 1, GLA_DV),
        "cw": conv_w,
        "pw": pw.astype(_BF16),
        "ps": pool_scale.reshape(depth, 1, POOL_WIDTH),
        "wo": w_out,
        "w1": w_mlp1,
        "w2": w_mlp2,
        "fnw": final_norm_w.reshape(1, D_MODEL),
    }


def kernel(x, c, ctx, c_ctx, w_mod, b_mod, norm1_w, norm2_w, w_in, gla_gate_w, gla_gate_b, gla_norm_w, conv_w, pool_w, pool_scale, w_out, w_mlp1, w_mlp2, final_norm_w):
    bsz, n, _ = x.shape
    ctx_len = ctx.shape[1]
    depth = w_in.shape[0]
    assert bsz + 1 <= MOD_ROWS and n % LATENT_IN_TILE == 0 and SUB_ROWS % ctx_len == 0
    ctx_row = bsz

    cv = jnp.concatenate([c, c_ctx[None, :], jnp.zeros((MOD_ROWS - bsz - 1, D_MODEL), _F32)], axis=0)
    mods3 = _adaln(cv, w_mod, b_mod)
    wts = _prepare_weights(norm1_w, norm2_w, w_in, gla_gate_w, gla_gate_b, gla_norm_w, conv_w, pool_w, pool_scale,
                           w_out, w_mlp1, w_mlp2, final_norm_w)

    xl = x.reshape(bsz * n, D_MODEL)
    xc = ctx.reshape(bsz * ctx_len, D_MODEL)
    lat_in_row = lambda j: j // (n // LATENT_IN_TILE)
    lat_mix_row = lambda j: j // (n // LATENT_MIX_TILE)

    for l in range(depth):
        last = l == depth - 1
        s_ctx, win, wvt, xc = _context_layer(xc, mods3, ctx_row, l, wts, ctx_len, with_mix=not last)

        tok, vt, ut, dec = _inproj(xl, mods3, lat_in_row, l, win, wvt, wts, LATENT_IN_TILE, GRID_W, LATENT_MIX_TILE)
        spt, _ = _scan(ut, dec, s_ctx, n // GLA_CHUNK)
        xl = _mix(xl, mods3, lat_mix_row, l, tok, vt, spt, wts, LATENT_MIX_TILE, last)
    return xl.reshape(bsz, n, D_MODEL)
```

```python
import functools

import jax
import jax.numpy as jnp
from jax import lax
from jax.experimental import pallas as pl
from jax.experimental.pallas import tpu as pltpu

D_MODEL = 1024
GLA_HEADS = 4
GLA_DK = 64
GLA_DV = 128
SLAB = 2 * GLA_DK
KEY_COLS = GLA_HEADS * SLAB
VAL_COLS = GLA_HEADS * GLA_DV
GATE_RANK = 16
GATE_TAU = 16.0
LOG2_E = 1.4426950408889634
GLA_CHUNK = 128
SUB_ROWS = 256
LATENT_IN_TILE = 1024
LATENT_MIX_TILE = 512
CONV_WIDTH = 256
POOL_WIDTH = 256
POOL_GROUP = 64
POOL_HALF_WINDOWS = (1, 2, 4, 8)
D_FF = 4096
FF_CHUNK = 1024
N_MOD = 6
EPS = 1e-6
GRID_W = 64
RC_COLS = VAL_COLS + 3 * CONV_WIDTH + POOL_WIDTH
RY_COLS = VAL_COLS + CONV_WIDTH + POOL_WIDTH
TOK_QE = slice(0, KEY_COLS)
TOK_KI = slice(KEY_COLS, 2 * KEY_COLS)
TOK_RY = slice(2 * KEY_COLS, 2 * KEY_COLS + RY_COLS)
TOK_COLS = TOK_RY.stop
WZ_COLS = slice(0, 2 * GATE_RANK)
WQ_COLS = slice(128, 128 + GLA_HEADS * GLA_DK)
WK_COLS = slice(WQ_COLS.stop, WQ_COLS.stop + GLA_HEADS * GLA_DK)
WRC_COLS = slice(WK_COLS.stop, WK_COLS.stop + RC_COLS)
WIN_COLS = WRC_COLS.stop
IN_QK = slice(0, 2 * GLA_HEADS * GLA_DK)
IN_V = slice(IN_QK.stop, IN_QK.stop + VAL_COLS)
IN_Z = slice(IN_V.stop, IN_V.stop + 2 * GATE_RANK)
IN_RC = slice(IN_Z.stop, IN_Z.stop + RC_COLS)
IN_COLS = IN_RC.stop
IN_STAGE_ROWS = 512
SCAN_UNROLL = 2
SCAN_HEADS = 2
MOD_ROWS = 8
ADALN_COLS = 1536
VMEM_LIMIT_BYTES = 56 * 1024 * 1024
STAGE_SLOTS = 2

_NT = (((1,), (1,)), ((), ()))
_BF16 = jnp.bfloat16
_F32 = jnp.float32


def _dot(a, b):
    return jnp.dot(a, b, preferred_element_type=_F32)


def _sigmoid(x):
    return 1.0 / (1.0 + jnp.exp(-x))


def _adaln_kernel(cv_ref, w_ref, b_ref, o_ref):
    cv = cv_ref[...]
    s = (cv * _sigmoid(cv)).astype(_BF16)
    mod = _dot(s, w_ref[...].astype(_BF16)) + b_ref[pl.ds(pl.program_id(0), 1), :]
    for r in range(MOD_ROWS):
        o_ref[r] = mod[r:r + 1, :]


def _adaln(cv, w_mod, b_mod):
    depth = w_mod.shape[0]
    cols = w_mod.shape[2]
    return pl.pallas_call(
        _adaln_kernel,
        grid=(depth, cols // ADALN_COLS),
        in_specs=[
            pl.BlockSpec((MOD_ROWS, D_MODEL), lambda l, j: (0, 0)),
            pl.BlockSpec((None, D_MODEL, ADALN_COLS), lambda l, j: (l, 0, j)),
            pl.BlockSpec((depth, ADALN_COLS), lambda l, j: (0, j)),
        ],
        out_specs=pl.BlockSpec((MOD_ROWS, 1, ADALN_COLS), lambda l, j: (l, 0, j)),
        out_shape=jax.ShapeDtypeStruct((depth * MOD_ROWS, 1, cols), _F32),
        compiler_params=pltpu.CompilerParams(vmem_limit_bytes=VMEM_LIMIT_BYTES),
    )(cv, w_mod, b_mod)


def _per_direction(t):
    lane = lax.broadcasted_iota(jnp.int32, (1, SLAB), 1)
    first = lane < GLA_DK
    slabs = []
    for p in range(GLA_HEADS // 2):
        pair = t[:, p * SLAB:(p + 1) * SLAB]
        swapped = pltpu.roll(pair, GLA_DK, 1)
        slabs += [jnp.where(first, pair, swapped), jnp.where(first, swapped, pair)]
    return jnp.concatenate(slabs, axis=1)


def _conv_and_pool(rc, cw_ref, pw_ref, ps_ref, seg):
    rows = rc.shape[0]
    pos = lax.broadcasted_iota(jnp.int32, (rows, 1), 0) % seg
    c0 = 0
    cb = rc[:, c0:c0 + CONV_WIDTH]
    zc = rc[:, c0 + CONV_WIDTH:c0 + 2 * CONV_WIDTH] * rc[:, c0 + 2 * CONV_WIDTH:c0 + 3 * CONV_WIDTH]
    z_prev = jnp.where(pos >= 1, pltpu.roll(zc, 1, 0), 0.0)
    z_next = jnp.where(pos <= seg - 2, pltpu.roll(zc, rows - 1, 0), 0.0)
    yc = cb * (cw_ref[0:1, :] * z_prev + cw_ref[1:2, :] * zc + cw_ref[2:3, :] * z_next)

    p0 = 3 * CONV_WIDTH
    lane128 = lax.broadcasted_iota(jnp.int32, (1, 2 * POOL_GROUP), 1)
    pooled = []
    for blk in range(POOL_WIDTH // (2 * POOL_GROUP)):
        u = rc[:, p0 + blk * 2 * POOL_GROUP:p0 + (blk + 1) * 2 * POOL_GROUP]
        h_lo, h_hi = POOL_HALF_WINDOWS[2 * blk], POOL_HALF_WINDOWS[2 * blk + 1]
        half = jnp.where(lane128 < POOL_GROUP, h_lo, h_hi)
        acc = jnp.zeros_like(u)
        for d in range(-h_hi, h_hi):
            shifted = u if d == 0 else pltpu.roll(u, (-d) % rows, 0)
            valid = (pos + d >= 0) & (pos + d < seg) & (d >= -half) & (d < half)
            acc = acc + jnp.where(valid, shifted, 0.0)
        cnt = (jnp.minimum(pos + half, seg) - jnp.maximum(pos - half, 0)).astype(_F32)
        pooled.append(acc / cnt - u)
    pool_in = jnp.concatenate(pooled, axis=1).astype(_BF16)
    return yc, _dot(pool_in, pw_ref[...]) * ps_ref[...]


def _inproj_kernel(x_ref, mod_ref, n1w_ref, win_ref, wvt_ref, gw_ref, gb_ref, cw_ref, pw_ref, ps_ref,
                   tok_ref, vt_ref, ut_ref, dec_ref, *, seg):
    qe_ref, ki_ref, ry_ref = tok_ref.at[:, TOK_QE], tok_ref.at[:, TOK_KI], tok_ref.at[:, TOK_RY]
    vt_tile = vt_ref.shape[2]
    tm = x_ref.shape[0]
    sh1 = mod_ref[:, 0:D_MODEL]
    scale1 = n1w_ref[...] * (1.0 + mod_ref[:, D_MODEL:2 * D_MODEL])
    row = lax.broadcasted_iota(jnp.int32, (GLA_CHUNK, GLA_CHUNK), 0)
    col = lax.broadcasted_iota(jnp.int32, (GLA_CHUNK, GLA_CHUNK), 1)
    scan_op = jnp.concatenate([(col <= row).astype(_BF16), (col >= row).astype(_BF16)], axis=1)
    lane = lax.broadcasted_iota(jnp.int32, (1, KEY_COLS), 1)
    is_fwd = (lane % SLAB) < GLA_DK
    mid = GLA_CHUNK // 2

    n_sub = tm // SUB_ROWS
    cps = SUB_ROWS // GLA_CHUNK
    sub_rows = [slice(s * SUB_ROWS, (s + 1) * SUB_ROWS) for s in range(n_sub)]
    hbs, projs, gates, q2s, k2s = [], [], [], [], []
    for s in range(n_sub):
        x = x_ref[sub_rows[s], :]
        xn = x * lax.rsqrt(jnp.mean(x * x, axis=-1, keepdims=True) + EPS)
        hbs.append((xn * scale1 + sh1).astype(_BF16))
    for s in range(n_sub):
        proj = lax.dot_general(hbs[s], win_ref[...], _NT, preferred_element_type=_F32)
        projs.append(proj)
        pre = _dot(proj[:, WZ_COLS].astype(_BF16), gw_ref[...]) + gb_ref[...]
        g = (jnp.minimum(pre, 0.0) * LOG2_E - jnp.log2(1.0 + jnp.exp2(jnp.abs(pre) * (-LOG2_E)))) * (1.0 / GATE_TAU)
        gates.append(g.astype(_BF16))
        q2s.append(_per_direction(proj[:, WQ_COLS] * (GLA_DK ** -0.5)))
        k2s.append(_per_direction(proj[:, WK_COLS]))

    decays = []
    for s in range(n_sub):
        for cl in range(cps):
            g = gates[s][cl * GLA_CHUNK:(cl + 1) * GLA_CHUNK]
            zero = jnp.zeros_like(g)
            by_direction = jnp.concatenate([jnp.where(is_fwd, g, zero), jnp.where(is_fwd, zero, g)], axis=0)
            decays.append(_dot(scan_op, by_direction))

    vtbs = []
    for s in range(n_sub):
        vtb = lax.dot_general(wvt_ref[...], hbs[s], _NT, preferred_element_type=_F32).astype(_BF16)
        first = s * SUB_ROWS
        vt_ref[first // vt_tile, :, first % vt_tile:first % vt_tile + SUB_ROWS] = vtb
        vtbs.append(vtb)
        rc = projs[s][:, WRC_COLS]
        yc, yp = _conv_and_pool(rc[:, VAL_COLS:RC_COLS], cw_ref, pw_ref, ps_ref, seg)
        ry_ref[sub_rows[s], 0:VAL_COLS] = rc[:, 0:VAL_COLS].astype(_BF16)
        ry_ref[sub_rows[s], VAL_COLS:VAL_COLS + CONV_WIDTH] = yc.astype(_BF16)
        ry_ref[sub_rows[s], VAL_COLS + CONV_WIDTH:RY_COLS] = yp.astype(_BF16)

    for s in range(n_sub):
        for cl in range(cps):
            c = s * cps + cl
            lrows = slice(cl * GLA_CHUNK, (cl + 1) * GLA_CHUNK)
            rows = slice(c * GLA_CHUNK, (c + 1) * GLA_CHUNK)
            e = decays[c]
            total = jnp.where(is_fwd, e[GLA_CHUNK - 1:GLA_CHUNK, :], e[0:1, :])
            mvec = jnp.where(is_fwd, e[mid - 1:mid, :], e[mid:mid + 1, :])
            bm = e - mvec
            qe_ref[rows, :] = (q2s[s][lrows] * jnp.exp2(bm)).astype(_BF16)
            k_mid = k2s[s][lrows] * jnp.exp2(-bm)
            ki_ref[rows, :] = k_mid.astype(_BF16)
            kd = (k_mid * jnp.exp2(total - mvec)).astype(_BF16)
            dec_ref[c] = jnp.concatenate([total, mvec, jnp.zeros((6, KEY_COLS), _F32)], axis=0)
            for h in range(GLA_HEADS):
                vt_h = vtbs[s][h * GLA_DV:(h + 1) * GLA_DV, lrows]
                ut_ref[c, h] = _dot(vt_h, kd[:, h * SLAB:(h + 1) * SLAB]).astype(ut_ref.dtype)


def _inproj(x2d, mods3, mod_row_of_tile, layer, win, wvt, wts, tm, seg, vt_tile):
    tokens = x2d.shape[0]
    n_tiles = tokens // tm
    n_chunks = tokens // GLA_CHUNK
    cpt = tm // GLA_CHUNK
    assert SUB_ROWS % seg == 0
    assert tm % vt_tile == 0 and vt_tile % SUB_ROWS == 0
    of_layer = lambda shape: pl.BlockSpec((None,) + shape, lambda j: (layer,) + (0,) * len(shape))
    return pl.pallas_call(
        functools.partial(_inproj_kernel, seg=seg),
        grid=(n_tiles,),
        in_specs=[
            pl.BlockSpec((tm, D_MODEL), lambda j: (j, 0)),
            pl.BlockSpec((None, 1, N_MOD * D_MODEL), lambda j: (layer * MOD_ROWS + mod_row_of_tile(j), 0, 0)),
            of_layer((1, D_MODEL)),
            pl.BlockSpec((WIN_COLS, D_MODEL), lambda j: (0, 0)),
            pl.BlockSpec((VAL_COLS, D_MODEL), lambda j: (0, 0)),
            of_layer((2 * GATE_RANK, KEY_COLS)),
            of_layer((1, KEY_COLS)),
            of_layer((3, CONV_WIDTH)),
            of_layer((POOL_WIDTH, POOL_WIDTH)),
            of_layer((1, POOL_WIDTH)),
        ],
        out_specs=[
            pl.BlockSpec((tm, TOK_COLS), lambda j: (j, 0)),
            pl.BlockSpec((tm // vt_tile, VAL_COLS, vt_tile), lambda j: (j, 0, 0)),
            pl.BlockSpec((cpt, GLA_HEADS, GLA_DV, SLAB), lambda j: (j, 0, 0, 0)),
            pl.BlockSpec((cpt, 8, KEY_COLS), lambda j: (j, 0, 0)),
        ],
        out_shape=[
            jax.ShapeDtypeStruct((tokens, TOK_COLS), _BF16),
            jax.ShapeDtypeStruct((tokens // vt_tile, VAL_COLS, vt_tile), _BF16),
            jax.ShapeDtypeStruct((n_chunks, GLA_HEADS, GLA_DV, SLAB), _BF16),
            jax.ShapeDtypeStruct((n_chunks, 8, KEY_COLS), _F32),
        ],
        compiler_params=pltpu.CompilerParams(vmem_limit_bytes=VMEM_LIMIT_BYTES),
    )(x2d, mods3, wts["n1w"], win, wvt, wts["gw"], wts["gb"], wts["cw"], wts["pw"], wts["ps"])


def _scan_chunks(ut_ref, dec_ref, states, spt_ref, nc, chunk0=0, head0=0):
    lane = lax.broadcasted_iota(jnp.int32, (1, SLAB), 1)
    is_fwd = lane < GLA_DK

    def step(t, states):
        i = chunk0 + t
        j = chunk0 + nc - 1 - t
        new_states = []
        for s in range(SCAN_HEADS):
            h = head0 + s
            lanes = slice(h * SLAB, (h + 1) * SLAB)
            log_decay = jnp.where(is_fwd, dec_ref[i, 0:1, lanes], dec_ref[j, 0:1, lanes])
            log_mid = jnp.where(is_fwd, dec_ref[i, 1:2, lanes], dec_ref[j, 1:2, lanes])
            entering = (states[s] * jnp.exp2(log_mid)).astype(_BF16)
            spt_ref[i, h, :, 0:GLA_DK] = entering[:, 0:GLA_DK]
            spt_ref[j, h, :, GLA_DK:SLAB] = entering[:, GLA_DK:SLAB]
            inc = jnp.where(is_fwd, ut_ref[i, h], ut_ref[j, h]).astype(_F32)
            new_states.append(states[s] * jnp.exp2(log_decay) + inc)
        return tuple(new_states)

    return lax.fori_loop(0, nc, step, tuple(states), unroll=SCAN_UNROLL)


def _scan_kernel(ut_ref, dec_ref, s0_ref, spt_ref, sfin_ref):
    final = _scan_chunks(ut_ref, dec_ref, [s0_ref[h] for h in range(SCAN_HEADS)], spt_ref, ut_ref.shape[0])
    for h in range(SCAN_HEADS):
        sfin_ref[h] = final[h]


def _scan(ut, dec, s0, chunks_per_seq):
    n_chunks = ut.shape[0]
    bsz = n_chunks // chunks_per_seq
    return pl.pallas_call(
        _scan_kernel,
        grid=(bsz, GLA_HEADS // SCAN_HEADS),
        in_specs=[
            pl.BlockSpec((chunks_per_seq, SCAN_HEADS, GLA_DV, SLAB), lambda b, g: (b, g, 0, 0)),
            pl.BlockSpec((chunks_per_seq, 8, SCAN_HEADS * SLAB), lambda b, g: (b, 0, g)),
            pl.BlockSpec((None, SCAN_HEADS, GLA_DV, SLAB), lambda b, g: (b, g, 0, 0)),
        ],
        out_specs=[
            pl.BlockSpec((chunks_per_seq, SCAN_HEADS, GLA_DV, SLAB), lambda b, g: (b, g, 0, 0)),
            pl.BlockSpec((None, SCAN_HEADS, GLA_DV, SLAB), lambda b, g: (b, g, 0, 0)),
        ],
        out_shape=[
            jax.ShapeDtypeStruct((n_chunks, GLA_HEADS, GLA_DV, SLAB), _BF16),
            jax.ShapeDtypeStruct((bsz, GLA_HEADS, GLA_DV, SLAB), _F32),
        ],
        compiler_params=pltpu.CompilerParams(vmem_limit_bytes=VMEM_LIMIT_BYTES),
    )(ut, dec, s0)


def _mix_weight_stager(layer, wo_hbm, w1_hbm, w2_hbm, wo_ref, w1_ref, w2_ref, stage, sems):
    pieces = [(wo_hbm.at[layer], wo_ref)]
    for i in range(D_FF // D_MODEL):
        cols = slice(i * D_MODEL, (i + 1) * D_MODEL)
        pieces.append((w1_hbm.at[layer, :, cols], w1_ref.at[:, cols]))
        pieces.append((w2_hbm.at[layer, cols, :], w2_ref.at[cols, :]))
    copies = [pltpu.make_async_copy(src, stage.at[k % STAGE_SLOTS], sems.at[k % STAGE_SLOTS])
              for k, (src, _) in enumerate(pieces)]
    ready = [None]

    def need(n):
        if ready[0] is None:
            for k in range(STAGE_SLOTS):
                copies[k].start()
            ready[0] = 0
        for k in range(ready[0], min(n, len(pieces))):
            copies[k].wait()
            pieces[k][1][...] = stage[k % STAGE_SLOTS].astype(_BF16)
            if k + STAGE_SLOTS < len(pieces):
                copies[k + STAGE_SLOTS].start()
            ready[0] = k + 1

    return need


def _mix_body(x_ref, mod_ref, n2w_ref, tok_ref, vt_ref, spt_ref, gnw_ref, fnw_ref, o_ref, y_scr, wo_ref, w1_ref, w2_ref,
              need, final):
    tm = x_ref.shape[0]
    qe_ref, ki_ref, ry_ref = tok_ref.at[:, TOK_QE], tok_ref.at[:, TOK_KI], tok_ref.at[:, TOK_RY]
    g1 = mod_ref[:, 2 * D_MODEL:3 * D_MODEL]
    sh2 = mod_ref[:, 3 * D_MODEL:4 * D_MODEL]
    scale2 = n2w_ref[...] * (1.0 + mod_ref[:, 4 * D_MODEL:5 * D_MODEL])
    g2 = mod_ref[:, 5 * D_MODEL:6 * D_MODEL]
    row = lax.broadcasted_iota(jnp.int32, (GLA_CHUNK, GLA_CHUNK), 0)
    col = lax.broadcasted_iota(jnp.int32, (GLA_CHUNK, GLA_CHUNK), 1)
    lane = lax.broadcasted_iota(jnp.int32, (GLA_CHUNK, SLAB), 1)
    fwd_lane = lane < GLA_DK
    gnw = gnw_ref[...]
    need(0)

    for c in range(tm // GLA_CHUNK):
        rows = slice(c * GLA_CHUNK, (c + 1) * GLA_CHUNK)
        for h in range(GLA_HEADS):
            lanes = slice(h * SLAB, (h + 1) * SLAB)
            qe = qe_ref[rows, lanes]
            ki = ki_ref[rows, lanes]
            zero = jnp.zeros_like(ki)
            keys = jnp.concatenate([jnp.where(fwd_lane, ki, zero), jnp.where(fwd_lane, zero, ki)], axis=0)
            a2 = lax.dot_general(qe, keys, _NT, preferred_element_type=_F32)
            am = jnp.where(col <= row, a2[:, 0:GLA_CHUNK], 0.0) + jnp.where(col >= row, a2[:, GLA_CHUNK:], 0.0)
            lhs = jnp.concatenate([am.astype(_BF16), qe], axis=1)
            rhs = jnp.concatenate([vt_ref[h * GLA_DV:(h + 1) * GLA_DV, rows], spt_ref[c, h]], axis=1)
            o = lax.dot_general(lhs, rhs, _NT, preferred_element_type=_F32)
            o = o * lax.rsqrt(jnp.mean(o * o, axis=-1, keepdims=True) + EPS) * gnw
            r = ry_ref[rows, h * GLA_DV:(h + 1) * GLA_DV].astype(_F32)
            y_scr[rows, h * GLA_DV:(h + 1) * GLA_DV] = (o * (r * _sigmoid(r))).astype(_BF16)
        need(c + 1)

    need(1)
    mixed = _dot(y_scr[...], wo_ref[0:VAL_COLS, :]) + _dot(ry_ref[:, VAL_COLS:RY_COLS], wo_ref[VAL_COLS:D_MODEL, :])
    x1 = x_ref[...] + g1 * mixed
    xn = x1 * lax.rsqrt(jnp.mean(x1 * x1, axis=-1, keepdims=True) + EPS)
    h2 = (xn * scale2 + sh2).astype(_BF16)
    acc = jnp.zeros((tm, D_MODEL), _F32)
    for f in range(D_FF // FF_CHUNK):
        cols = slice(f * FF_CHUNK, (f + 1) * FF_CHUNK)
        need(1 + 2 * pl.cdiv(cols.stop, D_MODEL))
        a = jnp.maximum(_dot(h2, w1_ref[:, cols]), 0.0)
        acc = acc + _dot((a * a).astype(_BF16), w2_ref[cols, :])
    out = x1 + g2 * acc
    if final:
        out = out * lax.rsqrt(jnp.mean(out * out, axis=-1, keepdims=True) + EPS) * fnw_ref[...]
    o_ref[...] = out


def _mix_kernel(x_ref, mod_ref, n2w_ref, tok_ref, vt_ref, spt_ref, gnw_ref, wo_hbm, w1_hbm, w2_hbm, fnw_ref,
                o_ref, y_scr, wo_ref, w1_ref, w2_ref, stage, sems, *, layer, final):
    body = functools.partial(_mix_body, x_ref, mod_ref, n2w_ref, tok_ref, vt_ref, spt_ref, gnw_ref, fnw_ref, o_ref, y_scr,
                             wo_ref, w1_ref, w2_ref, final=final)

    def staging_step():
        body(need=_mix_weight_stager(layer, wo_hbm, w1_hbm, w2_hbm, wo_ref, w1_ref, w2_ref, stage, sems))

    first = pl.program_id(0) == 0
    pl.when(first)(staging_step)
    pl.when(jnp.logical_not(first))(functools.partial(body, need=lambda n: None))


def _mix_weight_scratch():
    return [pltpu.VMEM((D_MODEL, D_MODEL), _BF16), pltpu.VMEM((D_MODEL, D_FF), _BF16), pltpu.VMEM((D_FF, D_MODEL), _BF16),
            pltpu.VMEM((STAGE_SLOTS, D_MODEL, D_MODEL), _F32), pltpu.SemaphoreType.DMA((STAGE_SLOTS,))]


def _mix(x2d, mods3, mod_row_of_tile, layer, tok, vt, spt, wts, tm, final):
    tokens = x2d.shape[0]
    cpt = tm // GLA_CHUNK
    of_layer = lambda shape: pl.BlockSpec((None,) + shape, lambda j: (layer,) + (0,) * len(shape),
                                          pipeline_mode=pl.Buffered(1))
    return pl.pallas_call(
        functools.partial(_mix_kernel, layer=layer, final=final),
        grid=(tokens // tm,),
        in_specs=[
            pl.BlockSpec((tm, D_MODEL), lambda j: (j, 0)),
            pl.BlockSpec((None, 1, N_MOD * D_MODEL), lambda j: (layer * MOD_ROWS + mod_row_of_tile(j), 0, 0)),
            of_layer((1, D_MODEL)),
            pl.BlockSpec((tm, TOK_COLS), lambda j: (j, 0)),
            pl.BlockSpec((None, VAL_COLS, tm), lambda j: (j, 0, 0)),
            pl.BlockSpec((cpt, GLA_HEADS, GLA_DV, SLAB), lambda j: (j, 0, 0, 0)),
            of_layer((1, GLA_DV)),
            pl.BlockSpec(memory_space=pl.ANY),
            pl.BlockSpec(memory_space=pl.ANY),
            pl.BlockSpec(memory_space=pl.ANY),
            pl.BlockSpec((1, D_MODEL), lambda j: (0, 0), pipeline_mode=pl.Buffered(1)),
        ],
        out_specs=pl.BlockSpec((tm, D_MODEL), lambda j: (j, 0)),
        out_shape=jax.ShapeDtypeStruct((tokens, D_MODEL), _F32),
        scratch_shapes=[pltpu.VMEM((tm, VAL_COLS), _BF16)] + _mix_weight_scratch(),
        compiler_params=pltpu.CompilerParams(vmem_limit_bytes=VMEM_LIMIT_BYTES),
    )(x2d, mods3, wts["n2w"], tok, vt, spt, wts["gnw"], wts["wo"], wts["w1"], wts["w2"], wts["fnw"])


def _stage_in_projection(layer, w_in_t_hbm, win_ref, wvt_ref, stage, sems):
    pieces = []
    for src, dst_ref, dst in ((IN_Z, win_ref, WZ_COLS.start), (IN_QK, win_ref, WQ_COLS.start),
                              (IN_V, wvt_ref, 0), (IN_RC, win_ref, WRC_COLS.start)):
        for off in range(0, src.stop - src.start, IN_STAGE_ROWS):
            pieces.append((src.start + off, dst_ref, dst + off, min(IN_STAGE_ROWS, src.stop - src.start - off)))
    copies = [pltpu.make_async_copy(w_in_t_hbm.at[layer, s0:s0 + n, :], stage.at[k % STAGE_SLOTS, 0:n, :],
                                    sems.at[k % STAGE_SLOTS]) for k, (s0, _, _, n) in enumerate(pieces)]
    for k in range(STAGE_SLOTS):
        copies[k].start()
    win_ref[WZ_COLS.stop:WQ_COLS.start, :] = jnp.zeros((WQ_COLS.start - WZ_COLS.stop, D_MODEL), _BF16)
    for k, (_, dst_ref, d0, n) in enumerate(pieces):
        copies[k].wait()
        dst_ref[d0:d0 + n, :] = stage[k % STAGE_SLOTS, 0:n, :].astype(_BF16)
        if k + STAGE_SLOTS < len(pieces):
            copies[k + STAGE_SLOTS].start()


def _context_kernel(*refs, layer, seg, chunks_per_seq, with_mix):
    n_in = 9
    x_ref, mod_ref, n1w_ref, w_in_t_hbm = refs[0:4]
    if with_mix:
        n2w_ref, gnw_ref, wo_hbm, w1_hbm, w2_hbm, fnw_ref = refs[n_in:n_in + 6]
        sfin_ref, win_ref, wvt_ref, o_ref = refs[n_in + 6:n_in + 10]
        tok_scr, vt_scr, ut_scr, dec_scr, spt_scr, in_stage, in_sems, y_scr = refs[n_in + 10:n_in + 18]
        mix_weight_scr = refs[n_in + 18:]
    else:
        sfin_ref, win_ref, wvt_ref = refs[n_in:n_in + 3]
        tok_scr, vt_scr, ut_scr, dec_scr, spt_scr, in_stage, in_sems = refs[n_in + 3:]

    need = lambda n: None
    if with_mix:
        wo_ref, w1_ref, w2_ref, mix_stage, mix_sems = mix_weight_scr
        need = _mix_weight_stager(layer, wo_hbm, w1_hbm, w2_hbm, wo_ref, w1_ref, w2_ref, mix_stage, mix_sems)

    _stage_in_projection(layer, w_in_t_hbm, win_ref, wvt_ref, in_stage, in_sems)
    need(0)
    _inproj_kernel(x_ref, mod_ref, n1w_ref, win_ref, wvt_ref, *refs[4:n_in], tok_scr, vt_scr, ut_scr, dec_scr, seg=seg)
    need(STAGE_SLOTS)

    sfin_ref[...] = jnp.zeros(sfin_ref.shape, _F32)
    for b in range(ut_scr.shape[0] // chunks_per_seq):
        for head0 in range(0, GLA_HEADS, SCAN_HEADS):
            start = [sfin_ref[b, head0 + s] for s in range(SCAN_HEADS)]
            final = _scan_chunks(ut_scr, dec_scr, start, spt_scr, chunks_per_seq,
                                 chunk0=b * chunks_per_seq, head0=head0)
            for s in range(SCAN_HEADS):
                sfin_ref[b, head0 + s] = final[s]

    if with_mix:
        need(2 * STAGE_SLOTS)
        _mix_body(x_ref, mod_ref, n2w_ref, tok_scr, vt_scr.at[0], spt_scr, gnw_ref, fnw_ref, o_ref, y_scr,
                  wo_ref, w1_ref, w2_ref, need=need, final=False)


def _context_layer(xc, mods3, mod_row, layer, wts, ctx_len, with_mix):
    rows = xc.shape[0]
    bsz = rows // ctx_len
    n_chunks = rows // GLA_CHUNK
    assert rows % SUB_ROWS == 0 and SUB_ROWS % ctx_len == 0
    of_layer = lambda shape: pl.BlockSpec((None,) + shape, lambda j: (layer,) + (0,) * len(shape),
                                          pipeline_mode=pl.Buffered(1))
    in_specs = [
        pl.BlockSpec((rows, D_MODEL), lambda j: (0, 0)),
        pl.BlockSpec((None, 1, N_MOD * D_MODEL), lambda j: (layer * MOD_ROWS + mod_row, 0, 0)),
        of_layer((1, D_MODEL)),
        pl.BlockSpec(memory_space=pl.ANY),
        of_layer((2 * GATE_RANK, KEY_COLS)),
        of_layer((1, KEY_COLS)),
        of_layer((3, CONV_WIDTH)),
        of_layer((POOL_WIDTH, POOL_WIDTH)),
        of_layer((1, POOL_WIDTH)),
    ]
    args = [xc, mods3, wts["n1w"], wts["w_in_t"], wts["gw"], wts["gb"], wts["cw"], wts["pw"], wts["ps"]]
    out_specs = [pl.BlockSpec((bsz, GLA_HEADS, GLA_DV, SLAB), lambda j: (0, 0, 0, 0)),
                 pl.BlockSpec((WIN_COLS, D_MODEL), lambda j: (0, 0)),
                 pl.BlockSpec((VAL_COLS, D_MODEL), lambda j: (0, 0))]
    out_shape = [jax.ShapeDtypeStruct((bsz, GLA_HEADS, GLA_DV, SLAB), _F32),
                 jax.ShapeDtypeStruct((WIN_COLS, D_MODEL), _BF16),
                 jax.ShapeDtypeStruct((VAL_COLS, D_MODEL), _BF16)]
    scratch = [
        pltpu.VMEM((rows, TOK_COLS), _BF16),
        pltpu.VMEM((1, VAL_COLS, rows), _BF16),
        pltpu.VMEM((n_chunks, GLA_HEADS, GLA_DV, SLAB), _BF16),
        pltpu.VMEM((n_chunks, 8, KEY_COLS), _F32),
        pltpu.VMEM((n_chunks, GLA_HEADS, GLA_DV, SLAB), _BF16),
        pltpu.VMEM((STAGE_SLOTS, IN_STAGE_ROWS, D_MODEL), _F32),
        pltpu.SemaphoreType.DMA((STAGE_SLOTS,)),
    ]
    if with_mix:
        in_specs += [of_layer((1, D_MODEL)), of_layer((1, GLA_DV))] + [pl.BlockSpec(memory_space=pl.ANY)] * 3
        in_specs += [pl.BlockSpec((1, D_MODEL), lambda j: (0, 0), pipeline_mode=pl.Buffered(1))]
        args += [wts["n2w"], wts["gnw"], wts["wo"], wts["w1"], wts["w2"], wts["fnw"]]
        out_specs.append(pl.BlockSpec((rows, D_MODEL), lambda j: (0, 0)))
        out_shape.append(jax.ShapeDtypeStruct((rows, D_MODEL), _F32))
        scratch += [pltpu.VMEM((rows, VAL_COLS), _BF16)] + _mix_weight_scratch()
    outs = pl.pallas_call(
        functools.partial(_context_kernel, layer=layer, seg=ctx_len, chunks_per_seq=ctx_len // GLA_CHUNK,
                          with_mix=with_mix),
        grid=(1,),
        in_specs=in_specs,
        out_specs=out_specs,
        out_shape=out_shape,
        scratch_shapes=scratch,
        compiler_params=pltpu.CompilerParams(vmem_limit_bytes=VMEM_LIMIT_BYTES),
    )(*args)
    return tuple(outs) if with_mix else tuple(outs) + (None,)


def _prepare_weights(norm1_w, norm2_w, w_in, gate_w, gate_b, gla_norm_w, conv_w, pool_w, pool_scale, w_out,
                     w_mlp1, w_mlp2, final_norm_w):
    depth = w_in.shape[0]
    assert w_in.shape[2] == IN_COLS
    zeros = jnp.zeros((depth, GATE_RANK, GLA_HEADS, GLA_DK), _F32)
    gf = gate_w[:, 0].reshape(depth, GATE_RANK, GLA_HEADS, GLA_DK)
    gb = gate_w[:, 1].reshape(depth, GATE_RANK, GLA_HEADS, GLA_DK)
    gw = jnp.concatenate([jnp.concatenate([gf, zeros], -1), jnp.concatenate([zeros, gb], -1)], 1)
    gbias = jnp.concatenate([gate_b[:, 0].reshape(depth, GLA_HEADS, GLA_DK),
                             gate_b[:, 1].reshape(depth, GLA_HEADS, GLA_DK)], -1)
    n_groups = POOL_WIDTH // POOL_GROUP
    same_group = jnp.eye(n_groups, dtype=_F32)[None, :, None, :, None]
    pw = (pool_w[:, :, :, None, :] * same_group).reshape(depth, POOL_WIDTH, POOL_WIDTH)
    return {
        "n1w": norm1_w.reshape(depth, 1, D_MODEL),
        "n2w": norm2_w.reshape(depth, 1, D_MODEL),
        "w_in_t": jnp.swapaxes(w_in, 1, 2),
        "gw": gw.reshape(depth, 2 * GATE_RANK, KEY_COLS).astype(_BF16),
        "gb": gbias.reshape(depth, 1, KEY_COLS),
        "gnw": gla_norm_w.reshape(depth, 1, GLA_DV),
        "cw": conv_w,
        "pw": pw.astype(_BF16),
        "ps": pool_scale.reshape(depth, 1, POOL_WIDTH),
        "wo": w_out,
        "w1": w_mlp1,
        "w2": w_mlp2,
        "fnw": final_norm_w.reshape(1, D_MODEL),
    }


def kernel(x, c, ctx, c_ctx, w_mod, b_mod, norm1_w, norm2_w, w_in, gla_gate_w, gla_gate_b, gla_norm_w, conv_w, pool_w, pool_scale, w_out, w_mlp1, w_mlp2, final_norm_w):
    bsz, n, _ = x.shape
    ctx_len = ctx.shape[1]
    depth = w_in.shape[0]
    assert bsz + 1 <= MOD_ROWS and n % LATENT_IN_TILE == 0 and SUB_ROWS % ctx_len == 0
    ctx_row = bsz

    cv = jnp.concatenate([c, c_ctx[None, :], jnp.zeros((MOD_ROWS - bsz - 1, D_MODEL), _F32)], axis=0)
    mods3 = _adaln(cv, w_mod, b_mod)
    wts = _prepare_weights(norm1_w, norm2_w, w_in, gla_gate_w, gla_gate_b, gla_norm_w, conv_w, pool_w, pool_scale,
                           w_out, w_mlp1, w_mlp2, final_norm_w)

    xl = x.reshape(bsz * n, D_MODEL)
    xc = ctx.reshape(bsz * ctx_len, D_MODEL)
    lat_in_row = lambda j: j // (n // LATENT_IN_TILE)
    lat_mix_row = lambda j: j // (n // LATENT_MIX_TILE)

    for l in range(depth):
        last = l == depth - 1
        s_ctx, win, wvt, xc = _context_layer(xc, mods3, ctx_row, l, wts, ctx_len, with_mix=not last)

        tok, vt, ut, dec = _inproj(xl, mods3, lat_in_row, l, win, wvt, wts, LATENT_IN_TILE, GRID_W, LATENT_MIX_TILE)
        spt, _ = _scan(ut, dec, s_ctx, n // GLA_CHUNK)
        xl = _mix(xl, mods3, lat_mix_row, l, tok, vt, spt, wts, LATENT_MIX_TILE, last)
    return xl.reshape(bsz, n, D_MODEL)
```

```python
import functools

import jax
import jax.numpy as jnp
from jax import lax
from jax.experimental import pallas as pl
from jax.experimental.pallas import tpu as pltpu

D_MODEL = 1024
GLA_HEADS = 4
GLA_DK = 64
GLA_DV = 128
SLAB = 2 * GLA_DK
KEY_COLS = GLA_HEADS * SLAB
VAL_COLS = GLA_HEADS * GLA_DV
GATE_RANK = 16
GATE_TAU = 16.0
LOG2_E = 1.4426950408889634
GLA_CHUNK = 128
SUB_ROWS = 256
LATENT_IN_TILE = 1024
LATENT_MIX_TILE = 512
CONV_WIDTH = 256
POOL_WIDTH = 256
POOL_GROUP = 64
POOL_HALF_WINDOWS = (1, 2, 4, 8)
assert all(h & (h - 1) == 0 for h in POOL_HALF_WINDOWS)
D_FF = 4096
FF_CHUNK = 1024
N_MOD = 6
EPS = 1e-6
GRID_W = 64
RC_COLS = VAL_COLS + 3 * CONV_WIDTH + POOL_WIDTH
RY_COLS = VAL_COLS + CONV_WIDTH + POOL_WIDTH
TOK_QE = slice(0, KEY_COLS)
TOK_KI = slice(KEY_COLS, 2 * KEY_COLS)
TOK_RY = slice(2 * KEY_COLS, 2 * KEY_COLS + RY_COLS)
TOK_COLS = TOK_RY.stop
WZ_COLS = slice(0, 2 * GATE_RANK)
WQ_COLS = slice(128, 128 + GLA_HEADS * GLA_DK)
WK_COLS = slice(WQ_COLS.stop, WQ_COLS.stop + GLA_HEADS * GLA_DK)
WRC_COLS = slice(WK_COLS.stop, WK_COLS.stop + RC_COLS)
WIN_COLS = WRC_COLS.stop
IN_QK = slice(0, 2 * GLA_HEADS * GLA_DK)
IN_V = slice(IN_QK.stop, IN_QK.stop + VAL_COLS)
IN_Z = slice(IN_V.stop, IN_V.stop + 2 * GATE_RANK)
IN_RC = slice(IN_Z.stop, IN_Z.stop + RC_COLS)
IN_COLS = IN_RC.stop
IN_STAGE_ROWS = 512
SCAN_UNROLL = 2
SCAN_HEADS = 2
MOD_ROWS = 8
ADALN_COLS = 1536
VMEM_LIMIT_BYTES = 56 * 1024 * 1024
STAGE_SLOTS = 2

_NT = (((1,), (1,)), ((), ()))
_BF16 = jnp.bfloat16
_F32 = jnp.float32


def _dot(a, b):
    return jnp.dot(a, b, preferred_element_type=_F32)


def _sigmoid(x):
    return 1.0 / (1.0 + jnp.exp(-x))


def _adaln_kernel(cv_ref, w_ref, b_ref, o_ref):
    cv = cv_ref[...]
    s = (cv * _sigmoid(cv)).astype(_BF16)
    mod = _dot(s, w_ref[...].astype(_BF16)) + b_ref[pl.ds(pl.program_id(0), 1), :]
    for r in range(MOD_ROWS):
        o_ref[r] = mod[r:r + 1, :]


def _adaln(cv, w_mod, b_mod):
    depth = w_mod.shape[0]
    cols = w_mod.shape[2]
    return pl.pallas_call(
        _adaln_kernel,
        grid=(depth, cols // ADALN_COLS),
        in_specs=[
            pl.BlockSpec((MOD_ROWS, D_MODEL), lambda l, j: (0, 0)),
            pl.BlockSpec((None, D_MODEL, ADALN_COLS), lambda l, j: (l, 0, j)),
            pl.BlockSpec((depth, ADALN_COLS), lambda l, j: (0, j)),
        ],
        out_specs=pl.BlockSpec((MOD_ROWS, 1, ADALN_COLS), lambda l, j: (l, 0, j)),
        out_shape=jax.ShapeDtypeStruct((depth * MOD_ROWS, 1, cols), _F32),
        compiler_params=pltpu.CompilerParams(vmem_limit_bytes=VMEM_LIMIT_BYTES),
    )(cv, w_mod, b_mod)


def _per_direction(t):
    lane = lax.broadcasted_iota(jnp.int32, (1, SLAB), 1)
    first = lane < GLA_DK
    slabs = []
    for p in range(GLA_HEADS // 2):
        pair = t[:, p * SLAB:(p + 1) * SLAB]
        swapped = pltpu.roll(pair, GLA_DK, 1)
        slabs += [jnp.where(first, pair, swapped), jnp.where(first, swapped, pair)]
    return jnp.concatenate(slabs, axis=1)


def _conv_and_pool(rc, cw_ref, pw_ref, ps_ref, seg):
    rows = rc.shape[0]
    pos = lax.broadcasted_iota(jnp.int32, (rows, 1), 0) % seg

    def in_segment(v, d):
        return jnp.where((pos + d >= 0) & (pos + d < seg), pltpu.roll(v, (-d) % rows, 0), 0.0)

    c0 = 0
    cb = rc[:, c0:c0 + CONV_WIDTH]
    zc = rc[:, c0 + CONV_WIDTH:c0 + 2 * CONV_WIDTH] * rc[:, c0 + 2 * CONV_WIDTH:c0 + 3 * CONV_WIDTH]
    yc = cb * (cw_ref[0:1, :] * in_segment(zc, -1) + cw_ref[1:2, :] * zc + cw_ref[2:3, :] * in_segment(zc, 1))

    p0 = 3 * CONV_WIDTH
    lane128 = lax.broadcasted_iota(jnp.int32, (1, 2 * POOL_GROUP), 1)
    pooled = []
    for blk in range(POOL_WIDTH // (2 * POOL_GROUP)):
        u = rc[:, p0 + blk * 2 * POOL_GROUP:p0 + (blk + 1) * 2 * POOL_GROUP]
        h_lo, h_hi = POOL_HALF_WINDOWS[2 * blk], POOL_HALF_WINDOWS[2 * blk + 1]
        half = jnp.where(lane128 < POOL_GROUP, h_lo, h_hi)
        ahead, behind, window_sums = u, in_segment(u, -1), {}
        for level in range(h_hi.bit_length()):
            h = 1 << level
            if level > 0:
                ahead = ahead + in_segment(ahead, h // 2)
                behind = behind + in_segment(behind, -(h // 2))
            if h in (h_lo, h_hi):
                window_sums[h] = ahead + behind
        acc = jnp.where(lane128 < POOL_GROUP, window_sums[h_lo], window_sums[h_hi])
        cnt = (jnp.minimum(pos + half, seg) - jnp.maximum(pos - half, 0)).astype(_F32)
        pooled.append(acc / cnt - u)
    pool_in = jnp.concatenate(pooled, axis=1).astype(_BF16)
    return yc, _dot(pool_in, pw_ref[...]) * ps_ref[...]


def _inproj_kernel(x_ref, mod_ref, n1w_ref, win_ref, wvt_ref, gw_ref, gb_ref, cw_ref, pw_ref, ps_ref,
                   tok_ref, vt_ref, ut_ref, dec_ref, *, seg):
    qe_ref, ki_ref, ry_ref = tok_ref.at[:, TOK_QE], tok_ref.at[:, TOK_KI], tok_ref.at[:, TOK_RY]
    vt_tile = vt_ref.shape[2]
    tm = x_ref.shape[0]
    sh1 = mod_ref[:, 0:D_MODEL]
    scale1 = n1w_ref[...] * (1.0 + mod_ref[:, D_MODEL:2 * D_MODEL])
    row = lax.broadcasted_iota(jnp.int32, (GLA_CHUNK, GLA_CHUNK), 0)
    col = lax.broadcasted_iota(jnp.int32, (GLA_CHUNK, GLA_CHUNK), 1)
    scan_op = jnp.concatenate([(col <= row).astype(_BF16), (col >= row).astype(_BF16)], axis=1)
    lane = lax.broadcasted_iota(jnp.int32, (1, KEY_COLS), 1)
    is_fwd = (lane % SLAB) < GLA_DK
    mid = GLA_CHUNK // 2

    n_sub = tm // SUB_ROWS
    cps = SUB_ROWS // GLA_CHUNK
    sub_rows = [slice(s * SUB_ROWS, (s + 1) * SUB_ROWS) for s in range(n_sub)]
    hbs, projs, gates, q2s, k2s = [], [], [], [], []
    for s in range(n_sub):
        x = x_ref[sub_rows[s], :]
        xn = x * lax.rsqrt(jnp.mean(x * x, axis=-1, keepdims=True) + EPS)
        hbs.append((xn * scale1 + sh1).astype(_BF16))
    for s in range(n_sub):
        proj = lax.dot_general(hbs[s], win_ref[...], _NT, preferred_element_type=_F32)
        projs.append(proj)
        pre = _dot(proj[:, WZ_COLS].astype(_BF16), gw_ref[...]) + gb_ref[...]
        g = (jnp.minimum(pre, 0.0) * LOG2_E - jnp.log2(1.0 + jnp.exp2(jnp.abs(pre) * (-LOG2_E)))) * (1.0 / GATE_TAU)
        gates.append(g.astype(_BF16))
        q2s.append(_per_direction(proj[:, WQ_COLS] * (GLA_DK ** -0.5)))
        k2s.append(_per_direction(proj[:, WK_COLS]))

    decays = []
    for s in range(n_sub):
        for cl in range(cps):
            g = gates[s][cl * GLA_CHUNK:(cl + 1) * GLA_CHUNK]
            zero = jnp.zeros_like(g)
            by_direction = jnp.concatenate([jnp.where(is_fwd, g, zero), jnp.where(is_fwd, zero, g)], axis=0)
            decays.append(_dot(scan_op, by_direction))

    vtbs = []
    for s in range(n_sub):
        vtb = lax.dot_general(wvt_ref[...], hbs[s], _NT, preferred_element_type=_F32).astype(_BF16)
        first = s * SUB_ROWS
        vt_ref[first // vt_tile, :, first % vt_tile:first % vt_tile + SUB_ROWS] = vtb
        vtbs.append(vtb)
        rc = projs[s][:, WRC_COLS]
        yc, yp = _conv_and_pool(rc[:, VAL_COLS:RC_COLS], cw_ref, pw_ref, ps_ref, seg)
        ry_ref[sub_rows[s], 0:VAL_COLS] = rc[:, 0:VAL_COLS].astype(_BF16)
        ry_ref[sub_rows[s], VAL_COLS:VAL_COLS + CONV_WIDTH] = yc.astype(_BF16)
        ry_ref[sub_rows[s], VAL_COLS + CONV_WIDTH:RY_COLS] = yp.astype(_BF16)

    for s in range(n_sub):
        for cl in range(cps):
            c = s * cps + cl
            lrows = slice(cl * GLA_CHUNK, (cl + 1) * GLA_CHUNK)
            rows = slice(c * GLA_CHUNK, (c + 1) * GLA_CHUNK)
            e = decays[c]
            total = jnp.where(is_fwd, e[GLA_CHUNK - 1:GLA_CHUNK, :], e[0:1, :])
            mvec = jnp.where(is_fwd, e[mid - 1:mid, :], e[mid:mid + 1, :])
            bm = e - mvec
            qe_ref[rows, :] = (q2s[s][lrows] * jnp.exp2(bm)).astype(_BF16)
            k_mid = k2s[s][lrows] * jnp.exp2(-bm)
            ki_ref[rows, :] = k_mid.astype(_BF16)
            kd = (k_mid * jnp.exp2(total - mvec)).astype(_BF16)
            dec_ref[c] = jnp.concatenate([total, mvec, jnp.zeros((6, KEY_COLS), _F32)], axis=0)
            for h in range(GLA_HEADS):
                vt_h = vtbs[s][h * GLA_DV:(h + 1) * GLA_DV, lrows]
                ut_ref[c, h] = _dot(vt_h, kd[:, h * SLAB:(h + 1) * SLAB]).astype(ut_ref.dtype)


def _inproj(x2d, mods3, mod_row_of_tile, layer, win, wvt, wts, tm, seg, vt_tile):
    tokens = x2d.shape[0]
    n_tiles = tokens // tm
    n_chunks = tokens // GLA_CHUNK
    cpt = tm // GLA_CHUNK
    assert SUB_ROWS % seg == 0
    assert tm % vt_tile == 0 and vt_tile % SUB_ROWS == 0
    of_layer = lambda shape: pl.BlockSpec((None,) + shape, lambda j: (layer,) + (0,) * len(shape))
    return pl.pallas_call(
        functools.partial(_inproj_kernel, seg=seg),
        grid=(n_tiles,),
        in_specs=[
            pl.BlockSpec((tm, D_MODEL), lambda j: (j, 0)),
            pl.BlockSpec((None, 1, N_MOD * D_MODEL), lambda j: (layer * MOD_ROWS + mod_row_of_tile(j), 0, 0)),
            of_layer((1, D_MODEL)),
            pl.BlockSpec((WIN_COLS, D_MODEL), lambda j: (0, 0)),
            pl.BlockSpec((VAL_COLS, D_MODEL), lambda j: (0, 0)),
            of_layer((2 * GATE_RANK, KEY_COLS)),
            of_layer((1, KEY_COLS)),
            of_layer((3, CONV_WIDTH)),
            of_layer((POOL_WIDTH, POOL_WIDTH)),
            of_layer((1, POOL_WIDTH)),
        ],
        out_specs=[
            pl.BlockSpec((tm, TOK_COLS), lambda j: (j, 0)),
            pl.BlockSpec((tm // vt_tile, VAL_COLS, vt_tile), lambda j: (j, 0, 0)),
            pl.BlockSpec((cpt, GLA_HEADS, GLA_DV, SLAB), lambda j: (j, 0, 0, 0)),
            pl.BlockSpec((cpt, 8, KEY_COLS), lambda j: (j, 0, 0)),
        ],
        out_shape=[
            jax.ShapeDtypeStruct((tokens, TOK_COLS), _BF16),
            jax.ShapeDtypeStruct((tokens // vt_tile, VAL_COLS, vt_tile), _BF16),
            jax.ShapeDtypeStruct((n_chunks, GLA_HEADS, GLA_DV, SLAB), _BF16),
            jax.ShapeDtypeStruct((n_chunks, 8, KEY_COLS), _F32),
        ],
        compiler_params=pltpu.CompilerParams(vmem_limit_bytes=VMEM_LIMIT_BYTES),
    )(x2d, mods3, wts["n1w"], win, wvt, wts["gw"], wts["gb"], wts["cw"], wts["pw"], wts["ps"])


def _scan_chunks(ut_ref, dec_ref, states, spt_ref, nc, chunk0=0, head0=0):
    lane = lax.broadcasted_iota(jnp.int32, (1, SLAB), 1)
    is_fwd = lane < GLA_DK

    def step(t, states):
        i = chunk0 + t
        j = chunk0 + nc - 1 - t
        new_states = []
        for s in range(SCAN_HEADS):
            h = head0 + s
            lanes = slice(h * SLAB, (h + 1) * SLAB)
            log_decay = jnp.where(is_fwd, dec_ref[i, 0:1, lanes], dec_ref[j, 0:1, lanes])
            log_mid = jnp.where(is_fwd, dec_ref[i, 1:2, lanes], dec_ref[j, 1:2, lanes])
            entering = (states[s] * jnp.exp2(log_mid)).astype(_BF16)
            spt_ref[i, h, :, 0:GLA_DK] = entering[:, 0:GLA_DK]
            spt_ref[j, h, :, GLA_DK:SLAB] = entering[:, GLA_DK:SLAB]
            inc = jnp.where(is_fwd, ut_ref[i, h], ut_ref[j, h]).astype(_F32)
            new_states.append(states[s] * jnp.exp2(log_decay) + inc)
        return tuple(new_states)

    return lax.fori_loop(0, nc, step, tuple(states), unroll=SCAN_UNROLL)


def _scan_kernel(ut_ref, dec_ref, s0_ref, spt_ref, sfin_ref):
    final = _scan_chunks(ut_ref, dec_ref, [s0_ref[h] for h in range(SCAN_HEADS)], spt_ref, ut_ref.shape[0])
    for h in range(SCAN_HEADS):
        sfin_ref[h] = final[h]


def _scan(ut, dec, s0, chunks_per_seq):
    n_chunks = ut.shape[0]
    bsz = n_chunks // chunks_per_seq
    return pl.pallas_call(
        _scan_kernel,
        grid=(bsz, GLA_HEADS // SCAN_HEADS),
        in_specs=[
            pl.BlockSpec((chunks_per_seq, SCAN_HEADS, GLA_DV, SLAB), lambda b, g: (b, g, 0, 0)),
            pl.BlockSpec((chunks_per_seq, 8, SCAN_HEADS * SLAB), lambda b, g: (b, 0, g)),
            pl.BlockSpec((None, SCAN_HEADS, GLA_DV, SLAB), lambda b, g: (b, g, 0, 0)),
        ],
        out_specs=[
            pl.BlockSpec((chunks_per_seq, SCAN_HEADS, GLA_DV, SLAB), lambda b, g: (b, g, 0, 0)),
            pl.BlockSpec((None, SCAN_HEADS, GLA_DV, SLAB), lambda b, g: (b, g, 0, 0)),
        ],
        out_shape=[
            jax.ShapeDtypeStruct((n_chunks, GLA_HEADS, GLA_DV, SLAB), _BF16),
            jax.ShapeDtypeStruct((bsz, GLA_HEADS, GLA_DV, SLAB), _F32),
        ],
        compiler_params=pltpu.CompilerParams(vmem_limit_bytes=VMEM_LIMIT_BYTES),
    )(ut, dec, s0)


def _mix_weight_stager(layer, wo_hbm, w1_hbm, w2_hbm, wo_ref, w1_ref, w2_ref, stage, sems):
    pieces = [(wo_hbm.at[layer], wo_ref)]
    for i in range(D_FF // D_MODEL):
        cols = slice(i * D_MODEL, (i + 1) * D_MODEL)
        pieces.append((w1_hbm.at[layer, :, cols], w1_ref.at[:, cols]))
        pieces.append((w2_hbm.at[layer, cols, :], w2_ref.at[cols, :]))
    copies = [pltpu.make_async_copy(src, stage.at[k % STAGE_SLOTS], sems.at[k % STAGE_SLOTS])
              for k, (src, _) in enumerate(pieces)]
    ready = [None]

    def need(n):
        if ready[0] is None:
            for k in range(STAGE_SLOTS):
                copies[k].start()
            ready[0] = 0
        for k in range(ready[0], min(n, len(pieces))):
            copies[k].wait()
            pieces[k][1][...] = stage[k % STAGE_SLOTS].astype(_BF16)
            if k + STAGE_SLOTS < len(pieces):
                copies[k + STAGE_SLOTS].start()
            ready[0] = k + 1

    return need


def _mix_body(x_ref, mod_ref, n2w_ref, tok_ref, vt_ref, spt_ref, gnw_ref, fnw_ref, o_ref, y_scr, wo_ref, w1_ref, w2_ref,
              need, final):
    tm = x_ref.shape[0]
    qe_ref, ki_ref, ry_ref = tok_ref.at[:, TOK_QE], tok_ref.at[:, TOK_KI], tok_ref.at[:, TOK_RY]
    g1 = mod_ref[:, 2 * D_MODEL:3 * D_MODEL]
    sh2 = mod_ref[:, 3 * D_MODEL:4 * D_MODEL]
    scale2 = n2w_ref[...] * (1.0 + mod_ref[:, 4 * D_MODEL:5 * D_MODEL])
    g2 = mod_ref[:, 5 * D_MODEL:6 * D_MODEL]
    row = lax.broadcasted_iota(jnp.int32, (GLA_CHUNK, GLA_CHUNK), 0)
    col = lax.broadcasted_iota(jnp.int32, (GLA_CHUNK, GLA_CHUNK), 1)
    lane = lax.broadcasted_iota(jnp.int32, (GLA_CHUNK, SLAB), 1)
    fwd_lane = lane < GLA_DK
    gnw = gnw_ref[...]
    need(0)

    for c in range(tm // GLA_CHUNK):
        rows = slice(c * GLA_CHUNK, (c + 1) * GLA_CHUNK)
        for h in range(GLA_HEADS):
            lanes = slice(h * SLAB, (h + 1) * SLAB)
            qe = qe_ref[rows, lanes]
            ki = ki_ref[rows, lanes]
            zero = jnp.zeros_like(ki)
            keys = jnp.concatenate([jnp.where(fwd_lane, ki, zero), jnp.where(fwd_lane, zero, ki)], axis=0)
            a2 = lax.dot_general(qe, keys, _NT, preferred_element_type=_F32)
            am = jnp.where(col <= row, a2[:, 0:GLA_CHUNK], 0.0) + jnp.where(col >= row, a2[:, GLA_CHUNK:], 0.0)
            lhs = jnp.concatenate([am.astype(_BF16), qe], axis=1)
            rhs = jnp.concatenate([vt_ref[h * GLA_DV:(h + 1) * GLA_DV, rows], spt_ref[c, h]], axis=1)
            o = lax.dot_general(lhs, rhs, _NT, preferred_element_type=_F32)
            o = o * lax.rsqrt(jnp.mean(o * o, axis=-1, keepdims=True) + EPS) * gnw
            r = ry_ref[rows, h * GLA_DV:(h + 1) * GLA_DV].astype(_F32)
            y_scr[rows, h * GLA_DV:(h + 1) * GLA_DV] = (o * (r * _sigmoid(r))).astype(_BF16)
        need(c + 1)

    need(1)
    mixed = _dot(y_scr[...], wo_ref[0:VAL_COLS, :]) + _dot(ry_ref[:, VAL_COLS:RY_COLS], wo_ref[VAL_COLS:D_MODEL, :])
    x1 = x_ref[...] + g1 * mixed
    xn = x1 * lax.rsqrt(jnp.mean(x1 * x1, axis=-1, keepdims=True) + EPS)
    h2 = (xn * scale2 + sh2).astype(_BF16)
    acc = jnp.zeros((tm, D_MODEL), _F32)
    for f in range(D_FF // FF_CHUNK):
        cols = slice(f * FF_CHUNK, (f + 1) * FF_CHUNK)
        need(1 + 2 * pl.cdiv(cols.stop, D_MODEL))
        a = jnp.maximum(_dot(h2, w1_ref[:, cols]), 0.0)
        acc = acc + _dot((a * a).astype(_BF16), w2_ref[cols, :])
    out = x1 + g2 * acc
    if final:
        out = out * lax.rsqrt(jnp.mean(out * out, axis=-1, keepdims=True) + EPS) * fnw_ref[...]
    o_ref[...] = out


def _mix_kernel(x_ref, mod_ref, n2w_ref, tok_ref, vt_ref, spt_ref, gnw_ref, wo_hbm, w1_hbm, w2_hbm, fnw_ref,
                o_ref, y_scr, wo_ref, w1_ref, w2_ref, stage, sems, *, layer, final):
    body = functools.partial(_mix_body, x_ref, mod_ref, n2w_ref, tok_ref, vt_ref, spt_ref, gnw_ref, fnw_ref, o_ref, y_scr,
                             wo_ref, w1_ref, w2_ref, final=final)

    def staging_step():
        body(need=_mix_weight_stager(layer, wo_hbm, w1_hbm, w2_hbm, wo_ref, w1_ref, w2_ref, stage, sems))

    first = pl.program_id(0) == 0
    pl.when(first)(staging_step)
    pl.when(jnp.logical_not(first))(functools.partial(body, need=lambda n: None))


def _mix_weight_scratch():
    return [pltpu.VMEM((D_MODEL, D_MODEL), _BF16), pltpu.VMEM((D_MODEL, D_FF), _BF16), pltpu.VMEM((D_FF, D_MODEL), _BF16),
            pltpu.VMEM((STAGE_SLOTS, D_MODEL, D_MODEL), _F32), pltpu.SemaphoreType.DMA((STAGE_SLOTS,))]


def _mix(x2d, mods3, mod_row_of_tile, layer, tok, vt, spt, wts, tm, final):
    tokens = x2d.shape[0]
    cpt = tm // GLA_CHUNK
    of_layer = lambda shape: pl.BlockSpec((None,) + shape, lambda j: (layer,) + (0,) * len(shape),
                                          pipeline_mode=pl.Buffered(1))
    return pl.pallas_call(
        functools.partial(_mix_kernel, layer=layer, final=final),
        grid=(tokens // tm,),
        in_specs=[
            pl.BlockSpec((tm, D_MODEL), lambda j: (j, 0)),
            pl.BlockSpec((None, 1, N_MOD * D_MODEL), lambda j: (layer * MOD_ROWS + mod_row_of_tile(j), 0, 0)),
            of_layer((1, D_MODEL)),
            pl.BlockSpec((tm, TOK_COLS), lambda j: (j, 0)),
            pl.BlockSpec((None, VAL_COLS, tm), lambda j: (j, 0, 0)),
            pl.BlockSpec((cpt, GLA_HEADS, GLA_DV, SLAB), lambda j: (j, 0, 0, 0)),
            of_layer((1, GLA_DV)),
            pl.BlockSpec(memory_space=pl.ANY),
            pl.BlockSpec(memory_space=pl.ANY),
            pl.BlockSpec(memory_space=pl.ANY),
            pl.BlockSpec((1, D_MODEL), lambda j: (0, 0), pipeline_mode=pl.Buffered(1)),
        ],
        out_specs=pl.BlockSpec((tm, D_MODEL), lambda j: (j, 0)),
        out_shape=jax.ShapeDtypeStruct((tokens, D_MODEL), _F32),
        scratch_shapes=[pltpu.VMEM((tm, VAL_COLS), _BF16)] + _mix_weight_scratch(),
        compiler_params=pltpu.CompilerParams(vmem_limit_bytes=VMEM_LIMIT_BYTES),
    )(x2d, mods3, wts["n2w"], tok, vt, spt, wts["gnw"], wts["wo"], wts["w1"], wts["w2"], wts["fnw"])


def _stage_in_projection(layer, w_in_t_hbm, win_ref, wvt_ref, stage, sems):
    pieces = []
    for src, dst_ref, dst in ((IN_Z, win_ref, WZ_COLS.start), (IN_QK, win_ref, WQ_COLS.start),
                              (IN_V, wvt_ref, 0), (IN_RC, win_ref, WRC_COLS.start)):
        for off in range(0, src.stop - src.start, IN_STAGE_ROWS):
            pieces.append((src.start + off, dst_ref, dst + off, min(IN_STAGE_ROWS, src.stop - src.start - off)))
    copies = [pltpu.make_async_copy(w_in_t_hbm.at[layer, s0:s0 + n, :], stage.at[k % STAGE_SLOTS, 0:n, :],
                                    sems.at[k % STAGE_SLOTS]) for k, (s0, _, _, n) in enumerate(pieces)]
    for k in range(STAGE_SLOTS):
        copies[k].start()
    win_ref[WZ_COLS.stop:WQ_COLS.start, :] = jnp.zeros((WQ_COLS.start - WZ_COLS.stop, D_MODEL), _BF16)
    for k, (_, dst_ref, d0, n) in enumerate(pieces):
        copies[k].wait()
        dst_ref[d0:d0 + n, :] = stage[k % STAGE_SLOTS, 0:n, :].astype(_BF16)
        if k + STAGE_SLOTS < len(pieces):
            copies[k + STAGE_SLOTS].start()


def _context_kernel(*refs, layer, seg, chunks_per_seq, with_mix):
    n_in = 9
    x_ref, mod_ref, n1w_ref, w_in_t_hbm = refs[0:4]
    if with_mix:
        n2w_ref, gnw_ref, wo_hbm, w1_hbm, w2_hbm, fnw_ref = refs[n_in:n_in + 6]
        sfin_ref, win_ref, wvt_ref, o_ref = refs[n_in + 6:n_in + 10]
        tok_scr, vt_scr, ut_scr, dec_scr, spt_scr, in_stage, in_sems, y_scr = refs[n_in + 10:n_in + 18]
        mix_weight_scr = refs[n_in + 18:]
    else:
        sfin_ref, win_ref, wvt_ref = refs[n_in:n_in + 3]
        tok_scr, vt_scr, ut_scr, dec_scr, spt_scr, in_stage, in_sems = refs[n_in + 3:]

    need = lambda n: None
    if with_mix:
        wo_ref, w1_ref, w2_ref, mix_stage, mix_sems = mix_weight_scr
        need = _mix_weight_stager(layer, wo_hbm, w1_hbm, w2_hbm, wo_ref, w1_ref, w2_ref, mix_stage, mix_sems)

    _stage_in_projection(layer, w_in_t_hbm, win_ref, wvt_ref, in_stage, in_sems)
    need(0)
    _inproj_kernel(x_ref, mod_ref, n1w_ref, win_ref, wvt_ref, *refs[4:n_in], tok_scr, vt_scr, ut_scr, dec_scr, seg=seg)
    need(STAGE_SLOTS)

    sfin_ref[...] = jnp.zeros(sfin_ref.shape, _F32)
    for b in range(ut_scr.shape[0] // chunks_per_seq):
        for head0 in range(0, GLA_HEADS, SCAN_HEADS):
            start = [sfin_ref[b, head0 + s] for s in range(SCAN_HEADS)]
            final = _scan_chunks(ut_scr, dec_scr, start, spt_scr, chunks_per_seq,
                                 chunk0=b * chunks_per_seq, head0=head0)
            for s in range(SCAN_HEADS):
                sfin_ref[b, head0 + s] = final[s]

    if with_mix:
        need(2 * STAGE_SLOTS)
        _mix_body(x_ref, mod_ref, n2w_ref, tok_scr, vt_scr.at[0], spt_scr, gnw_ref, fnw_ref, o_ref, y_scr,
                  wo_ref, w1_ref, w2_ref, need=need, final=False)


def _context_layer(xc, mods3, mod_row, layer, wts, ctx_len, with_mix):
    rows = xc.shape[0]
    bsz = rows // ctx_len
    n_chunks = rows // GLA_CHUNK
    assert rows % SUB_ROWS == 0 and SUB_ROWS % ctx_len == 0
    of_layer = lambda shape: pl.BlockSpec((None,) + shape, lambda j: (layer,) + (0,) * len(shape),
                                          pipeline_mode=pl.Buffered(1))
    in_specs = [
        pl.BlockSpec((rows, D_MODEL), lambda j: (0, 0)),
        pl.BlockSpec((None, 1, N_MOD * D_MODEL), lambda j: (layer * MOD_ROWS + mod_row, 0, 0)),
        of_layer((1, D_MODEL)),
        pl.BlockSpec(memory_space=pl.ANY),
        of_layer((2 * GATE_RANK, KEY_COLS)),
        of_layer((1, KEY_COLS)),
        of_layer((3, CONV_WIDTH)),
        of_layer((POOL_WIDTH, POOL_WIDTH)),
        of_layer((1, POOL_WIDTH)),
    ]
    args = [xc, mods3, wts["n1w"], wts["w_in_t"], wts["gw"], wts["gb"], wts["cw"], wts["pw"], wts["ps"]]
    out_specs = [pl.BlockSpec((bsz, GLA_HEADS, GLA_DV, SLAB), lambda j: (0, 0, 0, 0)),
                 pl.BlockSpec((WIN_COLS, D_MODEL), lambda j: (0, 0)),
                 pl.BlockSpec((VAL_COLS, D_MODEL), lambda j: (0, 0))]
    out_shape = [jax.ShapeDtypeStruct((bsz, GLA_HEADS, GLA_DV, SLAB), _F32),
                 jax.ShapeDtypeStruct((WIN_COLS, D_MODEL), _BF16),
                 jax.ShapeDtypeStruct((VAL_COLS, D_MODEL), _BF16)]
    scratch = [
        pltpu.VMEM((rows, TOK_COLS), _BF16),
        pltpu.VMEM((1, VAL_COLS, rows), _BF16),
        pltpu.VMEM((n_chunks, GLA_HEADS, GLA_DV, SLAB), _BF16),
        pltpu.VMEM((n_chunks, 8, KEY_COLS), _F32),
        pltpu.VMEM((n_chunks, GLA_HEADS, GLA_DV, SLAB), _BF16),
        pltpu.VMEM((STAGE_SLOTS, IN_STAGE_ROWS, D_MODEL), _F32),
        pltpu.SemaphoreType.DMA((STAGE_SLOTS,)),
    ]
    if with_mix:
        in_specs += [of_layer((1, D_MODEL)), of_layer((1, GLA_DV))] + [pl.BlockSpec(memory_space=pl.ANY)] * 3
        in_specs += [pl.BlockSpec((1, D_MODEL), lambda j: (0, 0), pipeline_mode=pl.Buffered(1))]
        args += [wts["n2w"], wts["gnw"], wts["wo"], wts["w1"], wts["w2"], wts["fnw"]]
        out_specs.append(pl.BlockSpec((rows, D_MODEL), lambda j: (0, 0)))
        out_shape.append(jax.ShapeDtypeStruct((rows, D_MODEL), _F32))
        scratch += [pltpu.VMEM((rows, VAL_COLS), _BF16)] + _mix_weight_scratch()
    outs = pl.pallas_call(
        functools.partial(_context_kernel, layer=layer, seg=ctx_len, chunks_per_seq=ctx_len // GLA_CHUNK,
                          with_mix=with_mix),
        grid=(1,),
        in_specs=in_specs,
        out_specs=out_specs,
        out_shape=out_shape,
        scratch_shapes=scratch,
        compiler_params=pltpu.CompilerParams(vmem_limit_bytes=VMEM_LIMIT_BYTES),
    )(*args)
    return tuple(outs) if with_mix else tuple(outs) + (None,)


def _prepare_weights(norm1_w, norm2_w, w_in, gate_w, gate_b, gla_norm_w, conv_w, pool_w, pool_scale, w_out,
                     w_mlp1, w_mlp2, final_norm_w):
    depth = w_in.shape[0]
    assert w_in.shape[2] == IN_COLS
    zeros = jnp.zeros((depth, GATE_RANK, GLA_HEADS, GLA_DK), _F32)
    gf = gate_w[:, 0].reshape(depth, GATE_RANK, GLA_HEADS, GLA_DK)
    gb = gate_w[:, 1].reshape(depth, GATE_RANK, GLA_HEADS, GLA_DK)
    gw = jnp.concatenate([jnp.concatenate([gf, zeros], -1), jnp.concatenate([zeros, gb], -1)], 1)
    gbias = jnp.concatenate([gate_b[:, 0].reshape(depth, GLA_HEADS, GLA_DK),
                             gate_b[:, 1].reshape(depth, GLA_HEADS, GLA_DK)], -1)
    n_groups = POOL_WIDTH // POOL_GROUP
    same_group = jnp.eye(n_groups, dtype=_F32)[None, :, None, :, None]
    pw = (pool_w[:, :, :, None, :] * same_group).reshape(depth, POOL_WIDTH, POOL_WIDTH)
    return {
        "n1w": norm1_w.reshape(depth, 1, D_MODEL),
        "n2w": norm2_w.reshape(depth, 1, D_MODEL),
        "w_in_t": jnp.swapaxes(w_in, 1, 2),
        "gw": gw.reshape(depth, 2 * GATE_RANK, KEY_COLS).astype(_BF16),
        "gb": gbias.reshape(depth, 1, KEY_COLS),
        "gnw": gla_norm_w.reshape(depth, 1, GLA_DV),
        "cw": conv_w,
        "pw": pw.astype(_BF16),
        "ps": pool_scale.reshape(depth, 1, POOL_WIDTH),
        "wo": w_out,
        "w1": w_mlp1,
        "w2": w_mlp2,
        "fnw": final_norm_w.reshape(1, D_MODEL),
    }


def kernel(x, c, ctx, c_ctx, w_mod, b_mod, norm1_w, norm2_w, w_in, gla_gate_w, gla_gate_b, gla_norm_w, conv_w, pool_w, pool_scale, w_out, w_mlp1, w_mlp2, final_norm_w):
    bsz, n, _ = x.shape
    ctx_len = ctx.shape[1]
    depth = w_in.shape[0]
    assert bsz + 1 <= MOD_ROWS and n % LATENT_IN_TILE == 0 and SUB_ROWS % ctx_len == 0
    ctx_row = bsz

    cv = jnp.concatenate([c, c_ctx[None, :], jnp.zeros((MOD_ROWS - bsz - 1, D_MODEL), _F32)], axis=0)
    mods3 = _adaln(cv, w_mod, b_mod)
    wts = _prepare_weights(norm1_w, norm2_w, w_in, gla_gate_w, gla_gate_b, gla_norm_w, conv_w, pool_w, pool_scale,
                           w_out, w_mlp1, w_mlp2, final_norm_w)

    xl = x.reshape(bsz * n, D_MODEL)
    xc = ctx.reshape(bsz * ctx_len, D_MODEL)
    lat_in_row = lambda j: j // (n // LATENT_IN_TILE)
    lat_mix_row = lambda j: j // (n // LATENT_MIX_TILE)

    for l in range(depth):
        last = l == depth - 1
        s_ctx, win, wvt, xc = _context_layer(xc, mods3, ctx_row, l, wts, ctx_len, with_mix=not last)

        tok, vt, ut, dec = _inproj(xl, mods3, lat_in_row, l, win, wvt, wts, LATENT_IN_TILE, GRID_W, LATENT_MIX_TILE)
        spt, _ = _scan(ut, dec, s_ctx, n // GLA_CHUNK)
        xl = _mix(xl, mods3, lat_mix_row, l, tok, vt, spt, wts, LATENT_MIX_TILE, last)
    return xl.reshape(bsz, n, D_MODEL)
```

```python
import functools

import jax
import jax.numpy as jnp
from jax import lax
from jax.experimental import pallas as pl
from jax.experimental.pallas import tpu as pltpu

D_MODEL = 1024
GLA_HEADS = 4
GLA_DK = 64
GLA_DV = 128
SLAB = 2 * GLA_DK
KEY_COLS = GLA_HEADS * SLAB
VAL_COLS = GLA_HEADS * GLA_DV
GATE_RANK = 16
GATE_TAU = 16.0
LOG2_E = 1.4426950408889634
GLA_CHUNK = 128
SUB_ROWS = 256
LATENT_IN_TILE = 1024
LATENT_MIX_TILE = 512
CONV_WIDTH = 256
POOL_WIDTH = 256
POOL_GROUP = 64
POOL_HALF_WINDOWS = (1, 2, 4, 8)
assert all(h & (h - 1) == 0 for h in POOL_HALF_WINDOWS)
D_FF = 4096
FF_CHUNK = 1024
N_MOD = 6
EPS = 1e-6
GRID_W = 64
RC_COLS = VAL_COLS + 3 * CONV_WIDTH + POOL_WIDTH
RY_COLS = VAL_COLS + CONV_WIDTH + POOL_WIDTH
TOK_QE = slice(0, KEY_COLS)
TOK_KI = slice(KEY_COLS, 2 * KEY_COLS)
TOK_RY = slice(2 * KEY_COLS, 2 * KEY_COLS + RY_COLS)
TOK_COLS = TOK_RY.stop
WZ_COLS = slice(0, 2 * GATE_RANK)
WQ_COLS = slice(128, 128 + GLA_HEADS * GLA_DK)
WK_COLS = slice(WQ_COLS.stop, WQ_COLS.stop + GLA_HEADS * GLA_DK)
WRC_COLS = slice(WK_COLS.stop, WK_COLS.stop + RC_COLS)
WIN_COLS = WRC_COLS.stop
IN_QK = slice(0, 2 * GLA_HEADS * GLA_DK)
IN_V = slice(IN_QK.stop, IN_QK.stop + VAL_COLS)
IN_Z = slice(IN_V.stop, IN_V.stop + 2 * GATE_RANK)
IN_RC = slice(IN_Z.stop, IN_Z.stop + RC_COLS)
IN_COLS = IN_RC.stop
IN_STAGE_ROWS = 512
SCAN_UNROLL = 2
SCAN_HEADS = 2
MOD_ROWS = 8
ADALN_COLS = 1536
VMEM_LIMIT_BYTES = 56 * 1024 * 1024
STAGE_SLOTS = 2

_NT = (((1,), (1,)), ((), ()))
_BF16 = jnp.bfloat16
_F32 = jnp.float32


def _dot(a, b):
    return jnp.dot(a, b, preferred_element_type=_F32)


def _sigmoid(x):
    return 1.0 / (1.0 + jnp.exp(-x))


def _adaln_kernel(cv_ref, w_ref, b_ref, o_ref):
    cv = cv_ref[...]
    s = (cv * _sigmoid(cv)).astype(_BF16)
    mod = _dot(s, w_ref[...].astype(_BF16)) + b_ref[pl.ds(pl.program_id(0), 1), :]
    for r in range(MOD_ROWS):
        o_ref[r] = mod[r:r + 1, :]


def _adaln(cv, w_mod, b_mod):
    depth = w_mod.shape[0]
    cols = w_mod.shape[2]
    return pl.pallas_call(
        _adaln_kernel,
        grid=(depth, cols // ADALN_COLS),
        in_specs=[
            pl.BlockSpec((MOD_ROWS, D_MODEL), lambda l, j: (0, 0)),
            pl.BlockSpec((None, D_MODEL, ADALN_COLS), lambda l, j: (l, 0, j)),
            pl.BlockSpec((depth, ADALN_COLS), lambda l, j: (0, j)),
        ],
        out_specs=pl.BlockSpec((MOD_ROWS, 1, ADALN_COLS), lambda l, j: (l, 0, j)),
        out_shape=jax.ShapeDtypeStruct((depth * MOD_ROWS, 1, cols), _F32),
        compiler_params=pltpu.CompilerParams(vmem_limit_bytes=VMEM_LIMIT_BYTES),
    )(cv, w_mod, b_mod)


def _per_direction(t):
    lane = lax.broadcasted_iota(jnp.int32, (1, SLAB), 1)
    first = lane < GLA_DK
    slabs = []
    for p in range(GLA_HEADS // 2):
        pair = t[:, p * SLAB:(p + 1) * SLAB]
        swapped = pltpu.roll(pair, GLA_DK, 1)
        slabs += [jnp.where(first, pair, swapped), jnp.where(first, swapped, pair)]
    return jnp.concatenate(slabs, axis=1)


def _conv_and_pool(rc, taps, pw_ref, pool_scale, seg):
    rows = rc.shape[0]
    pos = lax.broadcasted_iota(jnp.int32, (rows, 1), 0) % seg

    def in_segment(v, d):
        return jnp.where((pos + d >= 0) & (pos + d < seg), pltpu.roll(v, (-d) % rows, 0), 0.0)

    c0 = 0
    cb = rc[:, c0:c0 + CONV_WIDTH]
    zc = rc[:, c0 + CONV_WIDTH:c0 + 2 * CONV_WIDTH] * rc[:, c0 + 2 * CONV_WIDTH:c0 + 3 * CONV_WIDTH]
    yc = cb * (taps[0] * in_segment(zc, -1) + taps[1] * zc + taps[2] * in_segment(zc, 1))

    p0 = 3 * CONV_WIDTH
    lane128 = lax.broadcasted_iota(jnp.int32, (1, 2 * POOL_GROUP), 1)
    pooled = []
    for blk in range(POOL_WIDTH // (2 * POOL_GROUP)):
        u = rc[:, p0 + blk * 2 * POOL_GROUP:p0 + (blk + 1) * 2 * POOL_GROUP]
        h_lo, h_hi = POOL_HALF_WINDOWS[2 * blk], POOL_HALF_WINDOWS[2 * blk + 1]
        half = jnp.where(lane128 < POOL_GROUP, h_lo, h_hi)
        ahead, behind, window_sums = u, in_segment(u, -1), {}
        for level in range(h_hi.bit_length()):
            h = 1 << level
            if level > 0:
                ahead = ahead + in_segment(ahead, h // 2)
                behind = behind + in_segment(behind, -(h // 2))
            if h in (h_lo, h_hi):
                window_sums[h] = ahead + behind
        acc = jnp.where(lane128 < POOL_GROUP, window_sums[h_lo], window_sums[h_hi])
        cnt = (jnp.minimum(pos + half, seg) - jnp.maximum(pos - half, 0)).astype(_F32)
        pooled.append(acc / cnt - u)
    pool_in = jnp.concatenate(pooled, axis=1).astype(_BF16)
    return yc, _dot(pool_in, pw_ref[...]) * pool_scale


def _inproj_kernel(x_ref, mod_ref, n1w_ref, win_ref, wvt_ref, gw_ref, gb_ref, cw_ref, pw_ref, ps_ref,
                   tok_ref, vt_ref, ut_ref, dec_ref, *, layer, seg):
    this_layer = slice(layer, layer + 1)
    taps = [cw_ref[t, this_layer, :] for t in range(cw_ref.shape[0])]
    pool_scale = ps_ref[this_layer, :]
    qe_ref, ki_ref, ry_ref = tok_ref.at[:, TOK_QE], tok_ref.at[:, TOK_KI], tok_ref.at[:, TOK_RY]
    vt_tile = vt_ref.shape[2]
    tm = x_ref.shape[0]
    sh1 = mod_ref[:, 0:D_MODEL]
    scale1 = n1w_ref[this_layer, :] * (1.0 + mod_ref[:, D_MODEL:2 * D_MODEL])
    row = lax.broadcasted_iota(jnp.int32, (GLA_CHUNK, GLA_CHUNK), 0)
    col = lax.broadcasted_iota(jnp.int32, (GLA_CHUNK, GLA_CHUNK), 1)
    scan_op = jnp.where(jnp.concatenate([col <= row, col >= row], axis=1), 1.0 / GATE_TAU, 0.0).astype(_BF16)
    lane = lax.broadcasted_iota(jnp.int32, (1, KEY_COLS), 1)
    is_fwd = (lane % SLAB) < GLA_DK
    mid = GLA_CHUNK // 2

    n_sub = tm // SUB_ROWS
    cps = SUB_ROWS // GLA_CHUNK
    sub_rows = [slice(s * SUB_ROWS, (s + 1) * SUB_ROWS) for s in range(n_sub)]
    hbs, projs, gates, q2s, k2s = [], [], [], [], []
    for s in range(n_sub):
        x = x_ref[sub_rows[s], :]
        xn = x * lax.rsqrt(jnp.mean(x * x, axis=-1, keepdims=True) + EPS)
        hbs.append((xn * scale1 + sh1).astype(_BF16))
    for s in range(n_sub):
        proj = lax.dot_general(hbs[s], win_ref[...], _NT, preferred_element_type=_F32)
        projs.append(proj)
        pre = _dot(proj[:, WZ_COLS].astype(_BF16), gw_ref[...]) + gb_ref[...]
        g = jnp.minimum(pre, 0.0) * LOG2_E - jnp.log2(1.0 + jnp.exp2(jnp.abs(pre) * (-LOG2_E)))
        gates.append(g.astype(_BF16))
        q2s.append(_per_direction(proj[:, WQ_COLS] * (GLA_DK ** -0.5)))
        k2s.append(_per_direction(proj[:, WK_COLS]))

    decays = []
    for s in range(n_sub):
        for cl in range(cps):
            g = gates[s][cl * GLA_CHUNK:(cl + 1) * GLA_CHUNK]
            zero = jnp.zeros_like(g)
            by_direction = jnp.concatenate([jnp.where(is_fwd, g, zero), jnp.where(is_fwd, zero, g)], axis=0)
            decays.append(_dot(scan_op, by_direction))

    vtbs = []
    for s in range(n_sub):
        vtb = lax.dot_general(wvt_ref[...], hbs[s], _NT, preferred_element_type=_F32).astype(_BF16)
        first = s * SUB_ROWS
        vt_ref[first // vt_tile, :, first % vt_tile:first % vt_tile + SUB_ROWS] = vtb
        vtbs.append(vtb)
        rc = projs[s][:, WRC_COLS]
        yc, yp = _conv_and_pool(rc[:, VAL_COLS:RC_COLS], taps, pw_ref, pool_scale, seg)
        ry_ref[sub_rows[s], 0:VAL_COLS] = rc[:, 0:VAL_COLS].astype(_BF16)
        ry_ref[sub_rows[s], VAL_COLS:VAL_COLS + CONV_WIDTH] = yc.astype(_BF16)
        ry_ref[sub_rows[s], VAL_COLS + CONV_WIDTH:RY_COLS] = yp.astype(_BF16)

    for s in range(n_sub):
        for cl in range(cps):
            c = s * cps + cl
            lrows = slice(cl * GLA_CHUNK, (cl + 1) * GLA_CHUNK)
            rows = slice(c * GLA_CHUNK, (c + 1) * GLA_CHUNK)
            e = decays[c]
            total = jnp.where(is_fwd, e[GLA_CHUNK - 1:GLA_CHUNK, :], e[0:1, :])
            mvec = jnp.where(is_fwd, e[mid - 1:mid, :], e[mid:mid + 1, :])
            bm = e - mvec
            qe_ref[rows, :] = (q2s[s][lrows] * jnp.exp2(bm)).astype(_BF16)
            k_mid = k2s[s][lrows] * jnp.exp2(-bm)
            ki_ref[rows, :] = k_mid.astype(_BF16)
            kd = (k_mid * jnp.exp2(total - mvec)).astype(_BF16)
            dec_ref[c] = jnp.concatenate([total, mvec, jnp.zeros((6, KEY_COLS), _F32)], axis=0)
            for h in range(GLA_HEADS):
                vt_h = vtbs[s][h * GLA_DV:(h + 1) * GLA_DV, lrows]
                ut_ref[c, h] = _dot(vt_h, kd[:, h * SLAB:(h + 1) * SLAB]).astype(ut_ref.dtype)


def _inproj(x2d, mods3, mod_row_of_tile, layer, win, wvt, wts, tm, seg, vt_tile):
    tokens = x2d.shape[0]
    n_tiles = tokens // tm
    n_chunks = tokens // GLA_CHUNK
    cpt = tm // GLA_CHUNK
    assert SUB_ROWS % seg == 0
    assert tm % vt_tile == 0 and vt_tile % SUB_ROWS == 0
    of_layer = lambda shape: pl.BlockSpec((None,) + shape, lambda j: (layer,) + (0,) * len(shape))
    whole = lambda a: pl.BlockSpec(a.shape, lambda j: (0,) * a.ndim)
    return pl.pallas_call(
        functools.partial(_inproj_kernel, layer=layer, seg=seg),
        grid=(n_tiles,),
        in_specs=[
            pl.BlockSpec((tm, D_MODEL), lambda j: (j, 0)),
            pl.BlockSpec((None, 1, N_MOD * D_MODEL), lambda j: (layer * MOD_ROWS + mod_row_of_tile(j), 0, 0)),
            whole(wts["n1w"]),
            pl.BlockSpec((WIN_COLS, D_MODEL), lambda j: (0, 0)),
            pl.BlockSpec((VAL_COLS, D_MODEL), lambda j: (0, 0)),
            of_layer((2 * GATE_RANK, KEY_COLS)),
            of_layer((1, KEY_COLS)),
            whole(wts["cw"]),
            of_layer((POOL_WIDTH, POOL_WIDTH)),
            whole(wts["ps"]),
        ],
        out_specs=[
            pl.BlockSpec((tm, TOK_COLS), lambda j: (j, 0)),
            pl.BlockSpec((tm // vt_tile, VAL_COLS, vt_tile), lambda j: (j, 0, 0)),
            pl.BlockSpec((cpt, GLA_HEADS, GLA_DV, SLAB), lambda j: (j, 0, 0, 0)),
            pl.BlockSpec((cpt, 8, KEY_COLS), lambda j: (j, 0, 0)),
        ],
        out_shape=[
            jax.ShapeDtypeStruct((tokens, TOK_COLS), _BF16),
            jax.ShapeDtypeStruct((tokens // vt_tile, VAL_COLS, vt_tile), _BF16),
            jax.ShapeDtypeStruct((n_chunks, GLA_HEADS, GLA_DV, SLAB), _BF16),
            jax.ShapeDtypeStruct((n_chunks, 8, KEY_COLS), _F32),
        ],
        compiler_params=pltpu.CompilerParams(vmem_limit_bytes=VMEM_LIMIT_BYTES),
    )(x2d, mods3, wts["n1w"], win, wvt, wts["gw"], wts["gb"], wts["cw"], wts["pw"], wts["ps"])


def _scan_chunks(ut_ref, dec_ref, states, spt_ref, nc, chunk0=0, head0=0):
    lane = lax.broadcasted_iota(jnp.int32, (1, SLAB), 1)
    is_fwd = lane < GLA_DK

    def step(t, states):
        i = chunk0 + t
        j = chunk0 + nc - 1 - t
        new_states = []
        for s in range(SCAN_HEADS):
            h = head0 + s
            lanes = slice(h * SLAB, (h + 1) * SLAB)
            log_decay = jnp.where(is_fwd, dec_ref[i, 0:1, lanes], dec_ref[j, 0:1, lanes])
            log_mid = jnp.where(is_fwd, dec_ref[i, 1:2, lanes], dec_ref[j, 1:2, lanes])
            entering = (states[s] * jnp.exp2(log_mid)).astype(_BF16)
            spt_ref[i, h, :, 0:GLA_DK] = entering[:, 0:GLA_DK]
            spt_ref[j, h, :, GLA_DK:SLAB] = entering[:, GLA_DK:SLAB]
            inc = jnp.where(is_fwd, ut_ref[i, h], ut_ref[j, h]).astype(_F32)
            new_states.append(states[s] * jnp.exp2(log_decay) + inc)
        return tuple(new_states)

    return lax.fori_loop(0, nc, step, tuple(states), unroll=SCAN_UNROLL)


def _scan_kernel(ut_ref, dec_ref, s0_ref, spt_ref, sfin_ref):
    final = _scan_chunks(ut_ref, dec_ref, [s0_ref[h] for h in range(SCAN_HEADS)], spt_ref, ut_ref.shape[0])
    for h in range(SCAN_HEADS):
        sfin_ref[h] = final[h]


def _scan(ut, dec, s0, chunks_per_seq):
    n_chunks = ut.shape[0]
    bsz = n_chunks // chunks_per_seq
    return pl.pallas_call(
        _scan_kernel,
        grid=(bsz, GLA_HEADS // SCAN_HEADS),
        in_specs=[
            pl.BlockSpec((chunks_per_seq, SCAN_HEADS, GLA_DV, SLAB), lambda b, g: (b, g, 0, 0)),
            pl.BlockSpec((chunks_per_seq, 8, SCAN_HEADS * SLAB), lambda b, g: (b, 0, g)),
            pl.BlockSpec((None, SCAN_HEADS, GLA_DV, SLAB), lambda b, g: (b, g, 0, 0)),
        ],
        out_specs=[
            pl.BlockSpec((chunks_per_seq, SCAN_HEADS, GLA_DV, SLAB), lambda b, g: (b, g, 0, 0)),
            pl.BlockSpec((None, SCAN_HEADS, GLA_DV, SLAB), lambda b, g: (b, g, 0, 0)),
        ],
        out_shape=[
            jax.ShapeDtypeStruct((n_chunks, GLA_HEADS, GLA_DV, SLAB), _BF16),
            jax.ShapeDtypeStruct((bsz, GLA_HEADS, GLA_DV, SLAB), _F32),
        ],
        compiler_params=pltpu.CompilerParams(vmem_limit_bytes=VMEM_LIMIT_BYTES),
    )(ut, dec, s0)


def _mix_weight_stager(layer, wo_hbm, w1_hbm, w2_hbm, wo_ref, w1_ref, w2_ref, stage, sems):
    pieces = [(wo_hbm.at[layer], wo_ref)]
    for i in range(D_FF // D_MODEL):
        cols = slice(i * D_MODEL, (i + 1) * D_MODEL)
        pieces.append((w1_hbm.at[layer, :, cols], w1_ref.at[:, cols]))
        pieces.append((w2_hbm.at[layer, cols, :], w2_ref.at[cols, :]))
    copies = [pltpu.make_async_copy(src, stage.at[k % STAGE_SLOTS], sems.at[k % STAGE_SLOTS])
              for k, (src, _) in enumerate(pieces)]
    ready = [None]

    def need(n):
        if ready[0] is None:
            for k in range(STAGE_SLOTS):
                copies[k].start()
            ready[0] = 0
        for k in range(ready[0], min(n, len(pieces))):
            copies[k].wait()
            pieces[k][1][...] = stage[k % STAGE_SLOTS].astype(_BF16)
            if k + STAGE_SLOTS < len(pieces):
                copies[k + STAGE_SLOTS].start()
            ready[0] = k + 1

    return need


def _mix_body(x_ref, mod_ref, n2w_ref, tok_ref, vt_ref, spt_ref, gnw_ref, fnw_ref, o_ref, y_scr, wo_ref, w1_ref, w2_ref,
              need, layer, final):
    tm = x_ref.shape[0]
    qe_ref, ki_ref, ry_ref = tok_ref.at[:, TOK_QE], tok_ref.at[:, TOK_KI], tok_ref.at[:, TOK_RY]
    g1 = mod_ref[:, 2 * D_MODEL:3 * D_MODEL]
    sh2 = mod_ref[:, 3 * D_MODEL:4 * D_MODEL]
    scale2 = n2w_ref[layer:layer + 1, :] * (1.0 + mod_ref[:, 4 * D_MODEL:5 * D_MODEL])
    g2 = mod_ref[:, 5 * D_MODEL:6 * D_MODEL]
    row = lax.broadcasted_iota(jnp.int32, (GLA_CHUNK, GLA_CHUNK), 0)
    col = lax.broadcasted_iota(jnp.int32, (GLA_CHUNK, GLA_CHUNK), 1)
    lane = lax.broadcasted_iota(jnp.int32, (GLA_CHUNK, SLAB), 1)
    fwd_lane = lane < GLA_DK
    gnw = gnw_ref[...]
    need(0)

    for c in range(tm // GLA_CHUNK):
        rows = slice(c * GLA_CHUNK, (c + 1) * GLA_CHUNK)
        for h in range(GLA_HEADS):
            lanes = slice(h * SLAB, (h + 1) * SLAB)
            qe = qe_ref[rows, lanes]
            ki = ki_ref[rows, lanes]
            zero = jnp.zeros_like(ki)
            keys = jnp.concatenate([jnp.where(fwd_lane, ki, zero), jnp.where(fwd_lane, zero, ki)], axis=0)
            a2 = lax.dot_general(qe, keys, _NT, preferred_element_type=_F32)
            am = jnp.where(col <= row, a2[:, 0:GLA_CHUNK], 0.0) + jnp.where(col >= row, a2[:, GLA_CHUNK:], 0.0)
            lhs = jnp.concatenate([am.astype(_BF16), qe], axis=1)
            rhs = jnp.concatenate([vt_ref[h * GLA_DV:(h + 1) * GLA_DV, rows], spt_ref[c, h]], axis=1)
            o = lax.dot_general(lhs, rhs, _NT, preferred_element_type=_F32)
            o = o * lax.rsqrt(jnp.mean(o * o, axis=-1, keepdims=True) + EPS) * gnw
            r = ry_ref[rows, h * GLA_DV:(h + 1) * GLA_DV].astype(_F32)
            y_scr[rows, h * GLA_DV:(h + 1) * GLA_DV] = (o * (r * _sigmoid(r))).astype(_BF16)
        need(c + 1)

    need(1)
    mixed = _dot(y_scr[...], wo_ref[0:VAL_COLS, :]) + _dot(ry_ref[:, VAL_COLS:RY_COLS], wo_ref[VAL_COLS:D_MODEL, :])
    x1 = x_ref[...] + g1 * mixed
    xn = x1 * lax.rsqrt(jnp.mean(x1 * x1, axis=-1, keepdims=True) + EPS)
    h2 = (xn * scale2 + sh2).astype(_BF16)
    acc = jnp.zeros((tm, D_MODEL), _F32)
    for f in range(D_FF // FF_CHUNK):
        cols = slice(f * FF_CHUNK, (f + 1) * FF_CHUNK)
        need(1 + 2 * pl.cdiv(cols.stop, D_MODEL))
        a = jnp.maximum(_dot(h2, w1_ref[:, cols]), 0.0)
        acc = acc + _dot((a * a).astype(_BF16), w2_ref[cols, :])
    out = x1 + g2 * acc
    if final:
        out = out * lax.rsqrt(jnp.mean(out * out, axis=-1, keepdims=True) + EPS) * fnw_ref[...]
    o_ref[...] = out


def _mix_kernel(x_ref, mod_ref, n2w_ref, tok_ref, vt_ref, spt_ref, gnw_ref, wo_hbm, w1_hbm, w2_hbm, fnw_ref,
                o_ref, y_scr, wo_ref, w1_ref, w2_ref, stage, sems, *, layer, final):
    body = functools.partial(_mix_body, x_ref, mod_ref, n2w_ref, tok_ref, vt_ref, spt_ref, gnw_ref, fnw_ref, o_ref, y_scr,
                             wo_ref, w1_ref, w2_ref, layer=layer, final=final)

    def staging_step():
        body(need=_mix_weight_stager(layer, wo_hbm, w1_hbm, w2_hbm, wo_ref, w1_ref, w2_ref, stage, sems))

    first = pl.program_id(0) == 0
    pl.when(first)(staging_step)
    pl.when(jnp.logical_not(first))(functools.partial(body, need=lambda n: None))


def _mix_weight_scratch():
    return [pltpu.VMEM((D_MODEL, D_MODEL), _BF16), pltpu.VMEM((D_MODEL, D_FF), _BF16), pltpu.VMEM((D_FF, D_MODEL), _BF16),
            pltpu.VMEM((STAGE_SLOTS, D_MODEL, D_MODEL), _F32), pltpu.SemaphoreType.DMA((STAGE_SLOTS,))]


def _mix(x2d, mods3, mod_row_of_tile, layer, tok, vt, spt, wts, tm, final):
    tokens = x2d.shape[0]
    cpt = tm // GLA_CHUNK
    of_layer = lambda shape: pl.BlockSpec((None,) + shape, lambda j: (layer,) + (0,) * len(shape),
                                          pipeline_mode=pl.Buffered(1))
    return pl.pallas_call(
        functools.partial(_mix_kernel, layer=layer, final=final),
        grid=(tokens // tm,),
        in_specs=[
            pl.BlockSpec((tm, D_MODEL), lambda j: (j, 0)),
            pl.BlockSpec((None, 1, N_MOD * D_MODEL), lambda j: (layer * MOD_ROWS + mod_row_of_tile(j), 0, 0)),
            pl.BlockSpec(wts["n2w"].shape, lambda j: (0, 0), pipeline_mode=pl.Buffered(1)),
            pl.BlockSpec((tm, TOK_COLS), lambda j: (j, 0)),
            pl.BlockSpec((None, VAL_COLS, tm), lambda j: (j, 0, 0)),
            pl.BlockSpec((cpt, GLA_HEADS, GLA_DV, SLAB), lambda j: (j, 0, 0, 0)),
            of_layer((1, GLA_DV)),
            pl.BlockSpec(memory_space=pl.ANY),
            pl.BlockSpec(memory_space=pl.ANY),
            pl.BlockSpec(memory_space=pl.ANY),
            pl.BlockSpec((1, D_MODEL), lambda j: (0, 0), pipeline_mode=pl.Buffered(1)),
        ],
        out_specs=pl.BlockSpec((tm, D_MODEL), lambda j: (j, 0)),
        out_shape=jax.ShapeDtypeStruct((tokens, D_MODEL), _F32),
        scratch_shapes=[pltpu.VMEM((tm, VAL_COLS), _BF16)] + _mix_weight_scratch(),
        compiler_params=pltpu.CompilerParams(vmem_limit_bytes=VMEM_LIMIT_BYTES),
    )(x2d, mods3, wts["n2w"], tok, vt, spt, wts["gnw"], wts["wo"], wts["w1"], wts["w2"], wts["fnw"])


def _stage_in_projection(layer, w_in_t_hbm, win_ref, wvt_ref, stage, sems):
    pieces = []
    for src, dst_ref, dst in ((IN_Z, win_ref, WZ_COLS.start), (IN_QK, win_ref, WQ_COLS.start),
                              (IN_V, wvt_ref, 0), (IN_RC, win_ref, WRC_COLS.start)):
        for off in range(0, src.stop - src.start, IN_STAGE_ROWS):
            pieces.append((src.start + off, dst_ref, dst + off, min(IN_STAGE_ROWS, src.stop - src.start - off)))
    copies = [pltpu.make_async_copy(w_in_t_hbm.at[layer, s0:s0 + n, :], stage.at[k % STAGE_SLOTS, 0:n, :],
                                    sems.at[k % STAGE_SLOTS]) for k, (s0, _, _, n) in enumerate(pieces)]
    for k in range(STAGE_SLOTS):
        copies[k].start()
    win_ref[WZ_COLS.stop:WQ_COLS.start, :] = jnp.zeros((WQ_COLS.start - WZ_COLS.stop, D_MODEL), _BF16)
    for k, (_, dst_ref, d0, n) in enumerate(pieces):
        copies[k].wait()
        dst_ref[d0:d0 + n, :] = stage[k % STAGE_SLOTS, 0:n, :].astype(_BF16)
        if k + STAGE_SLOTS < len(pieces):
            copies[k + STAGE_SLOTS].start()


def _context_kernel(*refs, layer, seg, chunks_per_seq, with_mix):
    n_in = 9
    x_ref, mod_ref, n1w_ref, w_in_t_hbm = refs[0:4]
    if with_mix:
        n2w_ref, gnw_ref, wo_hbm, w1_hbm, w2_hbm, fnw_ref = refs[n_in:n_in + 6]
        sfin_ref, win_ref, wvt_ref, o_ref = refs[n_in + 6:n_in + 10]
        tok_scr, vt_scr, ut_scr, dec_scr, spt_scr, in_stage, in_sems, y_scr = refs[n_in + 10:n_in + 18]
        mix_weight_scr = refs[n_in + 18:]
    else:
        sfin_ref, win_ref, wvt_ref = refs[n_in:n_in + 3]
        tok_scr, vt_scr, ut_scr, dec_scr, spt_scr, in_stage, in_sems = refs[n_in + 3:]

    need = lambda n: None
    if with_mix:
        wo_ref, w1_ref, w2_ref, mix_stage, mix_sems = mix_weight_scr
        need = _mix_weight_stager(layer, wo_hbm, w1_hbm, w2_hbm, wo_ref, w1_ref, w2_ref, mix_stage, mix_sems)

    _stage_in_projection(layer, w_in_t_hbm, win_ref, wvt_ref, in_stage, in_sems)
    need(0)
    _inproj_kernel(x_ref, mod_ref, n1w_ref, win_ref, wvt_ref, *refs[4:n_in], tok_scr, vt_scr, ut_scr, dec_scr,
                   layer=layer, seg=seg)
    need(STAGE_SLOTS)

    sfin_ref[...] = jnp.zeros(sfin_ref.shape, _F32)
    for b in range(ut_scr.shape[0] // chunks_per_seq):
        for head0 in range(0, GLA_HEADS, SCAN_HEADS):
            start = [sfin_ref[b, head0 + s] for s in range(SCAN_HEADS)]
            final = _scan_chunks(ut_scr, dec_scr, start, spt_scr, chunks_per_seq,
                                 chunk0=b * chunks_per_seq, head0=head0)
            for s in range(SCAN_HEADS):
                sfin_ref[b, head0 + s] = final[s]

    if with_mix:
        need(2 * STAGE_SLOTS)
        _mix_body(x_ref, mod_ref, n2w_ref, tok_scr, vt_scr.at[0], spt_scr, gnw_ref, fnw_ref, o_ref, y_scr,
                  wo_ref, w1_ref, w2_ref, need=need, layer=layer, final=False)


def _context_layer(xc, mods3, mod_row, layer, wts, ctx_len, with_mix):
    rows = xc.shape[0]
    bsz = rows // ctx_len
    n_chunks = rows // GLA_CHUNK
    assert rows % SUB_ROWS == 0 and SUB_ROWS % ctx_len == 0
    of_layer = lambda shape: pl.BlockSpec((None,) + shape, lambda j: (layer,) + (0,) * len(shape),
                                          pipeline_mode=pl.Buffered(1))
    whole = lambda a: pl.BlockSpec(a.shape, lambda j: (0,) * a.ndim, pipeline_mode=pl.Buffered(1))
    in_specs = [
        pl.BlockSpec((rows, D_MODEL), lambda j: (0, 0)),
        pl.BlockSpec((None, 1, N_MOD * D_MODEL), lambda j: (layer * MOD_ROWS + mod_row, 0, 0)),
        whole(wts["n1w"]),
        pl.BlockSpec(memory_space=pl.ANY),
        of_layer((2 * GATE_RANK, KEY_COLS)),
        of_layer((1, KEY_COLS)),
        whole(wts["cw"]),
        of_layer((POOL_WIDTH, POOL_WIDTH)),
        whole(wts["ps"]),
    ]
    args = [xc, mods3, wts["n1w"], wts["w_in_t"], wts["gw"], wts["gb"], wts["cw"], wts["pw"], wts["ps"]]
    out_specs = [pl.BlockSpec((bsz, GLA_HEADS, GLA_DV, SLAB), lambda j: (0, 0, 0, 0)),
                 pl.BlockSpec((WIN_COLS, D_MODEL), lambda j: (0, 0)),
                 pl.BlockSpec((VAL_COLS, D_MODEL), lambda j: (0, 0))]
    out_shape = [jax.ShapeDtypeStruct((bsz, GLA_HEADS, GLA_DV, SLAB), _F32),
                 jax.ShapeDtypeStruct((WIN_COLS, D_MODEL), _BF16),
                 jax.ShapeDtypeStruct((VAL_COLS, D_MODEL), _BF16)]
    scratch = [
        pltpu.VMEM((rows, TOK_COLS), _BF16),
        pltpu.VMEM((1, VAL_COLS, rows), _BF16),
        pltpu.VMEM((n_chunks, GLA_HEADS, GLA_DV, SLAB), _BF16),
        pltpu.VMEM((n_chunks, 8, KEY_COLS), _F32),
        pltpu.VMEM((n_chunks, GLA_HEADS, GLA_DV, SLAB), _BF16),
        pltpu.VMEM((STAGE_SLOTS, IN_STAGE_ROWS, D_MODEL), _F32),
        pltpu.SemaphoreType.DMA((STAGE_SLOTS,)),
    ]
    if with_mix:
        in_specs += [whole(wts["n2w"]), of_layer((1, GLA_DV))] + [pl.BlockSpec(memory_space=pl.ANY)] * 3
        in_specs += [pl.BlockSpec((1, D_MODEL), lambda j: (0, 0), pipeline_mode=pl.Buffered(1))]
        args += [wts["n2w"], wts["gnw"], wts["wo"], wts["w1"], wts["w2"], wts["fnw"]]
        out_specs.append(pl.BlockSpec((rows, D_MODEL), lambda j: (0, 0)))
        out_shape.append(jax.ShapeDtypeStruct((rows, D_MODEL), _F32))
        scratch += [pltpu.VMEM((rows, VAL_COLS), _BF16)] + _mix_weight_scratch()
    outs = pl.pallas_call(
        functools.partial(_context_kernel, layer=layer, seg=ctx_len, chunks_per_seq=ctx_len // GLA_CHUNK,
                          with_mix=with_mix),
        grid=(1,),
        in_specs=in_specs,
        out_specs=out_specs,
        out_shape=out_shape,
        scratch_shapes=scratch,
        compiler_params=pltpu.CompilerParams(vmem_limit_bytes=VMEM_LIMIT_BYTES),
    )(*args)
    return tuple(outs) if with_mix else tuple(outs) + (None,)


def _prepare_weights(norm1_w, norm2_w, w_in, gate_w, gate_b, gla_norm_w, conv_w, pool_w, pool_scale, w_out,
                     w_mlp1, w_mlp2, final_norm_w):
    depth = w_in.shape[0]
    assert w_in.shape[2] == IN_COLS
    zeros = jnp.zeros((depth, GATE_RANK, GLA_HEADS, GLA_DK), _F32)
    gf = gate_w[:, 0].reshape(depth, GATE_RANK, GLA_HEADS, GLA_DK)
    gb = gate_w[:, 1].reshape(depth, GATE_RANK, GLA_HEADS, GLA_DK)
    gw = jnp.concatenate([jnp.concatenate([gf, zeros], -1), jnp.concatenate([zeros, gb], -1)], 1)
    gbias = jnp.concatenate([gate_b[:, 0].reshape(depth, GLA_HEADS, GLA_DK),
                             gate_b[:, 1].reshape(depth, GLA_HEADS, GLA_DK)], -1)
    n_groups = POOL_WIDTH // POOL_GROUP
    same_group = jnp.eye(n_groups, dtype=_F32)[None, :, None, :, None]
    pw = (pool_w[:, :, :, None, :] * same_group).reshape(depth, POOL_WIDTH, POOL_WIDTH)
    return {
        "n1w": norm1_w,
        "n2w": norm2_w,
        "w_in_t": jnp.swapaxes(w_in, 1, 2),
        "gw": gw.reshape(depth, 2 * GATE_RANK, KEY_COLS).astype(_BF16),
        "gb": gbias.reshape(depth, 1, KEY_COLS),
        "gnw": gla_norm_w.reshape(depth, 1, GLA_DV),
        "cw": jnp.swapaxes(conv_w, 0, 1),
        "pw": pw.astype(_BF16),
        "ps": pool_scale,
        "wo": w_out,
        "w1": w_mlp1,
        "w2": w_mlp2,
        "fnw": final_norm_w.reshape(1, D_MODEL),
    }


def kernel(x, c, ctx, c_ctx, w_mod, b_mod, norm1_w, norm2_w, w_in, gla_gate_w, gla_gate_b, gla_norm_w, conv_w, pool_w, pool_scale, w_out, w_mlp1, w_mlp2, final_norm_w):
    bsz, n, _ = x.shape
    ctx_len = ctx.shape[1]
    depth = w_in.shape[0]
    assert bsz + 1 <= MOD_ROWS and n % LATENT_IN_TILE == 0 and SUB_ROWS % ctx_len == 0
    ctx_row = bsz

    cv = jnp.concatenate([c, c_ctx[None, :], jnp.zeros((MOD_ROWS - bsz - 1, D_MODEL), _F32)], axis=0)
    mods3 = _adaln(cv, w_mod, b_mod)
    wts = _prepare_weights(norm1_w, norm2_w, w_in, gla_gate_w, gla_gate_b, gla_norm_w, conv_w, pool_w, pool_scale,
                           w_out, w_mlp1, w_mlp2, final_norm_w)

    xl = x.reshape(bsz * n, D_MODEL)
    xc = ctx.reshape(bsz * ctx_len, D_MODEL)
    lat_in_row = lambda j: j // (n // LATENT_IN_TILE)
    lat_mix_row = lambda j: j // (n // LATENT_MIX_TILE)

    for l in range(depth):
        last = l == depth - 1
        s_ctx, win, wvt, xc = _context_layer(xc, mods3, ctx_row, l, wts, ctx_len, with_mix=not last)

        tok, vt, ut, dec = _inproj(xl, mods3, lat_in_row, l, win, wvt, wts, LATENT_IN_TILE, GRID_W, LATENT_MIX_TILE)
        spt, _ = _scan(ut, dec, s_ctx, n // GLA_CHUNK)
        xl = _mix(xl, mods3, lat_mix_row, l, tok, vt, spt, wts, LATENT_MIX_TILE, last)
    return xl.reshape(bsz, n, D_MODEL)
```

```python
import functools

import jax
import jax.numpy as jnp
from jax import lax
from jax.experimental import pallas as pl
from jax.experimental.pallas import tpu as pltpu

D_MODEL = 1024
GLA_HEADS = 4
GLA_DK = 64
GLA_DV = 128
SLAB = 2 * GLA_DK
KEY_COLS = GLA_HEADS * SLAB
VAL_COLS = GLA_HEADS * GLA_DV
GATE_RANK = 16
GATE_TAU = 16.0
LOG2_E = 1.4426950408889634
GLA_CHUNK = 128
SUB_ROWS = 256
LATENT_IN_TILE = 1024
LATENT_MIX_TILE = 512
CONV_WIDTH = 256
POOL_WIDTH = 256
POOL_GROUP = 64
POOL_HALF_WINDOWS = (1, 2, 4, 8)
assert all(h & (h - 1) == 0 for h in POOL_HALF_WINDOWS)
D_FF = 4096
FF_CHUNK = 1024
N_MOD = 6
EPS = 1e-6
GRID_W = 64
RC_COLS = VAL_COLS + 3 * CONV_WIDTH + POOL_WIDTH
RY_COLS = VAL_COLS + CONV_WIDTH + POOL_WIDTH
TOK_QE = slice(0, KEY_COLS)
TOK_KI = slice(KEY_COLS, 2 * KEY_COLS)
TOK_RY = slice(2 * KEY_COLS, 2 * KEY_COLS + RY_COLS)
TOK_COLS = TOK_RY.stop
WZ_COLS = slice(0, 2 * GATE_RANK)
WQ_COLS = slice(128, 128 + GLA_HEADS * GLA_DK)
WK_COLS = slice(WQ_COLS.stop, WQ_COLS.stop + GLA_HEADS * GLA_DK)
WRC_COLS = slice(WK_COLS.stop, WK_COLS.stop + RC_COLS)
WIN_COLS = WRC_COLS.stop
IN_QK = slice(0, 2 * GLA_HEADS * GLA_DK)
IN_V = slice(IN_QK.stop, IN_QK.stop + VAL_COLS)
IN_Z = slice(IN_V.stop, IN_V.stop + 2 * GATE_RANK)
IN_RC = slice(IN_Z.stop, IN_Z.stop + RC_COLS)
IN_COLS = IN_RC.stop
IN_STAGE_ROWS = 512
SCAN_UNROLL = 2
SCAN_HEADS = 2
MOD_ROWS = 8
ADALN_COLS = 1536
VMEM_LIMIT_BYTES = 56 * 1024 * 1024
STAGE_SLOTS = 4
STAGE_ROWS = 512
GLA_PHASE_PIECES = 4
PIECES_PER_BLOCK = D_MODEL // STAGE_ROWS
IN_STAGE_SLOTS = 2

_NT = (((1,), (1,)), ((), ()))
_BF16 = jnp.bfloat16
_F32 = jnp.float32


def _dot(a, b):
    return jnp.dot(a, b, preferred_element_type=_F32)


def _sigmoid(x):
    return 1.0 / (1.0 + jnp.exp(-x))


def _adaln_kernel(cv_ref, w_ref, b_ref, o_ref):
    cv = cv_ref[...]
    s = (cv * _sigmoid(cv)).astype(_BF16)
    mod = _dot(s, w_ref[...].astype(_BF16)) + b_ref[pl.ds(pl.program_id(0), 1), :]
    for r in range(MOD_ROWS):
        o_ref[r] = mod[r:r + 1, :]


def _adaln(cv, w_mod, b_mod):
    depth = w_mod.shape[0]
    cols = w_mod.shape[2]
    return pl.pallas_call(
        _adaln_kernel,
        grid=(depth, cols // ADALN_COLS),
        in_specs=[
            pl.BlockSpec((MOD_ROWS, D_MODEL), lambda l, j: (0, 0)),
            pl.BlockSpec((None, D_MODEL, ADALN_COLS), lambda l, j: (l, 0, j)),
            pl.BlockSpec((depth, ADALN_COLS), lambda l, j: (0, j)),
        ],
        out_specs=pl.BlockSpec((MOD_ROWS, 1, ADALN_COLS), lambda l, j: (l, 0, j)),
        out_shape=jax.ShapeDtypeStruct((depth * MOD_ROWS, 1, cols), _F32),
        compiler_params=pltpu.CompilerParams(vmem_limit_bytes=VMEM_LIMIT_BYTES),
    )(cv, w_mod, b_mod)


def _per_direction(t):
    lane = lax.broadcasted_iota(jnp.int32, (1, SLAB), 1)
    first = lane < GLA_DK
    slabs = []
    for p in range(GLA_HEADS // 2):
        pair = t[:, p * SLAB:(p + 1) * SLAB]
        swapped = pltpu.roll(pair, GLA_DK, 1)
        slabs += [jnp.where(first, pair, swapped), jnp.where(first, swapped, pair)]
    return jnp.concatenate(slabs, axis=1)


def _conv_and_pool(rc, taps, pw_ref, pool_scale, seg):
    rows = rc.shape[0]
    pos = lax.broadcasted_iota(jnp.int32, (rows, 1), 0) % seg

    def in_segment(v, d):
        return jnp.where((pos + d >= 0) & (pos + d < seg), pltpu.roll(v, (-d) % rows, 0), 0.0)

    c0 = 0
    cb = rc[:, c0:c0 + CONV_WIDTH]
    zc = rc[:, c0 + CONV_WIDTH:c0 + 2 * CONV_WIDTH] * rc[:, c0 + 2 * CONV_WIDTH:c0 + 3 * CONV_WIDTH]
    yc = cb * (taps[0] * in_segment(zc, -1) + taps[1] * zc + taps[2] * in_segment(zc, 1))

    p0 = 3 * CONV_WIDTH
    lane128 = lax.broadcasted_iota(jnp.int32, (1, 2 * POOL_GROUP), 1)
    pooled = []
    for blk in range(POOL_WIDTH // (2 * POOL_GROUP)):
        u = rc[:, p0 + blk * 2 * POOL_GROUP:p0 + (blk + 1) * 2 * POOL_GROUP]
        h_lo, h_hi = POOL_HALF_WINDOWS[2 * blk], POOL_HALF_WINDOWS[2 * blk + 1]
        half = jnp.where(lane128 < POOL_GROUP, h_lo, h_hi)
        ahead, behind, window_sums = u, in_segment(u, -1), {}
        for level in range(h_hi.bit_length()):
            h = 1 << level
            if level > 0:
                ahead = ahead + in_segment(ahead, h // 2)
                behind = behind + in_segment(behind, -(h // 2))
            if h in (h_lo, h_hi):
                window_sums[h] = ahead + behind
        acc = jnp.where(lane128 < POOL_GROUP, window_sums[h_lo], window_sums[h_hi])
        cnt = (jnp.minimum(pos + half, seg) - jnp.maximum(pos - half, 0)).astype(_F32)
        pooled.append(acc / cnt - u)
    pool_in = jnp.concatenate(pooled, axis=1).astype(_BF16)
    return yc, _dot(pool_in, pw_ref[...]) * pool_scale


def _inproj_kernel(x_ref, mod_ref, n1w_ref, win_ref, wvt_ref, gw_ref, gb_ref, cw_ref, pw_ref, ps_ref,
                   tok_ref, vt_ref, ut_ref, dec_ref, *, layer, seg):
    this_layer = slice(layer, layer + 1)
    taps = [cw_ref[t, this_layer, :] for t in range(cw_ref.shape[0])]
    pool_scale = ps_ref[this_layer, :]
    qe_ref, ki_ref, ry_ref = tok_ref.at[:, TOK_QE], tok_ref.at[:, TOK_KI], tok_ref.at[:, TOK_RY]
    vt_tile = vt_ref.shape[2]
    tm = x_ref.shape[0]
    sh1 = mod_ref[:, 0:D_MODEL]
    scale1 = n1w_ref[this_layer, :] * (1.0 + mod_ref[:, D_MODEL:2 * D_MODEL])
    row = lax.broadcasted_iota(jnp.int32, (GLA_CHUNK, GLA_CHUNK), 0)
    col = lax.broadcasted_iota(jnp.int32, (GLA_CHUNK, GLA_CHUNK), 1)
    scan_op = jnp.where(jnp.concatenate([col <= row, col >= row], axis=1), 1.0 / GATE_TAU, 0.0).astype(_BF16)
    lane = lax.broadcasted_iota(jnp.int32, (1, KEY_COLS), 1)
    is_fwd = (lane % SLAB) < GLA_DK
    mid = GLA_CHUNK // 2

    n_sub = tm // SUB_ROWS
    cps = SUB_ROWS // GLA_CHUNK
    sub_rows = [slice(s * SUB_ROWS, (s + 1) * SUB_ROWS) for s in range(n_sub)]
    hbs, projs, gates, q2s, k2s = [], [], [], [], []
    for s in range(n_sub):
        x = x_ref[sub_rows[s], :]
        xn = x * lax.rsqrt(jnp.mean(x * x, axis=-1, keepdims=True) + EPS)
        hbs.append((xn * scale1 + sh1).astype(_BF16))
    for s in range(n_sub):
        proj = lax.dot_general(hbs[s], win_ref[...], _NT, preferred_element_type=_F32)
        projs.append(proj)
        pre = _dot(proj[:, WZ_COLS].astype(_BF16), gw_ref[...]) + gb_ref[...]
        g = jnp.minimum(pre, 0.0) * LOG2_E - jnp.log2(1.0 + jnp.exp2(jnp.abs(pre) * (-LOG2_E)))
        gates.append(g.astype(_BF16))
        q2s.append(_per_direction(proj[:, WQ_COLS] * (GLA_DK ** -0.5)))
        k2s.append(_per_direction(proj[:, WK_COLS]))

    decays = []
    for s in range(n_sub):
        for cl in range(cps):
            g = gates[s][cl * GLA_CHUNK:(cl + 1) * GLA_CHUNK]
            zero = jnp.zeros_like(g)
            by_direction = jnp.concatenate([jnp.where(is_fwd, g, zero), jnp.where(is_fwd, zero, g)], axis=0)
            decays.append(_dot(scan_op, by_direction))

    vtbs = []
    for s in range(n_sub):
        vtb = lax.dot_general(wvt_ref[...], hbs[s], _NT, preferred_element_type=_F32).astype(_BF16)
        first = s * SUB_ROWS
        vt_ref[first // vt_tile, :, first % vt_tile:first % vt_tile + SUB_ROWS] = vtb
        vtbs.append(vtb)
        rc = projs[s][:, WRC_COLS]
        yc, yp = _conv_and_pool(rc[:, VAL_COLS:RC_COLS], taps, pw_ref, pool_scale, seg)
        ry_ref[sub_rows[s], 0:VAL_COLS] = rc[:, 0:VAL_COLS].astype(_BF16)
        ry_ref[sub_rows[s], VAL_COLS:VAL_COLS + CONV_WIDTH] = yc.astype(_BF16)
        ry_ref[sub_rows[s], VAL_COLS + CONV_WIDTH:RY_COLS] = yp.astype(_BF16)

    for s in range(n_sub):
        for cl in range(cps):
            c = s * cps + cl
            lrows = slice(cl * GLA_CHUNK, (cl + 1) * GLA_CHUNK)
            rows = slice(c * GLA_CHUNK, (c + 1) * GLA_CHUNK)
            e = decays[c]
            total = jnp.where(is_fwd, e[GLA_CHUNK - 1:GLA_CHUNK, :], e[0:1, :])
            mvec = jnp.where(is_fwd, e[mid - 1:mid, :], e[mid:mid + 1, :])
            bm = e - mvec
            qe_ref[rows, :] = (q2s[s][lrows] * jnp.exp2(bm)).astype(_BF16)
            k_mid = k2s[s][lrows] * jnp.exp2(-bm)
            ki_ref[rows, :] = k_mid.astype(_BF16)
            kd = (k_mid * jnp.exp2(total - mvec)).astype(_BF16)
            dec_ref[c] = jnp.concatenate([total, mvec, jnp.zeros((6, KEY_COLS), _F32)], axis=0)
            for h in range(GLA_HEADS):
                vt_h = vtbs[s][h * GLA_DV:(h + 1) * GLA_DV, lrows]
                ut_ref[c, h] = _dot(vt_h, kd[:, h * SLAB:(h + 1) * SLAB]).astype(ut_ref.dtype)


def _inproj(x2d, mods3, mod_row_of_tile, layer, win, wvt, wts, tm, seg, vt_tile):
    tokens = x2d.shape[0]
    n_tiles = tokens // tm
    n_chunks = tokens // GLA_CHUNK
    cpt = tm // GLA_CHUNK
    assert SUB_ROWS % seg == 0
    assert tm % vt_tile == 0 and vt_tile % SUB_ROWS == 0
    of_layer = lambda shape: pl.BlockSpec((None,) + shape, lambda j: (layer,) + (0,) * len(shape))
    whole = lambda a: pl.BlockSpec(a.shape, lambda j: (0,) * a.ndim)
    return pl.pallas_call(
        functools.partial(_inproj_kernel, layer=layer, seg=seg),
        grid=(n_tiles,),
        in_specs=[
            pl.BlockSpec((tm, D_MODEL), lambda j: (j, 0)),
            pl.BlockSpec((None, 1, N_MOD * D_MODEL), lambda j: (layer * MOD_ROWS + mod_row_of_tile(j), 0, 0)),
            whole(wts["n1w"]),
            pl.BlockSpec((WIN_COLS, D_MODEL), lambda j: (0, 0)),
            pl.BlockSpec((VAL_COLS, D_MODEL), lambda j: (0, 0)),
            of_layer((2 * GATE_RANK, KEY_COLS)),
            of_layer((1, KEY_COLS)),
            whole(wts["cw"]),
            of_layer((POOL_WIDTH, POOL_WIDTH)),
            whole(wts["ps"]),
        ],
        out_specs=[
            pl.BlockSpec((tm, TOK_COLS), lambda j: (j, 0)),
            pl.BlockSpec((tm // vt_tile, VAL_COLS, vt_tile), lambda j: (j, 0, 0)),
            pl.BlockSpec((cpt, GLA_HEADS, GLA_DV, SLAB), lambda j: (j, 0, 0, 0)),
            pl.BlockSpec((cpt, 8, KEY_COLS), lambda j: (j, 0, 0)),
        ],
        out_shape=[
            jax.ShapeDtypeStruct((tokens, TOK_COLS), _BF16),
            jax.ShapeDtypeStruct((tokens // vt_tile, VAL_COLS, vt_tile), _BF16),
            jax.ShapeDtypeStruct((n_chunks, GLA_HEADS, GLA_DV, SLAB), _BF16),
            jax.ShapeDtypeStruct((n_chunks, 8, KEY_COLS), _F32),
        ],
        compiler_params=pltpu.CompilerParams(vmem_limit_bytes=VMEM_LIMIT_BYTES),
    )(x2d, mods3, wts["n1w"], win, wvt, wts["gw"], wts["gb"], wts["cw"], wts["pw"], wts["ps"])


def _scan_chunks(ut_ref, dec_ref, states, spt_ref, nc, chunk0=0, head0=0):
    lane = lax.broadcasted_iota(jnp.int32, (1, SLAB), 1)
    is_fwd = lane < GLA_DK

    def step(t, states):
        i = chunk0 + t
        j = chunk0 + nc - 1 - t
        new_states = []
        for s in range(SCAN_HEADS):
            h = head0 + s
            lanes = slice(h * SLAB, (h + 1) * SLAB)
            log_decay = jnp.where(is_fwd, dec_ref[i, 0:1, lanes], dec_ref[j, 0:1, lanes])
            log_mid = jnp.where(is_fwd, dec_ref[i, 1:2, lanes], dec_ref[j, 1:2, lanes])
            entering = (states[s] * jnp.exp2(log_mid)).astype(_BF16)
            spt_ref[i, h, :, 0:GLA_DK] = entering[:, 0:GLA_DK]
            spt_ref[j, h, :, GLA_DK:SLAB] = entering[:, GLA_DK:SLAB]
            inc = jnp.where(is_fwd, ut_ref[i, h], ut_ref[j, h]).astype(_F32)
            new_states.append(states[s] * jnp.exp2(log_decay) + inc)
        return tuple(new_states)

    return lax.fori_loop(0, nc, step, tuple(states), unroll=SCAN_UNROLL)


def _scan_kernel(ut_ref, dec_ref, s0_ref, spt_ref, sfin_ref):
    final = _scan_chunks(ut_ref, dec_ref, [s0_ref[h] for h in range(SCAN_HEADS)], spt_ref, ut_ref.shape[0])
    for h in range(SCAN_HEADS):
        sfin_ref[h] = final[h]


def _scan(ut, dec, s0, chunks_per_seq):
    n_chunks = ut.shape[0]
    bsz = n_chunks // chunks_per_seq
    return pl.pallas_call(
        _scan_kernel,
        grid=(bsz, GLA_HEADS // SCAN_HEADS),
        in_specs=[
            pl.BlockSpec((chunks_per_seq, SCAN_HEADS, GLA_DV, SLAB), lambda b, g: (b, g, 0, 0)),
            pl.BlockSpec((chunks_per_seq, 8, SCAN_HEADS * SLAB), lambda b, g: (b, 0, g)),
            pl.BlockSpec((None, SCAN_HEADS, GLA_DV, SLAB), lambda b, g: (b, g, 0, 0)),
        ],
        out_specs=[
            pl.BlockSpec((chunks_per_seq, SCAN_HEADS, GLA_DV, SLAB), lambda b, g: (b, g, 0, 0)),
            pl.BlockSpec((None, SCAN_HEADS, GLA_DV, SLAB), lambda b, g: (b, g, 0, 0)),
        ],
        out_shape=[
            jax.ShapeDtypeStruct((n_chunks, GLA_HEADS, GLA_DV, SLAB), _BF16),
            jax.ShapeDtypeStruct((bsz, GLA_HEADS, GLA_DV, SLAB), _F32),
        ],
        compiler_params=pltpu.CompilerParams(vmem_limit_bytes=VMEM_LIMIT_BYTES),
    )(ut, dec, s0)


def _mix_weight_stager(layer, wo_hbm, w1_hbm, w2_hbm, wo_ref, w1_ref, w2_ref, stage, sems):
    blocks = [(wo_hbm.at[layer], wo_ref)]
    for i in range(D_FF // D_MODEL):
        cols = slice(i * D_MODEL, (i + 1) * D_MODEL)
        blocks.append((w1_hbm.at[layer, :, cols], w1_ref.at[:, cols]))
        blocks.append((w2_hbm.at[layer, cols, :], w2_ref.at[cols, :]))
    pieces = [(src.at[p * STAGE_ROWS:(p + 1) * STAGE_ROWS, :], dst.at[p * STAGE_ROWS:(p + 1) * STAGE_ROWS, :])
              for src, dst in blocks for p in range(PIECES_PER_BLOCK)]
    copies = [pltpu.make_async_copy(src, stage.at[k % STAGE_SLOTS], sems.at[k % STAGE_SLOTS])
              for k, (src, _) in enumerate(pieces)]
    ready = [None]

    def need(n):
        if ready[0] is None:
            for k in range(STAGE_SLOTS):
                copies[k].start()
            ready[0] = 0
        for k in range(ready[0], min(n, len(pieces))):
            copies[k].wait()
            pieces[k][1][...] = stage[k % STAGE_SLOTS].astype(_BF16)
            if k + STAGE_SLOTS < len(pieces):
                copies[k + STAGE_SLOTS].start()
            ready[0] = k + 1

    return need


def _mix_body(x_ref, mod_ref, n2w_ref, tok_ref, vt_ref, spt_ref, gnw_ref, fnw_ref, o_ref, y_scr, wo_ref, w1_ref, w2_ref,
              need, layer, final):
    tm = x_ref.shape[0]
    qe_ref, ki_ref, ry_ref = tok_ref.at[:, TOK_QE], tok_ref.at[:, TOK_KI], tok_ref.at[:, TOK_RY]
    g1 = mod_ref[:, 2 * D_MODEL:3 * D_MODEL]
    sh2 = mod_ref[:, 3 * D_MODEL:4 * D_MODEL]
    scale2 = n2w_ref[layer:layer + 1, :] * (1.0 + mod_ref[:, 4 * D_MODEL:5 * D_MODEL])
    g2 = mod_ref[:, 5 * D_MODEL:6 * D_MODEL]
    row = lax.broadcasted_iota(jnp.int32, (GLA_CHUNK, GLA_CHUNK), 0)
    col = lax.broadcasted_iota(jnp.int32, (GLA_CHUNK, GLA_CHUNK), 1)
    lane = lax.broadcasted_iota(jnp.int32, (GLA_CHUNK, SLAB), 1)
    fwd_lane = lane < GLA_DK
    gnw = gnw_ref[...]
    need(0)

    for c in range(tm // GLA_CHUNK):
        rows = slice(c * GLA_CHUNK, (c + 1) * GLA_CHUNK)
        for h in range(GLA_HEADS):
            lanes = slice(h * SLAB, (h + 1) * SLAB)
            qe = qe_ref[rows, lanes]
            ki = ki_ref[rows, lanes]
            zero = jnp.zeros_like(ki)
            keys = jnp.concatenate([jnp.where(fwd_lane, ki, zero), jnp.where(fwd_lane, zero, ki)], axis=0)
            a2 = lax.dot_general(qe, keys, _NT, preferred_element_type=_F32)
            am = jnp.where(col <= row, a2[:, 0:GLA_CHUNK], 0.0) + jnp.where(col >= row, a2[:, GLA_CHUNK:], 0.0)
            lhs = jnp.concatenate([am.astype(_BF16), qe], axis=1)
            rhs = jnp.concatenate([vt_ref[h * GLA_DV:(h + 1) * GLA_DV, rows], spt_ref[c, h]], axis=1)
            o = lax.dot_general(lhs, rhs, _NT, preferred_element_type=_F32)
            o = o * lax.rsqrt(jnp.mean(o * o, axis=-1, keepdims=True) + EPS) * gnw
            r = ry_ref[rows, h * GLA_DV:(h + 1) * GLA_DV].astype(_F32)
            y_scr[rows, h * GLA_DV:(h + 1) * GLA_DV] = (o * (r * _sigmoid(r))).astype(_BF16)
        need((c + 1) * GLA_PHASE_PIECES // (tm // GLA_CHUNK))

    need(PIECES_PER_BLOCK)
    mixed = _dot(y_scr[...], wo_ref[0:VAL_COLS, :]) + _dot(ry_ref[:, VAL_COLS:RY_COLS], wo_ref[VAL_COLS:D_MODEL, :])
    x1 = x_ref[...] + g1 * mixed
    xn = x1 * lax.rsqrt(jnp.mean(x1 * x1, axis=-1, keepdims=True) + EPS)
    h2 = (xn * scale2 + sh2).astype(_BF16)
    acc = jnp.zeros((tm, D_MODEL), _F32)
    for f in range(D_FF // FF_CHUNK):
        cols = slice(f * FF_CHUNK, (f + 1) * FF_CHUNK)
        blocks_before = 2 * pl.cdiv(cols.stop, D_MODEL)
        need(PIECES_PER_BLOCK * blocks_before)
        a = jnp.maximum(_dot(h2, w1_ref[:, cols]), 0.0)
        need(PIECES_PER_BLOCK * (blocks_before + 1))
        acc = acc + _dot((a * a).astype(_BF16), w2_ref[cols, :])
    out = x1 + g2 * acc
    if final:
        out = out * lax.rsqrt(jnp.mean(out * out, axis=-1, keepdims=True) + EPS) * fnw_ref[...]
    o_ref[...] = out


def _mix_kernel(x_ref, mod_ref, n2w_ref, tok_ref, vt_ref, spt_ref, gnw_ref, wo_hbm, w1_hbm, w2_hbm, fnw_ref,
                o_ref, y_scr, wo_ref, w1_ref, w2_ref, stage, sems, *, layer, final):
    body = functools.partial(_mix_body, x_ref, mod_ref, n2w_ref, tok_ref, vt_ref, spt_ref, gnw_ref, fnw_ref, o_ref, y_scr,
                             wo_ref, w1_ref, w2_ref, layer=layer, final=final)

    def staging_step():
        body(need=_mix_weight_stager(layer, wo_hbm, w1_hbm, w2_hbm, wo_ref, w1_ref, w2_ref, stage, sems))

    first = pl.program_id(0) == 0
    pl.when(first)(staging_step)
    pl.when(jnp.logical_not(first))(functools.partial(body, need=lambda n: None))


def _mix_weight_scratch():
    return [pltpu.VMEM((D_MODEL, D_MODEL), _BF16), pltpu.VMEM((D_MODEL, D_FF), _BF16), pltpu.VMEM((D_FF, D_MODEL), _BF16),
            pltpu.VMEM((STAGE_SLOTS, STAGE_ROWS, D_MODEL), _F32), pltpu.SemaphoreType.DMA((STAGE_SLOTS,))]


def _mix(x2d, mods3, mod_row_of_tile, layer, tok, vt, spt, wts, tm, final):
    tokens = x2d.shape[0]
    cpt = tm // GLA_CHUNK
    of_layer = lambda shape: pl.BlockSpec((None,) + shape, lambda j: (layer,) + (0,) * len(shape),
                                          pipeline_mode=pl.Buffered(1))
    return pl.pallas_call(
        functools.partial(_mix_kernel, layer=layer, final=final),
        grid=(tokens // tm,),
        in_specs=[
            pl.BlockSpec((tm, D_MODEL), lambda j: (j, 0)),
            pl.BlockSpec((None, 1, N_MOD * D_MODEL), lambda j: (layer * MOD_ROWS + mod_row_of_tile(j), 0, 0)),
            pl.BlockSpec(wts["n2w"].shape, lambda j: (0, 0), pipeline_mode=pl.Buffered(1)),
            pl.BlockSpec((tm, TOK_COLS), lambda j: (j, 0)),
            pl.BlockSpec((None, VAL_COLS, tm), lambda j: (j, 0, 0)),
            pl.BlockSpec((cpt, GLA_HEADS, GLA_DV, SLAB), lambda j: (j, 0, 0, 0)),
            of_layer((1, GLA_DV)),
            pl.BlockSpec(memory_space=pl.ANY),
            pl.BlockSpec(memory_space=pl.ANY),
            pl.BlockSpec(memory_space=pl.ANY),
            pl.BlockSpec((1, D_MODEL), lambda j: (0, 0), pipeline_mode=pl.Buffered(1)),
        ],
        out_specs=pl.BlockSpec((tm, D_MODEL), lambda j: (j, 0)),
        out_shape=jax.ShapeDtypeStruct((tokens, D_MODEL), _F32),
        scratch_shapes=[pltpu.VMEM((tm, VAL_COLS), _BF16)] + _mix_weight_scratch(),
        compiler_params=pltpu.CompilerParams(vmem_limit_bytes=VMEM_LIMIT_BYTES),
    )(x2d, mods3, wts["n2w"], tok, vt, spt, wts["gnw"], wts["wo"], wts["w1"], wts["w2"], wts["fnw"])


def _stage_in_projection(layer, w_in_t_hbm, win_ref, wvt_ref, stage, sems):
    pieces = []
    for src, dst_ref, dst in ((IN_Z, win_ref, WZ_COLS.start), (IN_QK, win_ref, WQ_COLS.start),
                              (IN_V, wvt_ref, 0), (IN_RC, win_ref, WRC_COLS.start)):
        for off in range(0, src.stop - src.start, IN_STAGE_ROWS):
            pieces.append((src.start + off, dst_ref, dst + off, min(IN_STAGE_ROWS, src.stop - src.start - off)))
    copies = [pltpu.make_async_copy(w_in_t_hbm.at[layer, s0:s0 + n, :], stage.at[k % IN_STAGE_SLOTS, 0:n, :],
                                    sems.at[k % IN_STAGE_SLOTS]) for k, (s0, _, _, n) in enumerate(pieces)]
    for k in range(IN_STAGE_SLOTS):
        copies[k].start()
    win_ref[WZ_COLS.stop:WQ_COLS.start, :] = jnp.zeros((WQ_COLS.start - WZ_COLS.stop, D_MODEL), _BF16)
    for k, (_, dst_ref, d0, n) in enumerate(pieces):
        copies[k].wait()
        dst_ref[d0:d0 + n, :] = stage[k % IN_STAGE_SLOTS, 0:n, :].astype(_BF16)
        if k + IN_STAGE_SLOTS < len(pieces):
            copies[k + IN_STAGE_SLOTS].start()


def _context_kernel(*refs, layer, seg, chunks_per_seq, with_mix):
    n_in = 9
    x_ref, mod_ref, n1w_ref, w_in_t_hbm = refs[0:4]
    if with_mix:
        n2w_ref, gnw_ref, wo_hbm, w1_hbm, w2_hbm, fnw_ref = refs[n_in:n_in + 6]
        sfin_ref, win_ref, wvt_ref, o_ref = refs[n_in + 6:n_in + 10]
        tok_scr, vt_scr, ut_scr, dec_scr, spt_scr, in_stage, in_sems, y_scr = refs[n_in + 10:n_in + 18]
        mix_weight_scr = refs[n_in + 18:]
    else:
        sfin_ref, win_ref, wvt_ref = refs[n_in:n_in + 3]
        tok_scr, vt_scr, ut_scr, dec_scr, spt_scr, in_stage, in_sems = refs[n_in + 3:]

    need = lambda n: None
    if with_mix:
        wo_ref, w1_ref, w2_ref, mix_stage, mix_sems = mix_weight_scr
        need = _mix_weight_stager(layer, wo_hbm, w1_hbm, w2_hbm, wo_ref, w1_ref, w2_ref, mix_stage, mix_sems)

    _stage_in_projection(layer, w_in_t_hbm, win_ref, wvt_ref, in_stage, in_sems)
    need(0)
    _inproj_kernel(x_ref, mod_ref, n1w_ref, win_ref, wvt_ref, *refs[4:n_in], tok_scr, vt_scr, ut_scr, dec_scr,
                   layer=layer, seg=seg)
    need(STAGE_SLOTS)

    sfin_ref[...] = jnp.zeros(sfin_ref.shape, _F32)
    for b in range(ut_scr.shape[0] // chunks_per_seq):
        for head0 in range(0, GLA_HEADS, SCAN_HEADS):
            start = [sfin_ref[b, head0 + s] for s in range(SCAN_HEADS)]
            final = _scan_chunks(ut_scr, dec_scr, start, spt_scr, chunks_per_seq,
                                 chunk0=b * chunks_per_seq, head0=head0)
            for s in range(SCAN_HEADS):
                sfin_ref[b, head0 + s] = final[s]

    if with_mix:
        need(2 * STAGE_SLOTS)
        _mix_body(x_ref, mod_ref, n2w_ref, tok_scr, vt_scr.at[0], spt_scr, gnw_ref, fnw_ref, o_ref, y_scr,
                  wo_ref, w1_ref, w2_ref, need=need, layer=layer, final=False)


def _context_layer(xc, mods3, mod_row, layer, wts, ctx_len, with_mix):
    rows = xc.shape[0]
    bsz = rows // ctx_len
    n_chunks = rows // GLA_CHUNK
    assert rows % SUB_ROWS == 0 and SUB_ROWS % ctx_len == 0
    of_layer = lambda shape: pl.BlockSpec((None,) + shape, lambda j: (layer,) + (0,) * len(shape),
                                          pipeline_mode=pl.Buffered(1))
    whole = lambda a: pl.BlockSpec(a.shape, lambda j: (0,) * a.ndim, pipeline_mode=pl.Buffered(1))
    in_specs = [
        pl.BlockSpec((rows, D_MODEL), lambda j: (0, 0)),
        pl.BlockSpec((None, 1, N_MOD * D_MODEL), lambda j: (layer * MOD_ROWS + mod_row, 0, 0)),
        whole(wts["n1w"]),
        pl.BlockSpec(memory_space=pl.ANY),
        of_layer((2 * GATE_RANK, KEY_COLS)),
        of_layer((1, KEY_COLS)),
        whole(wts["cw"]),
        of_layer((POOL_WIDTH, POOL_WIDTH)),
        whole(wts["ps"]),
    ]
    args = [xc, mods3, wts["n1w"], wts["w_in_t"], wts["gw"], wts["gb"], wts["cw"], wts["pw"], wts["ps"]]
    out_specs = [pl.BlockSpec((bsz, GLA_HEADS, GLA_DV, SLAB), lambda j: (0, 0, 0, 0)),
                 pl.BlockSpec((WIN_COLS, D_MODEL), lambda j: (0, 0)),
                 pl.BlockSpec((VAL_COLS, D_MODEL), lambda j: (0, 0))]
    out_shape = [jax.ShapeDtypeStruct((bsz, GLA_HEADS, GLA_DV, SLAB), _F32),
                 jax.ShapeDtypeStruct((WIN_COLS, D_MODEL), _BF16),
                 jax.ShapeDtypeStruct((VAL_COLS, D_MODEL), _BF16)]
    scratch = [
        pltpu.VMEM((rows, TOK_COLS), _BF16),
        pltpu.VMEM((1, VAL_COLS, rows), _BF16),
        pltpu.VMEM((n_chunks, GLA_HEADS, GLA_DV, SLAB), _BF16),
        pltpu.VMEM((n_chunks, 8, KEY_COLS), _F32),
        pltpu.VMEM((n_chunks, GLA_HEADS, GLA_DV, SLAB), _BF16),
        pltpu.VMEM((IN_STAGE_SLOTS, IN_STAGE_ROWS, D_MODEL), _F32),
        pltpu.SemaphoreType.DMA((IN_STAGE_SLOTS,)),
    ]
    if with_mix:
        in_specs += [whole(wts["n2w"]), of_layer((1, GLA_DV))] + [pl.BlockSpec(memory_space=pl.ANY)] * 3
        in_specs += [pl.BlockSpec((1, D_MODEL), lambda j: (0, 0), pipeline_mode=pl.Buffered(1))]
        args += [wts["n2w"], wts["gnw"], wts["wo"], wts["w1"], wts["w2"], wts["fnw"]]
        out_specs.append(pl.BlockSpec((rows, D_MODEL), lambda j: (0, 0)))
        out_shape.append(jax.ShapeDtypeStruct((rows, D_MODEL), _F32))
        scratch += [pltpu.VMEM((rows, VAL_COLS), _BF16)] + _mix_weight_scratch()
    outs = pl.pallas_call(
        functools.partial(_context_kernel, layer=layer, seg=ctx_len, chunks_per_seq=ctx_len // GLA_CHUNK,
                          with_mix=with_mix),
        grid=(1,),
        in_specs=in_specs,
        out_specs=out_specs,
        out_shape=out_shape,
        scratch_shapes=scratch,
        compiler_params=pltpu.CompilerParams(vmem_limit_bytes=VMEM_LIMIT_BYTES),
    )(*args)
    return tuple(outs) if with_mix else tuple(outs) + (None,)


def _prepare_weights(norm1_w, norm2_w, w_in, gate_w, gate_b, gla_norm_w, conv_w, pool_w, pool_scale, w_out,
                     w_mlp1, w_mlp2, final_norm_w):
    depth = w_in.shape[0]
    assert w_in.shape[2] == IN_COLS
    zeros = jnp.zeros((depth, GATE_RANK, GLA_HEADS, GLA_DK), _F32)
    gf = gate_w[:, 0].reshape(depth, GATE_RANK, GLA_HEADS, GLA_DK)
    gb = gate_w[:, 1].reshape(depth, GATE_RANK, GLA_HEADS, GLA_DK)
    gw = jnp.concatenate([jnp.concatenate([gf, zeros], -1), jnp.concatenate([zeros, gb], -1)], 1)
    gbias = jnp.concatenate([gate_b[:, 0].reshape(depth, GLA_HEADS, GLA_DK),
                             gate_b[:, 1].reshape(depth, GLA_HEADS, GLA_DK)], -1)
    n_groups = POOL_WIDTH // POOL_GROUP
    same_group = jnp.eye(n_groups, dtype=_F32)[None, :, None, :, None]
    pw = (pool_w[:, :, :, None, :] * same_group).reshape(depth, POOL_WIDTH, POOL_WIDTH)
    return {
        "n1w": norm1_w,
        "n2w": norm2_w,
        "w_in_t": jnp.swapaxes(w_in, 1, 2),
        "gw": gw.reshape(depth, 2 * GATE_RANK, KEY_COLS).astype(_BF16),
        "gb": gbias.reshape(depth, 1, KEY_COLS),
        "gnw": gla_norm_w.reshape(depth, 1, GLA_DV),
        "cw": jnp.swapaxes(conv_w, 0, 1),
        "pw": pw.astype(_BF16),
        "ps": pool_scale,
        "wo": w_out,
        "w1": w_mlp1,
        "w2": w_mlp2,
        "fnw": final_norm_w.reshape(1, D_MODEL),
    }


def kernel(x, c, ctx, c_ctx, w_mod, b_mod, norm1_w, norm2_w, w_in, gla_gate_w, gla_gate_b, gla_norm_w, conv_w, pool_w, pool_scale, w_out, w_mlp1, w_mlp2, final_norm_w):
    bsz, n, _ = x.shape
    ctx_len = ctx.shape[1]
    depth = w_in.shape[0]
    assert bsz + 1 <= MOD_ROWS and n % LATENT_IN_TILE == 0 and SUB_ROWS % ctx_len == 0
    ctx_row = bsz

    cv = jnp.concatenate([c, c_ctx[None, :], jnp.zeros((MOD_ROWS - bsz - 1, D_MODEL), _F32)], axis=0)
    mods3 = _adaln(cv, w_mod, b_mod)
    wts = _prepare_weights(norm1_w, norm2_w, w_in, gla_gate_w, gla_gate_b, gla_norm_w, conv_w, pool_w, pool_scale,
                           w_out, w_mlp1, w_mlp2, final_norm_w)

    xl = x.reshape(bsz * n, D_MODEL)
    xc = ctx.reshape(bsz * ctx_len, D_MODEL)
    lat_in_row = lambda j: j // (n // LATENT_IN_TILE)
    lat_mix_row = lambda j: j // (n // LATENT_MIX_TILE)

    for l in range(depth):
        last = l == depth - 1
        s_ctx, win, wvt, xc = _context_layer(xc, mods3, ctx_row, l, wts, ctx_len, with_mix=not last)

        tok, vt, ut, dec = _inproj(xl, mods3, lat_in_row, l, win, wvt, wts, LATENT_IN_TILE, GRID_W, LATENT_MIX_TILE)
        spt, _ = _scan(ut, dec, s_ctx, n // GLA_CHUNK)
        xl = _mix(xl, mods3, lat_mix_row, l, tok, vt, spt, wts, LATENT_MIX_TILE, last)
    return xl.reshape(bsz, n, D_MODEL)
```

```python
import functools

import jax
import jax.numpy as jnp
from jax import lax
from jax.experimental import pallas as pl
from jax.experimental.pallas import tpu as pltpu

D_MODEL = 1024
GLA_HEADS = 4
GLA_DK = 64
GLA_DV = 128
SLAB = 2 * GLA_DK
KEY_COLS = GLA_HEADS * SLAB
VAL_COLS = GLA_HEADS * GLA_DV
GATE_RANK = 16
GATE_TAU = 16.0
LOG2_E = 1.4426950408889634
GLA_CHUNK = 128
SUB_ROWS = 256
LATENT_IN_TILE = 1024
LATENT_MIX_TILE = 512
CONV_WIDTH = 256
POOL_WIDTH = 256
POOL_GROUP = 64
POOL_HALF_WINDOWS = (1, 2, 4, 8)
assert all(h & (h - 1) == 0 for h in POOL_HALF_WINDOWS)
D_FF = 4096
FF_CHUNK = 1024
N_MOD = 6
EPS = 1e-6
GRID_W = 64
RC_COLS = VAL_COLS + 3 * CONV_WIDTH + POOL_WIDTH
RY_COLS = VAL_COLS + CONV_WIDTH + POOL_WIDTH
TOK_QE = slice(0, KEY_COLS)
TOK_KI = slice(KEY_COLS, 2 * KEY_COLS)
TOK_RY = slice(2 * KEY_COLS, 2 * KEY_COLS + RY_COLS)
TOK_COLS = TOK_RY.stop
WZ_COLS = slice(0, 2 * GATE_RANK)
WQ_COLS = slice(128, 128 + GLA_HEADS * GLA_DK)
WK_COLS = slice(WQ_COLS.stop, WQ_COLS.stop + GLA_HEADS * GLA_DK)
WRC_COLS = slice(WK_COLS.stop, WK_COLS.stop + RC_COLS)
WIN_COLS = WRC_COLS.stop
IN_QK = slice(0, 2 * GLA_HEADS * GLA_DK)
IN_V = slice(IN_QK.stop, IN_QK.stop + VAL_COLS)
IN_Z = slice(IN_V.stop, IN_V.stop + 2 * GATE_RANK)
IN_RC = slice(IN_Z.stop, IN_Z.stop + RC_COLS)
IN_COLS = IN_RC.stop
IN_STAGE_ROWS = 512
SCAN_UNROLL = 2
SCAN_HEADS = 2
MOD_ROWS = 8
ADALN_COLS = 1536
VMEM_LIMIT_BYTES = 56 * 1024 * 1024
STAGE_SLOTS = 6
STAGE_ROWS = 512
GLA_PHASE_PIECES = 4
PIECES_PER_BLOCK = D_MODEL // STAGE_ROWS
IN_STAGE_SLOTS = 2

_NT = (((1,), (1,)), ((), ()))
_BF16 = jnp.bfloat16
_F32 = jnp.float32


def _dot(a, b):
    return jnp.dot(a, b, preferred_element_type=_F32)


def _sigmoid(x):
    return 1.0 / (1.0 + jnp.exp(-x))


def _adaln_kernel(cv_ref, w_ref, b_ref, o_ref):
    cv = cv_ref[...]
    s = (cv * _sigmoid(cv)).astype(_BF16)
    mod = _dot(s, w_ref[...].astype(_BF16)) + b_ref[pl.ds(pl.program_id(0), 1), :]
    for r in range(MOD_ROWS):
        o_ref[r] = mod[r:r + 1, :]


def _adaln(cv, w_mod, b_mod):
    depth = w_mod.shape[0]
    cols = w_mod.shape[2]
    return pl.pallas_call(
        _adaln_kernel,
        grid=(depth, cols // ADALN_COLS),
        in_specs=[
            pl.BlockSpec((MOD_ROWS, D_MODEL), lambda l, j: (0, 0)),
            pl.BlockSpec((None, D_MODEL, ADALN_COLS), lambda l, j: (l, 0, j)),
            pl.BlockSpec((depth, ADALN_COLS), lambda l, j: (0, j)),
        ],
        out_specs=pl.BlockSpec((MOD_ROWS, 1, ADALN_COLS), lambda l, j: (l, 0, j)),
        out_shape=jax.ShapeDtypeStruct((depth * MOD_ROWS, 1, cols), _F32),
        compiler_params=pltpu.CompilerParams(vmem_limit_bytes=VMEM_LIMIT_BYTES),
    )(cv, w_mod, b_mod)


def _per_direction(t):
    lane = lax.broadcasted_iota(jnp.int32, (1, SLAB), 1)
    first = lane < GLA_DK
    slabs = []
    for p in range(GLA_HEADS // 2):
        pair = t[:, p * SLAB:(p + 1) * SLAB]
        swapped = pltpu.roll(pair, GLA_DK, 1)
        slabs += [jnp.where(first, pair, swapped), jnp.where(first, swapped, pair)]
    return jnp.concatenate(slabs, axis=1)


def _conv_and_pool(rc, taps, pw_ref, pool_scale, seg):
    rows = rc.shape[0]
    pos = lax.broadcasted_iota(jnp.int32, (rows, 1), 0) % seg

    def in_segment(v, d):
        return jnp.where((pos + d >= 0) & (pos + d < seg), pltpu.roll(v, (-d) % rows, 0), 0.0)

    c0 = 0
    cb = rc[:, c0:c0 + CONV_WIDTH]
    zc = rc[:, c0 + CONV_WIDTH:c0 + 2 * CONV_WIDTH] * rc[:, c0 + 2 * CONV_WIDTH:c0 + 3 * CONV_WIDTH]
    yc = cb * (taps[0] * in_segment(zc, -1) + taps[1] * zc + taps[2] * in_segment(zc, 1))

    p0 = 3 * CONV_WIDTH
    lane128 = lax.broadcasted_iota(jnp.int32, (1, 2 * POOL_GROUP), 1)
    pooled = []
    for blk in range(POOL_WIDTH // (2 * POOL_GROUP)):
        u = rc[:, p0 + blk * 2 * POOL_GROUP:p0 + (blk + 1) * 2 * POOL_GROUP]
        h_lo, h_hi = POOL_HALF_WINDOWS[2 * blk], POOL_HALF_WINDOWS[2 * blk + 1]
        half = jnp.where(lane128 < POOL_GROUP, h_lo, h_hi)
        ahead, behind, window_sums = u, in_segment(u, -1), {}
        for level in range(h_hi.bit_length()):
            h = 1 << level
            if level > 0:
                ahead = ahead + in_segment(ahead, h // 2)
                behind = behind + in_segment(behind, -(h // 2))
            if h in (h_lo, h_hi):
                window_sums[h] = ahead + behind
        acc = jnp.where(lane128 < POOL_GROUP, window_sums[h_lo], window_sums[h_hi])
        cnt = (jnp.minimum(pos + half, seg) - jnp.maximum(pos - half, 0)).astype(_F32)
        pooled.append(acc / cnt - u)
    pool_in = jnp.concatenate(pooled, axis=1).astype(_BF16)
    return yc, _dot(pool_in, pw_ref[...]) * pool_scale


def _inproj_kernel(x_ref, mod_ref, n1w_ref, win_ref, wvt_ref, gw_ref, gb_ref, cw_ref, pw_ref, ps_ref,
                   tok_ref, vt_ref, ut_ref, dec_ref, *, layer, seg):
    this_layer = slice(layer, layer + 1)
    taps = [cw_ref[t, this_layer, :] for t in range(cw_ref.shape[0])]
    pool_scale = ps_ref[this_layer, :]
    qe_ref, ki_ref, ry_ref = tok_ref.at[:, TOK_QE], tok_ref.at[:, TOK_KI], tok_ref.at[:, TOK_RY]
    vt_tile = vt_ref.shape[2]
    tm = x_ref.shape[0]
    sh1 = mod_ref[:, 0:D_MODEL]
    scale1 = n1w_ref[this_layer, :] * (1.0 + mod_ref[:, D_MODEL:2 * D_MODEL])
    row = lax.broadcasted_iota(jnp.int32, (GLA_CHUNK, GLA_CHUNK), 0)
    col = lax.broadcasted_iota(jnp.int32, (GLA_CHUNK, GLA_CHUNK), 1)
    scan_op = jnp.where(jnp.concatenate([col <= row, col >= row], axis=1), 1.0 / GATE_TAU, 0.0).astype(_BF16)
    lane = lax.broadcasted_iota(jnp.int32, (1, KEY_COLS), 1)
    is_fwd = (lane % SLAB) < GLA_DK
    mid = GLA_CHUNK // 2

    n_sub = tm // SUB_ROWS
    cps = SUB_ROWS // GLA_CHUNK
    sub_rows = [slice(s * SUB_ROWS, (s + 1) * SUB_ROWS) for s in range(n_sub)]
    hbs, projs, gates, q2s, k2s = [], [], [], [], []
    for s in range(n_sub):
        x = x_ref[sub_rows[s], :]
        xn = x * lax.rsqrt(jnp.mean(x * x, axis=-1, keepdims=True) + EPS)
        hbs.append((xn * scale1 + sh1).astype(_BF16))
    for s in range(n_sub):
        proj = lax.dot_general(hbs[s], win_ref[...], _NT, preferred_element_type=_F32)
        projs.append(proj)
        pre = _dot(proj[:, WZ_COLS].astype(_BF16), gw_ref[...]) + gb_ref[...]
        g = jnp.minimum(pre, 0.0) * LOG2_E - jnp.log2(1.0 + jnp.exp2(jnp.abs(pre) * (-LOG2_E)))
        gates.append(g.astype(_BF16))
        q2s.append(_per_direction(proj[:, WQ_COLS] * (GLA_DK ** -0.5)))
        k2s.append(_per_direction(proj[:, WK_COLS]))

    decays = []
    for s in range(n_sub):
        for cl in range(cps):
            g = gates[s][cl * GLA_CHUNK:(cl + 1) * GLA_CHUNK]
            zero = jnp.zeros_like(g)
            by_direction = jnp.concatenate([jnp.where(is_fwd, g, zero), jnp.where(is_fwd, zero, g)], axis=0)
            decays.append(_dot(scan_op, by_direction))

    vtbs = []
    for s in range(n_sub):
        vtb = lax.dot_general(wvt_ref[...], hbs[s], _NT, preferred_element_type=_F32).astype(_BF16)
        first = s * SUB_ROWS
        vt_ref[first // vt_tile, :, first % vt_tile:first % vt_tile + SUB_ROWS] = vtb
        vtbs.append(vtb)
        rc = projs[s][:, WRC_COLS]
        yc, yp = _conv_and_pool(rc[:, VAL_COLS:RC_COLS], taps, pw_ref, pool_scale, seg)
        ry_ref[sub_rows[s], 0:VAL_COLS] = rc[:, 0:VAL_COLS].astype(_BF16)
        ry_ref[sub_rows[s], VAL_COLS:VAL_COLS + CONV_WIDTH] = yc.astype(_BF16)
        ry_ref[sub_rows[s], VAL_COLS + CONV_WIDTH:RY_COLS] = yp.astype(_BF16)

    for s in range(n_sub):
        for cl in range(cps):
            c = s * cps + cl
            lrows = slice(cl * GLA_CHUNK, (cl + 1) * GLA_CHUNK)
            rows = slice(c * GLA_CHUNK, (c + 1) * GLA_CHUNK)
            e = decays[c]
            total = jnp.where(is_fwd, e[GLA_CHUNK - 1:GLA_CHUNK, :], e[0:1, :])
            mvec = jnp.where(is_fwd, e[mid - 1:mid, :], e[mid:mid + 1, :])
            bm = e - mvec
            qe_ref[rows, :] = (q2s[s][lrows] * jnp.exp2(bm)).astype(_BF16)
            k_mid = k2s[s][lrows] * jnp.exp2(-bm)
            ki_ref[rows, :] = k_mid.astype(_BF16)
            kd = (k_mid * jnp.exp2(total - mvec)).astype(_BF16)
            dec_ref[c] = jnp.concatenate([total, mvec, jnp.zeros((6, KEY_COLS), _F32)], axis=0)
            for h in range(GLA_HEADS):
                vt_h = vtbs[s][h * GLA_DV:(h + 1) * GLA_DV, lrows]
                ut_ref[c, h] = _dot(vt_h, kd[:, h * SLAB:(h + 1) * SLAB]).astype(ut_ref.dtype)


def _inproj(x2d, mods3, mod_row_of_tile, layer, win, wvt, wts, tm, seg, vt_tile):
    tokens = x2d.shape[0]
    n_tiles = tokens // tm
    n_chunks = tokens // GLA_CHUNK
    cpt = tm // GLA_CHUNK
    assert SUB_ROWS % seg == 0
    assert tm % vt_tile == 0 and vt_tile % SUB_ROWS == 0
    of_layer = lambda shape: pl.BlockSpec((None,) + shape, lambda j: (layer,) + (0,) * len(shape))
    whole = lambda a: pl.BlockSpec(a.shape, lambda j: (0,) * a.ndim)
    return pl.pallas_call(
        functools.partial(_inproj_kernel, layer=layer, seg=seg),
        grid=(n_tiles,),
        in_specs=[
            pl.BlockSpec((tm, D_MODEL), lambda j: (j, 0)),
            pl.BlockSpec((None, 1, N_MOD * D_MODEL), lambda j: (layer * MOD_ROWS + mod_row_of_tile(j), 0, 0)),
            whole(wts["n1w"]),
            pl.BlockSpec((WIN_COLS, D_MODEL), lambda j: (0, 0)),
            pl.BlockSpec((VAL_COLS, D_MODEL), lambda j: (0, 0)),
            of_layer((2 * GATE_RANK, KEY_COLS)),
            of_layer((1, KEY_COLS)),
            whole(wts["cw"]),
            of_layer((POOL_WIDTH, POOL_WIDTH)),
            whole(wts["ps"]),
        ],
        out_specs=[
            pl.BlockSpec((tm, TOK_COLS), lambda j: (j, 0)),
            pl.BlockSpec((tm // vt_tile, VAL_COLS, vt_tile), lambda j: (j, 0, 0)),
            pl.BlockSpec((cpt, GLA_HEADS, GLA_DV, SLAB), lambda j: (j, 0, 0, 0)),
            pl.BlockSpec((cpt, 8, KEY_COLS), lambda j: (j, 0, 0)),
        ],
        out_shape=[
            jax.ShapeDtypeStruct((tokens, TOK_COLS), _BF16),
            jax.ShapeDtypeStruct((tokens // vt_tile, VAL_COLS, vt_tile), _BF16),
            jax.ShapeDtypeStruct((n_chunks, GLA_HEADS, GLA_DV, SLAB), _BF16),
            jax.ShapeDtypeStruct((n_chunks, 8, KEY_COLS), _F32),
        ],
        compiler_params=pltpu.CompilerParams(vmem_limit_bytes=VMEM_LIMIT_BYTES),
    )(x2d, mods3, wts["n1w"], win, wvt, wts["gw"], wts["gb"], wts["cw"], wts["pw"], wts["ps"])


def _scan_chunks(ut_ref, dec_ref, states, spt_ref, nc, chunk0=0, head0=0):
    lane = lax.broadcasted_iota(jnp.int32, (1, SLAB), 1)
    is_fwd = lane < GLA_DK

    def step(t, states):
        i = chunk0 + t
        j = chunk0 + nc - 1 - t
        new_states = []
        for s in range(SCAN_HEADS):
            h = head0 + s
            lanes = slice(h * SLAB, (h + 1) * SLAB)
            log_decay = jnp.where(is_fwd, dec_ref[i, 0:1, lanes], dec_ref[j, 0:1, lanes])
            log_mid = jnp.where(is_fwd, dec_ref[i, 1:2, lanes], dec_ref[j, 1:2, lanes])
            entering = (states[s] * jnp.exp2(log_mid)).astype(_BF16)
            spt_ref[i, h, :, 0:GLA_DK] = entering[:, 0:GLA_DK]
            spt_ref[j, h, :, GLA_DK:SLAB] = entering[:, GLA_DK:SLAB]
            inc = jnp.where(is_fwd, ut_ref[i, h], ut_ref[j, h]).astype(_F32)
            new_states.append(states[s] * jnp.exp2(log_decay) + inc)
        return tuple(new_states)

    return lax.fori_loop(0, nc, step, tuple(states), unroll=SCAN_UNROLL)


def _scan_kernel(ut_ref, dec_ref, s0_ref, spt_ref, sfin_ref):
    final = _scan_chunks(ut_ref, dec_ref, [s0_ref[h] for h in range(SCAN_HEADS)], spt_ref, ut_ref.shape[0])
    for h in range(SCAN_HEADS):
        sfin_ref[h] = final[h]


def _scan(ut, dec, s0, chunks_per_seq):
    n_chunks = ut.shape[0]
    bsz = n_chunks // chunks_per_seq
    return pl.pallas_call(
        _scan_kernel,
        grid=(bsz, GLA_HEADS // SCAN_HEADS),
        in_specs=[
            pl.BlockSpec((chunks_per_seq, SCAN_HEADS, GLA_DV, SLAB), lambda b, g: (b, g, 0, 0)),
            pl.BlockSpec((chunks_per_seq, 8, SCAN_HEADS * SLAB), lambda b, g: (b, 0, g)),
            pl.BlockSpec((None, SCAN_HEADS, GLA_DV, SLAB), lambda b, g: (b, g, 0, 0)),
        ],
        out_specs=[
            pl.BlockSpec((chunks_per_seq, SCAN_HEADS, GLA_DV, SLAB), lambda b, g: (b, g, 0, 0)),
            pl.BlockSpec((None, SCAN_HEADS, GLA_DV, SLAB), lambda b, g: (b, g, 0, 0)),
        ],
        out_shape=[
            jax.ShapeDtypeStruct((n_chunks, GLA_HEADS, GLA_DV, SLAB), _BF16),
            jax.ShapeDtypeStruct((bsz, GLA_HEADS, GLA_DV, SLAB), _F32),
        ],
        compiler_params=pltpu.CompilerParams(vmem_limit_bytes=VMEM_LIMIT_BYTES),
    )(ut, dec, s0)


def _mix_weight_stager(layer, wo_hbm, w1_hbm, w2_hbm, wo_ref, w1_ref, w2_ref, stage, sems):
    blocks = [(wo_hbm.at[layer], wo_ref)]
    for i in range(D_FF // D_MODEL):
        cols = slice(i * D_MODEL, (i + 1) * D_MODEL)
        blocks.append((w1_hbm.at[layer, :, cols], w1_ref.at[:, cols]))
        blocks.append((w2_hbm.at[layer, cols, :], w2_ref.at[cols, :]))
    pieces = [(src.at[p * STAGE_ROWS:(p + 1) * STAGE_ROWS, :], dst.at[p * STAGE_ROWS:(p + 1) * STAGE_ROWS, :])
              for src, dst in blocks for p in range(PIECES_PER_BLOCK)]
    copies = [pltpu.make_async_copy(src, stage.at[k % STAGE_SLOTS], sems.at[k % STAGE_SLOTS])
              for k, (src, _) in enumerate(pieces)]
    ready = [None]

    def need(n):
        if ready[0] is None:
            for k in range(STAGE_SLOTS):
                copies[k].start()
            ready[0] = 0
        for k in range(ready[0], min(n, len(pieces))):
            copies[k].wait()
            pieces[k][1][...] = stage[k % STAGE_SLOTS].astype(_BF16)
            if k + STAGE_SLOTS < len(pieces):
                copies[k + STAGE_SLOTS].start()
            ready[0] = k + 1

    return need


def _mix_body(x_ref, mod_ref, n2w_ref, tok_ref, vt_ref, spt_ref, gnw_ref, fnw_ref, o_ref, y_scr, wo_ref, w1_ref, w2_ref,
              need, layer, final):
    tm = x_ref.shape[0]
    qe_ref, ki_ref, ry_ref = tok_ref.at[:, TOK_QE], tok_ref.at[:, TOK_KI], tok_ref.at[:, TOK_RY]
    g1 = mod_ref[:, 2 * D_MODEL:3 * D_MODEL]
    sh2 = mod_ref[:, 3 * D_MODEL:4 * D_MODEL]
    scale2 = n2w_ref[layer:layer + 1, :] * (1.0 + mod_ref[:, 4 * D_MODEL:5 * D_MODEL])
    g2 = mod_ref[:, 5 * D_MODEL:6 * D_MODEL]
    row = lax.broadcasted_iota(jnp.int32, (GLA_CHUNK, GLA_CHUNK), 0)
    col = lax.broadcasted_iota(jnp.int32, (GLA_CHUNK, GLA_CHUNK), 1)
    lane = lax.broadcasted_iota(jnp.int32, (GLA_CHUNK, SLAB), 1)
    fwd_lane = lane < GLA_DK
    gnw = gnw_ref[...]
    need(0)

    for c in range(tm // GLA_CHUNK):
        rows = slice(c * GLA_CHUNK, (c + 1) * GLA_CHUNK)
        for h in range(GLA_HEADS):
            lanes = slice(h * SLAB, (h + 1) * SLAB)
            qe = qe_ref[rows, lanes]
            ki = ki_ref[rows, lanes]
            zero = jnp.zeros_like(ki)
            keys = jnp.concatenate([jnp.where(fwd_lane, ki, zero), jnp.where(fwd_lane, zero, ki)], axis=0)
            a2 = lax.dot_general(qe, keys, _NT, preferred_element_type=_F32)
            am = jnp.where(col <= row, a2[:, 0:GLA_CHUNK], 0.0) + jnp.where(col >= row, a2[:, GLA_CHUNK:], 0.0)
            lhs = jnp.concatenate([am.astype(_BF16), qe], axis=1)
            rhs = jnp.concatenate([vt_ref[h * GLA_DV:(h + 1) * GLA_DV, rows], spt_ref[c, h]], axis=1)
            o = lax.dot_general(lhs, rhs, _NT, preferred_element_type=_F32)
            o = o * lax.rsqrt(jnp.mean(o * o, axis=-1, keepdims=True) + EPS) * gnw
            r = ry_ref[rows, h * GLA_DV:(h + 1) * GLA_DV].astype(_F32)
            y_scr[rows, h * GLA_DV:(h + 1) * GLA_DV] = (o * (r * _sigmoid(r))).astype(_BF16)
        need((c + 1) * GLA_PHASE_PIECES // (tm // GLA_CHUNK))

    need(PIECES_PER_BLOCK)
    mixed = _dot(y_scr[...], wo_ref[0:VAL_COLS, :]) + _dot(ry_ref[:, VAL_COLS:RY_COLS], wo_ref[VAL_COLS:D_MODEL, :])
    x1 = x_ref[...] + g1 * mixed
    xn = x1 * lax.rsqrt(jnp.mean(x1 * x1, axis=-1, keepdims=True) + EPS)
    h2 = (xn * scale2 + sh2).astype(_BF16)
    acc = jnp.zeros((tm, D_MODEL), _F32)
    for f in range(D_FF // FF_CHUNK):
        cols = slice(f * FF_CHUNK, (f + 1) * FF_CHUNK)
        blocks_before = 2 * pl.cdiv(cols.stop, D_MODEL)
        need(PIECES_PER_BLOCK * blocks_before)
        a = jnp.maximum(_dot(h2, w1_ref[:, cols]), 0.0)
        need(PIECES_PER_BLOCK * (blocks_before + 1))
        acc = acc + _dot((a * a).astype(_BF16), w2_ref[cols, :])
    out = x1 + g2 * acc
    if final:
        out = out * lax.rsqrt(jnp.mean(out * out, axis=-1, keepdims=True) + EPS) * fnw_ref[...]
    o_ref[...] = out


def _mix_kernel(x_ref, mod_ref, n2w_ref, tok_ref, vt_ref, spt_ref, gnw_ref, wo_hbm, w1_hbm, w2_hbm, fnw_ref,
                o_ref, y_scr, wo_ref, w1_ref, w2_ref, stage, sems, *, layer, final):
    body = functools.partial(_mix_body, x_ref, mod_ref, n2w_ref, tok_ref, vt_ref, spt_ref, gnw_ref, fnw_ref, o_ref, y_scr,
                             wo_ref, w1_ref, w2_ref, layer=layer, final=final)

    def staging_step():
        body(need=_mix_weight_stager(layer, wo_hbm, w1_hbm, w2_hbm, wo_ref, w1_ref, w2_ref, stage, sems))

    first = pl.program_id(0) == 0
    pl.when(first)(staging_step)
    pl.when(jnp.logical_not(first))(functools.partial(body, need=lambda n: None))


def _mix_weight_scratch():
    return [pltpu.VMEM((D_MODEL, D_MODEL), _BF16), pltpu.VMEM((D_MODEL, D_FF), _BF16), pltpu.VMEM((D_FF, D_MODEL), _BF16),
            pltpu.VMEM((STAGE_SLOTS, STAGE_ROWS, D_MODEL), _F32), pltpu.SemaphoreType.DMA((STAGE_SLOTS,))]


def _mix(x2d, mods3, mod_row_of_tile, layer, tok, vt, spt, wts, tm, final):
    tokens = x2d.shape[0]
    cpt = tm // GLA_CHUNK
    of_layer = lambda shape: pl.BlockSpec((None,) + shape, lambda j: (layer,) + (0,) * len(shape),
                                          pipeline_mode=pl.Buffered(1))
    return pl.pallas_call(
        functools.partial(_mix_kernel, layer=layer, final=final),
        grid=(tokens // tm,),
        in_specs=[
            pl.BlockSpec((tm, D_MODEL), lambda j: (j, 0)),
            pl.BlockSpec((None, 1, N_MOD * D_MODEL), lambda j: (layer * MOD_ROWS + mod_row_of_tile(j), 0, 0)),
            pl.BlockSpec(wts["n2w"].shape, lambda j: (0, 0), pipeline_mode=pl.Buffered(1)),
            pl.BlockSpec((tm, TOK_COLS), lambda j: (j, 0)),
            pl.BlockSpec((None, VAL_COLS, tm), lambda j: (j, 0, 0)),
            pl.BlockSpec((cpt, GLA_HEADS, GLA_DV, SLAB), lambda j: (j, 0, 0, 0)),
            of_layer((1, GLA_DV)),
            pl.BlockSpec(memory_space=pl.ANY),
            pl.BlockSpec(memory_space=pl.ANY),
            pl.BlockSpec(memory_space=pl.ANY),
            pl.BlockSpec((1, D_MODEL), lambda j: (0, 0), pipeline_mode=pl.Buffered(1)),
        ],
        out_specs=pl.BlockSpec((tm, D_MODEL), lambda j: (j, 0)),
        out_shape=jax.ShapeDtypeStruct((tokens, D_MODEL), _F32),
        scratch_shapes=[pltpu.VMEM((tm, VAL_COLS), _BF16)] + _mix_weight_scratch(),
        compiler_params=pltpu.CompilerParams(vmem_limit_bytes=VMEM_LIMIT_BYTES),
    )(x2d, mods3, wts["n2w"], tok, vt, spt, wts["gnw"], wts["wo"], wts["w1"], wts["w2"], wts["fnw"])


def _stage_in_projection(layer, w_in_t_hbm, win_ref, wvt_ref, stage, sems):
    pieces = []
    for src, dst_ref, dst in ((IN_Z, win_ref, WZ_COLS.start), (IN_QK, win_ref, WQ_COLS.start),
                              (IN_V, wvt_ref, 0), (IN_RC, win_ref, WRC_COLS.start)):
        for off in range(0, src.stop - src.start, IN_STAGE_ROWS):
            pieces.append((src.start + off, dst_ref, dst + off, min(IN_STAGE_ROWS, src.stop - src.start - off)))
    copies = [pltpu.make_async_copy(w_in_t_hbm.at[layer, s0:s0 + n, :], stage.at[k % IN_STAGE_SLOTS, 0:n, :],
                                    sems.at[k % IN_STAGE_SLOTS]) for k, (s0, _, _, n) in enumerate(pieces)]
    for k in range(IN_STAGE_SLOTS):
        copies[k].start()
    win_ref[WZ_COLS.stop:WQ_COLS.start, :] = jnp.zeros((WQ_COLS.start - WZ_COLS.stop, D_MODEL), _BF16)
    for k, (_, dst_ref, d0, n) in enumerate(pieces):
        copies[k].wait()
        dst_ref[d0:d0 + n, :] = stage[k % IN_STAGE_SLOTS, 0:n, :].astype(_BF16)
        if k + IN_STAGE_SLOTS < len(pieces):
            copies[k + IN_STAGE_SLOTS].start()


def _context_kernel(*refs, layer, seg, chunks_per_seq, with_mix):
    n_in = 9
    x_ref, mod_ref, n1w_ref, w_in_t_hbm = refs[0:4]
    if with_mix:
        n2w_ref, gnw_ref, wo_hbm, w1_hbm, w2_hbm, fnw_ref = refs[n_in:n_in + 6]
        sfin_ref, win_ref, wvt_ref, o_ref = refs[n_in + 6:n_in + 10]
        tok_scr, vt_scr, ut_scr, dec_scr, spt_scr, in_stage, in_sems, y_scr = refs[n_in + 10:n_in + 18]
        mix_weight_scr = refs[n_in + 18:]
    else:
        sfin_ref, win_ref, wvt_ref = refs[n_in:n_in + 3]
        tok_scr, vt_scr, ut_scr, dec_scr, spt_scr, in_stage, in_sems = refs[n_in + 3:]

    need = lambda n: None
    if with_mix:
        wo_ref, w1_ref, w2_ref, mix_stage, mix_sems = mix_weight_scr
        need = _mix_weight_stager(layer, wo_hbm, w1_hbm, w2_hbm, wo_ref, w1_ref, w2_ref, mix_stage, mix_sems)

    _stage_in_projection(layer, w_in_t_hbm, win_ref, wvt_ref, in_stage, in_sems)
    need(0)
    _inproj_kernel(x_ref, mod_ref, n1w_ref, win_ref, wvt_ref, *refs[4:n_in], tok_scr, vt_scr, ut_scr, dec_scr,
                   layer=layer, seg=seg)
    need(STAGE_SLOTS)

    sfin_ref[...] = jnp.zeros(sfin_ref.shape, _F32)
    for b in range(ut_scr.shape[0] // chunks_per_seq):
        for head0 in range(0, GLA_HEADS, SCAN_HEADS):
            start = [sfin_ref[b, head0 + s] for s in range(SCAN_HEADS)]
            final = _scan_chunks(ut_scr, dec_scr, start, spt_scr, chunks_per_seq,
                                 chunk0=b * chunks_per_seq, head0=head0)
            for s in range(SCAN_HEADS):
                sfin_ref[b, head0 + s] = final[s]

    if with_mix:
        need(2 * STAGE_SLOTS)
        _mix_body(x_ref, mod_ref, n2w_ref, tok_scr, vt_scr.at[0], spt_scr, gnw_ref, fnw_ref, o_ref, y_scr,
                  wo_ref, w1_ref, w2_ref, need=need, layer=layer, final=False)


def _context_layer(xc, mods3, mod_row, layer, wts, ctx_len, with_mix):
    rows = xc.shape[0]
    bsz = rows // ctx_len
    n_chunks = rows // GLA_CHUNK
    assert rows % SUB_ROWS == 0 and SUB_ROWS % ctx_len == 0
    of_layer = lambda shape: pl.BlockSpec((None,) + shape, lambda j: (layer,) + (0,) * len(shape),
                                          pipeline_mode=pl.Buffered(1))
    whole = lambda a: pl.BlockSpec(a.shape, lambda j: (0,) * a.ndim, pipeline_mode=pl.Buffered(1))
    in_specs = [
        pl.BlockSpec((rows, D_MODEL), lambda j: (0, 0)),
        pl.BlockSpec((None, 1, N_MOD * D_MODEL), lambda j: (layer * MOD_ROWS + mod_row, 0, 0)),
        whole(wts["n1w"]),
        pl.BlockSpec(memory_space=pl.ANY),
        of_layer((2 * GATE_RANK, KEY_COLS)),
        of_layer((1, KEY_COLS)),
        whole(wts["cw"]),
        of_layer((POOL_WIDTH, POOL_WIDTH)),
        whole(wts["ps"]),
    ]
    args = [xc, mods3, wts["n1w"], wts["w_in_t"], wts["gw"], wts["gb"], wts["cw"], wts["pw"], wts["ps"]]
    out_specs = [pl.BlockSpec((bsz, GLA_HEADS, GLA_DV, SLAB), lambda j: (0, 0, 0, 0)),
                 pl.BlockSpec((WIN_COLS, D_MODEL), lambda j: (0, 0)),
                 pl.BlockSpec((VAL_COLS, D_MODEL), lambda j: (0, 0))]
    out_shape = [jax.ShapeDtypeStruct((bsz, GLA_HEADS, GLA_DV, SLAB), _F32),
                 jax.ShapeDtypeStruct((WIN_COLS, D_MODEL), _BF16),
                 jax.ShapeDtypeStruct((VAL_COLS, D_MODEL), _BF16)]
    scratch = [
        pltpu.VMEM((rows, TOK_COLS), _BF16),
        pltpu.VMEM((1, VAL_COLS, rows), _BF16),
        pltpu.VMEM((n_chunks, GLA_HEADS, GLA_DV, SLAB), _BF16),
        pltpu.VMEM((n_chunks, 8, KEY_COLS), _F32),
        pltpu.VMEM((n_chunks, GLA_HEADS, GLA_DV, SLAB), _BF16),
        pltpu.VMEM((IN_STAGE_SLOTS, IN_STAGE_ROWS, D_MODEL), _F32),
        pltpu.SemaphoreType.DMA((IN_STAGE_SLOTS,)),
    ]
    if with_mix:
        in_specs += [whole(wts["n2w"]), of_layer((1, GLA_DV))] + [pl.BlockSpec(memory_space=pl.ANY)] * 3
        in_specs += [pl.BlockSpec((1, D_MODEL), lambda j: (0, 0), pipeline_mode=pl.Buffered(1))]
        args += [wts["n2w"], wts["gnw"], wts["wo"], wts["w1"], wts["w2"], wts["fnw"]]
        out_specs.append(pl.BlockSpec((rows, D_MODEL), lambda j: (0, 0)))
        out_shape.append(jax.ShapeDtypeStruct((rows, D_MODEL), _F32))
        scratch += [pltpu.VMEM((rows, VAL_COLS), _BF16)] + _mix_weight_scratch()
    outs = pl.pallas_call(
        functools.partial(_context_kernel, layer=layer, seg=ctx_len, chunks_per_seq=ctx_len // GLA_CHUNK,
                          with_mix=with_mix),
        grid=(1,),
        in_specs=in_specs,
        out_specs=out_specs,
        out_shape=out_shape,
        scratch_shapes=scratch,
        compiler_params=pltpu.CompilerParams(vmem_limit_bytes=VMEM_LIMIT_BYTES),
    )(*args)
    return tuple(outs) if with_mix else tuple(outs) + (None,)


def _prepare_weights(norm1_w, norm2_w, w_in, gate_w, gate_b, gla_norm_w, conv_w, pool_w, pool_scale, w_out,
                     w_mlp1, w_mlp2, final_norm_w):
    depth = w_in.shape[0]
    assert w_in.shape[2] == IN_COLS
    zeros = jnp.zeros((depth, GATE_RANK, GLA_HEADS, GLA_DK), _F32)
    gf = gate_w[:, 0].reshape(depth, GATE_RANK, GLA_HEADS, GLA_DK)
    gb = gate_w[:, 1].reshape(depth, GATE_RANK, GLA_HEADS, GLA_DK)
    gw = jnp.concatenate([jnp.concatenate([gf, zeros], -1), jnp.concatenate([zeros, gb], -1)], 1)
    gbias = jnp.concatenate([gate_b[:, 0].reshape(depth, GLA_HEADS, GLA_DK),
                             gate_b[:, 1].reshape(depth, GLA_HEADS, GLA_DK)], -1)
    n_groups = POOL_WIDTH // POOL_GROUP
    same_group = jnp.eye(n_groups, dtype=_F32)[None, :, None, :, None]
    pw = (pool_w[:, :, :, None, :] * same_group).reshape(depth, POOL_WIDTH, POOL_WIDTH)
    return {
        "n1w": norm1_w,
        "n2w": norm2_w,
        "w_in_t": jnp.swapaxes(w_in, 1, 2),
        "gw": gw.reshape(depth, 2 * GATE_RANK, KEY_COLS).astype(_BF16),
        "gb": gbias.reshape(depth, 1, KEY_COLS),
        "gnw": gla_norm_w.reshape(depth, 1, GLA_DV),
        "cw": jnp.swapaxes(conv_w, 0, 1),
        "pw": pw.astype(_BF16),
        "ps": pool_scale,
        "wo": w_out,
        "w1": w_mlp1,
        "w2": w_mlp2,
        "fnw": final_norm_w.reshape(1, D_MODEL),
    }


def kernel(x, c, ctx, c_ctx, w_mod, b_mod, norm1_w, norm2_w, w_in, gla_gate_w, gla_gate_b, gla_norm_w, conv_w, pool_w, pool_scale, w_out, w_mlp1, w_mlp2, final_norm_w):
    bsz, n, _ = x.shape
    ctx_len = ctx.shape[1]
    depth = w_in.shape[0]
    assert bsz + 1 <= MOD_ROWS and n % LATENT_IN_TILE == 0 and SUB_ROWS % ctx_len == 0
    ctx_row = bsz

    cv = jnp.concatenate([c, c_ctx[None, :], jnp.zeros((MOD_ROWS - bsz - 1, D_MODEL), _F32)], axis=0)
    mods3 = _adaln(cv, w_mod, b_mod)
    wts = _prepare_weights(norm1_w, norm2_w, w_in, gla_gate_w, gla_gate_b, gla_norm_w, conv_w, pool_w, pool_scale,
                           w_out, w_mlp1, w_mlp2, final_norm_w)

    xl = x.reshape(bsz * n, D_MODEL)
    xc = ctx.reshape(bsz * ctx_len, D_MODEL)
    lat_in_row = lambda j: j // (n // LATENT_IN_TILE)
    lat_mix_row = lambda j: j // (n // LATENT_MIX_TILE)

    for l in range(depth):
        last = l == depth - 1
        s_ctx, win, wvt, xc = _context_layer(xc, mods3, ctx_row, l, wts, ctx_len, with_mix=not last)

        tok, vt, ut, dec = _inproj(xl, mods3, lat_in_row, l, win, wvt, wts, LATENT_IN_TILE, GRID_W, LATENT_MIX_TILE)
        spt, _ = _scan(ut, dec, s_ctx, n // GLA_CHUNK)
        xl = _mix(xl, mods3, lat_mix_row, l, tok, vt, spt, wts, LATENT_MIX_TILE, last)
    return xl.reshape(bsz, n, D_MODEL)
```

```python
import functools

import jax
import jax.numpy as jnp
from jax import lax
from jax.experimental import pallas as pl
from jax.experimental.pallas import tpu as pltpu

D_MODEL = 1024
GLA_HEADS = 4
GLA_DK = 64
GLA_DV = 128
SLAB = 2 * GLA_DK
KEY_COLS = GLA_HEADS * SLAB
VAL_COLS = GLA_HEADS * GLA_DV
GATE_RANK = 16
GATE_TAU = 16.0
LOG2_E = 1.4426950408889634
GLA_CHUNK = 128
SUB_ROWS = 256
LATENT_IN_TILE = 1024
LATENT_MIX_TILE = 512
CONV_WIDTH = 256
POOL_WIDTH = 256
POOL_GROUP = 64
POOL_HALF_WINDOWS = (1, 2, 4, 8)
assert all(h & (h - 1) == 0 for h in POOL_HALF_WINDOWS)
D_FF = 4096
FF_CHUNK = 1024
N_MOD = 6
EPS = 1e-6
GRID_W = 64
RC_COLS = VAL_COLS + 3 * CONV_WIDTH + POOL_WIDTH
RY_COLS = VAL_COLS + CONV_WIDTH + POOL_WIDTH
TOK_QE = slice(0, KEY_COLS)
TOK_KI = slice(KEY_COLS, 2 * KEY_COLS)
TOK_RY = slice(2 * KEY_COLS, 2 * KEY_COLS + RY_COLS)
TOK_COLS = TOK_RY.stop
WZ_COLS = slice(0, 2 * GATE_RANK)
WQ_COLS = slice(128, 128 + GLA_HEADS * GLA_DK)
WK_COLS = slice(WQ_COLS.stop, WQ_COLS.stop + GLA_HEADS * GLA_DK)
WRC_COLS = slice(WK_COLS.stop, WK_COLS.stop + RC_COLS)
WIN_COLS = WRC_COLS.stop
IN_QK = slice(0, 2 * GLA_HEADS * GLA_DK)
IN_V = slice(IN_QK.stop, IN_QK.stop + VAL_COLS)
IN_Z = slice(IN_V.stop, IN_V.stop + 2 * GATE_RANK)
IN_RC = slice(IN_Z.stop, IN_Z.stop + RC_COLS)
IN_COLS = IN_RC.stop
IN_STAGE_ROWS = 512
SCAN_UNROLL = 2
SCAN_HEADS = 2
MOD_ROWS = 8
ADALN_COLS = 1536
VMEM_LIMIT_BYTES = 56 * 1024 * 1024
STAGE_SLOTS = 12
STAGE_ROWS = 256
GLA_PHASE_PIECES = 8
CONTEXT_EARLY_PIECES = (12, 16)
PIECES_PER_BLOCK = D_MODEL // STAGE_ROWS
IN_STAGE_SLOTS = 4

_NT = (((1,), (1,)), ((), ()))
_BF16 = jnp.bfloat16
_F32 = jnp.float32


def _dot(a, b):
    return jnp.dot(a, b, preferred_element_type=_F32)


def _sigmoid(x):
    return 1.0 / (1.0 + jnp.exp(-x))


def _adaln_kernel(cv_ref, w_ref, b_ref, o_ref):
    cv = cv_ref[...]
    s = (cv * _sigmoid(cv)).astype(_BF16)
    mod = _dot(s, w_ref[...].astype(_BF16)) + b_ref[pl.ds(pl.program_id(0), 1), :]
    for r in range(MOD_ROWS):
        o_ref[r] = mod[r:r + 1, :]


def _adaln(cv, w_mod, b_mod):
    depth = w_mod.shape[0]
    cols = w_mod.shape[2]
    return pl.pallas_call(
        _adaln_kernel,
        grid=(depth, cols // ADALN_COLS),
        in_specs=[
            pl.BlockSpec((MOD_ROWS, D_MODEL), lambda l, j: (0, 0)),
            pl.BlockSpec((None, D_MODEL, ADALN_COLS), lambda l, j: (l, 0, j)),
            pl.BlockSpec((depth, ADALN_COLS), lambda l, j: (0, j)),
        ],
        out_specs=pl.BlockSpec((MOD_ROWS, 1, ADALN_COLS), lambda l, j: (l, 0, j)),
        out_shape=jax.ShapeDtypeStruct((depth * MOD_ROWS, 1, cols), _F32),
        compiler_params=pltpu.CompilerParams(vmem_limit_bytes=VMEM_LIMIT_BYTES),
    )(cv, w_mod, b_mod)


def _per_direction(t):
    lane = lax.broadcasted_iota(jnp.int32, (1, SLAB), 1)
    first = lane < GLA_DK
    slabs = []
    for p in range(GLA_HEADS // 2):
        pair = t[:, p * SLAB:(p + 1) * SLAB]
        swapped = pltpu.roll(pair, GLA_DK, 1)
        slabs += [jnp.where(first, pair, swapped), jnp.where(first, swapped, pair)]
    return jnp.concatenate(slabs, axis=1)


def _conv_and_pool(rc, taps, pw_ref, pool_scale, seg):
    rows = rc.shape[0]
    pos = lax.broadcasted_iota(jnp.int32, (rows, 1), 0) % seg

    def in_segment(v, d):
        return jnp.where((pos + d >= 0) & (pos + d < seg), pltpu.roll(v, (-d) % rows, 0), 0.0)

    c0 = 0
    cb = rc[:, c0:c0 + CONV_WIDTH]
    zc = rc[:, c0 + CONV_WIDTH:c0 + 2 * CONV_WIDTH] * rc[:, c0 + 2 * CONV_WIDTH:c0 + 3 * CONV_WIDTH]
    yc = cb * (taps[0] * in_segment(zc, -1) + taps[1] * zc + taps[2] * in_segment(zc, 1))

    p0 = 3 * CONV_WIDTH
    lane128 = lax.broadcasted_iota(jnp.int32, (1, 2 * POOL_GROUP), 1)
    pooled = []
    for blk in range(POOL_WIDTH // (2 * POOL_GROUP)):
        u = rc[:, p0 + blk * 2 * POOL_GROUP:p0 + (blk + 1) * 2 * POOL_GROUP]
        h_lo, h_hi = POOL_HALF_WINDOWS[2 * blk], POOL_HALF_WINDOWS[2 * blk + 1]
        half = jnp.where(lane128 < POOL_GROUP, h_lo, h_hi)
        ahead, behind, window_sums = u, in_segment(u, -1), {}
        for level in range(h_hi.bit_length()):
            h = 1 << level
            if level > 0:
                ahead = ahead + in_segment(ahead, h // 2)
                behind = behind + in_segment(behind, -(h // 2))
            if h in (h_lo, h_hi):
                window_sums[h] = ahead + behind
        acc = jnp.where(lane128 < POOL_GROUP, window_sums[h_lo], window_sums[h_hi])
        cnt = (jnp.minimum(pos + half, seg) - jnp.maximum(pos - half, 0)).astype(_F32)
        pooled.append(acc / cnt - u)
    pool_in = jnp.concatenate(pooled, axis=1).astype(_BF16)
    return yc, _dot(pool_in, pw_ref[...]) * pool_scale


def _inproj_kernel(x_ref, mod_ref, n1w_ref, win_ref, wvt_ref, gw_ref, gb_ref, cw_ref, pw_ref, ps_ref,
                   tok_ref, vt_ref, ut_ref, dec_ref, *, layer, seg):
    this_layer = slice(layer, layer + 1)
    taps = [cw_ref[t, this_layer, :] for t in range(cw_ref.shape[0])]
    pool_scale = ps_ref[this_layer, :]
    qe_ref, ki_ref, ry_ref = tok_ref.at[:, TOK_QE], tok_ref.at[:, TOK_KI], tok_ref.at[:, TOK_RY]
    vt_tile = vt_ref.shape[2]
    tm = x_ref.shape[0]
    sh1 = mod_ref[:, 0:D_MODEL]
    scale1 = n1w_ref[this_layer, :] * (1.0 + mod_ref[:, D_MODEL:2 * D_MODEL])
    row = lax.broadcasted_iota(jnp.int32, (GLA_CHUNK, GLA_CHUNK), 0)
    col = lax.broadcasted_iota(jnp.int32, (GLA_CHUNK, GLA_CHUNK), 1)
    scan_op = jnp.where(jnp.concatenate([col <= row, col >= row], axis=1), 1.0 / GATE_TAU, 0.0).astype(_BF16)
    lane = lax.broadcasted_iota(jnp.int32, (1, KEY_COLS), 1)
    is_fwd = (lane % SLAB) < GLA_DK
    mid = GLA_CHUNK // 2

    n_sub = tm // SUB_ROWS
    cps = SUB_ROWS // GLA_CHUNK
    sub_rows = [slice(s * SUB_ROWS, (s + 1) * SUB_ROWS) for s in range(n_sub)]
    hbs, projs, gates, q2s, k2s = [], [], [], [], []
    for s in range(n_sub):
        x = x_ref[sub_rows[s], :]
        xn = x * lax.rsqrt(jnp.mean(x * x, axis=-1, keepdims=True) + EPS)
        hbs.append((xn * scale1 + sh1).astype(_BF16))
    for s in range(n_sub):
        proj = lax.dot_general(hbs[s], win_ref[...], _NT, preferred_element_type=_F32)
        projs.append(proj)
        pre = _dot(proj[:, WZ_COLS].astype(_BF16), gw_ref[...]) + gb_ref[...]
        g = jnp.minimum(pre, 0.0) * LOG2_E - jnp.log2(1.0 + jnp.exp2(jnp.abs(pre) * (-LOG2_E)))
        gates.append(g.astype(_BF16))
        q2s.append(_per_direction(proj[:, WQ_COLS] * (GLA_DK ** -0.5)))
        k2s.append(_per_direction(proj[:, WK_COLS]))

    decays = []
    for s in range(n_sub):
        for cl in range(cps):
            g = gates[s][cl * GLA_CHUNK:(cl + 1) * GLA_CHUNK]
            zero = jnp.zeros_like(g)
            by_direction = jnp.concatenate([jnp.where(is_fwd, g, zero), jnp.where(is_fwd, zero, g)], axis=0)
            decays.append(_dot(scan_op, by_direction))

    vtbs = []
    for s in range(n_sub):
        vtb = lax.dot_general(wvt_ref[...], hbs[s], _NT, preferred_element_type=_F32).astype(_BF16)
        first = s * SUB_ROWS
        vt_ref[first // vt_tile, :, first % vt_tile:first % vt_tile + SUB_ROWS] = vtb
        vtbs.append(vtb)
        rc = projs[s][:, WRC_COLS]
        yc, yp = _conv_and_pool(rc[:, VAL_COLS:RC_COLS], taps, pw_ref, pool_scale, seg)
        ry_ref[sub_rows[s], 0:VAL_COLS] = rc[:, 0:VAL_COLS].astype(_BF16)
        ry_ref[sub_rows[s], VAL_COLS:VAL_COLS + CONV_WIDTH] = yc.astype(_BF16)
        ry_ref[sub_rows[s], VAL_COLS + CONV_WIDTH:RY_COLS] = yp.astype(_BF16)

    for s in range(n_sub):
        for cl in range(cps):
            c = s * cps + cl
            lrows = slice(cl * GLA_CHUNK, (cl + 1) * GLA_CHUNK)
            rows = slice(c * GLA_CHUNK, (c + 1) * GLA_CHUNK)
            e = decays[c]
            total = jnp.where(is_fwd, e[GLA_CHUNK - 1:GLA_CHUNK, :], e[0:1, :])
            mvec = jnp.where(is_fwd, e[mid - 1:mid, :], e[mid:mid + 1, :])
            bm = e - mvec
            qe_ref[rows, :] = (q2s[s][lrows] * jnp.exp2(bm)).astype(_BF16)
            k_mid = k2s[s][lrows] * jnp.exp2(-bm)
            ki_ref[rows, :] = k_mid.astype(_BF16)
            kd = (k_mid * jnp.exp2(total - mvec)).astype(_BF16)
            dec_ref[c] = jnp.concatenate([total, mvec, jnp.zeros((6, KEY_COLS), _F32)], axis=0)
            for h in range(GLA_HEADS):
                vt_h = vtbs[s][h * GLA_DV:(h + 1) * GLA_DV, lrows]
                ut_ref[c, h] = _dot(vt_h, kd[:, h * SLAB:(h + 1) * SLAB]).astype(ut_ref.dtype)


def _inproj(x2d, mods3, mod_row_of_tile, layer, win, wvt, wts, tm, seg, vt_tile):
    tokens = x2d.shape[0]
    n_tiles = tokens // tm
    n_chunks = tokens // GLA_CHUNK
    cpt = tm // GLA_CHUNK
    assert SUB_ROWS % seg == 0
    assert tm % vt_tile == 0 and vt_tile % SUB_ROWS == 0
    of_layer = lambda shape: pl.BlockSpec((None,) + shape, lambda j: (layer,) + (0,) * len(shape))
    whole = lambda a: pl.BlockSpec(a.shape, lambda j: (0,) * a.ndim)
    return pl.pallas_call(
        functools.partial(_inproj_kernel, layer=layer, seg=seg),
        grid=(n_tiles,),
        in_specs=[
            pl.BlockSpec((tm, D_MODEL), lambda j: (j, 0)),
            pl.BlockSpec((None, 1, N_MOD * D_MODEL), lambda j: (layer * MOD_ROWS + mod_row_of_tile(j), 0, 0)),
            whole(wts["n1w"]),
            pl.BlockSpec((WIN_COLS, D_MODEL), lambda j: (0, 0)),
            pl.BlockSpec((VAL_COLS, D_MODEL), lambda j: (0, 0)),
            of_layer((2 * GATE_RANK, KEY_COLS)),
            of_layer((1, KEY_COLS)),
            whole(wts["cw"]),
            of_layer((POOL_WIDTH, POOL_WIDTH)),
            whole(wts["ps"]),
        ],
        out_specs=[
            pl.BlockSpec((tm, TOK_COLS), lambda j: (j, 0)),
            pl.BlockSpec((tm // vt_tile, VAL_COLS, vt_tile), lambda j: (j, 0, 0)),
            pl.BlockSpec((cpt, GLA_HEADS, GLA_DV, SLAB), lambda j: (j, 0, 0, 0)),
            pl.BlockSpec((cpt, 8, KEY_COLS), lambda j: (j, 0, 0)),
        ],
        out_shape=[
            jax.ShapeDtypeStruct((tokens, TOK_COLS), _BF16),
            jax.ShapeDtypeStruct((tokens // vt_tile, VAL_COLS, vt_tile), _BF16),
            jax.ShapeDtypeStruct((n_chunks, GLA_HEADS, GLA_DV, SLAB), _BF16),
            jax.ShapeDtypeStruct((n_chunks, 8, KEY_COLS), _F32),
        ],
        compiler_params=pltpu.CompilerParams(vmem_limit_bytes=VMEM_LIMIT_BYTES),
    )(x2d, mods3, wts["n1w"], win, wvt, wts["gw"], wts["gb"], wts["cw"], wts["pw"], wts["ps"])


def _scan_chunks(ut_ref, dec_ref, states, spt_ref, nc, chunk0=0, head0=0):
    lane = lax.broadcasted_iota(jnp.int32, (1, SLAB), 1)
    is_fwd = lane < GLA_DK

    def step(t, states):
        i = chunk0 + t
        j = chunk0 + nc - 1 - t
        new_states = []
        for s in range(SCAN_HEADS):
            h = head0 + s
            lanes = slice(h * SLAB, (h + 1) * SLAB)
            log_decay = jnp.where(is_fwd, dec_ref[i, 0:1, lanes], dec_ref[j, 0:1, lanes])
            log_mid = jnp.where(is_fwd, dec_ref[i, 1:2, lanes], dec_ref[j, 1:2, lanes])
            entering = (states[s] * jnp.exp2(log_mid)).astype(_BF16)
            spt_ref[i, h, :, 0:GLA_DK] = entering[:, 0:GLA_DK]
            spt_ref[j, h, :, GLA_DK:SLAB] = entering[:, GLA_DK:SLAB]
            inc = jnp.where(is_fwd, ut_ref[i, h], ut_ref[j, h]).astype(_F32)
            new_states.append(states[s] * jnp.exp2(log_decay) + inc)
        return tuple(new_states)

    return lax.fori_loop(0, nc, step, tuple(states), unroll=SCAN_UNROLL)


def _scan_kernel(ut_ref, dec_ref, s0_ref, spt_ref, sfin_ref):
    final = _scan_chunks(ut_ref, dec_ref, [s0_ref[h] for h in range(SCAN_HEADS)], spt_ref, ut_ref.shape[0])
    for h in range(SCAN_HEADS):
        sfin_ref[h] = final[h]


def _scan(ut, dec, s0, chunks_per_seq):
    n_chunks = ut.shape[0]
    bsz = n_chunks // chunks_per_seq
    return pl.pallas_call(
        _scan_kernel,
        grid=(bsz, GLA_HEADS // SCAN_HEADS),
        in_specs=[
            pl.BlockSpec((chunks_per_seq, SCAN_HEADS, GLA_DV, SLAB), lambda b, g: (b, g, 0, 0)),
            pl.BlockSpec((chunks_per_seq, 8, SCAN_HEADS * SLAB), lambda b, g: (b, 0, g)),
            pl.BlockSpec((None, SCAN_HEADS, GLA_DV, SLAB), lambda b, g: (b, g, 0, 0)),
        ],
        out_specs=[
            pl.BlockSpec((chunks_per_seq, SCAN_HEADS, GLA_DV, SLAB), lambda b, g: (b, g, 0, 0)),
            pl.BlockSpec((None, SCAN_HEADS, GLA_DV, SLAB), lambda b, g: (b, g, 0, 0)),
        ],
        out_shape=[
            jax.ShapeDtypeStruct((n_chunks, GLA_HEADS, GLA_DV, SLAB), _BF16),
            jax.ShapeDtypeStruct((bsz, GLA_HEADS, GLA_DV, SLAB), _F32),
        ],
        compiler_params=pltpu.CompilerParams(vmem_limit_bytes=VMEM_LIMIT_BYTES),
    )(ut, dec, s0)


def _mix_weight_stager(layer, wo_hbm, w1_hbm, w2_hbm, wo_ref, w1_ref, w2_ref, stage, sems):
    blocks = [(wo_hbm.at[layer], wo_ref)]
    for i in range(D_FF // D_MODEL):
        cols = slice(i * D_MODEL, (i + 1) * D_MODEL)
        blocks.append((w1_hbm.at[layer, :, cols], w1_ref.at[:, cols]))
        blocks.append((w2_hbm.at[layer, cols, :], w2_ref.at[cols, :]))
    pieces = [(src.at[p * STAGE_ROWS:(p + 1) * STAGE_ROWS, :], dst.at[p * STAGE_ROWS:(p + 1) * STAGE_ROWS, :])
              for src, dst in blocks for p in range(PIECES_PER_BLOCK)]
    copies = [pltpu.make_async_copy(src, stage.at[k % STAGE_SLOTS], sems.at[k % STAGE_SLOTS])
              for k, (src, _) in enumerate(pieces)]
    ready = [None]

    def need(n):
        if ready[0] is None:
            for k in range(STAGE_SLOTS):
                copies[k].start()
            ready[0] = 0
        for k in range(ready[0], min(n, len(pieces))):
            copies[k].wait()
            pieces[k][1][...] = stage[k % STAGE_SLOTS].astype(_BF16)
            if k + STAGE_SLOTS < len(pieces):
                copies[k + STAGE_SLOTS].start()
            ready[0] = k + 1

    return need


def _mix_body(x_ref, mod_ref, n2w_ref, tok_ref, vt_ref, spt_ref, gnw_ref, fnw_ref, o_ref, y_scr, wo_ref, w1_ref, w2_ref,
              need, layer, final):
    tm = x_ref.shape[0]
    qe_ref, ki_ref, ry_ref = tok_ref.at[:, TOK_QE], tok_ref.at[:, TOK_KI], tok_ref.at[:, TOK_RY]
    g1 = mod_ref[:, 2 * D_MODEL:3 * D_MODEL]
    sh2 = mod_ref[:, 3 * D_MODEL:4 * D_MODEL]
    scale2 = n2w_ref[layer:layer + 1, :] * (1.0 + mod_ref[:, 4 * D_MODEL:5 * D_MODEL])
    g2 = mod_ref[:, 5 * D_MODEL:6 * D_MODEL]
    row = lax.broadcasted_iota(jnp.int32, (GLA_CHUNK, GLA_CHUNK), 0)
    col = lax.broadcasted_iota(jnp.int32, (GLA_CHUNK, GLA_CHUNK), 1)
    lane = lax.broadcasted_iota(jnp.int32, (GLA_CHUNK, SLAB), 1)
    fwd_lane = lane < GLA_DK
    gnw = gnw_ref[...]
    need(0)

    for c in range(tm // GLA_CHUNK):
        rows = slice(c * GLA_CHUNK, (c + 1) * GLA_CHUNK)
        for h in range(GLA_HEADS):
            lanes = slice(h * SLAB, (h + 1) * SLAB)
            qe = qe_ref[rows, lanes]
            ki = ki_ref[rows, lanes]
            zero = jnp.zeros_like(ki)
            keys = jnp.concatenate([jnp.where(fwd_lane, ki, zero), jnp.where(fwd_lane, zero, ki)], axis=0)
            a2 = lax.dot_general(qe, keys, _NT, preferred_element_type=_F32)
            am = jnp.where(col <= row, a2[:, 0:GLA_CHUNK], 0.0) + jnp.where(col >= row, a2[:, GLA_CHUNK:], 0.0)
            lhs = jnp.concatenate([am.astype(_BF16), qe], axis=1)
            rhs = jnp.concatenate([vt_ref[h * GLA_DV:(h + 1) * GLA_DV, rows], spt_ref[c, h]], axis=1)
            o = lax.dot_general(lhs, rhs, _NT, preferred_element_type=_F32)
            o = o * lax.rsqrt(jnp.mean(o * o, axis=-1, keepdims=True) + EPS) * gnw
            r = ry_ref[rows, h * GLA_DV:(h + 1) * GLA_DV].astype(_F32)
            y_scr[rows, h * GLA_DV:(h + 1) * GLA_DV] = (o * (r * _sigmoid(r))).astype(_BF16)
        need((c + 1) * GLA_PHASE_PIECES // (tm // GLA_CHUNK))

    need(PIECES_PER_BLOCK)
    mixed = _dot(y_scr[...], wo_ref[0:VAL_COLS, :]) + _dot(ry_ref[:, VAL_COLS:RY_COLS], wo_ref[VAL_COLS:D_MODEL, :])
    x1 = x_ref[...] + g1 * mixed
    xn = x1 * lax.rsqrt(jnp.mean(x1 * x1, axis=-1, keepdims=True) + EPS)
    h2 = (xn * scale2 + sh2).astype(_BF16)
    acc = jnp.zeros((tm, D_MODEL), _F32)
    for f in range(D_FF // FF_CHUNK):
        cols = slice(f * FF_CHUNK, (f + 1) * FF_CHUNK)
        blocks_before = 2 * pl.cdiv(cols.stop, D_MODEL)
        need(PIECES_PER_BLOCK * blocks_before)
        a = jnp.maximum(_dot(h2, w1_ref[:, cols]), 0.0)
        need(PIECES_PER_BLOCK * (blocks_before + 1))
        acc = acc + _dot((a * a).astype(_BF16), w2_ref[cols, :])
    out = x1 + g2 * acc
    if final:
        out = out * lax.rsqrt(jnp.mean(out * out, axis=-1, keepdims=True) + EPS) * fnw_ref[...]
    o_ref[...] = out


def _mix_kernel(x_ref, mod_ref, n2w_ref, tok_ref, vt_ref, spt_ref, gnw_ref, wo_hbm, w1_hbm, w2_hbm, fnw_ref,
                o_ref, y_scr, wo_ref, w1_ref, w2_ref, stage, sems, *, layer, final):
    body = functools.partial(_mix_body, x_ref, mod_ref, n2w_ref, tok_ref, vt_ref, spt_ref, gnw_ref, fnw_ref, o_ref, y_scr,
                             wo_ref, w1_ref, w2_ref, layer=layer, final=final)

    def staging_step():
        body(need=_mix_weight_stager(layer, wo_hbm, w1_hbm, w2_hbm, wo_ref, w1_ref, w2_ref, stage, sems))

    first = pl.program_id(0) == 0
    pl.when(first)(staging_step)
    pl.when(jnp.logical_not(first))(functools.partial(body, need=lambda n: None))


def _mix_weight_scratch():
    return [pltpu.VMEM((D_MODEL, D_MODEL), _BF16), pltpu.VMEM((D_MODEL, D_FF), _BF16), pltpu.VMEM((D_FF, D_MODEL), _BF16),
            pltpu.VMEM((STAGE_SLOTS, STAGE_ROWS, D_MODEL), _F32), pltpu.SemaphoreType.DMA((STAGE_SLOTS,))]


def _mix(x2d, mods3, mod_row_of_tile, layer, tok, vt, spt, wts, tm, final):
    tokens = x2d.shape[0]
    cpt = tm // GLA_CHUNK
    of_layer = lambda shape: pl.BlockSpec((None,) + shape, lambda j: (layer,) + (0,) * len(shape),
                                          pipeline_mode=pl.Buffered(1))
    return pl.pallas_call(
        functools.partial(_mix_kernel, layer=layer, final=final),
        grid=(tokens // tm,),
        in_specs=[
            pl.BlockSpec((tm, D_MODEL), lambda j: (j, 0)),
            pl.BlockSpec((None, 1, N_MOD * D_MODEL), lambda j: (layer * MOD_ROWS + mod_row_of_tile(j), 0, 0)),
            pl.BlockSpec(wts["n2w"].shape, lambda j: (0, 0), pipeline_mode=pl.Buffered(1)),
            pl.BlockSpec((tm, TOK_COLS), lambda j: (j, 0)),
            pl.BlockSpec((None, VAL_COLS, tm), lambda j: (j, 0, 0)),
            pl.BlockSpec((cpt, GLA_HEADS, GLA_DV, SLAB), lambda j: (j, 0, 0, 0)),
            of_layer((1, GLA_DV)),
            pl.BlockSpec(memory_space=pl.ANY),
            pl.BlockSpec(memory_space=pl.ANY),
            pl.BlockSpec(memory_space=pl.ANY),
            pl.BlockSpec((1, D_MODEL), lambda j: (0, 0), pipeline_mode=pl.Buffered(1)),
        ],
        out_specs=pl.BlockSpec((tm, D_MODEL), lambda j: (j, 0)),
        out_shape=jax.ShapeDtypeStruct((tokens, D_MODEL), _F32),
        scratch_shapes=[pltpu.VMEM((tm, VAL_COLS), _BF16)] + _mix_weight_scratch(),
        compiler_params=pltpu.CompilerParams(vmem_limit_bytes=VMEM_LIMIT_BYTES),
    )(x2d, mods3, wts["n2w"], tok, vt, spt, wts["gnw"], wts["wo"], wts["w1"], wts["w2"], wts["fnw"])


def _stage_in_projection(layer, w_in_t_hbm, win_ref, wvt_ref, stage, sems):
    pieces = []
    for src, dst_ref, dst in ((IN_Z, win_ref, WZ_COLS.start), (IN_QK, win_ref, WQ_COLS.start),
                              (IN_V, wvt_ref, 0), (IN_RC, win_ref, WRC_COLS.start)):
        for off in range(0, src.stop - src.start, IN_STAGE_ROWS):
            pieces.append((src.start + off, dst_ref, dst + off, min(IN_STAGE_ROWS, src.stop - src.start - off)))
    copies = [pltpu.make_async_copy(w_in_t_hbm.at[layer, s0:s0 + n, :], stage.at[k % IN_STAGE_SLOTS, 0:n, :],
                                    sems.at[k % IN_STAGE_SLOTS]) for k, (s0, _, _, n) in enumerate(pieces)]
    for k in range(IN_STAGE_SLOTS):
        copies[k].start()
    win_ref[WZ_COLS.stop:WQ_COLS.start, :] = jnp.zeros((WQ_COLS.start - WZ_COLS.stop, D_MODEL), _BF16)
    for k, (_, dst_ref, d0, n) in enumerate(pieces):
        copies[k].wait()
        dst_ref[d0:d0 + n, :] = stage[k % IN_STAGE_SLOTS, 0:n, :].astype(_BF16)
        if k + IN_STAGE_SLOTS < len(pieces):
            copies[k + IN_STAGE_SLOTS].start()


def _context_kernel(*refs, layer, seg, chunks_per_seq, with_mix):
    n_in = 9
    x_ref, mod_ref, n1w_ref, w_in_t_hbm = refs[0:4]
    if with_mix:
        n2w_ref, gnw_ref, wo_hbm, w1_hbm, w2_hbm, fnw_ref = refs[n_in:n_in + 6]
        sfin_ref, win_ref, wvt_ref, o_ref = refs[n_in + 6:n_in + 10]
        tok_scr, vt_scr, ut_scr, dec_scr, spt_scr, in_stage, in_sems, y_scr = refs[n_in + 10:n_in + 18]
        mix_weight_scr = refs[n_in + 18:]
    else:
        sfin_ref, win_ref, wvt_ref = refs[n_in:n_in + 3]
        tok_scr, vt_scr, ut_scr, dec_scr, spt_scr, in_stage, in_sems = refs[n_in + 3:]

    need = lambda n: None
    if with_mix:
        wo_ref, w1_ref, w2_ref, mix_stage, mix_sems = mix_weight_scr
        need = _mix_weight_stager(layer, wo_hbm, w1_hbm, w2_hbm, wo_ref, w1_ref, w2_ref, mix_stage, mix_sems)

    _stage_in_projection(layer, w_in_t_hbm, win_ref, wvt_ref, in_stage, in_sems)
    need(0)
    _inproj_kernel(x_ref, mod_ref, n1w_ref, win_ref, wvt_ref, *refs[4:n_in], tok_scr, vt_scr, ut_scr, dec_scr,
                   layer=layer, seg=seg)
    need(CONTEXT_EARLY_PIECES[0])

    sfin_ref[...] = jnp.zeros(sfin_ref.shape, _F32)
    for b in range(ut_scr.shape[0] // chunks_per_seq):
        for head0 in range(0, GLA_HEADS, SCAN_HEADS):
            start = [sfin_ref[b, head0 + s] for s in range(SCAN_HEADS)]
            final = _scan_chunks(ut_scr, dec_scr, start, spt_scr, chunks_per_seq,
                                 chunk0=b * chunks_per_seq, head0=head0)
            for s in range(SCAN_HEADS):
                sfin_ref[b, head0 + s] = final[s]

    if with_mix:
        need(CONTEXT_EARLY_PIECES[1])
        _mix_body(x_ref, mod_ref, n2w_ref, tok_scr, vt_scr.at[0], spt_scr, gnw_ref, fnw_ref, o_ref, y_scr,
                  wo_ref, w1_ref, w2_ref, need=need, layer=layer, final=False)


def _context_layer(xc, mods3, mod_row, layer, wts, ctx_len, with_mix):
    rows = xc.shape[0]
    bsz = rows // ctx_len
    n_chunks = rows // GLA_CHUNK
    assert rows % SUB_ROWS == 0 and SUB_ROWS % ctx_len == 0
    of_layer = lambda shape: pl.BlockSpec((None,) + shape, lambda j: (layer,) + (0,) * len(shape),
                                          pipeline_mode=pl.Buffered(1))
    whole = lambda a: pl.BlockSpec(a.shape, lambda j: (0,) * a.ndim, pipeline_mode=pl.Buffered(1))
    in_specs = [
        pl.BlockSpec((rows, D_MODEL), lambda j: (0, 0)),
        pl.BlockSpec((None, 1, N_MOD * D_MODEL), lambda j: (layer * MOD_ROWS + mod_row, 0, 0)),
        whole(wts["n1w"]),
        pl.BlockSpec(memory_space=pl.ANY),
        of_layer((2 * GATE_RANK, KEY_COLS)),
        of_layer((1, KEY_COLS)),
        whole(wts["cw"]),
        of_layer((POOL_WIDTH, POOL_WIDTH)),
        whole(wts["ps"]),
    ]
    args = [xc, mods3, wts["n1w"], wts["w_in_t"], wts["gw"], wts["gb"], wts["cw"], wts["pw"], wts["ps"]]
    out_specs = [pl.BlockSpec((bsz, GLA_HEADS, GLA_DV, SLAB), lambda j: (0, 0, 0, 0)),
                 pl.BlockSpec((WIN_COLS, D_MODEL), lambda j: (0, 0)),
                 pl.BlockSpec((VAL_COLS, D_MODEL), lambda j: (0, 0))]
    out_shape = [jax.ShapeDtypeStruct((bsz, GLA_HEADS, GLA_DV, SLAB), _F32),
                 jax.ShapeDtypeStruct((WIN_COLS, D_MODEL), _BF16),
                 jax.ShapeDtypeStruct((VAL_COLS, D_MODEL), _BF16)]
    scratch = [
        pltpu.VMEM((rows, TOK_COLS), _BF16),
        pltpu.VMEM((1, VAL_COLS, rows), _BF16),
        pltpu.VMEM((n_chunks, GLA_HEADS, GLA_DV, SLAB), _BF16),
        pltpu.VMEM((n_chunks, 8, KEY_COLS), _F32),
        pltpu.VMEM((n_chunks, GLA_HEADS, GLA_DV, SLAB), _BF16),
        pltpu.VMEM((IN_STAGE_SLOTS, IN_STAGE_ROWS, D_MODEL), _F32),
        pltpu.SemaphoreType.DMA((IN_STAGE_SLOTS,)),
    ]
    if with_mix:
        in_specs += [whole(wts["n2w"]), of_layer((1, GLA_DV))] + [pl.BlockSpec(memory_space=pl.ANY)] * 3
        in_specs += [pl.BlockSpec((1, D_MODEL), lambda j: (0, 0), pipeline_mode=pl.Buffered(1))]
        args += [wts["n2w"], wts["gnw"], wts["wo"], wts["w1"], wts["w2"], wts["fnw"]]
        out_specs.append(pl.BlockSpec((rows, D_MODEL), lambda j: (0, 0)))
        out_shape.append(jax.ShapeDtypeStruct((rows, D_MODEL), _F32))
        scratch += [pltpu.VMEM((rows, VAL_COLS), _BF16)] + _mix_weight_scratch()
    outs = pl.pallas_call(
        functools.partial(_context_kernel, layer=layer, seg=ctx_len, chunks_per_seq=ctx_len // GLA_CHUNK,
                          with_mix=with_mix),
        grid=(1,),
        in_specs=in_specs,
        out_specs=out_specs,
        out_shape=out_shape,
        scratch_shapes=scratch,
        compiler_params=pltpu.CompilerParams(vmem_limit_bytes=VMEM_LIMIT_BYTES),
    )(*args)
    return tuple(outs) if with_mix else tuple(outs) + (None,)


def _prepare_weights(norm1_w, norm2_w, w_in, gate_w, gate_b, gla_norm_w, conv_w, pool_w, pool_scale, w_out,
                     w_mlp1, w_mlp2, final_norm_w):
    depth = w_in.shape[0]
    assert w_in.shape[2] == IN_COLS
    zeros = jnp.zeros((depth, GATE_RANK, GLA_HEADS, GLA_DK), _F32)
    gf = gate_w[:, 0].reshape(depth, GATE_RANK, GLA_HEADS, GLA_DK)
    gb = gate_w[:, 1].reshape(depth, GATE_RANK, GLA_HEADS, GLA_DK)
    gw = jnp.concatenate([jnp.concatenate([gf, zeros], -1), jnp.concatenate([zeros, gb], -1)], 1)
    gbias = jnp.concatenate([gate_b[:, 0].reshape(depth, GLA_HEADS, GLA_DK),
                             gate_b[:, 1].reshape(depth, GLA_HEADS, GLA_DK)], -1)
    n_groups = POOL_WIDTH // POOL_GROUP
    same_group = jnp.eye(n_groups, dtype=_F32)[None, :, None, :, None]
    pw = (pool_w[:, :, :, None, :] * same_group).reshape(depth, POOL_WIDTH, POOL_WIDTH)
    return {
        "n1w": norm1_w,
        "n2w": norm2_w,
        "w_in_t": jnp.swapaxes(w_in, 1, 2),
        "gw": gw.reshape(depth, 2 * GATE_RANK, KEY_COLS).astype(_BF16),
        "gb": gbias.reshape(depth, 1, KEY_COLS),
        "gnw": gla_norm_w.reshape(depth, 1, GLA_DV),
        "cw": jnp.swapaxes(conv_w, 0, 1),
        "pw": pw.astype(_BF16),
        "ps": pool_scale,
        "wo": w_out,
        "w1": w_mlp1,
        "w2": w_mlp2,
        "fnw": final_norm_w.reshape(1, D_MODEL),
    }


def kernel(x, c, ctx, c_ctx, w_mod, b_mod, norm1_w, norm2_w, w_in, gla_gate_w, gla_gate_b, gla_norm_w, conv_w, pool_w, pool_scale, w_out, w_mlp1, w_mlp2, final_norm_w):
    bsz, n, _ = x.shape
    ctx_len = ctx.shape[1]
    depth = w_in.shape[0]
    assert bsz + 1 <= MOD_ROWS and n % LATENT_IN_TILE == 0 and SUB_ROWS % ctx_len == 0
    ctx_row = bsz

    cv = jnp.concatenate([c, c_ctx[None, :], jnp.zeros((MOD_ROWS - bsz - 1, D_MODEL), _F32)], axis=0)
    mods3 = _adaln(cv, w_mod, b_mod)
    wts = _prepare_weights(norm1_w, norm2_w, w_in, gla_gate_w, gla_gate_b, gla_norm_w, conv_w, pool_w, pool_scale,
                           w_out, w_mlp1, w_mlp2, final_norm_w)

    xl = x.reshape(bsz * n, D_MODEL)
    xc = ctx.reshape(bsz * ctx_len, D_MODEL)
    lat_in_row = lambda j: j // (n // LATENT_IN_TILE)
    lat_mix_row = lambda j: j // (n // LATENT_MIX_TILE)

    for l in range(depth):
        last = l == depth - 1
        s_ctx, win, wvt, xc = _context_layer(xc, mods3, ctx_row, l, wts, ctx_len, with_mix=not last)

        tok, vt, ut, dec = _inproj(xl, mods3, lat_in_row, l, win, wvt, wts, LATENT_IN_TILE, GRID_W, LATENT_MIX_TILE)
        spt, _ = _scan(ut, dec, s_ctx, n // GLA_CHUNK)
        xl = _mix(xl, mods3, lat_mix_row, l, tok, vt, spt, wts, LATENT_MIX_TILE, last)
    return xl.reshape(bsz, n, D_MODEL)
```

```python
import functools

import jax
import jax.numpy as jnp
from jax import lax
from jax.experimental import pallas as pl
from jax.experimental.pallas import tpu as pltpu

D_MODEL = 1024
GLA_HEADS = 4
GLA_DK = 64
GLA_DV = 128
SLAB = 2 * GLA_DK
KEY_COLS = GLA_HEADS * SLAB
VAL_COLS = GLA_HEADS * GLA_DV
GATE_RANK = 16
GATE_TAU = 16.0
LOG2_E = 1.4426950408889634
GLA_CHUNK = 128
SUB_ROWS = 256
LATENT_IN_TILE = 1024
LATENT_MIX_TILE = 512
CONV_WIDTH = 256
POOL_WIDTH = 256
POOL_GROUP = 64
POOL_HALF_WINDOWS = (1, 2, 4, 8)
assert all(h & (h - 1) == 0 for h in POOL_HALF_WINDOWS)
D_FF = 4096
FF_CHUNK = 1024
N_MOD = 6
EPS = 1e-6
GRID_W = 64
RC_COLS = VAL_COLS + 3 * CONV_WIDTH + POOL_WIDTH
RY_COLS = VAL_COLS + CONV_WIDTH + POOL_WIDTH
TOK_QE = slice(0, KEY_COLS)
TOK_KI = slice(KEY_COLS, 2 * KEY_COLS)
TOK_RY = slice(2 * KEY_COLS, 2 * KEY_COLS + RY_COLS)
TOK_COLS = TOK_RY.stop
WZ_COLS = slice(0, 2 * GATE_RANK)
WQ_COLS = slice(128, 128 + GLA_HEADS * GLA_DK)
WK_COLS = slice(WQ_COLS.stop, WQ_COLS.stop + GLA_HEADS * GLA_DK)
WRC_COLS = slice(WK_COLS.stop, WK_COLS.stop + RC_COLS)
WIN_COLS = WRC_COLS.stop
IN_QK = slice(0, 2 * GLA_HEADS * GLA_DK)
IN_V = slice(IN_QK.stop, IN_QK.stop + VAL_COLS)
IN_Z = slice(IN_V.stop, IN_V.stop + 2 * GATE_RANK)
IN_RC = slice(IN_Z.stop, IN_Z.stop + RC_COLS)
IN_COLS = IN_RC.stop
IN_STAGE_ROWS = 512
SCAN_UNROLL = 2
SCAN_HEADS = 2
SCAN_RING_SLOTS = 3
MOD_ROWS = 8
ADALN_COLS = 1536
VMEM_LIMIT_BYTES = 56 * 1024 * 1024
STAGE_SLOTS = 12
STAGE_ROWS = 256
GLA_PHASE_PIECES = 8
CONTEXT_EARLY_PIECES = (12, 16)
PIECES_PER_BLOCK = D_MODEL // STAGE_ROWS
IN_STAGE_SLOTS = 4

_NT = (((1,), (1,)), ((), ()))
_BF16 = jnp.bfloat16
_F32 = jnp.float32


def _dot(a, b):
    return jnp.dot(a, b, preferred_element_type=_F32)


def _sigmoid(x):
    return 1.0 / (1.0 + jnp.exp(-x))


def _adaln_kernel(cv_ref, w_ref, b_ref, o_ref):
    cv = cv_ref[...]
    s = (cv * _sigmoid(cv)).astype(_BF16)
    mod = _dot(s, w_ref[...].astype(_BF16)) + b_ref[pl.ds(pl.program_id(0), 1), :]
    for r in range(MOD_ROWS):
        o_ref[r] = mod[r:r + 1, :]


def _adaln(cv, w_mod, b_mod):
    depth = w_mod.shape[0]
    cols = w_mod.shape[2]
    return pl.pallas_call(
        _adaln_kernel,
        grid=(depth, cols // ADALN_COLS),
        in_specs=[
            pl.BlockSpec((MOD_ROWS, D_MODEL), lambda l, j: (0, 0)),
            pl.BlockSpec((None, D_MODEL, ADALN_COLS), lambda l, j: (l, 0, j)),
            pl.BlockSpec((depth, ADALN_COLS), lambda l, j: (0, j)),
        ],
        out_specs=pl.BlockSpec((MOD_ROWS, 1, ADALN_COLS), lambda l, j: (l, 0, j)),
        out_shape=jax.ShapeDtypeStruct((depth * MOD_ROWS, 1, cols), _F32),
        compiler_params=pltpu.CompilerParams(vmem_limit_bytes=VMEM_LIMIT_BYTES),
    )(cv, w_mod, b_mod)


def _per_direction(t):
    lane = lax.broadcasted_iota(jnp.int32, (1, SLAB), 1)
    first = lane < GLA_DK
    slabs = []
    for p in range(GLA_HEADS // 2):
        pair = t[:, p * SLAB:(p + 1) * SLAB]
        swapped = pltpu.roll(pair, GLA_DK, 1)
        slabs += [jnp.where(first, pair, swapped), jnp.where(first, swapped, pair)]
    return jnp.concatenate(slabs, axis=1)


def _conv_and_pool(rc, taps, pw_ref, pool_scale, seg):
    rows = rc.shape[0]
    pos = lax.broadcasted_iota(jnp.int32, (rows, 1), 0) % seg

    def in_segment(v, d):
        return jnp.where((pos + d >= 0) & (pos + d < seg), pltpu.roll(v, (-d) % rows, 0), 0.0)

    c0 = 0
    cb = rc[:, c0:c0 + CONV_WIDTH]
    zc = rc[:, c0 + CONV_WIDTH:c0 + 2 * CONV_WIDTH] * rc[:, c0 + 2 * CONV_WIDTH:c0 + 3 * CONV_WIDTH]
    yc = cb * (taps[0] * in_segment(zc, -1) + taps[1] * zc + taps[2] * in_segment(zc, 1))

    p0 = 3 * CONV_WIDTH
    lane128 = lax.broadcasted_iota(jnp.int32, (1, 2 * POOL_GROUP), 1)
    pooled = []
    for blk in range(POOL_WIDTH // (2 * POOL_GROUP)):
        u = rc[:, p0 + blk * 2 * POOL_GROUP:p0 + (blk + 1) * 2 * POOL_GROUP]
        h_lo, h_hi = POOL_HALF_WINDOWS[2 * blk], POOL_HALF_WINDOWS[2 * blk + 1]
        half = jnp.where(lane128 < POOL_GROUP, h_lo, h_hi)
        ahead, behind, window_sums = u, in_segment(u, -1), {}
        for level in range(h_hi.bit_length()):
            h = 1 << level
            if level > 0:
                ahead = ahead + in_segment(ahead, h // 2)
                behind = behind + in_segment(behind, -(h // 2))
            if h in (h_lo, h_hi):
                window_sums[h] = ahead + behind
        acc = jnp.where(lane128 < POOL_GROUP, window_sums[h_lo], window_sums[h_hi])
        cnt = (jnp.minimum(pos + half, seg) - jnp.maximum(pos - half, 0)).astype(_F32)
        pooled.append(acc / cnt - u)
    pool_in = jnp.concatenate(pooled, axis=1).astype(_BF16)
    return yc, _dot(pool_in, pw_ref[...]) * pool_scale


def _inproj_kernel(x_ref, mod_ref, n1w_ref, win_ref, wvt_ref, gw_ref, gb_ref, cw_ref, pw_ref, ps_ref,
                   tok_ref, vt_ref, ut_ref, dec_ref, *, layer, seg):
    this_layer = slice(layer, layer + 1)
    taps = [cw_ref[t, this_layer, :] for t in range(cw_ref.shape[0])]
    pool_scale = ps_ref[this_layer, :]
    qe_ref, ki_ref, ry_ref = tok_ref.at[:, TOK_QE], tok_ref.at[:, TOK_KI], tok_ref.at[:, TOK_RY]
    vt_tile = vt_ref.shape[2]
    tm = x_ref.shape[0]
    sh1 = mod_ref[:, 0:D_MODEL]
    scale1 = n1w_ref[this_layer, :] * (1.0 + mod_ref[:, D_MODEL:2 * D_MODEL])
    row = lax.broadcasted_iota(jnp.int32, (GLA_CHUNK, GLA_CHUNK), 0)
    col = lax.broadcasted_iota(jnp.int32, (GLA_CHUNK, GLA_CHUNK), 1)
    scan_op = jnp.where(jnp.concatenate([col <= row, col >= row], axis=1), 1.0 / GATE_TAU, 0.0).astype(_BF16)
    lane = lax.broadcasted_iota(jnp.int32, (1, KEY_COLS), 1)
    is_fwd = (lane % SLAB) < GLA_DK
    mid = GLA_CHUNK // 2

    n_sub = tm // SUB_ROWS
    cps = SUB_ROWS // GLA_CHUNK
    sub_rows = [slice(s * SUB_ROWS, (s + 1) * SUB_ROWS) for s in range(n_sub)]
    hbs, projs, gates, q2s, k2s = [], [], [], [], []
    for s in range(n_sub):
        x = x_ref[sub_rows[s], :]
        xn = x * lax.rsqrt(jnp.mean(x * x, axis=-1, keepdims=True) + EPS)
        hbs.append((xn * scale1 + sh1).astype(_BF16))
    for s in range(n_sub):
        proj = lax.dot_general(hbs[s], win_ref[...], _NT, preferred_element_type=_F32)
        projs.append(proj)
        pre = _dot(proj[:, WZ_COLS].astype(_BF16), gw_ref[...]) + gb_ref[...]
        g = jnp.minimum(pre, 0.0) * LOG2_E - jnp.log2(1.0 + jnp.exp2(jnp.abs(pre) * (-LOG2_E)))
        gates.append(g.astype(_BF16))
        q2s.append(_per_direction(proj[:, WQ_COLS] * (GLA_DK ** -0.5)))
        k2s.append(_per_direction(proj[:, WK_COLS]))

    decays = []
    for s in range(n_sub):
        for cl in range(cps):
            g = gates[s][cl * GLA_CHUNK:(cl + 1) * GLA_CHUNK]
            zero = jnp.zeros_like(g)
            by_direction = jnp.concatenate([jnp.where(is_fwd, g, zero), jnp.where(is_fwd, zero, g)], axis=0)
            decays.append(_dot(scan_op, by_direction))

    vtbs = []
    for s in range(n_sub):
        vtb = lax.dot_general(wvt_ref[...], hbs[s], _NT, preferred_element_type=_F32).astype(_BF16)
        first = s * SUB_ROWS
        vt_ref[first // vt_tile, :, first % vt_tile:first % vt_tile + SUB_ROWS] = vtb
        vtbs.append(vtb)
        rc = projs[s][:, WRC_COLS]
        yc, yp = _conv_and_pool(rc[:, VAL_COLS:RC_COLS], taps, pw_ref, pool_scale, seg)
        ry_ref[sub_rows[s], 0:VAL_COLS] = rc[:, 0:VAL_COLS].astype(_BF16)
        ry_ref[sub_rows[s], VAL_COLS:VAL_COLS + CONV_WIDTH] = yc.astype(_BF16)
        ry_ref[sub_rows[s], VAL_COLS + CONV_WIDTH:RY_COLS] = yp.astype(_BF16)

    for s in range(n_sub):
        for cl in range(cps):
            c = s * cps + cl
            lrows = slice(cl * GLA_CHUNK, (cl + 1) * GLA_CHUNK)
            rows = slice(c * GLA_CHUNK, (c + 1) * GLA_CHUNK)
            e = decays[c]
            total = jnp.where(is_fwd, e[GLA_CHUNK - 1:GLA_CHUNK, :], e[0:1, :])
            mvec = jnp.where(is_fwd, e[mid - 1:mid, :], e[mid:mid + 1, :])
            bm = e - mvec
            qe_ref[rows, :] = (q2s[s][lrows] * jnp.exp2(bm)).astype(_BF16)
            k_mid = k2s[s][lrows] * jnp.exp2(-bm)
            ki_ref[rows, :] = k_mid.astype(_BF16)
            kd = (k_mid * jnp.exp2(total - mvec)).astype(_BF16)
            dec_ref[c] = jnp.concatenate([total, mvec, jnp.zeros((6, KEY_COLS), _F32)], axis=0)
            for h in range(GLA_HEADS):
                vt_h = vtbs[s][h * GLA_DV:(h + 1) * GLA_DV, lrows]
                ut_ref[c, h] = _dot(vt_h, kd[:, h * SLAB:(h + 1) * SLAB]).astype(ut_ref.dtype)


def _inproj(x2d, mods3, mod_row_of_tile, layer, win, wvt, wts, tm, seg, vt_tile):
    tokens = x2d.shape[0]
    n_tiles = tokens // tm
    n_chunks = tokens // GLA_CHUNK
    cpt = tm // GLA_CHUNK
    assert SUB_ROWS % seg == 0
    assert tm % vt_tile == 0 and vt_tile % SUB_ROWS == 0
    of_layer = lambda shape: pl.BlockSpec((None,) + shape, lambda j: (layer,) + (0,) * len(shape))
    whole = lambda a: pl.BlockSpec(a.shape, lambda j: (0,) * a.ndim)
    return pl.pallas_call(
        functools.partial(_inproj_kernel, layer=layer, seg=seg),
        grid=(n_tiles,),
        in_specs=[
            pl.BlockSpec((tm, D_MODEL), lambda j: (j, 0)),
            pl.BlockSpec((None, 1, N_MOD * D_MODEL), lambda j: (layer * MOD_ROWS + mod_row_of_tile(j), 0, 0)),
            whole(wts["n1w"]),
            pl.BlockSpec((WIN_COLS, D_MODEL), lambda j: (0, 0)),
            pl.BlockSpec((VAL_COLS, D_MODEL), lambda j: (0, 0)),
            of_layer((2 * GATE_RANK, KEY_COLS)),
            of_layer((1, KEY_COLS)),
            whole(wts["cw"]),
            of_layer((POOL_WIDTH, POOL_WIDTH)),
            whole(wts["ps"]),
        ],
        out_specs=[
            pl.BlockSpec((tm, TOK_COLS), lambda j: (j, 0)),
            pl.BlockSpec((tm // vt_tile, VAL_COLS, vt_tile), lambda j: (j, 0, 0)),
            pl.BlockSpec((cpt, GLA_HEADS, GLA_DV, SLAB), lambda j: (j, 0, 0, 0)),
            pl.BlockSpec((cpt, 8, KEY_COLS), lambda j: (j, 0, 0)),
        ],
        out_shape=[
            jax.ShapeDtypeStruct((tokens, TOK_COLS), _BF16),
            jax.ShapeDtypeStruct((tokens // vt_tile, VAL_COLS, vt_tile), _BF16),
            jax.ShapeDtypeStruct((n_chunks, GLA_HEADS, GLA_DV, SLAB), _BF16),
            jax.ShapeDtypeStruct((n_chunks, 8, KEY_COLS), _F32),
        ],
        compiler_params=pltpu.CompilerParams(vmem_limit_bytes=VMEM_LIMIT_BYTES),
    )(x2d, mods3, wts["n1w"], win, wvt, wts["gw"], wts["gb"], wts["cw"], wts["pw"], wts["ps"])


def _scan_chunks(ut_ref, dec_ref, states, spt_ref, nc, chunk0=0, head0=0, ut_chunk0=None, ut_head0=None):
    lane = lax.broadcasted_iota(jnp.int32, (1, SLAB), 1)
    is_fwd = lane < GLA_DK
    ut_shift = 0 if ut_chunk0 is None else ut_chunk0 - chunk0
    ut_head_shift = 0 if ut_head0 is None else ut_head0 - head0

    def step(t, states):
        i = chunk0 + t
        j = chunk0 + nc - 1 - t
        new_states = []
        for s in range(SCAN_HEADS):
            h = head0 + s
            lanes = slice(h * SLAB, (h + 1) * SLAB)
            log_decay = jnp.where(is_fwd, dec_ref[i, 0:1, lanes], dec_ref[j, 0:1, lanes])
            log_mid = jnp.where(is_fwd, dec_ref[i, 1:2, lanes], dec_ref[j, 1:2, lanes])
            entering = (states[s] * jnp.exp2(log_mid)).astype(_BF16)
            spt_ref[i, h, :, 0:GLA_DK] = entering[:, 0:GLA_DK]
            spt_ref[j, h, :, GLA_DK:SLAB] = entering[:, GLA_DK:SLAB]
            hu = h + ut_head_shift
            inc = jnp.where(is_fwd, ut_ref[i + ut_shift, hu], ut_ref[j + ut_shift, hu]).astype(_F32)
            new_states.append(states[s] * jnp.exp2(log_decay) + inc)
        return tuple(new_states)

    return lax.fori_loop(0, nc, step, tuple(states), unroll=SCAN_UNROLL)


def _scan_kernel(ut_hbm, dec_ref, s0_ref, spt_ref, sfin_ref, ring, sems, *, chunks_per_seq, n_steps):
    n_groups = pl.num_programs(1)
    step = pl.program_id(0) * n_groups + pl.program_id(1)

    def fetch(s):
        slot = s % SCAN_RING_SLOTS
        src = ut_hbm.at[pl.ds((s // n_groups) * chunks_per_seq, chunks_per_seq),
                        pl.ds((s % n_groups) * SCAN_HEADS, SCAN_HEADS)]
        return pltpu.make_async_copy(src, ring.at[pl.ds(slot * chunks_per_seq, chunks_per_seq)], sems.at[slot])

    @pl.when(step == 0)
    def _():
        for s in range(min(SCAN_RING_SLOTS - 1, n_steps)):
            fetch(s).start()

    fetch(step).wait()

    @pl.when(step + SCAN_RING_SLOTS - 1 < n_steps)
    def _():
        fetch(step + SCAN_RING_SLOTS - 1).start()

    final = _scan_chunks(ring, dec_ref, [s0_ref[h] for h in range(SCAN_HEADS)], spt_ref, chunks_per_seq,
                         ut_chunk0=(step % SCAN_RING_SLOTS) * chunks_per_seq)
    for h in range(SCAN_HEADS):
        sfin_ref[h] = final[h]


def _scan(ut, dec, s0, chunks_per_seq):
    n_chunks = ut.shape[0]
    bsz = n_chunks // chunks_per_seq
    n_groups = GLA_HEADS // SCAN_HEADS
    return pl.pallas_call(
        functools.partial(_scan_kernel, chunks_per_seq=chunks_per_seq, n_steps=bsz * n_groups),
        grid=(bsz, n_groups),
        in_specs=[
            pl.BlockSpec(memory_space=pl.ANY),
            pl.BlockSpec((chunks_per_seq, 8, SCAN_HEADS * SLAB), lambda b, g: (b, 0, g)),
            pl.BlockSpec((None, SCAN_HEADS, GLA_DV, SLAB), lambda b, g: (b, g, 0, 0)),
        ],
        out_specs=[
            pl.BlockSpec((chunks_per_seq, SCAN_HEADS, GLA_DV, SLAB), lambda b, g: (b, g, 0, 0)),
            pl.BlockSpec((None, SCAN_HEADS, GLA_DV, SLAB), lambda b, g: (b, g, 0, 0)),
        ],
        out_shape=[
            jax.ShapeDtypeStruct((n_chunks, GLA_HEADS, GLA_DV, SLAB), _BF16),
            jax.ShapeDtypeStruct((bsz, GLA_HEADS, GLA_DV, SLAB), _F32),
        ],
        scratch_shapes=[pltpu.VMEM((SCAN_RING_SLOTS * chunks_per_seq, SCAN_HEADS, GLA_DV, SLAB), _BF16),
                        pltpu.SemaphoreType.DMA((SCAN_RING_SLOTS,))],
        compiler_params=pltpu.CompilerParams(dimension_semantics=("arbitrary", "arbitrary"),
                                             vmem_limit_bytes=VMEM_LIMIT_BYTES),
    )(ut, dec, s0)


def _mix_weight_stager(layer, wo_hbm, w1_hbm, w2_hbm, wo_ref, w1_ref, w2_ref, stage, sems):
    blocks = [(wo_hbm.at[layer], wo_ref)]
    for i in range(D_FF // D_MODEL):
        cols = slice(i * D_MODEL, (i + 1) * D_MODEL)
        blocks.append((w1_hbm.at[layer, :, cols], w1_ref.at[:, cols]))
        blocks.append((w2_hbm.at[layer, cols, :], w2_ref.at[cols, :]))
    pieces = [(src.at[p * STAGE_ROWS:(p + 1) * STAGE_ROWS, :], dst.at[p * STAGE_ROWS:(p + 1) * STAGE_ROWS, :])
              for src, dst in blocks for p in range(PIECES_PER_BLOCK)]
    copies = [pltpu.make_async_copy(src, stage.at[k % STAGE_SLOTS], sems.at[k % STAGE_SLOTS])
              for k, (src, _) in enumerate(pieces)]
    ready = [None]

    def need(n):
        if ready[0] is None:
            for k in range(STAGE_SLOTS):
                copies[k].start()
            ready[0] = 0
        for k in range(ready[0], min(n, len(pieces))):
            copies[k].wait()
            pieces[k][1][...] = stage[k % STAGE_SLOTS].astype(_BF16)
            if k + STAGE_SLOTS < len(pieces):
                copies[k + STAGE_SLOTS].start()
            ready[0] = k + 1

    return need


def _mix_body(x_ref, mod_ref, n2w_ref, tok_ref, vt_ref, spt_ref, gnw_ref, fnw_ref, o_ref, y_scr, wo_ref, w1_ref, w2_ref,
              need, layer, final):
    tm = x_ref.shape[0]
    qe_ref, ki_ref, ry_ref = tok_ref.at[:, TOK_QE], tok_ref.at[:, TOK_KI], tok_ref.at[:, TOK_RY]
    g1 = mod_ref[:, 2 * D_MODEL:3 * D_MODEL]
    sh2 = mod_ref[:, 3 * D_MODEL:4 * D_MODEL]
    scale2 = n2w_ref[layer:layer + 1, :] * (1.0 + mod_ref[:, 4 * D_MODEL:5 * D_MODEL])
    g2 = mod_ref[:, 5 * D_MODEL:6 * D_MODEL]
    row = lax.broadcasted_iota(jnp.int32, (GLA_CHUNK, GLA_CHUNK), 0)
    col = lax.broadcasted_iota(jnp.int32, (GLA_CHUNK, GLA_CHUNK), 1)
    lane = lax.broadcasted_iota(jnp.int32, (GLA_CHUNK, SLAB), 1)
    fwd_lane = lane < GLA_DK
    gnw = gnw_ref[...]
    need(0)

    for c in range(tm // GLA_CHUNK):
        rows = slice(c * GLA_CHUNK, (c + 1) * GLA_CHUNK)
        for h in range(GLA_HEADS):
            lanes = slice(h * SLAB, (h + 1) * SLAB)
            qe = qe_ref[rows, lanes]
            ki = ki_ref[rows, lanes]
            zero = jnp.zeros_like(ki)
            keys = jnp.concatenate([jnp.where(fwd_lane, ki, zero), jnp.where(fwd_lane, zero, ki)], axis=0)
            a2 = lax.dot_general(qe, keys, _NT, preferred_element_type=_F32)
            am = jnp.where(col <= row, a2[:, 0:GLA_CHUNK], 0.0) + jnp.where(col >= row, a2[:, GLA_CHUNK:], 0.0)
            lhs = jnp.concatenate([am.astype(_BF16), qe], axis=1)
            rhs = jnp.concatenate([vt_ref[h * GLA_DV:(h + 1) * GLA_DV, rows], spt_ref[c, h]], axis=1)
            o = lax.dot_general(lhs, rhs, _NT, preferred_element_type=_F32)
            o = o * lax.rsqrt(jnp.mean(o * o, axis=-1, keepdims=True) + EPS) * gnw
            r = ry_ref[rows, h * GLA_DV:(h + 1) * GLA_DV].astype(_F32)
            y_scr[rows, h * GLA_DV:(h + 1) * GLA_DV] = (o * (r * _sigmoid(r))).astype(_BF16)
        need((c + 1) * GLA_PHASE_PIECES // (tm // GLA_CHUNK))

    need(PIECES_PER_BLOCK)
    mixed = _dot(y_scr[...], wo_ref[0:VAL_COLS, :]) + _dot(ry_ref[:, VAL_COLS:RY_COLS], wo_ref[VAL_COLS:D_MODEL, :])
    x1 = x_ref[...] + g1 * mixed
    xn = x1 * lax.rsqrt(jnp.mean(x1 * x1, axis=-1, keepdims=True) + EPS)
    h2 = (xn * scale2 + sh2).astype(_BF16)
    acc = jnp.zeros((tm, D_MODEL), _F32)
    for f in range(D_FF // FF_CHUNK):
        cols = slice(f * FF_CHUNK, (f + 1) * FF_CHUNK)
        blocks_before = 2 * pl.cdiv(cols.stop, D_MODEL)
        need(PIECES_PER_BLOCK * blocks_before)
        a = jnp.maximum(_dot(h2, w1_ref[:, cols]), 0.0)
        need(PIECES_PER_BLOCK * (blocks_before + 1))
        acc = acc + _dot((a * a).astype(_BF16), w2_ref[cols, :])
    out = x1 + g2 * acc
    if final:
        out = out * lax.rsqrt(jnp.mean(out * out, axis=-1, keepdims=True) + EPS) * fnw_ref[...]
    o_ref[...] = out


def _mix_kernel(x_ref, mod_ref, n2w_ref, tok_ref, vt_ref, spt_ref, gnw_ref, wo_hbm, w1_hbm, w2_hbm, fnw_ref,
                o_ref, y_scr, wo_ref, w1_ref, w2_ref, stage, sems, *, layer, final):
    body = functools.partial(_mix_body, x_ref, mod_ref, n2w_ref, tok_ref, vt_ref, spt_ref, gnw_ref, fnw_ref, o_ref, y_scr,
                             wo_ref, w1_ref, w2_ref, layer=layer, final=final)

    def staging_step():
        body(need=_mix_weight_stager(layer, wo_hbm, w1_hbm, w2_hbm, wo_ref, w1_ref, w2_ref, stage, sems))

    first = pl.program_id(0) == 0
    pl.when(first)(staging_step)
    pl.when(jnp.logical_not(first))(functools.partial(body, need=lambda n: None))


def _mix_weight_scratch():
    return [pltpu.VMEM((D_MODEL, D_MODEL), _BF16), pltpu.VMEM((D_MODEL, D_FF), _BF16), pltpu.VMEM((D_FF, D_MODEL), _BF16),
            pltpu.VMEM((STAGE_SLOTS, STAGE_ROWS, D_MODEL), _F32), pltpu.SemaphoreType.DMA((STAGE_SLOTS,))]


def _mix(x2d, mods3, mod_row_of_tile, layer, tok, vt, spt, wts, tm, final):
    tokens = x2d.shape[0]
    cpt = tm // GLA_CHUNK
    of_layer = lambda shape: pl.BlockSpec((None,) + shape, lambda j: (layer,) + (0,) * len(shape),
                                          pipeline_mode=pl.Buffered(1))
    return pl.pallas_call(
        functools.partial(_mix_kernel, layer=layer, final=final),
        grid=(tokens // tm,),
        in_specs=[
            pl.BlockSpec((tm, D_MODEL), lambda j: (j, 0)),
            pl.BlockSpec((None, 1, N_MOD * D_MODEL), lambda j: (layer * MOD_ROWS + mod_row_of_tile(j), 0, 0)),
            pl.BlockSpec(wts["n2w"].shape, lambda j: (0, 0), pipeline_mode=pl.Buffered(1)),
            pl.BlockSpec((tm, TOK_COLS), lambda j: (j, 0)),
            pl.BlockSpec((None, VAL_COLS, tm), lambda j: (j, 0, 0)),
            pl.BlockSpec((cpt, GLA_HEADS, GLA_DV, SLAB), lambda j: (j, 0, 0, 0)),
            of_layer((1, GLA_DV)),
            pl.BlockSpec(memory_space=pl.ANY),
            pl.BlockSpec(memory_space=pl.ANY),
            pl.BlockSpec(memory_space=pl.ANY),
            pl.BlockSpec((1, D_MODEL), lambda j: (0, 0), pipeline_mode=pl.Buffered(1)),
        ],
        out_specs=pl.BlockSpec((tm, D_MODEL), lambda j: (j, 0)),
        out_shape=jax.ShapeDtypeStruct((tokens, D_MODEL), _F32),
        scratch_shapes=[pltpu.VMEM((tm, VAL_COLS), _BF16)] + _mix_weight_scratch(),
        compiler_params=pltpu.CompilerParams(vmem_limit_bytes=VMEM_LIMIT_BYTES),
    )(x2d, mods3, wts["n2w"], tok, vt, spt, wts["gnw"], wts["wo"], wts["w1"], wts["w2"], wts["fnw"])


def _stage_in_projection(layer, w_in_t_hbm, win_ref, wvt_ref, stage, sems):
    pieces = []
    for src, dst_ref, dst in ((IN_Z, win_ref, WZ_COLS.start), (IN_QK, win_ref, WQ_COLS.start),
                              (IN_V, wvt_ref, 0), (IN_RC, win_ref, WRC_COLS.start)):
        for off in range(0, src.stop - src.start, IN_STAGE_ROWS):
            pieces.append((src.start + off, dst_ref, dst + off, min(IN_STAGE_ROWS, src.stop - src.start - off)))
    copies = [pltpu.make_async_copy(w_in_t_hbm.at[layer, s0:s0 + n, :], stage.at[k % IN_STAGE_SLOTS, 0:n, :],
                                    sems.at[k % IN_STAGE_SLOTS]) for k, (s0, _, _, n) in enumerate(pieces)]
    for k in range(IN_STAGE_SLOTS):
        copies[k].start()
    win_ref[WZ_COLS.stop:WQ_COLS.start, :] = jnp.zeros((WQ_COLS.start - WZ_COLS.stop, D_MODEL), _BF16)
    for k, (_, dst_ref, d0, n) in enumerate(pieces):
        copies[k].wait()
        dst_ref[d0:d0 + n, :] = stage[k % IN_STAGE_SLOTS, 0:n, :].astype(_BF16)
        if k + IN_STAGE_SLOTS < len(pieces):
            copies[k + IN_STAGE_SLOTS].start()


def _context_kernel(*refs, layer, seg, chunks_per_seq, with_mix):
    n_in = 9
    x_ref, mod_ref, n1w_ref, w_in_t_hbm = refs[0:4]
    if with_mix:
        n2w_ref, gnw_ref, wo_hbm, w1_hbm, w2_hbm, fnw_ref = refs[n_in:n_in + 6]
        sfin_ref, win_ref, wvt_ref, o_ref = refs[n_in + 6:n_in + 10]
        tok_scr, vt_scr, ut_scr, dec_scr, spt_scr, in_stage, in_sems, y_scr = refs[n_in + 10:n_in + 18]
        mix_weight_scr = refs[n_in + 18:]
    else:
        sfin_ref, win_ref, wvt_ref = refs[n_in:n_in + 3]
        tok_scr, vt_scr, ut_scr, dec_scr, spt_scr, in_stage, in_sems = refs[n_in + 3:]

    need = lambda n: None
    if with_mix:
        wo_ref, w1_ref, w2_ref, mix_stage, mix_sems = mix_weight_scr
        need = _mix_weight_stager(layer, wo_hbm, w1_hbm, w2_hbm, wo_ref, w1_ref, w2_ref, mix_stage, mix_sems)

    _stage_in_projection(layer, w_in_t_hbm, win_ref, wvt_ref, in_stage, in_sems)
    need(0)
    _inproj_kernel(x_ref, mod_ref, n1w_ref, win_ref, wvt_ref, *refs[4:n_in], tok_scr, vt_scr, ut_scr, dec_scr,
                   layer=layer, seg=seg)
    need(CONTEXT_EARLY_PIECES[0])

    sfin_ref[...] = jnp.zeros(sfin_ref.shape, _F32)
    for b in range(ut_scr.shape[0] // chunks_per_seq):
        for head0 in range(0, GLA_HEADS, SCAN_HEADS):
            start = [sfin_ref[b, head0 + s] for s in range(SCAN_HEADS)]
            final = _scan_chunks(ut_scr, dec_scr, start, spt_scr, chunks_per_seq,
                                 chunk0=b * chunks_per_seq, head0=head0)
            for s in range(SCAN_HEADS):
                sfin_ref[b, head0 + s] = final[s]

    if with_mix:
        need(CONTEXT_EARLY_PIECES[1])
        _mix_body(x_ref, mod_ref, n2w_ref, tok_scr, vt_scr.at[0], spt_scr, gnw_ref, fnw_ref, o_ref, y_scr,
                  wo_ref, w1_ref, w2_ref, need=need, layer=layer, final=False)


def _context_layer(xc, mods3, mod_row, layer, wts, ctx_len, with_mix):
    rows = xc.shape[0]
    bsz = rows // ctx_len
    n_chunks = rows // GLA_CHUNK
    assert rows % SUB_ROWS == 0 and SUB_ROWS % ctx_len == 0
    of_layer = lambda shape: pl.BlockSpec((None,) + shape, lambda j: (layer,) + (0,) * len(shape),
                                          pipeline_mode=pl.Buffered(1))
    whole = lambda a: pl.BlockSpec(a.shape, lambda j: (0,) * a.ndim, pipeline_mode=pl.Buffered(1))
    in_specs = [
        pl.BlockSpec((rows, D_MODEL), lambda j: (0, 0)),
        pl.BlockSpec((None, 1, N_MOD * D_MODEL), lambda j: (layer * MOD_ROWS + mod_row, 0, 0)),
        whole(wts["n1w"]),
        pl.BlockSpec(memory_space=pl.ANY),
        of_layer((2 * GATE_RANK, KEY_COLS)),
        of_layer((1, KEY_COLS)),
        whole(wts["cw"]),
        of_layer((POOL_WIDTH, POOL_WIDTH)),
        whole(wts["ps"]),
    ]
    args = [xc, mods3, wts["n1w"], wts["w_in_t"], wts["gw"], wts["gb"], wts["cw"], wts["pw"], wts["ps"]]
    out_specs = [pl.BlockSpec((bsz, GLA_HEADS, GLA_DV, SLAB), lambda j: (0, 0, 0, 0)),
                 pl.BlockSpec((WIN_COLS, D_MODEL), lambda j: (0, 0)),
                 pl.BlockSpec((VAL_COLS, D_MODEL), lambda j: (0, 0))]
    out_shape = [jax.ShapeDtypeStruct((bsz, GLA_HEADS, GLA_DV, SLAB), _F32),
                 jax.ShapeDtypeStruct((WIN_COLS, D_MODEL), _BF16),
                 jax.ShapeDtypeStruct((VAL_COLS, D_MODEL), _BF16)]
    scratch = [
        pltpu.VMEM((rows, TOK_COLS), _BF16),
        pltpu.VMEM((1, VAL_COLS, rows), _BF16),
        pltpu.VMEM((n_chunks, GLA_HEADS, GLA_DV, SLAB), _BF16),
        pltpu.VMEM((n_chunks, 8, KEY_COLS), _F32),
        pltpu.VMEM((n_chunks, GLA_HEADS, GLA_DV, SLAB), _BF16),
        pltpu.VMEM((IN_STAGE_SLOTS, IN_STAGE_ROWS, D_MODEL), _F32),
        pltpu.SemaphoreType.DMA((IN_STAGE_SLOTS,)),
    ]
    if with_mix:
        in_specs += [whole(wts["n2w"]), of_layer((1, GLA_DV))] + [pl.BlockSpec(memory_space=pl.ANY)] * 3
        in_specs += [pl.BlockSpec((1, D_MODEL), lambda j: (0, 0), pipeline_mode=pl.Buffered(1))]
        args += [wts["n2w"], wts["gnw"], wts["wo"], wts["w1"], wts["w2"], wts["fnw"]]
        out_specs.append(pl.BlockSpec((rows, D_MODEL), lambda j: (0, 0)))
        out_shape.append(jax.ShapeDtypeStruct((rows, D_MODEL), _F32))
        scratch += [pltpu.VMEM((rows, VAL_COLS), _BF16)] + _mix_weight_scratch()
    outs = pl.pallas_call(
        functools.partial(_context_kernel, layer=layer, seg=ctx_len, chunks_per_seq=ctx_len // GLA_CHUNK,
                          with_mix=with_mix),
        grid=(1,),
        in_specs=in_specs,
        out_specs=out_specs,
        out_shape=out_shape,
        scratch_shapes=scratch,
        compiler_params=pltpu.CompilerParams(vmem_limit_bytes=VMEM_LIMIT_BYTES),
    )(*args)
    return tuple(outs) if with_mix else tuple(outs) + (None,)


def _prepare_weights(norm1_w, norm2_w, w_in, gate_w, gate_b, gla_norm_w, conv_w, pool_w, pool_scale, w_out,
                     w_mlp1, w_mlp2, final_norm_w):
    depth = w_in.shape[0]
    assert w_in.shape[2] == IN_COLS
    zeros = jnp.zeros((depth, GATE_RANK, GLA_HEADS, GLA_DK), _F32)
    gf = gate_w[:, 0].reshape(depth, GATE_RANK, GLA_HEADS, GLA_DK)
    gb = gate_w[:, 1].reshape(depth, GATE_RANK, GLA_HEADS, GLA_DK)
    gw = jnp.concatenate([jnp.concatenate([gf, zeros], -1), jnp.concatenate([zeros, gb], -1)], 1)
    gbias = jnp.concatenate([gate_b[:, 0].reshape(depth, GLA_HEADS, GLA_DK),
                             gate_b[:, 1].reshape(depth, GLA_HEADS, GLA_DK)], -1)
    n_groups = POOL_WIDTH // POOL_GROUP
    same_group = jnp.eye(n_groups, dtype=_F32)[None, :, None, :, None]
    pw = (pool_w[:, :, :, None, :] * same_group).reshape(depth, POOL_WIDTH, POOL_WIDTH)
    return {
        "n1w": norm1_w,
        "n2w": norm2_w,
        "w_in_t": jnp.swapaxes(w_in, 1, 2),
        "gw": gw.reshape(depth, 2 * GATE_RANK, KEY_COLS).astype(_BF16),
        "gb": gbias.reshape(depth, 1, KEY_COLS),
        "gnw": gla_norm_w.reshape(depth, 1, GLA_DV),
        "cw": jnp.swapaxes(conv_w, 0, 1),
        "pw": pw.astype(_BF16),
        "ps": pool_scale,
        "wo": w_out,
        "w1": w_mlp1,
        "w2": w_mlp2,
        "fnw": final_norm_w.reshape(1, D_MODEL),
    }


def kernel(x, c, ctx, c_ctx, w_mod, b_mod, norm1_w, norm2_w, w_in, gla_gate_w, gla_gate_b, gla_norm_w, conv_w, pool_w, pool_scale, w_out, w_mlp1, w_mlp2, final_norm_w):
    bsz, n, _ = x.shape
    ctx_len = ctx.shape[1]
    depth = w_in.shape[0]
    assert bsz + 1 <= MOD_ROWS and n % LATENT_IN_TILE == 0 and SUB_ROWS % ctx_len == 0
    ctx_row = bsz

    cv = jnp.concatenate([c, c_ctx[None, :], jnp.zeros((MOD_ROWS - bsz - 1, D_MODEL), _F32)], axis=0)
    mods3 = _adaln(cv, w_mod, b_mod)
    wts = _prepare_weights(norm1_w, norm2_w, w_in, gla_gate_w, gla_gate_b, gla_norm_w, conv_w, pool_w, pool_scale,
                           w_out, w_mlp1, w_mlp2, final_norm_w)

    xl = x.reshape(bsz * n, D_MODEL)
    xc = ctx.reshape(bsz * ctx_len, D_MODEL)
    lat_in_row = lambda j: j // (n // LATENT_IN_TILE)
    lat_mix_row = lambda j: j // (n // LATENT_MIX_TILE)

    for l in range(depth):
        last = l == depth - 1
        s_ctx, win, wvt, xc = _context_layer(xc, mods3, ctx_row, l, wts, ctx_len, with_mix=not last)

        tok, vt, ut, dec = _inproj(xl, mods3, lat_in_row, l, win, wvt, wts, LATENT_IN_TILE, GRID_W, LATENT_MIX_TILE)
        spt, _ = _scan(ut, dec, s_ctx, n // GLA_CHUNK)
        xl = _mix(xl, mods3, lat_mix_row, l, tok, vt, spt, wts, LATENT_MIX_TILE, last)
    return xl.reshape(bsz, n, D_MODEL)
```

```python
import functools

import jax
import jax.numpy as jnp
from jax import lax
from jax.experimental import pallas as pl
from jax.experimental.pallas import tpu as pltpu

D_MODEL = 1024
GLA_HEADS = 4
GLA_DK = 64
GLA_DV = 128
SLAB = 2 * GLA_DK
KEY_COLS = GLA_HEADS * SLAB
VAL_COLS = GLA_HEADS * GLA_DV
GATE_RANK = 16
GATE_TAU = 16.0
LOG2_E = 1.4426950408889634
GLA_CHUNK = 128
SUB_ROWS = 256
LATENT_IN_TILE = 1024
LATENT_MIX_TILE = 512
CONV_WIDTH = 256
POOL_WIDTH = 256
POOL_GROUP = 64
POOL_HALF_WINDOWS = (1, 2, 4, 8)
assert all(h & (h - 1) == 0 for h in POOL_HALF_WINDOWS)
D_FF = 4096
FF_CHUNK = 1024
N_MOD = 6
EPS = 1e-6
GRID_W = 64
RC_COLS = VAL_COLS + 3 * CONV_WIDTH + POOL_WIDTH
RY_COLS = VAL_COLS + CONV_WIDTH + POOL_WIDTH
TOK_QE = slice(0, KEY_COLS)
TOK_KI = slice(KEY_COLS, 2 * KEY_COLS)
TOK_RY = slice(2 * KEY_COLS, 2 * KEY_COLS + RY_COLS)
TOK_COLS = TOK_RY.stop
WZ_COLS = slice(0, 2 * GATE_RANK)
WQ_COLS = slice(128, 128 + GLA_HEADS * GLA_DK)
WK_COLS = slice(WQ_COLS.stop, WQ_COLS.stop + GLA_HEADS * GLA_DK)
WRC_COLS = slice(WK_COLS.stop, WK_COLS.stop + RC_COLS)
WIN_COLS = WRC_COLS.stop
IN_QK = slice(0, 2 * GLA_HEADS * GLA_DK)
IN_V = slice(IN_QK.stop, IN_QK.stop + VAL_COLS)
IN_Z = slice(IN_V.stop, IN_V.stop + 2 * GATE_RANK)
IN_RC = slice(IN_Z.stop, IN_Z.stop + RC_COLS)
IN_COLS = IN_RC.stop
IN_STAGE_ROWS = 512
SCAN_UNROLL = 2
SCAN_HEADS = 2
SCAN_RING_SLOTS = 3
MOD_ROWS = 8
ADALN_COLS = 1536
VMEM_LIMIT_BYTES = 56 * 1024 * 1024
STAGE_SLOTS = 12
STAGE_ROWS = 256
GLA_PHASE_PIECES = 8
CONTEXT_EARLY_PIECES = (12, 16)
PIECES_PER_BLOCK = D_MODEL // STAGE_ROWS
IN_STAGE_SLOTS = 4

_NT = (((1,), (1,)), ((), ()))
_BF16 = jnp.bfloat16
_F32 = jnp.float32


def _dot(a, b):
    return jnp.dot(a, b, preferred_element_type=_F32)


def _sigmoid(x):
    return 1.0 / (1.0 + jnp.exp(-x))


def _adaln_kernel(cv_ref, w_ref, b_ref, o_ref):
    cv = cv_ref[...]
    s = (cv * _sigmoid(cv)).astype(_BF16)
    mod = _dot(s, w_ref[...].astype(_BF16)) + b_ref[pl.ds(pl.program_id(0), 1), :]
    for r in range(MOD_ROWS):
        o_ref[r] = mod[r:r + 1, :]


def _adaln(cv, w_mod, b_mod):
    depth = w_mod.shape[0]
    cols = w_mod.shape[2]
    return pl.pallas_call(
        _adaln_kernel,
        grid=(depth, cols // ADALN_COLS),
        in_specs=[
            pl.BlockSpec((MOD_ROWS, D_MODEL), lambda l, j: (0, 0)),
            pl.BlockSpec((None, D_MODEL, ADALN_COLS), lambda l, j: (l, 0, j)),
            pl.BlockSpec((depth, ADALN_COLS), lambda l, j: (0, j)),
        ],
        out_specs=pl.BlockSpec((MOD_ROWS, 1, ADALN_COLS), lambda l, j: (l, 0, j)),
        out_shape=jax.ShapeDtypeStruct((depth * MOD_ROWS, 1, cols), _F32),
        compiler_params=pltpu.CompilerParams(vmem_limit_bytes=VMEM_LIMIT_BYTES),
    )(cv, w_mod, b_mod)


def _per_direction(t):
    lane = lax.broadcasted_iota(jnp.int32, (1, SLAB), 1)
    first = lane < GLA_DK
    slabs = []
    for p in range(GLA_HEADS // 2):
        pair = t[:, p * SLAB:(p + 1) * SLAB]
        swapped = pltpu.roll(pair, GLA_DK, 1)
        slabs += [jnp.where(first, pair, swapped), jnp.where(first, swapped, pair)]
    return jnp.concatenate(slabs, axis=1)


def _conv_and_pool(rc, taps, pw_ref, pool_scale, seg):
    rows = rc.shape[0]
    pos = lax.broadcasted_iota(jnp.int32, (rows, 1), 0) % seg

    def in_segment(v, d):
        return jnp.where((pos + d >= 0) & (pos + d < seg), pltpu.roll(v, (-d) % rows, 0), 0.0)

    c0 = 0
    cb = rc[:, c0:c0 + CONV_WIDTH]
    zc = rc[:, c0 + CONV_WIDTH:c0 + 2 * CONV_WIDTH] * rc[:, c0 + 2 * CONV_WIDTH:c0 + 3 * CONV_WIDTH]
    yc = cb * (taps[0] * in_segment(zc, -1) + taps[1] * zc + taps[2] * in_segment(zc, 1))

    p0 = 3 * CONV_WIDTH
    lane128 = lax.broadcasted_iota(jnp.int32, (1, 2 * POOL_GROUP), 1)
    pooled = []
    for blk in range(POOL_WIDTH // (2 * POOL_GROUP)):
        u = rc[:, p0 + blk * 2 * POOL_GROUP:p0 + (blk + 1) * 2 * POOL_GROUP]
        h_lo, h_hi = POOL_HALF_WINDOWS[2 * blk], POOL_HALF_WINDOWS[2 * blk + 1]
        half = jnp.where(lane128 < POOL_GROUP, h_lo, h_hi)
        ahead, behind, window_sums = u, in_segment(u, -1), {}
        for level in range(h_hi.bit_length()):
            h = 1 << level
            if level > 0:
                ahead = ahead + in_segment(ahead, h // 2)
                behind = behind + in_segment(behind, -(h // 2))
            if h in (h_lo, h_hi):
                window_sums[h] = ahead + behind
        acc = jnp.where(lane128 < POOL_GROUP, window_sums[h_lo], window_sums[h_hi])
        cnt = (jnp.minimum(pos + half, seg) - jnp.maximum(pos - half, 0)).astype(_F32)
        pooled.append(acc / cnt - u)
    pool_in = jnp.concatenate(pooled, axis=1).astype(_BF16)
    return yc, _dot(pool_in, pw_ref[...]) * pool_scale


def _inproj_kernel(x_ref, mod_ref, n1w_ref, win_ref, wvt_ref, gw_ref, gb_ref, cw_ref, pw_ref, ps_ref,
                   tok_ref, vt_ref, ut_ref, dec_ref, *, layer, seg):
    this_layer = slice(layer, layer + 1)
    taps = [cw_ref[t, this_layer, :] for t in range(cw_ref.shape[0])]
    pool_scale = ps_ref[this_layer, :]
    qe_ref, ki_ref, ry_ref = tok_ref.at[:, TOK_QE], tok_ref.at[:, TOK_KI], tok_ref.at[:, TOK_RY]
    vt_tile = vt_ref.shape[2]
    tm = x_ref.shape[0]
    sh1 = mod_ref[:, 0:D_MODEL]
    scale1 = n1w_ref[this_layer, :] * (1.0 + mod_ref[:, D_MODEL:2 * D_MODEL])
    row = lax.broadcasted_iota(jnp.int32, (GLA_CHUNK, GLA_CHUNK), 0)
    col = lax.broadcasted_iota(jnp.int32, (GLA_CHUNK, GLA_CHUNK), 1)
    scan_op = jnp.where(jnp.concatenate([col <= row, col >= row], axis=1), 1.0 / GATE_TAU, 0.0).astype(_BF16)
    lane = lax.broadcasted_iota(jnp.int32, (1, KEY_COLS), 1)
    is_fwd = (lane % SLAB) < GLA_DK
    mid = GLA_CHUNK // 2

    n_sub = tm // SUB_ROWS
    cps = SUB_ROWS // GLA_CHUNK
    sub_rows = [slice(s * SUB_ROWS, (s + 1) * SUB_ROWS) for s in range(n_sub)]
    hbs, projs, gates, q2s, k2s = [], [], [], [], []
    for s in range(n_sub):
        x = x_ref[sub_rows[s], :]
        xn = x * lax.rsqrt(jnp.mean(x * x, axis=-1, keepdims=True) + EPS)
        hbs.append((xn * scale1 + sh1).astype(_BF16))
    for s in range(n_sub):
        proj = lax.dot_general(hbs[s], win_ref[...], _NT, preferred_element_type=_F32)
        projs.append(proj)
        pre = _dot(proj[:, WZ_COLS].astype(_BF16), gw_ref[...]) + gb_ref[...]
        g = jnp.minimum(pre, 0.0) * LOG2_E - jnp.log2(1.0 + jnp.exp2(jnp.abs(pre) * (-LOG2_E)))
        gates.append(g.astype(_BF16))
        q2s.append(_per_direction(proj[:, WQ_COLS] * (GLA_DK ** -0.5)))
        k2s.append(_per_direction(proj[:, WK_COLS]))

    decays = []
    for s in range(n_sub):
        for cl in range(cps):
            g = gates[s][cl * GLA_CHUNK:(cl + 1) * GLA_CHUNK]
            zero = jnp.zeros_like(g)
            by_direction = jnp.concatenate([jnp.where(is_fwd, g, zero), jnp.where(is_fwd, zero, g)], axis=0)
            decays.append(_dot(scan_op, by_direction))

    vtbs = []
    for s in range(n_sub):
        vtb = lax.dot_general(wvt_ref[...], hbs[s], _NT, preferred_element_type=_F32).astype(_BF16)
        first = s * SUB_ROWS
        vt_ref[first // vt_tile, :, first % vt_tile:first % vt_tile + SUB_ROWS] = vtb
        vtbs.append(vtb)
        rc = projs[s][:, WRC_COLS]
        yc, yp = _conv_and_pool(rc[:, VAL_COLS:RC_COLS], taps, pw_ref, pool_scale, seg)
        ry_ref[sub_rows[s], 0:VAL_COLS] = rc[:, 0:VAL_COLS].astype(_BF16)
        ry_ref[sub_rows[s], VAL_COLS:VAL_COLS + CONV_WIDTH] = yc.astype(_BF16)
        ry_ref[sub_rows[s], VAL_COLS + CONV_WIDTH:RY_COLS] = yp.astype(_BF16)

    for s in range(n_sub):
        for cl in range(cps):
            c = s * cps + cl
            lrows = slice(cl * GLA_CHUNK, (cl + 1) * GLA_CHUNK)
            rows = slice(c * GLA_CHUNK, (c + 1) * GLA_CHUNK)
            e = decays[c]
            total = jnp.where(is_fwd, e[GLA_CHUNK - 1:GLA_CHUNK, :], e[0:1, :])
            mvec = jnp.where(is_fwd, e[mid - 1:mid, :], e[mid:mid + 1, :])
            bm = e - mvec
            qe_ref[rows, :] = (q2s[s][lrows] * jnp.exp2(bm)).astype(_BF16)
            k_mid = k2s[s][lrows] * jnp.exp2(-bm)
            ki_ref[rows, :] = k_mid.astype(_BF16)
            kd = (k_mid * jnp.exp2(total - mvec)).astype(_BF16)
            dec_ref[c] = jnp.concatenate([total, mvec, jnp.zeros((6, KEY_COLS), _F32)], axis=0)
            for h in range(GLA_HEADS):
                vt_h = vtbs[s][h * GLA_DV:(h + 1) * GLA_DV, lrows]
                ut_ref[c, h] = _dot(vt_h, kd[:, h * SLAB:(h + 1) * SLAB]).astype(ut_ref.dtype)


def _inproj(x2d, mods3, mod_row_of_tile, layer, win, wvt, wts, tm, seg, vt_tile):
    tokens = x2d.shape[0]
    n_tiles = tokens // tm
    n_chunks = tokens // GLA_CHUNK
    cpt = tm // GLA_CHUNK
    assert SUB_ROWS % seg == 0
    assert tm % vt_tile == 0 and vt_tile % SUB_ROWS == 0
    of_layer = lambda shape: pl.BlockSpec((None,) + shape, lambda j: (layer,) + (0,) * len(shape))
    whole = lambda a: pl.BlockSpec(a.shape, lambda j: (0,) * a.ndim)
    return pl.pallas_call(
        functools.partial(_inproj_kernel, layer=layer, seg=seg),
        grid=(n_tiles,),
        in_specs=[
            pl.BlockSpec((tm, D_MODEL), lambda j: (j, 0)),
            pl.BlockSpec((None, 1, N_MOD * D_MODEL), lambda j: (layer * MOD_ROWS + mod_row_of_tile(j), 0, 0)),
            whole(wts["n1w"]),
            pl.BlockSpec((WIN_COLS, D_MODEL), lambda j: (0, 0)),
            pl.BlockSpec((VAL_COLS, D_MODEL), lambda j: (0, 0)),
            of_layer((2 * GATE_RANK, KEY_COLS)),
            of_layer((1, KEY_COLS)),
            whole(wts["cw"]),
            of_layer((POOL_WIDTH, POOL_WIDTH)),
            whole(wts["ps"]),
        ],
        out_specs=[
            pl.BlockSpec((tm, TOK_COLS), lambda j: (j, 0)),
            pl.BlockSpec((tm // vt_tile, VAL_COLS, vt_tile), lambda j: (j, 0, 0)),
            pl.BlockSpec((cpt, GLA_HEADS, GLA_DV, SLAB), lambda j: (j, 0, 0, 0)),
            pl.BlockSpec((cpt, 8, KEY_COLS), lambda j: (j, 0, 0)),
        ],
        out_shape=[
            jax.ShapeDtypeStruct((tokens, TOK_COLS), _BF16),
            jax.ShapeDtypeStruct((tokens // vt_tile, VAL_COLS, vt_tile), _BF16),
            jax.ShapeDtypeStruct((n_chunks, GLA_HEADS, GLA_DV, SLAB), _BF16),
            jax.ShapeDtypeStruct((n_chunks, 8, KEY_COLS), _F32),
        ],
        compiler_params=pltpu.CompilerParams(vmem_limit_bytes=VMEM_LIMIT_BYTES),
    )(x2d, mods3, wts["n1w"], win, wvt, wts["gw"], wts["gb"], wts["cw"], wts["pw"], wts["ps"])


def _scan_chunks(ut_ref, dec_ref, states, spt_ref, nc, chunk0=0, head0=0, ut_chunk0=None, ut_head0=None):
    lane = lax.broadcasted_iota(jnp.int32, (1, SLAB), 1)
    is_fwd = lane < GLA_DK
    ut_shift = 0 if ut_chunk0 is None else ut_chunk0 - chunk0
    ut_head_shift = 0 if ut_head0 is None else ut_head0 - head0

    def step(t, states):
        i = chunk0 + t
        j = chunk0 + nc - 1 - t
        new_states = []
        for s in range(SCAN_HEADS):
            h = head0 + s
            lanes = slice(h * SLAB, (h + 1) * SLAB)
            log_decay = jnp.where(is_fwd, dec_ref[i, 0:1, lanes], dec_ref[j, 0:1, lanes])
            log_mid = jnp.where(is_fwd, dec_ref[i, 1:2, lanes], dec_ref[j, 1:2, lanes])
            entering = (states[s] * jnp.exp2(log_mid)).astype(_BF16)
            spt_ref[i, h, :, 0:GLA_DK] = entering[:, 0:GLA_DK]
            spt_ref[j, h, :, GLA_DK:SLAB] = entering[:, GLA_DK:SLAB]
            hu = h + ut_head_shift
            inc = jnp.where(is_fwd, ut_ref[i + ut_shift, hu], ut_ref[j + ut_shift, hu]).astype(_F32)
            new_states.append(states[s] * jnp.exp2(log_decay) + inc)
        return tuple(new_states)

    return lax.fori_loop(0, nc, step, tuple(states), unroll=SCAN_UNROLL)


def _scan_kernel(ut_hbm, dec_ref, s0_ref, spt_ref, sfin_ref, ring, sems, *, chunks_per_seq, n_steps):
    n_groups = pl.num_programs(1)
    step = pl.program_id(0) * n_groups + pl.program_id(1)

    def fetch(s):
        slot = s % SCAN_RING_SLOTS
        src = ut_hbm.at[pl.ds((s // n_groups) * chunks_per_seq, chunks_per_seq),
                        pl.ds((s % n_groups) * SCAN_HEADS, SCAN_HEADS)]
        return pltpu.make_async_copy(src, ring.at[pl.ds(slot * chunks_per_seq, chunks_per_seq)], sems.at[slot])

    @pl.when(step == 0)
    def _():
        for s in range(min(SCAN_RING_SLOTS - 1, n_steps)):
            fetch(s).start()

    fetch(step).wait()

    @pl.when(step + SCAN_RING_SLOTS - 1 < n_steps)
    def _():
        fetch(step + SCAN_RING_SLOTS - 1).start()

    final = _scan_chunks(ring, dec_ref, [s0_ref[h] for h in range(SCAN_HEADS)], spt_ref, chunks_per_seq,
                         ut_chunk0=(step % SCAN_RING_SLOTS) * chunks_per_seq)
    for h in range(SCAN_HEADS):
        sfin_ref[h] = final[h]


def _scan(ut, dec, s0, chunks_per_seq):
    n_chunks = ut.shape[0]
    bsz = n_chunks // chunks_per_seq
    n_groups = GLA_HEADS // SCAN_HEADS
    return pl.pallas_call(
        functools.partial(_scan_kernel, chunks_per_seq=chunks_per_seq, n_steps=bsz * n_groups),
        grid=(bsz, n_groups),
        in_specs=[
            pl.BlockSpec(memory_space=pl.ANY),
            pl.BlockSpec((chunks_per_seq, 8, SCAN_HEADS * SLAB), lambda b, g: (b, 0, g)),
            pl.BlockSpec((None, SCAN_HEADS, GLA_DV, SLAB), lambda b, g: (b, g, 0, 0)),
        ],
        out_specs=[
            pl.BlockSpec((chunks_per_seq, SCAN_HEADS, GLA_DV, SLAB), lambda b, g: (b, g, 0, 0)),
            pl.BlockSpec((None, SCAN_HEADS, GLA_DV, SLAB), lambda b, g: (b, g, 0, 0)),
        ],
        out_shape=[
            jax.ShapeDtypeStruct((n_chunks, GLA_HEADS, GLA_DV, SLAB), _BF16),
            jax.ShapeDtypeStruct((bsz, GLA_HEADS, GLA_DV, SLAB), _F32),
        ],
        scratch_shapes=[pltpu.VMEM((SCAN_RING_SLOTS * chunks_per_seq, SCAN_HEADS, GLA_DV, SLAB), _BF16),
                        pltpu.SemaphoreType.DMA((SCAN_RING_SLOTS,))],
        compiler_params=pltpu.CompilerParams(dimension_semantics=("arbitrary", "arbitrary"),
                                             vmem_limit_bytes=VMEM_LIMIT_BYTES),
    )(ut, dec, s0)


def _mix_weight_stager(layer, wo_hbm, w1_hbm, w2_hbm, wo_ref, w1_ref, w2_ref, stage, sems):
    blocks = [(wo_hbm.at[layer], wo_ref)]
    for i in range(D_FF // D_MODEL):
        cols = slice(i * D_MODEL, (i + 1) * D_MODEL)
        blocks.append((w1_hbm.at[layer, :, cols], w1_ref.at[:, cols]))
        blocks.append((w2_hbm.at[layer, cols, :], w2_ref.at[cols, :]))
    pieces = [(src.at[p * STAGE_ROWS:(p + 1) * STAGE_ROWS, :], dst.at[p * STAGE_ROWS:(p + 1) * STAGE_ROWS, :])
              for src, dst in blocks for p in range(PIECES_PER_BLOCK)]
    copies = [pltpu.make_async_copy(src, stage.at[k % STAGE_SLOTS], sems.at[k % STAGE_SLOTS])
              for k, (src, _) in enumerate(pieces)]
    ready = [None]

    def need(n):
        if ready[0] is None:
            for k in range(STAGE_SLOTS):
                copies[k].start(priority=k % 2)
            ready[0] = 0
        for k in range(ready[0], min(n, len(pieces))):
            copies[k].wait()
            pieces[k][1][...] = stage[k % STAGE_SLOTS].astype(_BF16)
            if k + STAGE_SLOTS < len(pieces):
                copies[k + STAGE_SLOTS].start(priority=(k + STAGE_SLOTS) % 2)
            ready[0] = k + 1

    return need


def _mix_body(x_ref, mod_ref, n2w_ref, tok_ref, vt_ref, spt_ref, gnw_ref, fnw_ref, o_ref, y_scr, wo_ref, w1_ref, w2_ref,
              need, layer, final):
    tm = x_ref.shape[0]
    qe_ref, ki_ref, ry_ref = tok_ref.at[:, TOK_QE], tok_ref.at[:, TOK_KI], tok_ref.at[:, TOK_RY]
    g1 = mod_ref[:, 2 * D_MODEL:3 * D_MODEL]
    sh2 = mod_ref[:, 3 * D_MODEL:4 * D_MODEL]
    scale2 = n2w_ref[layer:layer + 1, :] * (1.0 + mod_ref[:, 4 * D_MODEL:5 * D_MODEL])
    g2 = mod_ref[:, 5 * D_MODEL:6 * D_MODEL]
    row = lax.broadcasted_iota(jnp.int32, (GLA_CHUNK, GLA_CHUNK), 0)
    col = lax.broadcasted_iota(jnp.int32, (GLA_CHUNK, GLA_CHUNK), 1)
    lane = lax.broadcasted_iota(jnp.int32, (GLA_CHUNK, SLAB), 1)
    fwd_lane = lane < GLA_DK
    gnw = gnw_ref[...]
    need(0)

    for c in range(tm // GLA_CHUNK):
        rows = slice(c * GLA_CHUNK, (c + 1) * GLA_CHUNK)
        for h in range(GLA_HEADS):
            lanes = slice(h * SLAB, (h + 1) * SLAB)
            qe = qe_ref[rows, lanes]
            ki = ki_ref[rows, lanes]
            zero = jnp.zeros_like(ki)
            keys = jnp.concatenate([jnp.where(fwd_lane, ki, zero), jnp.where(fwd_lane, zero, ki)], axis=0)
            a2 = lax.dot_general(qe, keys, _NT, preferred_element_type=_F32)
            am = jnp.where(col <= row, a2[:, 0:GLA_CHUNK], 0.0) + jnp.where(col >= row, a2[:, GLA_CHUNK:], 0.0)
            lhs = jnp.concatenate([am.astype(_BF16), qe], axis=1)
            rhs = jnp.concatenate([vt_ref[h * GLA_DV:(h + 1) * GLA_DV, rows], spt_ref[c, h]], axis=1)
            o = lax.dot_general(lhs, rhs, _NT, preferred_element_type=_F32)
            o = o * lax.rsqrt(jnp.mean(o * o, axis=-1, keepdims=True) + EPS) * gnw
            r = ry_ref[rows, h * GLA_DV:(h + 1) * GLA_DV].astype(_F32)
            y_scr[rows, h * GLA_DV:(h + 1) * GLA_DV] = (o * (r * _sigmoid(r))).astype(_BF16)
        need((c + 1) * GLA_PHASE_PIECES // (tm // GLA_CHUNK))

    need(PIECES_PER_BLOCK)
    mixed = _dot(y_scr[...], wo_ref[0:VAL_COLS, :]) + _dot(ry_ref[:, VAL_COLS:RY_COLS], wo_ref[VAL_COLS:D_MODEL, :])
    x1 = x_ref[...] + g1 * mixed
    xn = x1 * lax.rsqrt(jnp.mean(x1 * x1, axis=-1, keepdims=True) + EPS)
    h2 = (xn * scale2 + sh2).astype(_BF16)
    acc = jnp.zeros((tm, D_MODEL), _F32)
    for f in range(D_FF // FF_CHUNK):
        cols = slice(f * FF_CHUNK, (f + 1) * FF_CHUNK)
        blocks_before = 2 * pl.cdiv(cols.stop, D_MODEL)
        need(PIECES_PER_BLOCK * blocks_before)
        a = jnp.maximum(_dot(h2, w1_ref[:, cols]), 0.0)
        need(PIECES_PER_BLOCK * (blocks_before + 1))
        acc = acc + _dot((a * a).astype(_BF16), w2_ref[cols, :])
    out = x1 + g2 * acc
    if final:
        out = out * lax.rsqrt(jnp.mean(out * out, axis=-1, keepdims=True) + EPS) * fnw_ref[...]
    o_ref[...] = out


def _mix_kernel(x_ref, mod_ref, n2w_ref, tok_ref, vt_ref, spt_ref, gnw_ref, wo_hbm, w1_hbm, w2_hbm, fnw_ref,
                o_ref, y_scr, wo_ref, w1_ref, w2_ref, stage, sems, *, layer, final):
    body = functools.partial(_mix_body, x_ref, mod_ref, n2w_ref, tok_ref, vt_ref, spt_ref, gnw_ref, fnw_ref, o_ref, y_scr,
                             wo_ref, w1_ref, w2_ref, layer=layer, final=final)

    def staging_step():
        body(need=_mix_weight_stager(layer, wo_hbm, w1_hbm, w2_hbm, wo_ref, w1_ref, w2_ref, stage, sems))

    first = pl.program_id(0) == 0
    pl.when(first)(staging_step)
    pl.when(jnp.logical_not(first))(functools.partial(body, need=lambda n: None))


def _mix_weight_scratch():
    return [pltpu.VMEM((D_MODEL, D_MODEL), _BF16), pltpu.VMEM((D_MODEL, D_FF), _BF16), pltpu.VMEM((D_FF, D_MODEL), _BF16),
            pltpu.VMEM((STAGE_SLOTS, STAGE_ROWS, D_MODEL), _F32), pltpu.SemaphoreType.DMA((STAGE_SLOTS,))]


def _mix(x2d, mods3, mod_row_of_tile, layer, tok, vt, spt, wts, tm, final):
    tokens = x2d.shape[0]
    cpt = tm // GLA_CHUNK
    of_layer = lambda shape: pl.BlockSpec((None,) + shape, lambda j: (layer,) + (0,) * len(shape),
                                          pipeline_mode=pl.Buffered(1))
    return pl.pallas_call(
        functools.partial(_mix_kernel, layer=layer, final=final),
        grid=(tokens // tm,),
        in_specs=[
            pl.BlockSpec((tm, D_MODEL), lambda j: (j, 0)),
            pl.BlockSpec((None, 1, N_MOD * D_MODEL), lambda j: (layer * MOD_ROWS + mod_row_of_tile(j), 0, 0)),
            pl.BlockSpec(wts["n2w"].shape, lambda j: (0, 0), pipeline_mode=pl.Buffered(1)),
            pl.BlockSpec((tm, TOK_COLS), lambda j: (j, 0)),
            pl.BlockSpec((None, VAL_COLS, tm), lambda j: (j, 0, 0)),
            pl.BlockSpec((cpt, GLA_HEADS, GLA_DV, SLAB), lambda j: (j, 0, 0, 0)),
            of_layer((1, GLA_DV)),
            pl.BlockSpec(memory_space=pl.ANY),
            pl.BlockSpec(memory_space=pl.ANY),
            pl.BlockSpec(memory_space=pl.ANY),
            pl.BlockSpec((1, D_MODEL), lambda j: (0, 0), pipeline_mode=pl.Buffered(1)),
        ],
        out_specs=pl.BlockSpec((tm, D_MODEL), lambda j: (j, 0)),
        out_shape=jax.ShapeDtypeStruct((tokens, D_MODEL), _F32),
        scratch_shapes=[pltpu.VMEM((tm, VAL_COLS), _BF16)] + _mix_weight_scratch(),
        compiler_params=pltpu.CompilerParams(vmem_limit_bytes=VMEM_LIMIT_BYTES),
    )(x2d, mods3, wts["n2w"], tok, vt, spt, wts["gnw"], wts["wo"], wts["w1"], wts["w2"], wts["fnw"])


def _stage_in_projection(layer, w_in_t_hbm, win_ref, wvt_ref, stage, sems):
    pieces = []
    for src, dst_ref, dst in ((IN_Z, win_ref, WZ_COLS.start), (IN_QK, win_ref, WQ_COLS.start),
                              (IN_V, wvt_ref, 0), (IN_RC, win_ref, WRC_COLS.start)):
        for off in range(0, src.stop - src.start, IN_STAGE_ROWS):
            pieces.append((src.start + off, dst_ref, dst + off, min(IN_STAGE_ROWS, src.stop - src.start - off)))
    copies = [pltpu.make_async_copy(w_in_t_hbm.at[layer, s0:s0 + n, :], stage.at[k % IN_STAGE_SLOTS, 0:n, :],
                                    sems.at[k % IN_STAGE_SLOTS]) for k, (s0, _, _, n) in enumerate(pieces)]
    for k in range(IN_STAGE_SLOTS):
        copies[k].start()
    win_ref[WZ_COLS.stop:WQ_COLS.start, :] = jnp.zeros((WQ_COLS.start - WZ_COLS.stop, D_MODEL), _BF16)
    for k, (_, dst_ref, d0, n) in enumerate(pieces):
        copies[k].wait()
        dst_ref[d0:d0 + n, :] = stage[k % IN_STAGE_SLOTS, 0:n, :].astype(_BF16)
        if k + IN_STAGE_SLOTS < len(pieces):
            copies[k + IN_STAGE_SLOTS].start()


def _context_kernel(*refs, layer, seg, chunks_per_seq, with_mix):
    n_in = 9
    x_ref, mod_ref, n1w_ref, w_in_t_hbm = refs[0:4]
    if with_mix:
        n2w_ref, gnw_ref, wo_hbm, w1_hbm, w2_hbm, fnw_ref = refs[n_in:n_in + 6]
        sfin_ref, win_ref, wvt_ref, o_ref = refs[n_in + 6:n_in + 10]
        tok_scr, vt_scr, ut_scr, dec_scr, spt_scr, in_stage, in_sems, y_scr = refs[n_in + 10:n_in + 18]
        mix_weight_scr = refs[n_in + 18:]
    else:
        sfin_ref, win_ref, wvt_ref = refs[n_in:n_in + 3]
        tok_scr, vt_scr, ut_scr, dec_scr, spt_scr, in_stage, in_sems = refs[n_in + 3:]

    need = lambda n: None
    if with_mix:
        wo_ref, w1_ref, w2_ref, mix_stage, mix_sems = mix_weight_scr
        need = _mix_weight_stager(layer, wo_hbm, w1_hbm, w2_hbm, wo_ref, w1_ref, w2_ref, mix_stage, mix_sems)

    _stage_in_projection(layer, w_in_t_hbm, win_ref, wvt_ref, in_stage, in_sems)
    need(0)
    _inproj_kernel(x_ref, mod_ref, n1w_ref, win_ref, wvt_ref, *refs[4:n_in], tok_scr, vt_scr, ut_scr, dec_scr,
                   layer=layer, seg=seg)
    need(CONTEXT_EARLY_PIECES[0])

    sfin_ref[...] = jnp.zeros(sfin_ref.shape, _F32)
    for b in range(ut_scr.shape[0] // chunks_per_seq):
        for head0 in range(0, GLA_HEADS, SCAN_HEADS):
            start = [sfin_ref[b, head0 + s] for s in range(SCAN_HEADS)]
            final = _scan_chunks(ut_scr, dec_scr, start, spt_scr, chunks_per_seq,
                                 chunk0=b * chunks_per_seq, head0=head0)
            for s in range(SCAN_HEADS):
                sfin_ref[b, head0 + s] = final[s]

    if with_mix:
        need(CONTEXT_EARLY_PIECES[1])
        _mix_body(x_ref, mod_ref, n2w_ref, tok_scr, vt_scr.at[0], spt_scr, gnw_ref, fnw_ref, o_ref, y_scr,
                  wo_ref, w1_ref, w2_ref, need=need, layer=layer, final=False)


def _context_layer(xc, mods3, mod_row, layer, wts, ctx_len, with_mix):
    rows = xc.shape[0]
    bsz = rows // ctx_len
    n_chunks = rows // GLA_CHUNK
    assert rows % SUB_ROWS == 0 and SUB_ROWS % ctx_len == 0
    of_layer = lambda shape: pl.BlockSpec((None,) + shape, lambda j: (layer,) + (0,) * len(shape),
                                          pipeline_mode=pl.Buffered(1))
    whole = lambda a: pl.BlockSpec(a.shape, lambda j: (0,) * a.ndim, pipeline_mode=pl.Buffered(1))
    in_specs = [
        pl.BlockSpec((rows, D_MODEL), lambda j: (0, 0)),
        pl.BlockSpec((None, 1, N_MOD * D_MODEL), lambda j: (layer * MOD_ROWS + mod_row, 0, 0)),
        whole(wts["n1w"]),
        pl.BlockSpec(memory_space=pl.ANY),
        of_layer((2 * GATE_RANK, KEY_COLS)),
        of_layer((1, KEY_COLS)),
        whole(wts["cw"]),
        of_layer((POOL_WIDTH, POOL_WIDTH)),
        whole(wts["ps"]),
    ]
    args = [xc, mods3, wts["n1w"], wts["w_in_t"], wts["gw"], wts["gb"], wts["cw"], wts["pw"], wts["ps"]]
    out_specs = [pl.BlockSpec((bsz, GLA_HEADS, GLA_DV, SLAB), lambda j: (0, 0, 0, 0)),
                 pl.BlockSpec((WIN_COLS, D_MODEL), lambda j: (0, 0)),
                 pl.BlockSpec((VAL_COLS, D_MODEL), lambda j: (0, 0))]
    out_shape = [jax.ShapeDtypeStruct((bsz, GLA_HEADS, GLA_DV, SLAB), _F32),
                 jax.ShapeDtypeStruct((WIN_COLS, D_MODEL), _BF16),
                 jax.ShapeDtypeStruct((VAL_COLS, D_MODEL), _BF16)]
    scratch = [
        pltpu.VMEM((rows, TOK_COLS), _BF16),
        pltpu.VMEM((1, VAL_COLS, rows), _BF16),
        pltpu.VMEM((n_chunks, GLA_HEADS, GLA_DV, SLAB), _BF16),
        pltpu.VMEM((n_chunks, 8, KEY_COLS), _F32),
        pltpu.VMEM((n_chunks, GLA_HEADS, GLA_DV, SLAB), _BF16),
        pltpu.VMEM((IN_STAGE_SLOTS, IN_STAGE_ROWS, D_MODEL), _F32),
        pltpu.SemaphoreType.DMA((IN_STAGE_SLOTS,)),
    ]
    if with_mix:
        in_specs += [whole(wts["n2w"]), of_layer((1, GLA_DV))] + [pl.BlockSpec(memory_space=pl.ANY)] * 3
        in_specs += [pl.BlockSpec((1, D_MODEL), lambda j: (0, 0), pipeline_mode=pl.Buffered(1))]
        args += [wts["n2w"], wts["gnw"], wts["wo"], wts["w1"], wts["w2"], wts["fnw"]]
        out_specs.append(pl.BlockSpec((rows, D_MODEL), lambda j: (0, 0)))
        out_shape.append(jax.ShapeDtypeStruct((rows, D_MODEL), _F32))
        scratch += [pltpu.VMEM((rows, VAL_COLS), _BF16)] + _mix_weight_scratch()
    outs = pl.pallas_call(
        functools.partial(_context_kernel, layer=layer, seg=ctx_len, chunks_per_seq=ctx_len // GLA_CHUNK,
                          with_mix=with_mix),
        grid=(1,),
        in_specs=in_specs,
        out_specs=out_specs,
        out_shape=out_shape,
        scratch_shapes=scratch,
        compiler_params=pltpu.CompilerParams(vmem_limit_bytes=VMEM_LIMIT_BYTES),
    )(*args)
    return tuple(outs) if with_mix else tuple(outs) + (None,)


def _prepare_weights(norm1_w, norm2_w, w_in, gate_w, gate_b, gla_norm_w, conv_w, pool_w, pool_scale, w_out,
                     w_mlp1, w_mlp2, final_norm_w):
    depth = w_in.shape[0]
    assert w_in.shape[2] == IN_COLS
    zeros = jnp.zeros((depth, GATE_RANK, GLA_HEADS, GLA_DK), _F32)
    gf = gate_w[:, 0].reshape(depth, GATE_RANK, GLA_HEADS, GLA_DK)
    gb = gate_w[:, 1].reshape(depth, GATE_RANK, GLA_HEADS, GLA_DK)
    gw = jnp.concatenate([jnp.concatenate([gf, zeros], -1), jnp.concatenate([zeros, gb], -1)], 1)
    gbias = jnp.concatenate([gate_b[:, 0].reshape(depth, GLA_HEADS, GLA_DK),
                             gate_b[:, 1].reshape(depth, GLA_HEADS, GLA_DK)], -1)
    n_groups = POOL_WIDTH // POOL_GROUP
    same_group = jnp.eye(n_groups, dtype=_F32)[None, :, None, :, None]
    pw = (pool_w[:, :, :, None, :] * same_group).reshape(depth, POOL_WIDTH, POOL_WIDTH)
    return {
        "n1w": norm1_w,
        "n2w": norm2_w,
        "w_in_t": jnp.swapaxes(w_in, 1, 2),
        "gw": gw.reshape(depth, 2 * GATE_RANK, KEY_COLS).astype(_BF16),
        "gb": gbias.reshape(depth, 1, KEY_COLS),
        "gnw": gla_norm_w.reshape(depth, 1, GLA_DV),
        "cw": jnp.swapaxes(conv_w, 0, 1),
        "pw": pw.astype(_BF16),
        "ps": pool_scale,
        "wo": w_out,
        "w1": w_mlp1,
        "w2": w_mlp2,
        "fnw": final_norm_w.reshape(1, D_MODEL),
    }


def kernel(x, c, ctx, c_ctx, w_mod, b_mod, norm1_w, norm2_w, w_in, gla_gate_w, gla_gate_b, gla_norm_w, conv_w, pool_w, pool_scale, w_out, w_mlp1, w_mlp2, final_norm_w):
    bsz, n, _ = x.shape
    ctx_len = ctx.shape[1]
    depth = w_in.shape[0]
    assert bsz + 1 <= MOD_ROWS and n % LATENT_IN_TILE == 0 and SUB_ROWS % ctx_len == 0
    ctx_row = bsz

    cv = jnp.concatenate([c, c_ctx[None, :], jnp.zeros((MOD_ROWS - bsz - 1, D_MODEL), _F32)], axis=0)
    mods3 = _adaln(cv, w_mod, b_mod)
    wts = _prepare_weights(norm1_w, norm2_w, w_in, gla_gate_w, gla_gate_b, gla_norm_w, conv_w, pool_w, pool_scale,
                           w_out, w_mlp1, w_mlp2, final_norm_w)

    xl = x.reshape(bsz * n, D_MODEL)
    xc = ctx.reshape(bsz * ctx_len, D_MODEL)
    lat_in_row = lambda j: j // (n // LATENT_IN_TILE)
    lat_mix_row = lambda j: j // (n // LATENT_MIX_TILE)

    for l in range(depth):
        last = l == depth - 1
        s_ctx, win, wvt, xc = _context_layer(xc, mods3, ctx_row, l, wts, ctx_len, with_mix=not last)

        tok, vt, ut, dec = _inproj(xl, mods3, lat_in_row, l, win, wvt, wts, LATENT_IN_TILE, GRID_W, LATENT_MIX_TILE)
        spt, _ = _scan(ut, dec, s_ctx, n // GLA_CHUNK)
        xl = _mix(xl, mods3, lat_mix_row, l, tok, vt, spt, wts, LATENT_MIX_TILE, last)
    return xl.reshape(bsz, n, D_MODEL)
```
